```python
import math
import jax, jax.numpy as jnp
from jax import lax
import numpy as np

D_MODEL = 1024
BATCH = 1
SEQ = 16384
DEPTH = 2
DEC_BATCH = 32
DEC_SEQ = 8
PAST_LEN = 16384
PAGE_SIZE = 128

F32 = jnp.float32
HD = 64
N_BRANCH = 4
BR_HEADS = 4
BR_W = BR_HEADS * HD
IDX_HEADS = 8
IDX_DIM = 64
TOPK_MAX = 256
Q_BLOCK = 128
ROPE_THETA = 10000.0
MLSTM_CHUNK = 64
GDN_CHUNK = 64
CONV_W = 4
RWKV_W_LORA = 64
RWKV_A_LORA = 64
RWKV_G_LORA = 128
RWKV_PROJ = 3 * BR_W + RWKV_W_LORA + RWKV_A_LORA + RWKV_G_LORA
GDN_CONV_CH = 3 * BR_W
D_FF = ((8 * D_MODEL // 3 + 127) // 128) * 128
EPS = 1e-6
LN_EPS = 1e-5
SPLIT_SIZES = (BR_W, BR_W, BR_W, IDX_HEADS * IDX_DIM, IDX_DIM, IDX_HEADS,
               BR_W, BR_W, BR_W, BR_HEADS, BR_HEADS, BR_W,
               RWKV_PROJ,
               GDN_CONV_CH, BR_HEADS, BR_HEADS, BR_W,
               N_BRANCH * D_MODEL)
D_IN = sum(SPLIT_SIZES)

kernel_name = 'hybrid_dsa_mlstm_rwkv7_gdn_macaron_step'


def _split(z):
    cuts = np.cumsum(SPLIT_SIZES)[:-1].tolist()
    return jnp.split(z, cuts, axis=-1)


def _rms(x, g):
    xf = x.astype(F32)
    y = xf * lax.rsqrt(jnp.mean(xf * xf, axis=-1, keepdims=True) + EPS)
    return (y * g.astype(F32)).astype(x.dtype)


def _head_ln(y, g, b=None):
    mu = jnp.mean(y, axis=-1, keepdims=True)
    var = jnp.mean(jnp.square(y - mu), axis=-1, keepdims=True)
    y = (y - mu) * lax.rsqrt(var + LN_EPS) * g
    return y if b is None else y + b


def _l2n(x):
    return x * lax.rsqrt(jnp.maximum(jnp.sum(x * x, axis=-1, keepdims=True), 1e-12))


def _swiglu(x, w1, w3, w2):
    return (jax.nn.silu(x @ w1) * (x @ w3)) @ w2


def _rope(x, pos):
    half = x.shape[-1] // 2
    inv = ROPE_THETA ** (-jnp.arange(half, dtype=F32) / half)
    ang = pos.astype(F32)[:, None] * inv[None, :]
    cos = jnp.cos(ang)[None, :, None, :]
    sin = jnp.sin(ang)[None, :, None, :]
    xf = x.astype(F32)
    x1, x2 = xf[..., :half], xf[..., half:]
    return jnp.concatenate([x1 * cos - x2 * sin, x2 * cos + x1 * sin], axis=-1).astype(x.dtype)


def _to_chunks(a, c):
    B, L = a.shape[:2]
    a = a.reshape((B, L // c, c) + a.shape[2:])
    return a.transpose((1, 0, 3, 2) + tuple(range(4, a.ndim)))


def _from_chunks(a):
    nc, B, H, c = a.shape[:4]
    a = a.transpose((1, 0, 3, 2) + tuple(range(4, a.ndim)))
    return a.reshape((B, nc * c, H) + a.shape[4:])


def _index_scores(qi, wi, ki):
    dots = jax.nn.relu(jnp.einsum('bthd,bsd->bths', qi, ki).astype(F32) * IDX_DIM ** -0.5)
    return jnp.einsum('bth,bths->bts', wi.astype(F32), dots)


def _sparse_attend(q, kg, vg, valid):
    s = jnp.einsum('bthd,btkhd->bthk', q, kg).astype(F32) * HD ** -0.5
    s = jnp.where(valid[:, :, None, :], s, -jnp.inf)
    p = jax.nn.softmax(s, axis=-1).astype(vg.dtype)
    return jnp.einsum('bthk,btkhd->bthd', p, vg)


def _dsa_prompt(q, k, v, qi, ki, wi):
    B, L = q.shape[:2]
    n_sel = min(TOPK_MAX, L // 4)
    qb = math.gcd(L, Q_BLOCK)
    nb = L // qb
    kpos = jnp.arange(L, dtype=jnp.int32)

    def blk(args):
        q_b, qi_b, wi_b, pos_b = args
        sc = _index_scores(qi_b, wi_b, ki)
        sc = jnp.where((kpos[None, :] <= pos_b[:, None])[None], sc, -jnp.inf)
        _, idx = lax.top_k(sc, n_sel)
        kg = jax.vmap(lambda a, i: a[i])(k, idx)
        vg = jax.vmap(lambda a, i: a[i])(v, idx)
        return _sparse_attend(q_b, kg, vg, idx <= pos_b[None, :, None])

    def blocks(a):
        return a.reshape((B, nb, qb) + a.shape[2:]).swapaxes(0, 1)

    out = lax.map(blk, (blocks(q), blocks(qi), blocks(wi), kpos.reshape(nb, qb)))
    return out.swapaxes(0, 1).reshape(q.shape)


def _dsa_sample(q, k, v, qi, ki, wi, cache_k, cache_v, cache_idx_k, page_table, layer):
    DB, T = q.shape[:2]
    past = page_table.shape[1] * PAGE_SIZE
    L = past + T
    n_sel = min(TOPK_MAX, L // 4)
    ki_past = cache_idx_k[layer, page_table].reshape(DB, past, IDX_DIM)
    ki_all = jnp.concatenate([ki_past, ki.astype(ki_past.dtype)], axis=1)
    sc = _index_scores(qi, wi, ki_all)
    qpos = past + jnp.arange(T, dtype=jnp.int32)
    sc = jnp.where((jnp.arange(L, dtype=jnp.int32)[None, :] <= qpos[:, None])[None], sc, -jnp.inf)
    _, idx = lax.top_k(sc, n_sel)
    is_past = (idx < past)[..., None, None]
    pidx = jnp.minimum(idx, past - 1)
    b = jnp.arange(DB)[:, None, None]
    phys = page_table[b, pidx // PAGE_SIZE]
    off = pidx % PAGE_SIZE
    nidx = jnp.clip(idx - past, 0, T - 1)
    kg = jnp.where(is_past, cache_k[layer, phys, off], k[b, nidx])
    vg = jnp.where(is_past, cache_v[layer, phys, off], v[b, nidx])
    return _sparse_attend(q, kg, vg, idx <= qpos[None, :, None])


def _mlstm(q, k, v, ig, fg, og, C0, n0, m0, norm_g):
    B, L = q.shape[:2]
    c = math.gcd(L, MLSTM_CHUNK)
    causal = jnp.tril(jnp.ones((c, c), dtype=bool))

    def step(carry, xs):
        C, n, m = carry
        qc, kc, vc, li, lf = xs
        b = jnp.cumsum(lf, axis=-1)
        logD = jnp.where(causal, b[..., :, None] - b[..., None, :] + li[..., None, :], -jnp.inf)
        inter = b + m[..., None]
        mt = jnp.maximum(inter, jnp.max(logD, axis=-1))
        S = jnp.einsum('bhtd,bhsd->bhts', qc, kc) * jnp.exp(logD - mt[..., None])
        ei = jnp.exp(inter - mt)
        num = S @ vc + ei[..., None] * jnp.einsum('bhed,bhtd->bhte', C, qc)
        den = jnp.sum(S, axis=-1) + ei * jnp.einsum('bhd,bhtd->bht', n, qc)
        h = num / jnp.maximum(jnp.abs(den), jnp.exp(-mt))[..., None]
        bc = b[..., -1]
        gs = bc[..., None] - b + li
        mn = jnp.maximum(bc + m, jnp.max(gs, axis=-1))
        ws = jnp.exp(gs - mn[..., None])
        dec = jnp.exp(bc + m - mn)
        C = dec[..., None, None] * C + jnp.einsum('bhs,bhse,bhsd->bhed', ws, vc, kc)
        n = dec[..., None] * n + jnp.einsum('bhs,bhsd->bhd', ws, kc)
        return (C, n, mn), h

    xs = (_to_chunks(q.astype(F32), c), _to_chunks(k.astype(F32) * HD ** -0.5, c), _to_chunks(v.astype(F32), c),
          _to_chunks(ig.astype(F32), c), _to_chunks(jax.nn.log_sigmoid(fg.astype(F32)), c))
    (C1, n1, m1), h = lax.scan(step, (C0.astype(F32), n0.astype(F32), m0.astype(F32)), xs)
    h = _head_ln(_from_chunks(h), norm_g.astype(F32).reshape(BR_HEADS, HD)) * jax.nn.sigmoid(og.astype(F32))
    return h.reshape(B, L, BR_W), C1, n1, m1


def _rwkv7(P, sh0, S0, lp):
    B, L, _ = P.shape
    Pf = P.astype(F32)
    prev = jnp.concatenate([sh0.astype(F32)[:, None], Pf[:, :-1]], axis=1)
    Pm = Pf + (prev - Pf) * lp['rw_mu']
    cuts = [BR_W, 2 * BR_W, 3 * BR_W, 3 * BR_W + RWKV_W_LORA, 3 * BR_W + RWKV_W_LORA + RWKV_A_LORA]
    r, k, v, wd, ad, gd = jnp.split(Pm, cuts, axis=-1)
    wlog = -jax.nn.softplus(-(lp['rw_w0'] + jnp.tanh(wd) @ lp['rw_w2'])) - 0.5
    decay = jnp.exp(-jnp.exp(wlog))
    a = jax.nn.sigmoid(lp['rw_a0'] + ad @ lp['rw_a2'])
    g = jax.nn.sigmoid(gd) @ lp['rw_g2']
    hd = lambda t: t.reshape(B, L, BR_HEADS, HD)
    kk = _l2n(hd(k * lp['rw_kk']))
    k = k * (1.0 + (a - 1.0) * lp['rw_ka'])
    r, k, v, decay, a = hd(r), hd(k), hd(v), hd(decay), hd(a)

    def step(S, xs):
        r_t, w_t, k_t, v_t, kk_t, a_t = xs
        sa = jnp.einsum('bhij,bhj->bhi', S, -kk_t)
        S = S * w_t[:, :, None, :] + sa[..., None] * (kk_t * a_t)[:, :, None, :] + v_t[..., None] * k_t[:, :, None, :]
        return S, jnp.einsum('bhij,bhj->bhi', S, r_t)

    tm = lambda t: jnp.moveaxis(t, 1, 0)
    S1, y = lax.scan(step, S0.astype(F32), (tm(r), tm(decay), tm(k), tm(v), tm(kk), tm(a)))
    y = jnp.moveaxis(y, 0, 1)
    y = _head_ln(y, lp['rw_ln_g'].reshape(BR_HEADS, HD), lp['rw_ln_b'].reshape(BR_HEADS, HD))
    y = y + jnp.sum(r * k * lp['rw_rk'], axis=-1, keepdims=True) * v
    return y.reshape(B, L, BR_W) * g, S1, P[:, -1]


def _gdn(qkv, braw, araw, z, cv0, S0, lp):
    B, L, _ = qkv.shape
    xp = jnp.concatenate([cv0.astype(qkv.dtype), qkv], axis=1)
    taps = lp['gd_conv'].astype(F32)
    conv = jax.nn.silu(sum(xp[:, j:j + L].astype(F32) * taps[j] for j in range(CONV_W)))
    q, k, v = [t.reshape(B, L, BR_HEADS, HD) for t in jnp.split(conv, 3, axis=-1)]
    q = _l2n(q) * HD ** -0.5
    k = _l2n(k)
    beta = jax.nn.sigmoid(braw.astype(F32))
    g = -jnp.exp(lp['gd_A_log'].astype(F32)) * jax.nn.softplus(araw.astype(F32) + lp['gd_dt_bias'])
    c = math.gcd(L, GDN_CHUNK)
    incl = jnp.tril(jnp.ones((c, c), dtype=bool))
    strict = jnp.tril(jnp.ones((c, c), dtype=bool), k=-1)
    eye = jnp.eye(c, dtype=F32)

    def step(S, xs):
        qc, kc, vc, bc, gc = xs
        gam = jnp.cumsum(gc, axis=-1)
        dec = jnp.exp(jnp.where(incl, gam[..., :, None] - gam[..., None, :], -jnp.inf))
        eg = jnp.exp(gam)
        A = bc[..., None] * jnp.einsum('bhtd,bhsd->bhts', kc, kc) * jnp.where(strict, dec, 0.0)
        rhs = bc[..., None] * (vc - eg[..., None] * jnp.einsum('bhed,bhtd->bhte', S, kc))
        delta = lax.linalg.triangular_solve(A + eye, rhs, left_side=True, lower=True, unit_diagonal=True)
        o = eg[..., None] * jnp.einsum('bhed,bhtd->bhte', S, qc) + jnp.einsum('bhts,bhse->bhte', jnp.einsum('bhtd,bhsd->bhts', qc, kc) * dec, delta)
        gl = gam[..., -1]
        S = jnp.exp(gl)[..., None, None] * S + jnp.einsum('bhs,bhse,bhsd->bhed', jnp.exp(gl[..., None] - gam), delta, kc)
        return S, o

    S1, o = lax.scan(step, S0.astype(F32), (_to_chunks(q, c), _to_chunks(k, c), _to_chunks(v, c), _to_chunks(beta, c), _to_chunks(g, c)))
    o = _from_chunks(o)
    o = o * lax.rsqrt(jnp.mean(o * o, axis=-1, keepdims=True) + EPS) * lp['gd_norm'] * jax.nn.silu(z.astype(F32))
    return o.reshape(B, L, BR_W), S1, xp[:, L:]


def _zero_state(B, dt):
    z = lambda *s: jnp.zeros(s, dt)
    return (z(B, BR_HEADS, HD, HD), z(B, BR_HEADS, HD), z(B, BR_HEADS), z(B, BR_HEADS, HD, HD),
            z(B, RWKV_PROJ), z(B, BR_HEADS, HD, HD), z(B, CONV_W - 1, GDN_CONV_CH))


def _layer(x, lp, pos, init, attn_fn, l):
    C0, n0, m0, Sr0, sh0, Sg0, cv0 = init
    B, L, _ = x.shape
    dt = x.dtype
    x = x + 0.5 * _swiglu(_rms(x, lp['ffn1_norm']), lp['ffn1_w1'], lp['ffn1_w3'], lp['ffn1_w2'])
    h = _rms(x, lp['mix_norm'])
    (a_q, a_k, a_v, a_iq, a_ik, a_iw, m_q, m_k, m_v, m_i, m_f, m_o,
     r_p, g_qkv, g_b, g_a, g_z, gate) = _split(h @ lp['w_in'])
    heads = lambda t: t.reshape(B, L, BR_HEADS, HD)
    q = _rope(heads(a_q), pos)
    k = _rope(heads(a_k), pos)
    v = heads(a_v)
    qi = _rope(a_iq.reshape(B, L, IDX_HEADS, IDX_DIM), pos)
    ki = _rope(_rms(a_ik, lp['idx_k_norm'])[:, :, None, :], pos)[:, :, 0, :]
    wi = a_iw * IDX_HEADS ** -0.5
    o_a = attn_fn(l, q, k, v, qi, ki, wi).reshape(B, L, BR_W)
    o_m, C1, n1, m1 = _mlstm(heads(m_q), heads(m_k), heads(m_v), m_i + lp['ml_i_bias'], m_f + lp['ml_f_bias'],
                             heads(m_o), C0, n0, m0, lp['ml_norm'])
    o_r, Sr1, sh1 = _rwkv7(r_p, sh0, Sr0, lp)
    o_g, Sg1, cv1 = _gdn(g_qkv, g_b, g_a, heads(g_z), cv0, Sg0, lp)
    hs = jnp.stack([o_a.astype(dt), o_m.astype(dt), o_r.astype(dt), o_g.astype(dt)], axis=2)
    up = jnp.einsum('blnc,ncd->blnd', hs, lp['w_branch'])
    gt = jax.nn.sigmoid(gate.reshape(B, L, N_BRANCH, D_MODEL))
    x = x + jnp.einsum('blnd,blnd->bld', gt, up) @ lp['w_out']
    x = x + 0.5 * _swiglu(_rms(x, lp['ffn2_norm']), lp['ffn2_w1'], lp['ffn2_w3'], lp['ffn2_w2'])
    new = (k, v, ki, C1, n1, m1, Sr1, sh1, Sg1, cv1)
    return x, [s.astype(dt) for s in new]


def setup_inputs(seed: int = 0) -> dict:
    key = jax.random.key(seed)
    ks = iter(jax.random.split(key, 64))

    def nrm(shape, scale=1.0):
        return jax.random.normal(next(ks), shape, F32) * scale

    def unif(shape, lo, hi):
        return jax.random.uniform(next(ks), shape, F32, lo, hi)

    def gain(shape):
        return 1.0 + nrm(shape, 0.02)

    H = BR_HEADS
    n_pages = PAST_LEN // PAGE_SIZE
    n_pool = (DEC_BATCH * n_pages * 5) // 4
    page_table = jax.random.permutation(next(ks), n_pool)[:DEC_BATCH * n_pages].reshape(DEC_BATCH, n_pages).astype(jnp.int32)
    dt_init = unif((DEPTH, H), 1e-3, 1e-1)
    return {
        'x_prompt': nrm((BATCH, SEQ, D_MODEL)),
        'x_sample': nrm((DEC_BATCH, DEC_SEQ, D_MODEL)),
        'cache_k': nrm((DEPTH, n_pool, PAGE_SIZE, H, HD)),
        'cache_v': nrm((DEPTH, n_pool, PAGE_SIZE, H, HD)),
        'cache_idx_k': nrm((DEPTH, n_pool, PAGE_SIZE, IDX_DIM)),
        'state_mlstm_C': nrm((DEPTH, DEC_BATCH, H, HD, HD), 0.05),
        'state_mlstm_n': nrm((DEPTH, DEC_BATCH, H, HD), 0.1),
        'state_mlstm_m': nrm((DEPTH, DEC_BATCH, H)),
        'state_rwkv_S': nrm((DEPTH, DEC_BATCH, H, HD, HD), 0.1),
        'state_rwkv_shift': nrm((DEPTH, DEC_BATCH, RWKV_PROJ)),
        'state_gdn_S': nrm((DEPTH, DEC_BATCH, H, HD, HD), 0.1),
        'state_gdn_conv': nrm((DEPTH, DEC_BATCH, CONV_W - 1, GDN_CONV_CH)),
        'page_table': page_table,
        'ffn1_norm': gain((DEPTH, D_MODEL)),
        'ffn1_w1': nrm((DEPTH, D_MODEL, D_FF), D_MODEL ** -0.5),
        'ffn1_w3': nrm((DEPTH, D_MODEL, D_FF), D_MODEL ** -0.5),
        'ffn1_w2': nrm((DEPTH, D_FF, D_MODEL), D_FF ** -0.5),
        'mix_norm': gain((DEPTH, D_MODEL)),
        'w_in': nrm((DEPTH, D_MODEL, D_IN), D_MODEL ** -0.5),
        'idx_k_norm': gain((DEPTH, IDX_DIM)),
        'ml_i_bias': nrm((DEPTH, H), 0.1),
        'ml_f_bias': jnp.linspace(3.0, 6.0, H, dtype=F32)[None, :] + nrm((DEPTH, H), 0.1),
        'ml_norm': gain((DEPTH, BR_W)),
        'rw_mu': unif((DEPTH, RWKV_PROJ), 0.1, 0.9),
        'rw_w0': unif((DEPTH, BR_W), -6.0, 0.0),
        'rw_w2': nrm((DEPTH, RWKV_W_LORA, BR_W), RWKV_W_LORA ** -0.5),
        'rw_a0': nrm((DEPTH, BR_W), 0.1),
        'rw_a2': nrm((DEPTH, RWKV_A_LORA, BR_W), RWKV_A_LORA ** -0.5),
        'rw_g2': nrm((DEPTH, RWKV_G_LORA, BR_W), RWKV_G_LORA ** -0.5),
        'rw_kk': 0.85 + nrm((DEPTH, BR_W), 0.02),
        'rw_ka': gain((DEPTH, BR_W)),
        'rw_rk': nrm((DEPTH, H, HD), 0.1),
        'rw_ln_g': gain((DEPTH, BR_W)),
        'rw_ln_b': nrm((DEPTH, BR_W), 0.02),
        'gd_conv': nrm((DEPTH, CONV_W, GDN_CONV_CH), CONV_W ** -0.5),
        'gd_A_log': jnp.log(unif((DEPTH, H), 1.0, 16.0)),
        'gd_dt_bias': dt_init + jnp.log(-jnp.expm1(-dt_init)),
        'gd_norm': gain((DEPTH, HD)),
        'w_branch': nrm((DEPTH, N_BRANCH, BR_W, D_MODEL), BR_W ** -0.5),
        'w_out': nrm((DEPTH, D_MODEL, D_MODEL), D_MODEL ** -0.5),
        'ffn2_norm': gain((DEPTH, D_MODEL)),
        'ffn2_w1': nrm((DEPTH, D_MODEL, D_FF), D_MODEL ** -0.5),
        'ffn2_w3': nrm((DEPTH, D_MODEL, D_FF), D_MODEL ** -0.5),
        'ffn2_w2': nrm((DEPTH, D_FF, D_MODEL), D_FF ** -0.5),
        'final_norm': gain((D_MODEL,)),
    }


def reference(x_prompt, x_sample, cache_k, cache_v, cache_idx_k, state_mlstm_C, state_mlstm_n, state_mlstm_m,
              state_rwkv_S, state_rwkv_shift, state_gdn_S, state_gdn_conv, page_table,
              ffn1_norm, ffn1_w1, ffn1_w3, ffn1_w2, mix_norm, w_in, idx_k_norm, ml_i_bias, ml_f_bias, ml_norm,
              rw_mu, rw_w0, rw_w2, rw_a0, rw_a2, rw_g2, rw_kk, rw_ka, rw_rk, rw_ln_g, rw_ln_b,
              gd_conv, gd_A_log, gd_dt_bias, gd_norm, w_branch, w_out,
              ffn2_norm, ffn2_w1, ffn2_w3, ffn2_w2, final_norm):
    past = page_table.shape[1] * PAGE_SIZE
    pos_p = jnp.arange(x_prompt.shape[1], dtype=jnp.int32)
    pos_s = past + jnp.arange(x_sample.shape[1], dtype=jnp.int32)

    def attn_p(l, q, k, v, qi, ki, wi):
        return _dsa_prompt(q, k, v, qi, ki, wi)

    def attn_s(l, q, k, v, qi, ki, wi):
        return _dsa_sample(q, k, v, qi, ki, wi, cache_k, cache_v, cache_idx_k, page_table, l)

    hp, hs = x_prompt, x_sample
    st_p, st_s = [], []
    for l in range(DEPTH):
        lp = dict(ffn1_norm=ffn1_norm[l], ffn1_w1=ffn1_w1[l], ffn1_w3=ffn1_w3[l], ffn1_w2=ffn1_w2[l],
                  mix_norm=mix_norm[l], w_in=w_in[l], idx_k_norm=idx_k_norm[l],
                  ml_i_bias=ml_i_bias[l], ml_f_bias=ml_f_bias[l], ml_norm=ml_norm[l],
                  rw_mu=rw_mu[l], rw_w0=rw_w0[l], rw_w2=rw_w2[l], rw_a0=rw_a0[l], rw_a2=rw_a2[l], rw_g2=rw_g2[l],
                  rw_kk=rw_kk[l], rw_ka=rw_ka[l], rw_rk=rw_rk[l], rw_ln_g=rw_ln_g[l], rw_ln_b=rw_ln_b[l],
                  gd_conv=gd_conv[l], gd_A_log=gd_A_log[l], gd_dt_bias=gd_dt_bias[l], gd_norm=gd_norm[l],
                  w_branch=w_branch[l], w_out=w_out[l],
                  ffn2_norm=ffn2_norm[l], ffn2_w1=ffn2_w1[l], ffn2_w3=ffn2_w3[l], ffn2_w2=ffn2_w2[l])
        init_s = (state_mlstm_C[l], state_mlstm_n[l], state_mlstm_m[l], state_rwkv_S[l], state_rwkv_shift[l],
                  state_gdn_S[l], state_gdn_conv[l])
        hp, sp = _layer(hp, lp, pos_p, _zero_state(x_prompt.shape[0], x_prompt.dtype), attn_p, l)
        hs, ss = _layer(hs, lp, pos_s, init_s, attn_s, l)
        st_p.append(sp)
        st_s.append(ss)
    (k_p, v_p, idxk_p, mC_p, mn_p, mm_p, rS_p, rsh_p, gS_p, gconv_p) = [jnp.stack(t) for t in zip(*st_p)]
    (k_s, v_s, idxk_s, mC_s, mn_s, mm_s, rS_s, rsh_s, gS_s, gconv_s) = [jnp.stack(t) for t in zip(*st_s)]
    y_prompt = _rms(hp, final_norm)
    y_sample = _rms(hs, final_norm)
    return (y_prompt, y_sample, k_p, v_p, idxk_p, mC_p, mn_p, mm_p, rS_p, rsh_p, gS_p, gconv_p,
            k_s, v_s, idxk_s, mC_s, mn_s, mm_s, rS_s, rsh_s, gS_s, gconv_s)
```

```python
import functools
import math

import jax
import jax.numpy as jnp
import numpy as np
from jax import lax
from jax.experimental import pallas as pl
from jax.experimental.pallas import tpu as pltpu

F32 = jnp.float32
BF16 = jnp.bfloat16
I32 = jnp.int32
SDS = jax.ShapeDtypeStruct

HD = 64
HEADS = 4
BR_W = HEADS * HD
N_BRANCH = 4
IDX_HEADS = 8
IDX_DIM = 64
TOPK_MAX = 256
ROPE_THETA = 10000.0
MLSTM_CHUNK = 64
GDN_CHUNK = 64
CONV_W = 4
RWKV_W_LORA = 64
RWKV_A_LORA = 64
RWKV_G_LORA = 128
RWKV_PROJ = 3 * BR_W + RWKV_W_LORA + RWKV_A_LORA + RWKV_G_LORA
GDN_CONV_CH = 3 * BR_W
EPS = 1e-6
LN_EPS = 1e-5
PAGE_SIZE = 128

LANES = 128
SUBLANES = 8
VMEM_LIMIT_BYTES = 58 * 1024 * 1024

SM_IK = 0
SM_IW = 64
SM_MI = 72
SM_MF = 76
SM_GB = 80
SM_GA = 84
SM_END = 88

PC_AQKV = 0
PC_IQ = 768
PC_SM = 1280
PC_M = 1408
PC_R = 2432
PC_G = 3456
PC_GATE = 4480
PC_END = 8576

HIGHEST = lax.Precision.HIGHEST
INT_MIN = -2147483648
NEG_INF = float("-inf")


def _cp(sem):
    return pltpu.CompilerParams(dimension_semantics=sem, vmem_limit_bytes=VMEM_LIMIT_BYTES)


def _resident(shape):
    return pl.BlockSpec(shape, lambda *_: (0,) * len(shape), pipeline_mode=pl.Buffered(1))


def _rows(tm, w, off_blocks=0):
    return pl.BlockSpec((tm, w), lambda i, *_: (i + off_blocks, 0))


def _dot(a, b):
    return jnp.dot(a, b, preferred_element_type=F32)


def _dot_nt(a, b, precision=None):
    return lax.dot_general(a, b, (((1,), (1,)), ((), ())), preferred_element_type=F32, precision=precision)


def _dot_tn(a, b, precision=None):
    return lax.dot_general(a, b, (((0,), (0,)), ((), ())), preferred_element_type=F32, precision=precision)


def _hdot(a, b):
    return jnp.dot(a, b, preferred_element_type=F32, precision=HIGHEST)


def _rms(x, g):
    return x * lax.rsqrt(jnp.mean(x * x, axis=-1, keepdims=True) + EPS) * g


def _iota(shape, dim):
    return lax.broadcasted_iota(I32, shape, dim)


def _head_masks(rows):
    lane = _iota((rows, BR_W), 1)
    return [(lane >= HD * h) & (lane < HD * (h + 1)) for h in range(HEADS)]


def _seg_sum(x, masks):
    out = jnp.zeros_like(x)
    for m in masks:
        s = jnp.sum(jnp.where(m, x, 0.0), axis=1, keepdims=True)
        out = jnp.where(m, s, out)
    return out


def _col_to_row(col, n):
    eye = _iota((n, n), 0) == _iota((n, n), 1)
    return jnp.sum(jnp.where(eye, jnp.broadcast_to(col, (n, n)), 0.0), axis=0, keepdims=True)


def _row_to_col(row, n):
    eye = _iota((n, n), 0) == _iota((n, n), 1)
    return jnp.sum(jnp.where(eye, jnp.broadcast_to(row, (n, n)), 0.0), axis=1, keepdims=True)


def _cumsum_col(col, n):
    r = _iota((n, n), 0)
    c = _iota((n, n), 1)
    row = jnp.sum(jnp.where(r <= c, jnp.broadcast_to(col, (n, n)), 0.0), axis=0, keepdims=True)
    return row, _row_to_col(row, n)


def _ffn_kernel(x_ref, g_ref, w1_ref, w3_ref, w2_ref, o_ref):
    x = x_ref[...]
    h = _rms(x, g_ref[...]).astype(BF16)
    a = _dot(h, w1_ref[...])
    b = _dot(h, w3_ref[...])
    y = (a * jax.nn.sigmoid(a) * b).astype(BF16)
    o_ref[...] = x + 0.5 * _dot(y, w2_ref[...])


def _ffn(x, g, w1, w3, w2):
    na, d = x.shape
    f = w1.shape[1]
    tm = math.gcd(na, 256)
    return pl.pallas_call(
        _ffn_kernel,
        out_shape=SDS((na, d), F32),
        grid=(na // tm,),
        in_specs=[_rows(tm, d), _resident((1, d)), _resident((d, f)), _resident((d, f)), _resident((f, d))],
        out_specs=_rows(tm, d),
        compiler_params=_cp(("parallel",)),
        name="ffn",
    )(x, g.reshape(1, d), w1.astype(BF16), w3.astype(BF16), w2.astype(BF16))


def _rms_kernel(x_ref, g_ref, o_ref):
    o_ref[...] = _rms(x_ref[...], g_ref[...])


def _final_rms(x, g):
    na, d = x.shape
    tm = math.gcd(na, 512)
    return pl.pallas_call(
        _rms_kernel,
        out_shape=SDS((na, d), F32),
        grid=(na // tm,),
        in_specs=[_rows(tm, d), _resident((1, d))],
        out_specs=_rows(tm, d),
        compiler_params=_cp(("parallel",)),
        name="final_rms",
    )(x, g.reshape(1, d))


def _rope_slab(z, cos, sa, sb):
    return z * cos + pltpu.roll(z, LANES - HD // 2, 1) * sa + pltpu.roll(z, HD // 2, 1) * sb


def _proj_kernel(x_ref, g_ref, w_ref, cos_ref, sa_ref, sb_ref, sp_ref,
                 aqkv_ref, iq_ref, sm_ref, m_ref, r_ref, gq_ref, gate_ref):
    h = _rms(x_ref[...], g_ref[...]).astype(BF16)
    cos, sa, sb = cos_ref[...], sa_ref[...], sb_ref[...]

    def mm(a, b):
        return _dot(h, w_ref[:, a:b])

    za = mm(PC_AQKV, PC_IQ)
    for s in range(4):
        aqkv_ref[:, s * LANES:(s + 1) * LANES] = _rope_slab(za[:, s * LANES:(s + 1) * LANES], cos, sa, sb)
    aqkv_ref[:, 2 * BR_W:3 * BR_W] = za[:, 2 * BR_W:3 * BR_W]
    zi = mm(PC_IQ, PC_SM)
    for s in range(4):
        iq_ref[:, s * LANES:(s + 1) * LANES] = _rope_slab(zi[:, s * LANES:(s + 1) * LANES], cos, sa, sb)

    zs = mm(PC_SM, PC_M)
    lane = _iota(zs.shape, 1)
    gn, bias, alog = sp_ref[0:1, :], sp_ref[1:2, :], sp_ref[2:3, :]
    ms = jnp.sum(jnp.where(lane < IDX_DIM, zs * zs, 0.0), axis=1, keepdims=True) * (1.0 / IDX_DIM)
    ik = _rope_slab(zs * lax.rsqrt(ms + EPS) * gn, cos, sa, sb)
    t = zs + bias
    out = jnp.where(lane < SM_IW, ik,
          jnp.where(lane < SM_MI, zs * (IDX_HEADS ** -0.5),
          jnp.where(lane < SM_MF, t,
          jnp.where(lane < SM_GB, jax.nn.log_sigmoid(t),
          jnp.where(lane < SM_GA, jax.nn.sigmoid(zs),
          jnp.where(lane < SM_END, -jnp.exp(alog) * jax.nn.softplus(t), 0.0))))))
    sm_ref[...] = out

    m_ref[...] = mm(PC_M, PC_R)
    r_ref[...] = mm(PC_R, PC_G)
    gq_ref[...] = mm(PC_G, PC_GATE)
    gate_ref[...] = mm(PC_GATE, PC_END)


def _proj(x, g, w_pad, cos, sa, sb, sp):
    na, d = x.shape
    tm = math.gcd(na, 256)
    widths = (3 * BR_W, IDX_HEADS * IDX_DIM, LANES, 4 * BR_W, RWKV_PROJ, 4 * BR_W, N_BRANCH * d)
    return pl.pallas_call(
        _proj_kernel,
        out_shape=tuple(SDS((na, w), F32) for w in widths),
        grid=(na // tm,),
        in_specs=[_rows(tm, d), _resident((1, d)), _resident((d, PC_END)),
                  _rows(tm, LANES), _rows(tm, LANES), _rows(tm, LANES), _resident((SUBLANES, LANES))],
        out_specs=tuple(_rows(tm, w) for w in widths),
        compiler_params=_cp(("parallel",)),
        name="proj_in",
    )(x, g.reshape(1, d), w_pad, cos, sa, sb, sp)


def _merge_kernel(x_ref, oa_ref, om_ref, or_ref, og_ref, gate_ref, wb_ref, wo_ref, o_ref):
    d = x_ref.shape[1]
    mix = None
    for n, br in enumerate((oa_ref, om_ref, or_ref, og_ref)):
        up = _dot(br[...].astype(BF16), wb_ref[n])
        t = jax.nn.sigmoid(gate_ref[:, n * d:(n + 1) * d]) * up
        mix = t if mix is None else mix + t
    o_ref[...] = x_ref[...] + _dot(mix.astype(BF16), wo_ref[...])


def _merge(x, oa, om, orr, og, gate, w_branch, w_out):
    na, d = x.shape
    tm = math.gcd(na, 256)
    return pl.pallas_call(
        _merge_kernel,
        out_shape=SDS((na, d), F32),
        grid=(na // tm,),
        in_specs=[_rows(tm, d)] + [_rows(tm, BR_W)] * 4 + [_rows(tm, N_BRANCH * d),
                  _resident((N_BRANCH, BR_W, d)), _resident((d, d))],
        out_specs=_rows(tm, d),
        compiler_params=_cp(("parallel",)),
        name="merge",
    )(x, oa, om, orr, og, gate, w_branch.astype(BF16), w_out.astype(BF16))


def _idx_queries(qi, sm, tq):
    qrows = jnp.concatenate([qi[:, h * IDX_DIM:(h + 1) * IDX_DIM] for h in range(IDX_HEADS)], axis=0).astype(BF16)
    wb = [jnp.broadcast_to(sm[:, SM_IW + h:SM_IW + h + 1] * (IDX_DIM ** -0.5), (tq, LANES)) for h in range(IDX_HEADS)]
    return qrows, wb


def _idx_scores(qrows, wb, kib, tq, tk):
    dots = _dot_nt(qrows, kib)
    cols = []
    for j in range(tk // LANES):
        acc = None
        for h in range(IDX_HEADS):
            t = jnp.maximum(dots[h * tq:(h + 1) * tq, j * LANES:(j + 1) * LANES], 0.0) * wb[h]
            acc = t if acc is None else acc + t
        cols.append(acc)
    return cols[0] if len(cols) == 1 else jnp.concatenate(cols, axis=1)


def _to_key(sc):
    b = pltpu.bitcast(jnp.where(sc == 0.0, 0.0, sc), I32)
    return b ^ ((b >> 31) & I32(0x7FFFFFFF))


def _key_to_f32(key):
    return pltpu.bitcast(key ^ ((key >> 31) & I32(0x7FFFFFFF)), F32)


def _kth_largest(count, rows, n_sel, pos_bits, thr_ref, cut_ref):
    int_min = I32(INT_MIN)

    def bit_step(it, ub):
        cand = ub | lax.shift_left(I32(1), 31 - it)
        cs = cand ^ int_min
        return jnp.where(count(lambda k, p: k >= cs) >= n_sel, cand, ub)

    ub = lax.fori_loop(0, 32, bit_step, jnp.zeros((rows, LANES), I32))
    tkey = ub ^ int_min
    n_gt = count(lambda k, p: k > tkey)
    n_ge = count(lambda k, p: k >= tkey)
    need = n_sel - n_gt
    ambiguous = (n_ge - n_gt) > need
    thr_ref[...] = _key_to_f32(tkey)
    cut_ref[...] = jnp.full((rows, LANES), 1 << pos_bits, I32)

    @pl.when(jnp.max(jnp.where(ambiguous, 1.0, 0.0)) > 0.0)
    def _():
        def pos_step(it, q):
            cand = q | lax.shift_left(I32(1), pos_bits - 1 - it)
            below = count(lambda k, p: (k == tkey) & (p < cand))
            return jnp.where(below < need, cand, q)

        q = lax.fori_loop(0, pos_bits, pos_step, jnp.zeros((rows, LANES), I32))
        cut_ref[...] = jnp.where(ambiguous, q + 1, 1 << pos_bits)


def _dsa_thresh_kernel(qi_ref, smq_ref, ki_ref, thr_ref, cut_ref, keys_s, *, tq, tk, n_sel, pos_bits):
    i = pl.program_id(0)
    q0 = i * tq
    nch = (q0 + tq - 1) // tk + 1
    qrows, wb = _idx_queries(qi_ref[...], smq_ref[...], tq)
    qpos = q0 + _iota((tq, tk), 0)

    def fill(c, carry):
        k0 = pl.multiple_of(c * tk, tk)
        sc = _idx_scores(qrows, wb, ki_ref[pl.ds(k0, tk), :], tq, tk)
        sc = jnp.where(k0 + _iota((tq, tk), 1) <= qpos, sc, NEG_INF)
        keys_s[c] = _to_key(sc)
        return carry

    lax.fori_loop(0, nch, fill, 0)
    lane = _iota((tq, LANES), 1)

    def count(pred):
        def body(c, acc):
            for j in range(tk // LANES):
                p = c * tk + j * LANES + lane
                acc = acc + jnp.where(pred(keys_s[c, :, j * LANES:(j + 1) * LANES], p), 1.0, 0.0)
            return acc

        acc = lax.fori_loop(0, nch, body, jnp.zeros((tq, LANES), F32))
        return jnp.broadcast_to(jnp.sum(acc, axis=1, keepdims=True), (tq, LANES))

    _kth_largest(count, tq, n_sel, pos_bits, thr_ref, cut_ref)


def _flash_update(s, v, m_s, l_s, acc_s):
    m_prev = m_s[...]
    m_new = jnp.maximum(m_prev, jnp.max(s, axis=1, keepdims=True))
    m_safe = jnp.where(m_new == NEG_INF, 0.0, m_new)
    alpha = jnp.exp(m_prev - m_safe)
    p = jnp.exp(s - m_safe)
    l_s[...] = alpha * l_s[...] + jnp.sum(p, axis=1, keepdims=True)
    m_s[...] = m_new
    acc_s[...] = alpha * acc_s[...] + _dot(p.astype(BF16), v)


def _stack_heads(q, t):
    masks = _head_masks(t)
    return jnp.concatenate([jnp.where(m, q, 0.0) for m in masks], axis=0).astype(BF16)


def _unstack_heads(acc, l, t):
    masks = _head_masks(t)
    out = jnp.zeros((t, BR_W), F32)
    for h, m in enumerate(masks):
        out = jnp.where(m, acc[h * t:(h + 1) * t] / l[h * t:(h + 1) * t], out)
    return out


def _dsa_attn_kernel(qi_ref, smq_ref, q_ref, thr_ref, cut_ref, ki_ref, k_ref, v_ref, o_ref,
                     m_s, l_s, acc_s, *, tq, tk):
    i = pl.program_id(0)
    q0 = i * tq
    nch = (q0 + tq - 1) // tk + 1
    qrows, wb = _idx_queries(qi_ref[...], smq_ref[...], tq)
    thr = thr_ref[:, 0:1]
    cut = cut_ref[:, 0:1]
    qbd = _stack_heads(q_ref[:, 0:BR_W], tq)
    m_s[...] = jnp.full(m_s.shape, NEG_INF, F32)
    l_s[...] = jnp.zeros(l_s.shape, F32)
    acc_s[...] = jnp.zeros(acc_s.shape, F32)
    qpos = q0 + _iota((tq, tk), 0)

    def body(c, carry):
        k0 = pl.multiple_of(c * tk, tk)
        sc = _idx_scores(qrows, wb, ki_ref[pl.ds(k0, tk), :], tq, tk)
        kidx = k0 + _iota((tq, tk), 1)
        causal = kidx <= qpos
        sc = jnp.where(causal, sc, NEG_INF)
        sel = ((sc > thr) | ((sc == thr) & (kidx < cut))) & causal
        bias = jnp.where(sel, 0.0, NEG_INF)
        s = _dot_nt(qbd, k_ref[pl.ds(k0, tk), :]) * (HD ** -0.5) + jnp.concatenate([bias] * HEADS, axis=0)
        _flash_update(s, v_ref[pl.ds(k0, tk), :], m_s, l_s, acc_s)
        return carry

    lax.fori_loop(0, nch, body, 0)
    o_ref[...] = _unstack_heads(acc_s[...], l_s[...], tq)


def _dsa_prompt(iq, sm, aqkv, lp):
    tq = min(128, lp)
    tk = min(512, lp)
    n_sel = min(TOPK_MAX, lp // 4)
    pos_bits = int(lp).bit_length()
    ki = sm[:lp, SM_IK:SM_IK + IDX_DIM].astype(BF16)
    kb = aqkv[:lp, BR_W:2 * BR_W].astype(BF16)
    vb = aqkv[:lp, 2 * BR_W:3 * BR_W].astype(BF16)
    nq = lp // tq
    thr, cut = pl.pallas_call(
        functools.partial(_dsa_thresh_kernel, tq=tq, tk=tk, n_sel=n_sel, pos_bits=pos_bits),
        out_shape=(SDS((lp, LANES), F32), SDS((lp, LANES), I32)),
        grid=(nq,),
        in_specs=[_rows(tq, IDX_HEADS * IDX_DIM), _rows(tq, LANES), _resident((lp, IDX_DIM))],
        out_specs=(_rows(tq, LANES), _rows(tq, LANES)),
        scratch_shapes=[pltpu.VMEM((lp // tk, tq, tk), I32)],
        compiler_params=_cp(("arbitrary",)),
        name="dsa_prompt_threshold",
    )(iq, sm, ki)
    return pl.pallas_call(
        functools.partial(_dsa_attn_kernel, tq=tq, tk=tk),
        out_shape=SDS((lp, BR_W), F32),
        grid=(nq,),
        in_specs=[_rows(tq, IDX_HEADS * IDX_DIM), _rows(tq, LANES), _rows(tq, 3 * BR_W),
                  _rows(tq, LANES), _rows(tq, LANES),
                  _resident((lp, IDX_DIM)), _resident((lp, BR_W)), _resident((lp, BR_W))],
        out_specs=_rows(tq, BR_W),
        scratch_shapes=[pltpu.VMEM((HEADS * tq, 1), F32), pltpu.VMEM((HEADS * tq, 1), F32),
                        pltpu.VMEM((HEADS * tq, BR_W), F32)],
        compiler_params=_cp(("arbitrary",)),
        name="dsa_prompt_attention",
    )(iq, sm, aqkv, thr, cut, ki, kb, vb)


def _dsa_sample_score_kernel(pt_ref, qr_ref, wb_ref, *refs, t, pg, ng, past, n_sel, pos_bits):
    pages = refs[:pg]
    knew_ref, scp_ref, scn_ref, thr_ref, cut_ref, keys_s, keyn_s = refs[pg:]
    j = pl.program_id(1)
    qrows = qr_ref[...]
    wb = wb_ref[...] * (IDX_DIM ** -0.5)

    def scores(kib):
        w = jnp.maximum(_dot_nt(qrows, kib), 0.0) * wb
        s = w[0:t]
        for h in range(1, IDX_HEADS):
            s = s + w[h * t:(h + 1) * t]
        return s

    sc = jnp.concatenate([scores(p[...].astype(BF16)) for p in pages], axis=1)
    scp_ref[...] = sc
    keys_s[j] = _to_key(sc)

    @pl.when(j == ng - 1)
    def _():
        lane = _iota((t, LANES), 1)
        sn = jnp.where(lane <= _iota((t, LANES), 0), scores(knew_ref[...].astype(BF16)), NEG_INF)
        scn_ref[...] = sn
        keyn_s[...] = _to_key(sn)

        def count(pred):
            def body(c, acc):
                for jj in range(pg):
                    p = c * (pg * LANES) + jj * LANES + lane
                    acc = acc + jnp.where(pred(keys_s[c, :, jj * LANES:(jj + 1) * LANES], p), 1.0, 0.0)
                return acc

            acc = lax.fori_loop(0, ng, body, jnp.zeros((t, LANES), F32))
            acc = acc + jnp.where(pred(keyn_s[...], past + lane), 1.0, 0.0)
            return jnp.broadcast_to(jnp.sum(acc, axis=1, keepdims=True), (t, LANES))

        _kth_largest(count, t, n_sel, pos_bits, thr_ref, cut_ref)


def _dsa_sample_attn_kernel(pt_ref, q_ref, *refs, t, pg, ng, past):
    kpages = refs[:pg]
    vpages = refs[pg:2 * pg]
    (scp_ref, scn_ref, thr_ref, cut_ref, knew_ref, vnew_ref, o_ref, m_s, l_s, acc_s) = refs[2 * pg:]
    j = pl.program_id(1)

    @pl.when(j == 0)
    def _():
        m_s[...] = jnp.full(m_s.shape, NEG_INF, F32)
        l_s[...] = jnp.zeros(l_s.shape, F32)
        acc_s[...] = jnp.zeros(acc_s.shape, F32)

    qbd = _stack_heads(q_ref[...], t)
    thr = thr_ref[:, 0:1]
    cut = cut_ref[:, 0:1]
    w = pg * LANES
    sc = scp_ref[...]
    kidx = j * w + _iota((t, w), 1)
    bias = jnp.where((sc > thr) | ((sc == thr) & (kidx < cut)), 0.0, NEG_INF)
    kcat = jnp.concatenate([p[...].astype(BF16) for p in kpages], axis=0)
    vcat = jnp.concatenate([p[...].astype(BF16) for p in vpages], axis=0)
    s = _dot_nt(qbd, kcat) * (HD ** -0.5) + jnp.concatenate([bias] * HEADS, axis=0)
    _flash_update(s, vcat, m_s, l_s, acc_s)

    @pl.when(j == ng - 1)
    def _():
        lane = _iota((t, LANES), 1)
        scn = scn_ref[...]
        seln = ((scn > thr) | ((scn == thr) & (past + lane < cut))) & (lane <= _iota((t, LANES), 0))
        biasn = jnp.where(seln, 0.0, NEG_INF)
        sn = _dot_nt(qbd, knew_ref[...].astype(BF16)) * (HD ** -0.5) + jnp.concatenate([biasn] * HEADS, axis=0)
        _flash_update(sn, vnew_ref[...].astype(BF16), m_s, l_s, acc_s)
        o_ref[...] = _unstack_heads(acc_s[...], l_s[...], t)


def _dsa_sample(iq_s, sm_s, aqkv_s, cache_k, cache_v, cache_idx_k, page_table, layer, db, t):
    n_pages = page_table.shape[1]
    past = n_pages * PAGE_SIZE
    n_sel = min(TOPK_MAX, (past + t) // 4)
    pos_bits = int(past + LANES).bit_length()
    pg = math.gcd(n_pages, 8)
    ng = n_pages // pg
    depth, n_pool = cache_k.shape[:2]
    ck = cache_k.reshape(depth, n_pool, PAGE_SIZE, BR_W)
    cv = cache_v.reshape(depth, n_pool, PAGE_SIZE, BR_W)

    qrows = iq_s.reshape(db, t, IDX_HEADS, IDX_DIM).transpose(0, 2, 1, 3).reshape(db, IDX_HEADS * t, IDX_DIM).astype(BF16)
    wcol = sm_s[:, SM_IW:SM_IW + IDX_HEADS].reshape(db, t, IDX_HEADS).transpose(0, 2, 1).reshape(db, IDX_HEADS * t, 1)
    wbs = jnp.broadcast_to(wcol, (db, IDX_HEADS * t, LANES))
    pad = lambda a: jnp.pad(a.reshape(db, t, a.shape[-1]), ((0, 0), (0, LANES - t), (0, 0)))
    ki_new = pad(sm_s[:, SM_IK:SM_IK + IDX_DIM])
    k_new = pad(aqkv_s[:, BR_W:2 * BR_W])
    v_new = pad(aqkv_s[:, 2 * BR_W:3 * BR_W])
    q3 = aqkv_s[:, 0:BR_W].reshape(db, t, BR_W)

    def page_spec(width, i):
        return pl.BlockSpec((None, None, PAGE_SIZE, width), lambda b, j, pt: (layer, pt[b, j * pg + i], 0, 0))

    per_b = lambda r, w: pl.BlockSpec((None, r, w), lambda b, j, pt: (b, 0, 0))
    sc_spec = pl.BlockSpec((None, t, pg * LANES), lambda b, j, pt: (b, 0, j))

    scp, scn, thr, cut = pl.pallas_call(
        functools.partial(_dsa_sample_score_kernel, t=t, pg=pg, ng=ng, past=past, n_sel=n_sel, pos_bits=pos_bits),
        out_shape=(SDS((db, t, past), F32), SDS((db, t, LANES), F32), SDS((db, t, LANES), F32), SDS((db, t, LANES), I32)),
        grid_spec=pltpu.PrefetchScalarGridSpec(
            num_scalar_prefetch=1, grid=(db, ng),
            in_specs=[per_b(IDX_HEADS * t, IDX_DIM), per_b(IDX_HEADS * t, LANES)]
                     + [page_spec(IDX_DIM, i) for i in range(pg)] + [per_b(LANES, IDX_DIM)],
            out_specs=(sc_spec, per_b(t, LANES), per_b(t, LANES), per_b(t, LANES)),
            scratch_shapes=[pltpu.VMEM((ng, t, pg * LANES), I32), pltpu.VMEM((t, LANES), I32)]),
        compiler_params=_cp(("arbitrary", "arbitrary")),
        name="dsa_sample_scores",
    )(page_table, qrows, wbs, *([cache_idx_k] * pg), ki_new)

    out = pl.pallas_call(
        functools.partial(_dsa_sample_attn_kernel, t=t, pg=pg, ng=ng, past=past),
        out_shape=SDS((db, t, BR_W), F32),
        grid_spec=pltpu.PrefetchScalarGridSpec(
            num_scalar_prefetch=1, grid=(db, ng),
            in_specs=[per_b(t, BR_W)] + [page_spec(BR_W, i) for i in range(pg)] * 2
                     + [sc_spec, per_b(t, LANES), per_b(t, LANES), per_b(t, LANES), per_b(LANES, BR_W), per_b(LANES, BR_W)],
            out_specs=per_b(t, BR_W),
            scratch_shapes=[pltpu.VMEM((HEADS * t, 1), F32), pltpu.VMEM((HEADS * t, 1), F32),
                            pltpu.VMEM((HEADS * t, BR_W), F32)]),
        compiler_params=_cp(("arbitrary", "arbitrary")),
        name="dsa_sample_attention",
    )(page_table, q3, *([ck] * pg), *([cv] * pg), scp, scn, thr, cut, k_new, v_new)
    return out.reshape(db * t, BR_W)


def _mlstm_head(q, k, v, li, lf, cst, n, m, c):
    r = _iota((c, c), 0)
    cc = _iota((c, c), 1)
    b_row, b_col = _cumsum_col(lf, c)
    li_row = _col_to_row(li, c)
    log_d = jnp.where(cc <= r, b_col - b_row + li_row, NEG_INF)
    inter = b_col + m
    mt = jnp.maximum(inter, jnp.max(log_d, axis=1, keepdims=True))
    s = _dot_nt(q, k, HIGHEST) * jnp.exp(log_d - mt)
    ei = jnp.exp(inter - mt)
    num = _hdot(s, v) + ei * _dot_nt(q, cst, HIGHEST)
    den = jnp.sum(s, axis=1, keepdims=True) + ei * jnp.sum(q * n, axis=1, keepdims=True)
    h = num / jnp.maximum(jnp.abs(den), jnp.exp(-mt))
    bc = b_col[c - 1:c, :]
    gs = bc - b_col + li
    mn = jnp.maximum(bc + m, jnp.max(gs, axis=0, keepdims=True))
    ws = jnp.exp(gs - mn)
    dec = jnp.exp(bc + m - mn)
    c_new = dec * cst + _dot_tn(ws * v, k, HIGHEST)
    n_new = dec * n + jnp.sum(ws * k, axis=0, keepdims=True)
    return h, c_new, n_new, mn


def _mlstm_kernel(x_ref, sm_ref, c0_ref, n0_ref, m0_ref, lng_ref, o_ref, c1_ref, n1_ref, m1_ref,
                  c_s, n_s, m_s, *, c, cb):
    i = pl.program_id(1)

    @pl.when(i == 0)
    def _():
        c_s[...] = c0_ref[...]
        n_s[...] = n0_ref[...]
        m_s[...] = m0_ref[...]

    def chunk(ci, carry):
        r0 = pl.multiple_of(ci * c, c)
        outs = []
        for h in range(HEADS):
            lo = HD * h
            q = x_ref[pl.ds(r0, c), lo:lo + HD]
            k = x_ref[pl.ds(r0, c), BR_W + lo:BR_W + lo + HD] * (HD ** -0.5)
            v = x_ref[pl.ds(r0, c), 2 * BR_W + lo:2 * BR_W + lo + HD]
            og = x_ref[pl.ds(r0, c), 3 * BR_W + lo:3 * BR_W + lo + HD]
            li = sm_ref[pl.ds(r0, c), SM_MI + h:SM_MI + h + 1]
            lf = sm_ref[pl.ds(r0, c), SM_MF + h:SM_MF + h + 1]
            hh, c_new, n_new, m_new = _mlstm_head(q, k, v, li, lf, c_s[h], n_s[h], m_s[h], c)
            c_s[h] = c_new
            n_s[h] = n_new
            m_s[h] = m_new
            mu = jnp.mean(hh, axis=1, keepdims=True)
            var = jnp.mean(jnp.square(hh - mu), axis=1, keepdims=True)
            y = (hh - mu) * lax.rsqrt(var + LN_EPS) * lng_ref[:, lo:lo + HD]
            outs.append(y * jax.nn.sigmoid(og))
        o_ref[pl.ds(r0, c), :] = jnp.concatenate(outs, axis=1)
        return carry

    lax.fori_loop(0, cb, chunk, 0)

    @pl.when(i == pl.num_programs(1) - 1)
    def _():
        c1_ref[...] = c_s[...]
        n1_ref[...] = n_s[...]
        m1_ref[...] = m_s[...]


def _seq_blocks(l, chunk, max_rows):
    nc = l // chunk
    cb = math.gcd(nc, max(1, max_rows // chunk))
    return chunk * cb, cb, nc // cb


def _state_spec(shape):
    return pl.BlockSpec((None,) + shape, lambda b, i: (b,) + (0,) * len(shape))


def _mlstm(mslab, sm, row_off, b, l, c0, n0, m0, ln_g):
    c = math.gcd(l, MLSTM_CHUNK)
    rows, cb, nblk = _seq_blocks(l, c, 512)
    off = row_off // rows
    seq = lambda w: pl.BlockSpec((rows, w), lambda bb, i: (off + bb * nblk + i, 0))
    o, c1, n1, m1 = pl.pallas_call(
        functools.partial(_mlstm_kernel, c=c, cb=cb),
        out_shape=(SDS((b * l, BR_W), F32), SDS((b, HEADS, HD, HD), F32), SDS((b, HEADS, 1, HD), F32),
                   SDS((b, HEADS, 1, 1), F32)),
        grid=(b, nblk),
        in_specs=[seq(4 * BR_W), seq(LANES), _state_spec((HEADS, HD, HD)), _state_spec((HEADS, 1, HD)),
                  _state_spec((HEADS, 1, 1)), pl.BlockSpec((1, BR_W), lambda bb, i: (0, 0))],
        out_specs=(pl.BlockSpec((rows, BR_W), lambda bb, i: (bb * nblk + i, 0)), _state_spec((HEADS, HD, HD)),
                   _state_spec((HEADS, 1, HD)), _state_spec((HEADS, 1, 1))),
        scratch_shapes=[pltpu.VMEM((HEADS, HD, HD), F32), pltpu.VMEM((HEADS, 1, HD), F32), pltpu.VMEM((HEADS, 1, 1), F32)],
        compiler_params=_cp(("parallel", "arbitrary")),
        name="mlstm",
    )(mslab, sm, c0, n0.reshape(b, HEADS, 1, HD), m0.reshape(b, HEADS, 1, 1), ln_g.reshape(1, BR_W))
    return o, c1, n1.reshape(b, HEADS, HD), m1.reshape(b, HEADS)


def _shift_rows(x, halo, s):
    rows = x.shape[0]
    rolled = pltpu.roll(x, s, 0)
    first = jnp.where(_iota((SUBLANES, x.shape[1]), 0) < s, pltpu.roll(halo, s, 0), rolled[0:SUBLANES])
    return first if rows == SUBLANES else jnp.concatenate([first, rolled[SUBLANES:]], axis=0)


def _halo_specs(rows, w, off_rows, nt):
    hb = rows // SUBLANES
    tile = pl.BlockSpec((rows, w), lambda b, i: (off_rows // rows + b * nt + i, 0))
    halo = pl.BlockSpec((SUBLANES, w), lambda b, i: (jnp.maximum(off_rows // SUBLANES + (b * nt + i) * hb - 1, 0), 0))
    state = pl.BlockSpec((None, SUBLANES, w), lambda b, i: (b, 0, 0))
    return tile, halo, state


def _pad_state_rows(st, w):
    b, r, w0 = st.shape
    return jnp.pad(st, ((0, 0), (SUBLANES - r, 0), (0, w - w0)))


def _rwkv_prep_kernel(p_ref, halo_ref, st_ref, vec_ref, w2_ref, a2_ref, g2_ref, rs_ref, post_ref):
    i = pl.program_id(1)
    p = p_ref[...]
    rows = p.shape[0]
    halo = jnp.where(i == 0, st_ref[...], halo_ref[...])
    prev = _shift_rows(p, halo, 1)
    mu = jnp.concatenate([vec_ref[0:1, :], vec_ref[1:2, :], vec_ref[2:3, :], vec_ref[3:4, :]], axis=1)
    pm = p + (prev - p) * mu
    r = pm[:, 0:BR_W]
    k = pm[:, BR_W:2 * BR_W]
    v = pm[:, 2 * BR_W:3 * BR_W]
    o = 3 * BR_W
    wd = pm[:, o:o + RWKV_W_LORA]
    ad = pm[:, o + RWKV_W_LORA:o + RWKV_W_LORA + RWKV_A_LORA]
    gd = pm[:, o + RWKV_W_LORA + RWKV_A_LORA:RWKV_PROJ]
    w0, a0, kkp, ka, rk = (vec_ref[4:5, :], vec_ref[5:6, :], vec_ref[6:7, :], vec_ref[7:8, :], vec_ref[8:9, :])
    wlog = -jax.nn.softplus(-(w0 + _hdot(jnp.tanh(wd), w2_ref[...]))) - 0.5
    decay = jnp.exp(-jnp.exp(wlog))
    a = jax.nn.sigmoid(a0 + _hdot(ad, a2_ref[...]))
    g = _hdot(jax.nn.sigmoid(gd), g2_ref[...])
    masks = _head_masks(rows)
    kkr = k * kkp
    kk = kkr * lax.rsqrt(jnp.maximum(_seg_sum(kkr * kkr, masks), 1e-12))
    k2 = k * (1.0 + (a - 1.0) * ka)
    bonus = _seg_sum(r * k2 * rk, masks) * v
    for n, t in enumerate((r, decay, k2, v, kk, kk * a)):
        rs_ref[:, n * BR_W:(n + 1) * BR_W] = t
    post_ref[:, 0:BR_W] = bonus
    post_ref[:, BR_W:2 * BR_W] = g


def _rwkv_scan_kernel(rs_ref, post_ref, s0_ref, ln_ref, o_ref, s1_ref, s_s, y_s, *, tb):
    i = pl.program_id(1)

    @pl.when(i == 0)
    def _():
        s_s[...] = s0_ref[...]

    eye = _iota((HD, HD), 0) == _iota((HD, HD), 1)

    def step(t, st):
        row = rs_ref[pl.ds(t, 1), :]
        new, ys = [], []
        for h in range(HEADS):
            lo = HD * h
            r, w, k, v, kk, kka = (row[:, n * BR_W + lo:n * BR_W + lo + HD] for n in range(6))
            s = st[h]
            sa = -jnp.sum(s * kk, axis=1, keepdims=True)
            v_col = jnp.sum(jnp.where(eye, jnp.broadcast_to(v, (HD, HD)), 0.0), axis=1, keepdims=True)
            s = s * w + sa * kka + v_col * k
            y_col = jnp.sum(s * r, axis=1, keepdims=True)
            ys.append(jnp.sum(jnp.where(eye, jnp.broadcast_to(y_col, (HD, HD)), 0.0), axis=0, keepdims=True))
            new.append(s)
        y_s[pl.ds(t, 1), :] = jnp.concatenate(ys, axis=1)
        return tuple(new)

    st = lax.fori_loop(0, tb, step, tuple(s_s[h] for h in range(HEADS)))
    for h in range(HEADS):
        s_s[h] = st[h]

    y = y_s[...]
    masks = _head_masks(tb)
    mu = _seg_sum(y, masks) * (1.0 / HD)
    var = _seg_sum(jnp.square(y - mu), masks) * (1.0 / HD)
    y = (y - mu) * lax.rsqrt(var + LN_EPS) * ln_ref[0:1, :] + ln_ref[1:2, :]
    o_ref[...] = (y + post_ref[:, 0:BR_W]) * post_ref[:, BR_W:2 * BR_W]

    @pl.when(i == pl.num_programs(1) - 1)
    def _():
        s1_ref[...] = s_s[...]


def _rwkv(rslab, row_off, b, l, sh0, s0, lp):
    rows = math.gcd(l, 256)
    nt = l // rows
    tile, halo, state = _halo_specs(rows, RWKV_PROJ, row_off, nt)
    quarter = lambda a: a.reshape(4, BR_W)
    vec = jnp.concatenate([quarter(lp["rw_mu"]), lp["rw_w0"][None], lp["rw_a0"][None], lp["rw_kk"][None],
                           lp["rw_ka"][None], lp["rw_rk"].reshape(1, BR_W), jnp.zeros((7, BR_W), F32)], axis=0)
    rs, post = pl.pallas_call(
        _rwkv_prep_kernel,
        out_shape=(SDS((b * l, 6 * BR_W), F32), SDS((b * l, 2 * BR_W), F32)),
        grid=(b, nt),
        in_specs=[tile, halo, state, pl.BlockSpec((16, BR_W), lambda bb, i: (0, 0)),
                  pl.BlockSpec((RWKV_W_LORA, BR_W), lambda bb, i: (0, 0)),
                  pl.BlockSpec((RWKV_A_LORA, BR_W), lambda bb, i: (0, 0)),
                  pl.BlockSpec((RWKV_G_LORA, BR_W), lambda bb, i: (0, 0))],
        out_specs=(pl.BlockSpec((rows, 6 * BR_W), lambda bb, i: (bb * nt + i, 0)),
                   pl.BlockSpec((rows, 2 * BR_W), lambda bb, i: (bb * nt + i, 0))),
        compiler_params=_cp(("parallel", "parallel")),
        name="rwkv_prep",
    )(rslab, rslab, _pad_state_rows(sh0[:, None, :], RWKV_PROJ), vec, lp["rw_w2"], lp["rw_a2"], lp["rw_g2"])

    tb = math.gcd(l, 128)
    nb = l // tb
    ln = jnp.stack([lp["rw_ln_g"], lp["rw_ln_b"]] + [jnp.zeros((BR_W,), F32)] * 6)
    o, s1 = pl.pallas_call(
        functools.partial(_rwkv_scan_kernel, tb=tb),
        out_shape=(SDS((b * l, BR_W), F32), SDS((b, HEADS, HD, HD), F32)),
        grid=(b, nb),
        in_specs=[pl.BlockSpec((tb, 6 * BR_W), lambda bb, i: (bb * nb + i, 0)),
                  pl.BlockSpec((tb, 2 * BR_W), lambda bb, i: (bb * nb + i, 0)),
                  _state_spec((HEADS, HD, HD)), pl.BlockSpec((SUBLANES, BR_W), lambda bb, i: (0, 0))],
        out_specs=(pl.BlockSpec((tb, BR_W), lambda bb, i: (bb * nb + i, 0)), _state_spec((HEADS, HD, HD))),
        scratch_shapes=[pltpu.VMEM((HEADS, HD, HD), F32), pltpu.VMEM((tb, BR_W), F32)],
        compiler_params=_cp(("parallel", "arbitrary")),
        name="rwkv_scan",
    )(rs, post, s0, ln)
    return o, s1


def _gdn_prep_kernel(x_ref, halo_ref, st_ref, taps_ref, o_ref):
    i = pl.program_id(1)
    x = x_ref[:, 0:GDN_CONV_CH]
    rows = x.shape[0]
    halo = jnp.where(i == 0, st_ref[:, 0:GDN_CONV_CH], halo_ref[:, 0:GDN_CONV_CH])
    conv = x * taps_ref[CONV_W - 1:CONV_W, :]
    for s in range(1, CONV_W):
        conv = conv + _shift_rows(x, halo, s) * taps_ref[CONV_W - 1 - s:CONV_W - s, :]
    conv = conv * jax.nn.sigmoid(conv)
    masks = _head_masks(rows)
    for n in range(3):
        t = conv[:, n * BR_W:(n + 1) * BR_W]
        if n < 2:
            t = t * lax.rsqrt(jnp.maximum(_seg_sum(t * t, masks), 1e-12))
        if n == 0:
            t = t * (HD ** -0.5)
        o_ref[:, n * BR_W:(n + 1) * BR_W] = t


def _gdn_head(q, k, v, beta, g, st, c):
    r = _iota((c, c), 0)
    cc = _iota((c, c), 1)
    gam_row, gam_col = _cumsum_col(g, c)
    dec = jnp.exp(jnp.where(cc <= r, gam_col - gam_row, NEG_INF))
    eg = jnp.exp(gam_col)
    a = beta * _dot_nt(k, k, HIGHEST) * jnp.where(cc < r, dec, 0.0)
    res = beta * (v - eg * _dot_nt(k, st, HIGHEST))
    for s in range(c - 1):
        res = res - a[:, s:s + 1] * res[s:s + 1, :]
    o = eg * _dot_nt(q, st, HIGHEST) + _hdot(_dot_nt(q, k, HIGHEST) * dec, res)
    gl = gam_col[c - 1:c, :]
    st_new = jnp.exp(gl) * st + _dot_tn(jnp.exp(gl - gam_col) * res, k, HIGHEST)
    return o, st_new


def _gdn_kernel(x_ref, z_ref, sm_ref, s0_ref, gn_ref, o_ref, s1_ref, s_s, *, c, cb):
    i = pl.program_id(1)

    @pl.when(i == 0)
    def _():
        s_s[...] = s0_ref[...]

    def chunk(ci, carry):
        r0 = pl.multiple_of(ci * c, c)
        outs = []
        for h in range(HEADS):
            lo = HD * h
            q = x_ref[pl.ds(r0, c), lo:lo + HD]
            k = x_ref[pl.ds(r0, c), BR_W + lo:BR_W + lo + HD]
            v = x_ref[pl.ds(r0, c), 2 * BR_W + lo:2 * BR_W + lo + HD]
            z = z_ref[pl.ds(r0, c), 3 * BR_W + lo:3 * BR_W + lo + HD]
            beta = sm_ref[pl.ds(r0, c), SM_GB + h:SM_GB + h + 1]
            g = sm_ref[pl.ds(r0, c), SM_GA + h:SM_GA + h + 1]
            o, st_new = _gdn_head(q, k, v, beta, g, s_s[h], c)
            s_s[h] = st_new
            o = o * lax.rsqrt(jnp.mean(o * o, axis=1, keepdims=True) + EPS) * gn_ref[...]
            outs.append(o * (z * jax.nn.sigmoid(z)))
        o_ref[pl.ds(r0, c), :] = jnp.concatenate(outs, axis=1)
        return carry

    lax.fori_loop(0, cb, chunk, 0)

    @pl.when(i == pl.num_programs(1) - 1)
    def _():
        s1_ref[...] = s_s[...]


def _gdn(gslab, sm, row_off, b, l, cv0, s0, lp):
    rows = math.gcd(l, 256)
    nt = l // rows
    w = 4 * BR_W
    tile, halo, state = _halo_specs(rows, w, row_off, nt)
    taps = jnp.pad(lp["gd_conv"], ((0, SUBLANES - CONV_W), (0, 0)))
    qkv = pl.pallas_call(
        _gdn_prep_kernel,
        out_shape=SDS((b * l, GDN_CONV_CH), F32),
        grid=(b, nt),
        in_specs=[tile, halo, state, pl.BlockSpec((SUBLANES, GDN_CONV_CH), lambda bb, i: (0, 0))],
        out_specs=pl.BlockSpec((rows, GDN_CONV_CH), lambda bb, i: (bb * nt + i, 0)),
        compiler_params=_cp(("parallel", "parallel")),
        name="gdn_prep",
    )(gslab, gslab, _pad_state_rows(cv0, w), taps)

    c = math.gcd(l, GDN_CHUNK)
    crow, cb, nblk = _seq_blocks(l, c, 512)
    off = row_off // crow
    o, s1 = pl.pallas_call(
        functools.partial(_gdn_kernel, c=c, cb=cb),
        out_shape=(SDS((b * l, BR_W), F32), SDS((b, HEADS, HD, HD), F32)),
        grid=(b, nblk),
        in_specs=[pl.BlockSpec((crow, GDN_CONV_CH), lambda bb, i: (bb * nblk + i, 0)),
                  pl.BlockSpec((crow, w), lambda bb, i: (off + bb * nblk + i, 0)),
                  pl.BlockSpec((crow, LANES), lambda bb, i: (off + bb * nblk + i, 0)),
                  _state_spec((HEADS, HD, HD)), pl.BlockSpec((1, HD), lambda bb, i: (0, 0))],
        out_specs=(pl.BlockSpec((crow, BR_W), lambda bb, i: (bb * nblk + i, 0)), _state_spec((HEADS, HD, HD))),
        scratch_shapes=[pltpu.VMEM((HEADS, HD, HD), F32)],
        compiler_params=_cp(("parallel", "arbitrary")),
        name="gdn",
    )(qkv, gslab, sm, s0, lp["gd_norm"].reshape(1, HD))
    return o, s1


def _rope_tables(pos):
    half = HD // 2
    inv = ROPE_THETA ** (-jnp.arange(half, dtype=F32) / half)
    ang = pos.astype(F32)[:, None] * inv[None, :]
    cos, sin = jnp.cos(ang), jnp.sin(ang)
    zero = jnp.zeros_like(sin)
    tile = lambda a, b: jnp.concatenate([a, b, a, b], axis=1)
    return tile(cos, cos), tile(-sin, zero), tile(zero, sin)


def _pad_w_in(w):
    sizes = (BR_W, BR_W, BR_W, IDX_HEADS * IDX_DIM, IDX_DIM, IDX_HEADS, BR_W, BR_W, BR_W, HEADS, HEADS, BR_W,
             RWKV_PROJ, GDN_CONV_CH, HEADS, HEADS, BR_W, N_BRANCH * w.shape[0])
    cuts = np.concatenate([[0], np.cumsum(sizes)])
    col = lambda i: w[:, cuts[i]:cuts[i + 1]]
    (a_q, a_k, a_v, a_iq, a_ik, a_iw, m_q, m_k, m_v, m_i, m_f, m_o, r_p, g_qkv, g_b, g_a, g_z, gate) = (
        col(i) for i in range(len(sizes)))
    small = jnp.concatenate([a_ik, a_iw, m_i, m_f, g_b, g_a, jnp.zeros((w.shape[0], LANES - SM_END), w.dtype)], axis=1)
    return jnp.concatenate([a_q, a_k, a_v, a_iq, small, m_q, m_k, m_v, m_o, r_p, g_qkv, g_z, gate], axis=1).astype(BF16)


def _small_params(lp):
    lane_put = lambda v, at: jnp.zeros((LANES,), F32).at[at:at + v.shape[0]].set(v)
    rows = [lane_put(lp["idx_k_norm"], SM_IK),
            lane_put(lp["ml_i_bias"], SM_MI) + lane_put(lp["ml_f_bias"], SM_MF) + lane_put(lp["gd_dt_bias"], SM_GA),
            lane_put(lp["gd_A_log"], SM_GA)]
    return jnp.stack(rows + [jnp.zeros((LANES,), F32)] * (SUBLANES - len(rows)))


def kernel(x_prompt, x_sample, cache_k, cache_v, cache_idx_k, state_mlstm_C, state_mlstm_n, state_mlstm_m, state_rwkv_S, state_rwkv_shift, state_gdn_S, state_gdn_conv, page_table, ffn1_norm, ffn1_w1, ffn1_w3, ffn1_w2, mix_norm, w_in, idx_k_norm, ml_i_bias, ml_f_bias, ml_norm, rw_mu, rw_w0, rw_w2, rw_a0, rw_a2, rw_g2, rw_kk, rw_ka, rw_rk, rw_ln_g, rw_ln_b, gd_conv, gd_A_log, gd_dt_bias, gd_norm, w_branch, w_out, ffn2_norm, ffn2_w1, ffn2_w3, ffn2_w2, final_norm):
    bp, lp_, d = x_prompt.shape
    db, t, _ = x_sample.shape
    assert bp == 1 and t % SUBLANES == 0 and t <= LANES and lp_ % SUBLANES == 0
    depth = w_in.shape[0]
    past = page_table.shape[1] * PAGE_SIZE
    ns = db * t

    x = jnp.concatenate([x_prompt.reshape(lp_, d), x_sample.reshape(ns, d)], axis=0)
    pos = jnp.concatenate([jnp.arange(lp_, dtype=I32), jnp.tile(past + jnp.arange(t, dtype=I32), db)])
    cos, sa, sb = _rope_tables(pos)
    zeros = lambda *s: jnp.zeros(s, F32)

    outs_p, outs_s = [], []
    for l in range(depth):
        lp = dict(idx_k_norm=idx_k_norm[l], ml_i_bias=ml_i_bias[l], ml_f_bias=ml_f_bias[l],
                  rw_mu=rw_mu[l], rw_w0=rw_w0[l], rw_w2=rw_w2[l], rw_a0=rw_a0[l], rw_a2=rw_a2[l], rw_g2=rw_g2[l],
                  rw_kk=rw_kk[l], rw_ka=rw_ka[l], rw_rk=rw_rk[l], rw_ln_g=rw_ln_g[l], rw_ln_b=rw_ln_b[l],
                  gd_conv=gd_conv[l], gd_A_log=gd_A_log[l], gd_dt_bias=gd_dt_bias[l], gd_norm=gd_norm[l])
        x = _ffn(x, ffn1_norm[l], ffn1_w1[l], ffn1_w3[l], ffn1_w2[l])
        aqkv, iq, sm, mslab, rslab, gslab, gate = _proj(x, mix_norm[l], _pad_w_in(w_in[l]), cos, sa, sb, _small_params(lp))

        oa_p = _dsa_prompt(iq, sm, aqkv, lp_)
        oa_s = _dsa_sample(iq[lp_:], sm[lp_:], aqkv[lp_:], cache_k, cache_v, cache_idx_k, page_table, l, db, t)
        om_p, c_p, n_p, m_p = _mlstm(mslab, sm, 0, 1, lp_, zeros(1, HEADS, HD, HD), zeros(1, HEADS, HD), zeros(1, HEADS), ml_norm[l])
        om_s, c_s, n_s, m_s = _mlstm(mslab, sm, lp_, db, t, state_mlstm_C[l], state_mlstm_n[l], state_mlstm_m[l], ml_norm[l])
        or_p, rs_p = _rwkv(rslab, 0, 1, lp_, zeros(1, RWKV_PROJ), zeros(1, HEADS, HD, HD), lp)
        or_s, rs_s = _rwkv(rslab, lp_, db, t, state_rwkv_shift[l], state_rwkv_S[l], lp)
        og_p, gs_p = _gdn(gslab, sm, 0, 1, lp_, zeros(1, CONV_W - 1, GDN_CONV_CH), zeros(1, HEADS, HD, HD), lp)
        og_s, gs_s = _gdn(gslab, sm, lp_, db, t, state_gdn_conv[l], state_gdn_S[l], lp)

        cat = lambda a, b: jnp.concatenate([a, b], axis=0)
        x = _merge(x, cat(oa_p, oa_s), cat(om_p, om_s), cat(or_p, or_s), cat(og_p, og_s), gate, w_branch[l], w_out[l])
        x = _ffn(x, ffn2_norm[l], ffn2_w1[l], ffn2_w3[l], ffn2_w2[l])

        k_all = aqkv[:, BR_W:2 * BR_W]
        v_all = aqkv[:, 2 * BR_W:3 * BR_W]
        ik_all = sm[:, SM_IK:SM_IK + IDX_DIM]
        gq = gslab[:, 0:GDN_CONV_CH]
        outs_p.append((k_all[:lp_].reshape(1, lp_, HEADS, HD), v_all[:lp_].reshape(1, lp_, HEADS, HD),
                       ik_all[:lp_].reshape(1, lp_, IDX_DIM), c_p, n_p, m_p, rs_p, rslab[lp_ - 1:lp_],
                       gs_p, gq[lp_ - (CONV_W - 1):lp_].reshape(1, CONV_W - 1, GDN_CONV_CH)))
        outs_s.append((k_all[lp_:].reshape(db, t, HEADS, HD), v_all[lp_:].reshape(db, t, HEADS, HD),
                       ik_all[lp_:].reshape(db, t, IDX_DIM), c_s, n_s, m_s, rs_s,
                       rslab[lp_:].reshape(db, t, RWKV_PROJ)[:, -1],
                       gs_s, gq[lp_:].reshape(db, t, GDN_CONV_CH)[:, t - (CONV_W - 1):]))

    y = _final_rms(x, final_norm)
    st_p = [jnp.stack(z) for z in zip(*outs_p)]
    st_s = [jnp.stack(z) for z in zip(*outs_s)]
    return (y[:lp_].reshape(1, lp_, d), y[lp_:].reshape(db, t, d), *st_p, *st_s)
```

```python
import functools
import math

import jax
import jax.numpy as jnp
import numpy as np
from jax import lax
from jax.experimental import pallas as pl
from jax.experimental.pallas import tpu as pltpu

F32 = jnp.float32
BF16 = jnp.bfloat16
I32 = jnp.int32
SDS = jax.ShapeDtypeStruct

HD = 64
HEADS = 4
BR_W = HEADS * HD
N_BRANCH = 4
IDX_HEADS = 8
IDX_DIM = 64
TOPK_MAX = 256
ROPE_THETA = 10000.0
MLSTM_CHUNK = 64
GDN_CHUNK = 64
CONV_W = 4
RWKV_W_LORA = 64
RWKV_A_LORA = 64
RWKV_G_LORA = 128
RWKV_PROJ = 3 * BR_W + RWKV_W_LORA + RWKV_A_LORA + RWKV_G_LORA
GDN_CONV_CH = 3 * BR_W
EPS = 1e-6
LN_EPS = 1e-5
PAGE_SIZE = 128

LANES = 128
SUBLANES = 8
VMEM_LIMIT_BYTES = 58 * 1024 * 1024

SM_IK = 0
SM_IW = 64
SM_MI = 72
SM_MF = 76
SM_GB = 80
SM_GA = 84
SM_END = 88

PC_AQKV = 0
PC_IQ = 768
PC_SM = 1280
PC_M = 1408
PC_R = 2432
PC_G = 3456
PC_GATE = 4480
PC_END = 8576

HIGHEST = lax.Precision.HIGHEST
INT_MIN = -2147483648
NEG_INF = float("-inf")


def _cp(sem):
    return pltpu.CompilerParams(dimension_semantics=sem, vmem_limit_bytes=VMEM_LIMIT_BYTES)


def _resident(shape):
    return pl.BlockSpec(shape, lambda *_: (0,) * len(shape), pipeline_mode=pl.Buffered(1))


def _rows(tm, w, off_blocks=0):
    return pl.BlockSpec((tm, w), lambda i, *_: (i + off_blocks, 0))


def _dot(a, b):
    return jnp.dot(a, b, preferred_element_type=F32)


def _dot_nt(a, b, precision=None):
    return lax.dot_general(a, b, (((1,), (1,)), ((), ())), preferred_element_type=F32, precision=precision)


def _dot_tn(a, b, precision=None):
    return lax.dot_general(a, b, (((0,), (0,)), ((), ())), preferred_element_type=F32, precision=precision)


def _hdot(a, b):
    return jnp.dot(a, b, preferred_element_type=F32, precision=HIGHEST)


def _rms(x, g):
    return x * lax.rsqrt(jnp.mean(x * x, axis=-1, keepdims=True) + EPS) * g


def _iota(shape, dim):
    return lax.broadcasted_iota(I32, shape, dim)


def _head_masks(rows):
    lane = _iota((rows, BR_W), 1)
    return [(lane >= HD * h) & (lane < HD * (h + 1)) for h in range(HEADS)]


def _seg_sum(x, masks):
    out = jnp.zeros_like(x)
    for m in masks:
        s = jnp.sum(jnp.where(m, x, 0.0), axis=1, keepdims=True)
        out = jnp.where(m, s, out)
    return out


def _col_to_row(col, n):
    eye = _iota((n, n), 0) == _iota((n, n), 1)
    return jnp.sum(jnp.where(eye, jnp.broadcast_to(col, (n, n)), 0.0), axis=0, keepdims=True)


def _row_to_col(row, n):
    eye = _iota((n, n), 0) == _iota((n, n), 1)
    return jnp.sum(jnp.where(eye, jnp.broadcast_to(row, (n, n)), 0.0), axis=1, keepdims=True)


def _cumsum_col(col, n):
    r = _iota((n, n), 0)
    c = _iota((n, n), 1)
    row = jnp.sum(jnp.where(r <= c, jnp.broadcast_to(col, (n, n)), 0.0), axis=0, keepdims=True)
    return row, _row_to_col(row, n)


def _ffn_kernel(x_ref, g_ref, w1_ref, w3_ref, w2_ref, o_ref):
    x = x_ref[...]
    h = _rms(x, g_ref[...]).astype(BF16)
    a = _dot(h, w1_ref[...])
    b = _dot(h, w3_ref[...])
    y = (a * jax.nn.sigmoid(a) * b).astype(BF16)
    o_ref[...] = x + 0.5 * _dot(y, w2_ref[...])


def _ffn(x, g, w1, w3, w2):
    na, d = x.shape
    f = w1.shape[1]
    tm = math.gcd(na, 256)
    return pl.pallas_call(
        _ffn_kernel,
        out_shape=SDS((na, d), F32),
        grid=(na // tm,),
        in_specs=[_rows(tm, d), _resident((1, d)), _resident((d, f)), _resident((d, f)), _resident((f, d))],
        out_specs=_rows(tm, d),
        compiler_params=_cp(("parallel",)),
        name="ffn",
    )(x, g.reshape(1, d), w1.astype(BF16), w3.astype(BF16), w2.astype(BF16))


def _rms_kernel(x_ref, g_ref, o_ref):
    o_ref[...] = _rms(x_ref[...], g_ref[...])


def _final_rms(x, g):
    na, d = x.shape
    tm = math.gcd(na, 512)
    return pl.pallas_call(
        _rms_kernel,
        out_shape=SDS((na, d), F32),
        grid=(na // tm,),
        in_specs=[_rows(tm, d), _resident((1, d))],
        out_specs=_rows(tm, d),
        compiler_params=_cp(("parallel",)),
        name="final_rms",
    )(x, g.reshape(1, d))


def _rope_slab(z, cos, sa, sb):
    return z * cos + pltpu.roll(z, LANES - HD // 2, 1) * sa + pltpu.roll(z, HD // 2, 1) * sb


def _proj_kernel(x_ref, g_ref, w_ref, cos_ref, sa_ref, sb_ref, sp_ref,
                 aqkv_ref, iq_ref, sm_ref, m_ref, r_ref, gq_ref, gate_ref):
    h = _rms(x_ref[...], g_ref[...]).astype(BF16)
    cos, sa, sb = cos_ref[...], sa_ref[...], sb_ref[...]

    def mm(a, b):
        return _dot(h, w_ref[:, a:b])

    za = mm(PC_AQKV, PC_IQ)
    for s in range(4):
        aqkv_ref[:, s * LANES:(s + 1) * LANES] = _rope_slab(za[:, s * LANES:(s + 1) * LANES], cos, sa, sb)
    aqkv_ref[:, 2 * BR_W:3 * BR_W] = za[:, 2 * BR_W:3 * BR_W]
    zi = mm(PC_IQ, PC_SM)
    for s in range(4):
        iq_ref[:, s * LANES:(s + 1) * LANES] = _rope_slab(zi[:, s * LANES:(s + 1) * LANES], cos, sa, sb)

    zs = mm(PC_SM, PC_M)
    lane = _iota(zs.shape, 1)
    gn, bias, alog = sp_ref[0:1, :], sp_ref[1:2, :], sp_ref[2:3, :]
    ms = jnp.sum(jnp.where(lane < IDX_DIM, zs * zs, 0.0), axis=1, keepdims=True) * (1.0 / IDX_DIM)
    ik = _rope_slab(zs * lax.rsqrt(ms + EPS) * gn, cos, sa, sb)
    t = zs + bias
    out = jnp.where(lane < SM_IW, ik,
          jnp.where(lane < SM_MI, zs * (IDX_HEADS ** -0.5),
          jnp.where(lane < SM_MF, t,
          jnp.where(lane < SM_GB, jax.nn.log_sigmoid(t),
          jnp.where(lane < SM_GA, jax.nn.sigmoid(zs),
          jnp.where(lane < SM_END, -jnp.exp(alog) * jax.nn.softplus(t), 0.0))))))
    sm_ref[...] = out

    m_ref[...] = mm(PC_M, PC_R)
    r_ref[...] = mm(PC_R, PC_G)
    gq_ref[...] = mm(PC_G, PC_GATE)
    gate_ref[...] = mm(PC_GATE, PC_END)


def _proj(x, g, w_pad, cos, sa, sb, sp):
    na, d = x.shape
    tm = math.gcd(na, 256)
    widths = (3 * BR_W, IDX_HEADS * IDX_DIM, LANES, 4 * BR_W, RWKV_PROJ, 4 * BR_W, N_BRANCH * d)
    return pl.pallas_call(
        _proj_kernel,
        out_shape=tuple(SDS((na, w), F32) for w in widths),
        grid=(na // tm,),
        in_specs=[_rows(tm, d), _resident((1, d)), _resident((d, PC_END)),
                  _rows(tm, LANES), _rows(tm, LANES), _rows(tm, LANES), _resident((SUBLANES, LANES))],
        out_specs=tuple(_rows(tm, w) for w in widths),
        compiler_params=_cp(("parallel",)),
        name="proj_in",
    )(x, g.reshape(1, d), w_pad, cos, sa, sb, sp)


def _merge_kernel(x_ref, oa_ref, om_ref, or_ref, og_ref, gate_ref, wb_ref, wo_ref, o_ref):
    d = x_ref.shape[1]
    mix = None
    for n, br in enumerate((oa_ref, om_ref, or_ref, og_ref)):
        up = _dot(br[...].astype(BF16), wb_ref[n])
        t = jax.nn.sigmoid(gate_ref[:, n * d:(n + 1) * d]) * up
        mix = t if mix is None else mix + t
    o_ref[...] = x_ref[...] + _dot(mix.astype(BF16), wo_ref[...])


def _merge(x, oa, om, orr, og, gate, w_branch, w_out):
    na, d = x.shape
    tm = math.gcd(na, 256)
    return pl.pallas_call(
        _merge_kernel,
        out_shape=SDS((na, d), F32),
        grid=(na // tm,),
        in_specs=[_rows(tm, d)] + [_rows(tm, BR_W)] * 4 + [_rows(tm, N_BRANCH * d),
                  _resident((N_BRANCH, BR_W, d)), _resident((d, d))],
        out_specs=_rows(tm, d),
        compiler_params=_cp(("parallel",)),
        name="merge",
    )(x, oa, om, orr, og, gate, w_branch.astype(BF16), w_out.astype(BF16))


def _idx_queries(qi, sm, tq):
    qrows = jnp.concatenate([qi[:, h * IDX_DIM:(h + 1) * IDX_DIM] for h in range(IDX_HEADS)], axis=0).astype(BF16)
    wb = [jnp.broadcast_to(sm[:, SM_IW + h:SM_IW + h + 1] * (IDX_DIM ** -0.5), (tq, LANES)) for h in range(IDX_HEADS)]
    return qrows, wb


def _idx_scores(qrows, wb, kib, tq, tk):
    dots = _dot_nt(qrows, kib)
    cols = []
    for j in range(tk // LANES):
        acc = None
        for h in range(IDX_HEADS):
            t = jnp.maximum(dots[h * tq:(h + 1) * tq, j * LANES:(j + 1) * LANES], 0.0) * wb[h]
            acc = t if acc is None else acc + t
        cols.append(acc)
    return cols[0] if len(cols) == 1 else jnp.concatenate(cols, axis=1)


def _to_key(sc):
    b = pltpu.bitcast(jnp.where(sc == 0.0, 0.0, sc), I32)
    return b ^ ((b >> 31) & I32(0x7FFFFFFF))


def _key_to_f32(key):
    return pltpu.bitcast(key ^ ((key >> 31) & I32(0x7FFFFFFF)), F32)


def _kth_largest(count, rows, n_sel, pos_bits, tkey_ref, cut_ref):
    int_min = I32(INT_MIN)

    def bit_step(it, ub):
        cand = ub | lax.shift_left(I32(1), 31 - it)
        cs = cand ^ int_min
        return jnp.where(count(lambda k, p: k >= cs) >= n_sel, cand, ub)

    ub = lax.fori_loop(0, 32, bit_step, jnp.zeros((rows, LANES), I32))
    tkey = ub ^ int_min
    n_gt = count(lambda k, p: k > tkey)
    n_ge = count(lambda k, p: k >= tkey)
    need = n_sel - n_gt
    ambiguous = (n_ge - n_gt) > need
    tkey_ref[...] = tkey
    cut_ref[...] = jnp.full((rows, LANES), 1 << pos_bits, I32)

    @pl.when(jnp.max(jnp.where(ambiguous, 1.0, 0.0)) > 0.0)
    def _():
        def pos_step(it, q):
            cand = q | lax.shift_left(I32(1), pos_bits - 1 - it)
            below = count(lambda k, p: (k == tkey) & (p < cand))
            return jnp.where(below < need, cand, q)

        q = lax.fori_loop(0, pos_bits, pos_step, jnp.zeros((rows, LANES), I32))
        cut_ref[...] = jnp.where(ambiguous, q + 1, 1 << pos_bits)


def _flash_update(s, v, m_s, l_s, acc_s, v_transposed=False):
    m_prev = m_s[...]
    m_new = jnp.maximum(m_prev, jnp.max(s, axis=1, keepdims=True))
    m_safe = jnp.where(m_new == NEG_INF, 0.0, m_new)
    alpha = jnp.exp(m_prev - m_safe)
    p = jnp.exp(s - m_safe)
    l_s[...] = alpha * l_s[...] + jnp.sum(p, axis=1, keepdims=True)
    m_s[...] = m_new
    pv = _dot_nt(p.astype(BF16), v) if v_transposed else _dot(p.astype(BF16), v)
    acc_s[...] = alpha * acc_s[...] + pv


def _stack_heads(q, t):
    masks = _head_masks(t)
    return jnp.concatenate([jnp.where(m, q * (HD ** -0.5), 0.0) for m in masks], axis=0).astype(BF16)


def _unstack_heads(acc, l, t):
    masks = _head_masks(t)
    out = jnp.zeros((t, BR_W), F32)
    for h, m in enumerate(masks):
        out = jnp.where(m, acc[h * t:(h + 1) * t] / l[h * t:(h + 1) * t], out)
    return out


def _dsa_prompt_kernel(qi_ref, smq_ref, q_ref, ki_ref, k_ref, v_ref, o_ref,
                       keys_s, tkey_s, cut_s, m_s, l_s, acc_s, *, tq, tk, n_sel, pos_bits):
    i = pl.program_id(0)
    q0 = i * tq
    nch = (q0 + tq - 1) // tk + 1
    qrows, wb = _idx_queries(qi_ref[...], smq_ref[...], tq)
    qpos = q0 + _iota((tq, tk), 0)

    def fill(c, carry):
        k0 = pl.multiple_of(c * tk, tk)
        sc = _idx_scores(qrows, wb, ki_ref[pl.ds(k0, tk), :], tq, tk)
        sc = jnp.where(k0 + _iota((tq, tk), 1) <= qpos, sc, NEG_INF)
        keys_s[c] = _to_key(sc)
        return carry

    lax.fori_loop(0, nch, fill, 0)
    lane = _iota((tq, LANES), 1)

    def count(pred):
        def body(c, acc):
            for j in range(tk // LANES):
                p = c * tk + j * LANES + lane
                acc = acc + jnp.where(pred(keys_s[c, :, j * LANES:(j + 1) * LANES], p), 1.0, 0.0)
            return acc

        acc = lax.fori_loop(0, nch, body, jnp.zeros((tq, LANES), F32))
        return jnp.broadcast_to(jnp.sum(acc, axis=1, keepdims=True), (tq, LANES))

    _kth_largest(count, tq, n_sel, pos_bits, tkey_s, cut_s)
    neg_key = _to_key(jnp.full((tq, LANES), NEG_INF, F32))
    qend = q0 + _iota((tq, LANES), 0) + 1
    cut_s[...] = jnp.where(tkey_s[...] == neg_key, jnp.minimum(cut_s[...], qend), cut_s[...])

    tkey = tkey_s[:, 0:1]
    cut = cut_s[:, 0:1]
    qbd = _stack_heads(q_ref[:, 0:BR_W], tq)
    m_s[...] = jnp.full(m_s.shape, NEG_INF, F32)
    l_s[...] = jnp.zeros(l_s.shape, F32)
    acc_s[...] = jnp.zeros(acc_s.shape, F32)

    def attend(c, carry):
        k0 = pl.multiple_of(c * tk, tk)
        key = keys_s[c]
        sel = (key > tkey) | ((key == tkey) & (k0 + _iota((tq, tk), 1) < cut))
        bias = jnp.where(sel, 0.0, NEG_INF)
        s = _dot_nt(qbd, k_ref[pl.ds(k0, tk), :]) + jnp.concatenate([bias] * HEADS, axis=0)
        _flash_update(s, v_ref[pl.ds(k0, tk), :], m_s, l_s, acc_s)
        return carry

    lax.fori_loop(0, nch, attend, 0)
    o_ref[...] = _unstack_heads(acc_s[...], l_s[...], tq)


def _dsa_prompt(iq, sm, aqkv, lp):
    tq = min(128, lp)
    tk = min(512, lp)
    n_sel = min(TOPK_MAX, lp // 4)
    pos_bits = int(lp).bit_length()
    ki = sm[:lp, SM_IK:SM_IK + IDX_DIM].astype(BF16)
    kb = aqkv[:lp, BR_W:2 * BR_W].astype(BF16)
    vb = aqkv[:lp, 2 * BR_W:3 * BR_W].astype(BF16)
    return pl.pallas_call(
        functools.partial(_dsa_prompt_kernel, tq=tq, tk=tk, n_sel=n_sel, pos_bits=pos_bits),
        out_shape=SDS((lp, BR_W), F32),
        grid=(lp // tq,),
        in_specs=[_rows(tq, IDX_HEADS * IDX_DIM), _rows(tq, LANES), _rows(tq, 3 * BR_W),
                  _resident((lp, IDX_DIM)), _resident((lp, BR_W)), _resident((lp, BR_W))],
        out_specs=_rows(tq, BR_W),
        scratch_shapes=[pltpu.VMEM((lp // tk, tq, tk), I32), pltpu.VMEM((tq, LANES), I32), pltpu.VMEM((tq, LANES), I32),
                        pltpu.VMEM((HEADS * tq, 1), F32), pltpu.VMEM((HEADS * tq, 1), F32),
                        pltpu.VMEM((HEADS * tq, BR_W), F32)],
        compiler_params=_cp(("arbitrary",)),
        name="dsa_prompt",
    )(iq, sm, aqkv, ki, kb, vb)


def _dsa_sample_score_kernel(pt_ref, qr_ref, wb_ref, *refs, t, pg, ng, past, n_sel, pos_bits):
    pages = refs[:pg]
    knew_ref, scp_ref, scn_ref, thr_ref, cut_ref, keys_s, keyn_s, tkey_s = refs[pg:]
    j = pl.program_id(1)
    qrows = qr_ref[...]
    wb = wb_ref[...] * (IDX_DIM ** -0.5)

    def reduce_heads(dots):
        w = jnp.maximum(dots, 0.0) * wb
        s = w[0:t]
        for h in range(1, IDX_HEADS):
            s = s + w[h * t:(h + 1) * t]
        return s

    sc = jnp.concatenate([reduce_heads(_dot(qrows, p[...].astype(BF16))) for p in pages], axis=1)
    scp_ref[...] = sc
    keys_s[j] = _to_key(sc)

    @pl.when(j == ng - 1)
    def _():
        lane = _iota((t, LANES), 1)
        sn = reduce_heads(_dot_nt(qrows, knew_ref[...].astype(BF16)))
        sn = jnp.where(lane <= _iota((t, LANES), 0), sn, NEG_INF)
        scn_ref[...] = sn
        keyn_s[...] = _to_key(sn)

        def count(pred):
            def body(c, acc):
                for jj in range(pg):
                    p = c * (pg * LANES) + jj * LANES + lane
                    acc = acc + jnp.where(pred(keys_s[c, :, jj * LANES:(jj + 1) * LANES], p), 1.0, 0.0)
                return acc

            acc = lax.fori_loop(0, ng, body, jnp.zeros((t, LANES), F32))
            acc = acc + jnp.where(pred(keyn_s[...], past + lane), 1.0, 0.0)
            return jnp.broadcast_to(jnp.sum(acc, axis=1, keepdims=True), (t, LANES))

        _kth_largest(count, t, n_sel, pos_bits, tkey_s, cut_ref)
        thr_ref[...] = _key_to_f32(tkey_s[...])


def _dsa_sample_attn_kernel(pt_ref, q_ref, *refs, t, pg, ng, past):
    kpages = refs[:pg]
    vpages = refs[pg:2 * pg]
    (scp_ref, scn_ref, thr_ref, cut_ref, knew_ref, vnew_ref, o_ref, m_s, l_s, acc_s) = refs[2 * pg:]
    j = pl.program_id(1)

    @pl.when(j == 0)
    def _():
        m_s[...] = jnp.full(m_s.shape, NEG_INF, F32)
        l_s[...] = jnp.zeros(l_s.shape, F32)
        acc_s[...] = jnp.zeros(acc_s.shape, F32)

    qbd = _stack_heads(q_ref[...], t)
    thr = thr_ref[:, 0:1]
    cut = cut_ref[:, 0:1]
    w = pg * LANES
    sc = scp_ref[...]
    kidx = j * w + _iota((t, w), 1)
    bias = jnp.where((sc > thr) | ((sc == thr) & (kidx < cut)), 0.0, NEG_INF)
    kcat = jnp.concatenate([p[...].reshape(BR_W, PAGE_SIZE).astype(BF16) for p in kpages], axis=1)
    vcat = jnp.concatenate([p[...].reshape(BR_W, PAGE_SIZE).astype(BF16) for p in vpages], axis=1)
    s = _dot(qbd, kcat) + jnp.concatenate([bias] * HEADS, axis=0)
    _flash_update(s, vcat, m_s, l_s, acc_s, v_transposed=True)

    @pl.when(j == ng - 1)
    def _():
        lane = _iota((t, LANES), 1)
        scn = scn_ref[...]
        seln = ((scn > thr) | ((scn == thr) & (past + lane < cut))) & (lane <= _iota((t, LANES), 0))
        biasn = jnp.where(seln, 0.0, NEG_INF)
        sn = _dot_nt(qbd, knew_ref[...].astype(BF16)) + jnp.concatenate([biasn] * HEADS, axis=0)
        _flash_update(sn, vnew_ref[...].astype(BF16), m_s, l_s, acc_s)
        o_ref[...] = _unstack_heads(acc_s[...], l_s[...], t)


def _dsa_sample(iq_s, sm_s, aqkv_s, cache_k, cache_v, cache_idx_k, page_table, layer, db, t):
    n_pages = page_table.shape[1]
    past = n_pages * PAGE_SIZE
    n_sel = min(TOPK_MAX, (past + t) // 4)
    pos_bits = int(past + LANES).bit_length()
    pg = math.gcd(n_pages, 8)
    ng = n_pages // pg
    ckt = cache_k.transpose(0, 1, 3, 4, 2)
    cvt = cache_v.transpose(0, 1, 3, 4, 2)
    cit = cache_idx_k.transpose(0, 1, 3, 2)

    qrows = iq_s.reshape(db, t, IDX_HEADS, IDX_DIM).transpose(0, 2, 1, 3).reshape(db, IDX_HEADS * t, IDX_DIM).astype(BF16)
    wcol = sm_s[:, SM_IW:SM_IW + IDX_HEADS].reshape(db, t, IDX_HEADS).transpose(0, 2, 1).reshape(db, IDX_HEADS * t, 1)
    wbs = jnp.broadcast_to(wcol, (db, IDX_HEADS * t, LANES))
    pad = lambda a: jnp.pad(a.reshape(db, t, a.shape[-1]), ((0, 0), (0, LANES - t), (0, 0)))
    ki_new = pad(sm_s[:, SM_IK:SM_IK + IDX_DIM])
    k_new = pad(aqkv_s[:, BR_W:2 * BR_W])
    v_new = pad(aqkv_s[:, 2 * BR_W:3 * BR_W])
    q3 = aqkv_s[:, 0:BR_W].reshape(db, t, BR_W)

    def idx_page(i):
        return pl.BlockSpec((None, None, IDX_DIM, PAGE_SIZE), lambda b, j, pt: (layer, pt[b, j * pg + i], 0, 0))

    def kv_page(i):
        return pl.BlockSpec((None, None, HEADS, HD, PAGE_SIZE), lambda b, j, pt: (layer, pt[b, j * pg + i], 0, 0, 0))

    per_b = lambda r, w: pl.BlockSpec((None, r, w), lambda b, j, pt: (b, 0, 0))
    sc_spec = pl.BlockSpec((None, t, pg * LANES), lambda b, j, pt: (b, 0, j))

    scp, scn, thr, cut = pl.pallas_call(
        functools.partial(_dsa_sample_score_kernel, t=t, pg=pg, ng=ng, past=past, n_sel=n_sel, pos_bits=pos_bits),
        out_shape=(SDS((db, t, past), F32), SDS((db, t, LANES), F32), SDS((db, t, LANES), F32), SDS((db, t, LANES), I32)),
        grid_spec=pltpu.PrefetchScalarGridSpec(
            num_scalar_prefetch=1, grid=(db, ng),
            in_specs=[per_b(IDX_HEADS * t, IDX_DIM), per_b(IDX_HEADS * t, LANES)]
                     + [idx_page(i) for i in range(pg)] + [per_b(LANES, IDX_DIM)],
            out_specs=(sc_spec, per_b(t, LANES), per_b(t, LANES), per_b(t, LANES)),
            scratch_shapes=[pltpu.VMEM((ng, t, pg * LANES), I32), pltpu.VMEM((t, LANES), I32), pltpu.VMEM((t, LANES), I32)]),
        compiler_params=_cp(("arbitrary", "arbitrary")),
        name="dsa_sample_scores",
    )(page_table, qrows, wbs, *([cit] * pg), ki_new)

    out = pl.pallas_call(
        functools.partial(_dsa_sample_attn_kernel, t=t, pg=pg, ng=ng, past=past),
        out_shape=SDS((db, t, BR_W), F32),
        grid_spec=pltpu.PrefetchScalarGridSpec(
            num_scalar_prefetch=1, grid=(db, ng),
            in_specs=[per_b(t, BR_W)] + [kv_page(i) for i in range(pg)] * 2
                     + [sc_spec, per_b(t, LANES), per_b(t, LANES), per_b(t, LANES), per_b(LANES, BR_W), per_b(LANES, BR_W)],
            out_specs=per_b(t, BR_W),
            scratch_shapes=[pltpu.VMEM((HEADS * t, 1), F32), pltpu.VMEM((HEADS * t, 1), F32),
                            pltpu.VMEM((HEADS * t, BR_W), F32)]),
        compiler_params=_cp(("arbitrary", "arbitrary")),
        name="dsa_sample_attention",
    )(page_table, q3, *([ckt] * pg), *([cvt] * pg), scp, scn, thr, cut, k_new, v_new)
    return out.reshape(db * t, BR_W)


def _mlstm_head(q, k, v, li, lf, cst, n, m, c):
    r = _iota((c, c), 0)
    cc = _iota((c, c), 1)
    b_row, b_col = _cumsum_col(lf, c)
    li_row = _col_to_row(li, c)
    log_d = jnp.where(cc <= r, b_col - b_row + li_row, NEG_INF)
    inter = b_col + m
    mt = jnp.maximum(inter, jnp.max(log_d, axis=1, keepdims=True))
    s = _dot_nt(q, k, HIGHEST) * jnp.exp(log_d - mt)
    ei = jnp.exp(inter - mt)
    num = _hdot(s, v) + ei * _dot_nt(q, cst, HIGHEST)
    den = jnp.sum(s, axis=1, keepdims=True) + ei * jnp.sum(q * n, axis=1, keepdims=True)
    h = num / jnp.maximum(jnp.abs(den), jnp.exp(-mt))
    bc = b_col[c - 1:c, :]
    gs = bc - b_col + li
    mn = jnp.maximum(bc + m, jnp.max(gs, axis=0, keepdims=True))
    ws = jnp.exp(gs - mn)
    dec = jnp.exp(bc + m - mn)
    c_new = dec * cst + _dot_tn(ws * v, k, HIGHEST)
    n_new = dec * n + jnp.sum(ws * k, axis=0, keepdims=True)
    return h, c_new, n_new, mn


def _mlstm_kernel(x_ref, sm_ref, c0_ref, n0_ref, m0_ref, lng_ref, o_ref, c1_ref, n1_ref, m1_ref,
                  c_s, n_s, m_s, *, c, cb):
    i = pl.program_id(1)

    @pl.when(i == 0)
    def _():
        c_s[...] = c0_ref[...]
        n_s[...] = n0_ref[...]
        m_s[...] = m0_ref[...]

    def chunk(ci, carry):
        r0 = pl.multiple_of(ci * c, c)
        outs = []
        for h in range(HEADS):
            lo = HD * h
            q = x_ref[pl.ds(r0, c), lo:lo + HD]
            k = x_ref[pl.ds(r0, c), BR_W + lo:BR_W + lo + HD] * (HD ** -0.5)
            v = x_ref[pl.ds(r0, c), 2 * BR_W + lo:2 * BR_W + lo + HD]
            og = x_ref[pl.ds(r0, c), 3 * BR_W + lo:3 * BR_W + lo + HD]
            li = sm_ref[pl.ds(r0, c), SM_MI + h:SM_MI + h + 1]
            lf = sm_ref[pl.ds(r0, c), SM_MF + h:SM_MF + h + 1]
            hh, c_new, n_new, m_new = _mlstm_head(q, k, v, li, lf, c_s[h], n_s[h], m_s[h], c)
            c_s[h] = c_new
            n_s[h] = n_new
            m_s[h] = m_new
            mu = jnp.mean(hh, axis=1, keepdims=True)
            var = jnp.mean(jnp.square(hh - mu), axis=1, keepdims=True)
            y = (hh - mu) * lax.rsqrt(var + LN_EPS) * lng_ref[:, lo:lo + HD]
            outs.append(y * jax.nn.sigmoid(og))
        o_ref[pl.ds(r0, c), :] = jnp.concatenate(outs, axis=1)
        return carry

    lax.fori_loop(0, cb, chunk, 0)

    @pl.when(i == pl.num_programs(1) - 1)
    def _():
        c1_ref[...] = c_s[...]
        n1_ref[...] = n_s[...]
        m1_ref[...] = m_s[...]


def _seq_blocks(l, chunk, max_rows):
    nc = l // chunk
    cb = math.gcd(nc, max(1, max_rows // chunk))
    return chunk * cb, cb, nc // cb


def _state_spec(shape):
    return pl.BlockSpec((None,) + shape, lambda b, i: (b,) + (0,) * len(shape))


def _mlstm(mslab, sm, row_off, b, l, c0, n0, m0, ln_g):
    c = math.gcd(l, MLSTM_CHUNK)
    rows, cb, nblk = _seq_blocks(l, c, 512)
    off = row_off // rows
    seq = lambda w: pl.BlockSpec((rows, w), lambda bb, i: (off + bb * nblk + i, 0))
    o, c1, n1, m1 = pl.pallas_call(
        functools.partial(_mlstm_kernel, c=c, cb=cb),
        out_shape=(SDS((b * l, BR_W), F32), SDS((b, HEADS, HD, HD), F32), SDS((b, HEADS, 1, HD), F32),
                   SDS((b, HEADS, 1, 1), F32)),
        grid=(b, nblk),
        in_specs=[seq(4 * BR_W), seq(LANES), _state_spec((HEADS, HD, HD)), _state_spec((HEADS, 1, HD)),
                  _state_spec((HEADS, 1, 1)), pl.BlockSpec((1, BR_W), lambda bb, i: (0, 0))],
        out_specs=(pl.BlockSpec((rows, BR_W), lambda bb, i: (bb * nblk + i, 0)), _state_spec((HEADS, HD, HD)),
                   _state_spec((HEADS, 1, HD)), _state_spec((HEADS, 1, 1))),
        scratch_shapes=[pltpu.VMEM((HEADS, HD, HD), F32), pltpu.VMEM((HEADS, 1, HD), F32), pltpu.VMEM((HEADS, 1, 1), F32)],
        compiler_params=_cp(("parallel", "arbitrary")),
        name="mlstm",
    )(mslab, sm, c0, n0.reshape(b, HEADS, 1, HD), m0.reshape(b, HEADS, 1, 1), ln_g.reshape(1, BR_W))
    return o, c1, n1.reshape(b, HEADS, HD), m1.reshape(b, HEADS)


def _shift_rows(x, halo, s):
    rows = x.shape[0]
    rolled = pltpu.roll(x, s, 0)
    first = jnp.where(_iota((SUBLANES, x.shape[1]), 0) < s, pltpu.roll(halo, s, 0), rolled[0:SUBLANES])
    return first if rows == SUBLANES else jnp.concatenate([first, rolled[SUBLANES:]], axis=0)


def _halo_specs(rows, w, off_rows, nt):
    hb = rows // SUBLANES
    tile = pl.BlockSpec((rows, w), lambda b, i: (off_rows // rows + b * nt + i, 0))
    halo = pl.BlockSpec((SUBLANES, w), lambda b, i: (jnp.maximum(off_rows // SUBLANES + (b * nt + i) * hb - 1, 0), 0))
    state = pl.BlockSpec((None, SUBLANES, w), lambda b, i: (b, 0, 0))
    return tile, halo, state


def _pad_state_rows(st, w):
    b, r, w0 = st.shape
    return jnp.pad(st, ((0, 0), (SUBLANES - r, 0), (0, w - w0)))


def _cumsum_rows(x):
    row = _iota(x.shape, 0)
    sh = 1
    while sh < x.shape[0]:
        x = x + jnp.where(row >= sh, pltpu.roll(x, sh, 0), 0.0)
        sh *= 2
    return x


def _unit_lower_inverse(m, c):
    r = _iota((c, c), 0)
    cc = _iota((c, c), 1)
    fold = jnp.where((_iota((c, SUBLANES), 0) & 7) == _iota((c, SUBLANES), 1), 1.0, 0.0)
    m8 = _hdot(jnp.where((r >> 3) == (cc >> 3), m, 0.0), fold)
    z = jnp.where(r == cc, 1.0, 0.0)
    for s in range(SUBLANES - 1):
        piv = jnp.broadcast_to(z.reshape(c // SUBLANES, SUBLANES, c)[:, s:s + 1, :],
                               (c // SUBLANES, SUBLANES, c)).reshape(c, c)
        z = z + m8[:, s:s + 1] * piv
    sh = 3
    while (1 << sh) < c:
        off = ((r >> (sh + 1)) == (cc >> (sh + 1))) & ((r >> sh) != (cc >> sh))
        z = z + _hdot(_hdot(z, jnp.where(off, m, 0.0)), z)
        sh += 1
    return z


def _lin_scan(a_ref, y0_ref, phi_ref, psi_ref, s_s, y_s, c, cb):
    def chunk(ci, carry):
        r0 = pl.multiple_of(ci * c, c)
        outs = []
        for h in range(HEADS):
            lo = HD * h
            st = s_s[h]
            outs.append(_dot_nt(a_ref[pl.ds(r0, c), lo:lo + HD], st, HIGHEST) + y0_ref[pl.ds(r0, c), lo:lo + HD])
            s_s[h] = _hdot(st, phi_ref[ci, h]) + psi_ref[ci, h]
        y_s[pl.ds(r0, c), :] = jnp.concatenate(outs, axis=1)
        return carry

    lax.fori_loop(0, cb, chunk, 0)


def _chunk_specs(rows, cb, nblk, widths):
    row_specs = [pl.BlockSpec((rows, w), lambda bb, i: (bb * nblk + i, 0)) for w in widths]
    mat = pl.BlockSpec((cb, HEADS, HD, HD), lambda bb, i: (bb * nblk + i, 0, 0, 0))
    return row_specs, mat


def _rwkv_prep_kernel(p_ref, halo_ref, st_ref, vec_ref, w2_ref, a2_ref, g2_ref, rs_ref, post_ref):
    i = pl.program_id(1)
    p = p_ref[...]
    rows = p.shape[0]
    halo = jnp.where(i == 0, st_ref[...], halo_ref[...])
    prev = _shift_rows(p, halo, 1)
    mu = jnp.concatenate([vec_ref[0:1, :], vec_ref[1:2, :], vec_ref[2:3, :], vec_ref[3:4, :]], axis=1)
    pm = p + (prev - p) * mu
    r = pm[:, 0:BR_W]
    k = pm[:, BR_W:2 * BR_W]
    v = pm[:, 2 * BR_W:3 * BR_W]
    o = 3 * BR_W
    wd = pm[:, o:o + RWKV_W_LORA]
    ad = pm[:, o + RWKV_W_LORA:o + RWKV_W_LORA + RWKV_A_LORA]
    gd = pm[:, o + RWKV_W_LORA + RWKV_A_LORA:RWKV_PROJ]
    w0, a0, kkp, ka, rk = (vec_ref[4:5, :], vec_ref[5:6, :], vec_ref[6:7, :], vec_ref[7:8, :], vec_ref[8:9, :])
    wlog = -jax.nn.softplus(-(w0 + _hdot(jnp.tanh(wd), w2_ref[...]))) - 0.5
    log_decay = -jnp.exp(wlog)
    a = jax.nn.sigmoid(a0 + _hdot(ad, a2_ref[...]))
    g = _hdot(jax.nn.sigmoid(gd), g2_ref[...])
    masks = _head_masks(rows)
    kkr = k * kkp
    kk = kkr * lax.rsqrt(jnp.maximum(_seg_sum(kkr * kkr, masks), 1e-12))
    k2 = k * (1.0 + (a - 1.0) * ka)
    bonus = _seg_sum(r * k2 * rk, masks) * v
    for n, t in enumerate((r, log_decay, k2, v, kk, kk * a)):
        rs_ref[:, n * BR_W:(n + 1) * BR_W] = t
    post_ref[:, 0:BR_W] = bonus
    post_ref[:, BR_W:2 * BR_W] = g


def _rwkv_chunk_kernel(rs_ref, rh_ref, y0_ref, phi_ref, psi_ref, *, c, cb):
    r_i = _iota((c, c), 0)
    c_i = _iota((c, c), 1)
    eye = _iota((HD, HD), 0) == _iota((HD, HD), 1)

    def chunk(ci, carry):
        r0 = pl.multiple_of(ci * c, c)
        r, lw, k, v, kk, kka = (rs_ref[pl.ds(r0, c), n * BR_W:(n + 1) * BR_W] for n in range(6))
        cum = _cumsum_rows(lw)
        pc = cum[c - 1:c, :]
        at = -kk * jnp.exp(cum - lw)
        inv = jnp.exp(-cum)
        bt = kka * inv
        kt = k * inv
        rt = r * jnp.exp(cum)
        rest = jnp.exp(pc - cum)
        bd = kka * rest
        kd = k * rest
        epc = jnp.exp(pc)
        rhs, y0s = [], []
        for h in range(HEADS):
            sl = slice(HD * h, HD * (h + 1))
            big = _dot_nt(jnp.concatenate([at[:, sl], rt[:, sl]], axis=0),
                          jnp.concatenate([bt[:, sl], kt[:, sl]], axis=0), HIGHEST)
            mab = jnp.where(c_i < r_i, big[0:c, 0:c], 0.0)
            mak = jnp.where(c_i < r_i, big[0:c, c:2 * c], 0.0)
            nrb = jnp.where(c_i <= r_i, big[c:2 * c, 0:c], 0.0)
            nrk = jnp.where(c_i <= r_i, big[c:2 * c, c:2 * c], 0.0)
            hg = _hdot(_unit_lower_inverse(mab, c), jnp.concatenate([at[:, sl], _hdot(mak, v[:, sl])], axis=1))
            low = jnp.concatenate([jnp.zeros((c, HD), F32), v[:, sl]], axis=1)
            out = _hdot(jnp.concatenate([nrb, nrk], axis=1), jnp.concatenate([hg, low], axis=0))
            rhs.append(rt[:, sl] + out[:, 0:HD])
            y0s.append(out[:, HD:2 * HD])
            hb = _dot_tn(hg, bd[:, sl], HIGHEST)
            phi_ref[ci, h] = jnp.where(eye, jnp.broadcast_to(epc[:, sl], (HD, HD)), 0.0) + hb[0:HD]
            psi_ref[ci, h] = hb[HD:2 * HD] + _dot_tn(v[:, sl], kd[:, sl], HIGHEST)
        rh_ref[pl.ds(r0, c), :] = jnp.concatenate(rhs, axis=1)
        y0_ref[pl.ds(r0, c), :] = jnp.concatenate(y0s, axis=1)
        return carry

    lax.fori_loop(0, cb, chunk, 0)


def _rwkv_scan_kernel(rh_ref, y0_ref, phi_ref, psi_ref, post_ref, s0_ref, ln_ref, o_ref, s1_ref, s_s, y_s, *, c, cb):
    i = pl.program_id(1)

    @pl.when(i == 0)
    def _():
        s_s[...] = s0_ref[...]

    _lin_scan(rh_ref, y0_ref, phi_ref, psi_ref, s_s, y_s, c, cb)
    y = y_s[...]
    masks = _head_masks(c * cb)
    mu = _seg_sum(y, masks) * (1.0 / HD)
    var = _seg_sum(jnp.square(y - mu), masks) * (1.0 / HD)
    y = (y - mu) * lax.rsqrt(var + LN_EPS) * ln_ref[0:1, :] + ln_ref[1:2, :]
    o_ref[...] = (y + post_ref[:, 0:BR_W]) * post_ref[:, BR_W:2 * BR_W]

    @pl.when(i == pl.num_programs(1) - 1)
    def _():
        s1_ref[...] = s_s[...]


def _rwkv(rslab, row_off, b, l, sh0, s0, lp):
    rows = math.gcd(l, 256)
    nt = l // rows
    tile, halo, state = _halo_specs(rows, RWKV_PROJ, row_off, nt)
    quarter = lambda a: a.reshape(4, BR_W)
    vec = jnp.concatenate([quarter(lp["rw_mu"]), lp["rw_w0"][None], lp["rw_a0"][None], lp["rw_kk"][None],
                           lp["rw_ka"][None], lp["rw_rk"].reshape(1, BR_W), jnp.zeros((7, BR_W), F32)], axis=0)
    rs, post = pl.pallas_call(
        _rwkv_prep_kernel,
        out_shape=(SDS((b * l, 6 * BR_W), F32), SDS((b * l, 2 * BR_W), F32)),
        grid=(b, nt),
        in_specs=[tile, halo, state, pl.BlockSpec((16, BR_W), lambda bb, i: (0, 0)),
                  pl.BlockSpec((RWKV_W_LORA, BR_W), lambda bb, i: (0, 0)),
                  pl.BlockSpec((RWKV_A_LORA, BR_W), lambda bb, i: (0, 0)),
                  pl.BlockSpec((RWKV_G_LORA, BR_W), lambda bb, i: (0, 0))],
        out_specs=(pl.BlockSpec((rows, 6 * BR_W), lambda bb, i: (bb * nt + i, 0)),
                   pl.BlockSpec((rows, 2 * BR_W), lambda bb, i: (bb * nt + i, 0))),
        compiler_params=_cp(("parallel", "parallel")),
        name="rwkv_prep",
    )(rslab, rslab, _pad_state_rows(sh0[:, None, :], RWKV_PROJ), vec, lp["rw_w2"], lp["rw_a2"], lp["rw_g2"])

    c = math.gcd(l, 64)
    nc = l // c
    prow, pcb, pblk = _seq_blocks(l, c, 256)
    (rs_spec, rh_spec, y0_spec), mat = _chunk_specs(prow, pcb, pblk, (6 * BR_W, BR_W, BR_W))
    rh, y0, phi, psi = pl.pallas_call(
        functools.partial(_rwkv_chunk_kernel, c=c, cb=pcb),
        out_shape=(SDS((b * l, BR_W), F32), SDS((b * l, BR_W), F32),
                   SDS((b * nc, HEADS, HD, HD), F32), SDS((b * nc, HEADS, HD, HD), F32)),
        grid=(b, pblk),
        in_specs=[rs_spec],
        out_specs=(rh_spec, y0_spec, mat, mat),
        compiler_params=_cp(("parallel", "parallel")),
        name="rwkv_chunk",
    )(rs)

    srow, scb, sblk = _seq_blocks(l, c, 512)
    (rh_spec, y0_spec, post_spec, o_spec), mat = _chunk_specs(srow, scb, sblk, (BR_W, BR_W, 2 * BR_W, BR_W))
    ln = jnp.stack([lp["rw_ln_g"], lp["rw_ln_b"]] + [jnp.zeros((BR_W,), F32)] * 6)
    o, s1 = pl.pallas_call(
        functools.partial(_rwkv_scan_kernel, c=c, cb=scb),
        out_shape=(SDS((b * l, BR_W), F32), SDS((b, HEADS, HD, HD), F32)),
        grid=(b, sblk),
        in_specs=[rh_spec, y0_spec, mat, mat, post_spec, _state_spec((HEADS, HD, HD)),
                  pl.BlockSpec((SUBLANES, BR_W), lambda bb, i: (0, 0))],
        out_specs=(o_spec, _state_spec((HEADS, HD, HD))),
        scratch_shapes=[pltpu.VMEM((HEADS, HD, HD), F32), pltpu.VMEM((srow, BR_W), F32)],
        compiler_params=_cp(("parallel", "arbitrary")),
        name="rwkv_scan",
    )(rh, y0, phi, psi, post, s0, ln)
    return o, s1


def _gdn_prep_kernel(x_ref, halo_ref, st_ref, taps_ref, o_ref):
    i = pl.program_id(1)
    x = x_ref[:, 0:GDN_CONV_CH]
    rows = x.shape[0]
    halo = jnp.where(i == 0, st_ref[:, 0:GDN_CONV_CH], halo_ref[:, 0:GDN_CONV_CH])
    conv = x * taps_ref[CONV_W - 1:CONV_W, :]
    for s in range(1, CONV_W):
        conv = conv + _shift_rows(x, halo, s) * taps_ref[CONV_W - 1 - s:CONV_W - s, :]
    conv = conv * jax.nn.sigmoid(conv)
    masks = _head_masks(rows)
    for n in range(3):
        t = conv[:, n * BR_W:(n + 1) * BR_W]
        if n < 2:
            t = t * lax.rsqrt(jnp.maximum(_seg_sum(t * t, masks), 1e-12))
        if n == 0:
            t = t * (HD ** -0.5)
        o_ref[:, n * BR_W:(n + 1) * BR_W] = t


def _gdn_chunk_kernel(x_ref, sm_ref, qh_ref, o0_ref, phi_ref, psi_ref, *, c, cb):
    r_i = _iota((c, c), 0)
    c_i = _iota((c, c), 1)
    eye = _iota((HD, HD), 0) == _iota((HD, HD), 1)

    def chunk(ci, carry):
        r0 = pl.multiple_of(ci * c, c)
        qhs, o0s = [], []
        for h in range(HEADS):
            lo = HD * h
            q = x_ref[pl.ds(r0, c), lo:lo + HD]
            k = x_ref[pl.ds(r0, c), BR_W + lo:BR_W + lo + HD]
            v = x_ref[pl.ds(r0, c), 2 * BR_W + lo:2 * BR_W + lo + HD]
            beta = sm_ref[pl.ds(r0, c), SM_GB + h:SM_GB + h + 1]
            g = sm_ref[pl.ds(r0, c), SM_GA + h:SM_GA + h + 1]
            gam_row, gam_col = _cumsum_col(g, c)
            dec = jnp.exp(jnp.where(c_i <= r_i, gam_col - gam_row, NEG_INF))
            eg = jnp.exp(gam_col)
            gl = gam_col[c - 1:c, :]
            big = _dot_nt(jnp.concatenate([k, q], axis=0), k, HIGHEST)
            a = beta * big[0:c] * jnp.where(c_i < r_i, dec, 0.0)
            x = _hdot(_unit_lower_inverse(-a, c), jnp.concatenate([beta * v, (beta * eg) * k], axis=1))
            out = _hdot(big[c:2 * c] * dec, x)
            qhs.append(eg * q - out[:, HD:2 * HD])
            o0s.append(out[:, 0:HD])
            xb = _dot_tn(x, jnp.exp(gl - gam_col) * k, HIGHEST)
            phi_ref[ci, h] = jnp.where(eye, jnp.broadcast_to(jnp.exp(gl), (HD, HD)), 0.0) - xb[HD:2 * HD]
            psi_ref[ci, h] = xb[0:HD]
        qh_ref[pl.ds(r0, c), :] = jnp.concatenate(qhs, axis=1)
        o0_ref[pl.ds(r0, c), :] = jnp.concatenate(o0s, axis=1)
        return carry

    lax.fori_loop(0, cb, chunk, 0)


def _gdn_scan_kernel(qh_ref, o0_ref, phi_ref, psi_ref, z_ref, s0_ref, gn_ref, o_ref, s1_ref, s_s, y_s, *, c, cb):
    i = pl.program_id(1)

    @pl.when(i == 0)
    def _():
        s_s[...] = s0_ref[...]

    _lin_scan(qh_ref, o0_ref, phi_ref, psi_ref, s_s, y_s, c, cb)
    o = y_s[...]
    ms = _seg_sum(o * o, _head_masks(c * cb)) * (1.0 / HD)
    z = z_ref[:, 3 * BR_W:4 * BR_W]
    o_ref[...] = o * lax.rsqrt(ms + EPS) * gn_ref[...] * (z * jax.nn.sigmoid(z))

    @pl.when(i == pl.num_programs(1) - 1)
    def _():
        s1_ref[...] = s_s[...]


def _gdn(gslab, sm, row_off, b, l, cv0, s0, lp):
    rows = math.gcd(l, 256)
    nt = l // rows
    w = 4 * BR_W
    tile, halo, state = _halo_specs(rows, w, row_off, nt)
    taps = jnp.pad(lp["gd_conv"], ((0, SUBLANES - CONV_W), (0, 0)))
    qkv = pl.pallas_call(
        _gdn_prep_kernel,
        out_shape=SDS((b * l, GDN_CONV_CH), F32),
        grid=(b, nt),
        in_specs=[tile, halo, state, pl.BlockSpec((SUBLANES, GDN_CONV_CH), lambda bb, i: (0, 0))],
        out_specs=pl.BlockSpec((rows, GDN_CONV_CH), lambda bb, i: (bb * nt + i, 0)),
        compiler_params=_cp(("parallel", "parallel")),
        name="gdn_prep",
    )(gslab, gslab, _pad_state_rows(cv0, w), taps)

    c = math.gcd(l, GDN_CHUNK)
    nc = l // c
    prow, pcb, pblk = _seq_blocks(l, c, 256)
    (x_spec, qh_spec, o0_spec), mat = _chunk_specs(prow, pcb, pblk, (GDN_CONV_CH, BR_W, BR_W))
    sm_spec = pl.BlockSpec((prow, LANES), lambda bb, i: (row_off // prow + bb * pblk + i, 0))
    qh, o0, phi, psi = pl.pallas_call(
        functools.partial(_gdn_chunk_kernel, c=c, cb=pcb),
        out_shape=(SDS((b * l, BR_W), F32), SDS((b * l, BR_W), F32),
                   SDS((b * nc, HEADS, HD, HD), F32), SDS((b * nc, HEADS, HD, HD), F32)),
        grid=(b, pblk),
        in_specs=[x_spec, sm_spec],
        out_specs=(qh_spec, o0_spec, mat, mat),
        compiler_params=_cp(("parallel", "parallel")),
        name="gdn_chunk",
    )(qkv, sm)

    srow, scb, sblk = _seq_blocks(l, c, 512)
    (qh_spec, o0_spec, o_spec), mat = _chunk_specs(srow, scb, sblk, (BR_W, BR_W, BR_W))
    z_spec = pl.BlockSpec((srow, w), lambda bb, i: (row_off // srow + bb * sblk + i, 0))
    o, s1 = pl.pallas_call(
        functools.partial(_gdn_scan_kernel, c=c, cb=scb),
        out_shape=(SDS((b * l, BR_W), F32), SDS((b, HEADS, HD, HD), F32)),
        grid=(b, sblk),
        in_specs=[qh_spec, o0_spec, mat, mat, z_spec, _state_spec((HEADS, HD, HD)),
                  pl.BlockSpec((1, BR_W), lambda bb, i: (0, 0))],
        out_specs=(o_spec, _state_spec((HEADS, HD, HD))),
        scratch_shapes=[pltpu.VMEM((HEADS, HD, HD), F32), pltpu.VMEM((srow, BR_W), F32)],
        compiler_params=_cp(("parallel", "arbitrary")),
        name="gdn_scan",
    )(qh, o0, phi, psi, gslab, s0, jnp.tile(lp["gd_norm"], HEADS).reshape(1, BR_W))
    return o, s1


def _rope_tables(pos):
    half = HD // 2
    inv = ROPE_THETA ** (-jnp.arange(half, dtype=F32) / half)
    ang = pos.astype(F32)[:, None] * inv[None, :]
    cos, sin = jnp.cos(ang), jnp.sin(ang)
    zero = jnp.zeros_like(sin)
    tile = lambda a, b: jnp.concatenate([a, b, a, b], axis=1)
    return tile(cos, cos), tile(-sin, zero), tile(zero, sin)


def _pad_w_in(w):
    sizes = (BR_W, BR_W, BR_W, IDX_HEADS * IDX_DIM, IDX_DIM, IDX_HEADS, BR_W, BR_W, BR_W, HEADS, HEADS, BR_W,
             RWKV_PROJ, GDN_CONV_CH, HEADS, HEADS, BR_W, N_BRANCH * w.shape[0])
    cuts = np.concatenate([[0], np.cumsum(sizes)])
    col = lambda i: w[:, cuts[i]:cuts[i + 1]]
    (a_q, a_k, a_v, a_iq, a_ik, a_iw, m_q, m_k, m_v, m_i, m_f, m_o, r_p, g_qkv, g_b, g_a, g_z, gate) = (
        col(i) for i in range(len(sizes)))
    small = jnp.concatenate([a_ik, a_iw, m_i, m_f, g_b, g_a, jnp.zeros((w.shape[0], LANES - SM_END), w.dtype)], axis=1)
    return jnp.concatenate([a_q, a_k, a_v, a_iq, small, m_q, m_k, m_v, m_o, r_p, g_qkv, g_z, gate], axis=1).astype(BF16)


def _small_params(lp):
    lane_put = lambda v, at: jnp.zeros((LANES,), F32).at[at:at + v.shape[0]].set(v)
    rows = [lane_put(lp["idx_k_norm"], SM_IK),
            lane_put(lp["ml_i_bias"], SM_MI) + lane_put(lp["ml_f_bias"], SM_MF) + lane_put(lp["gd_dt_bias"], SM_GA),
            lane_put(lp["gd_A_log"], SM_GA)]
    return jnp.stack(rows + [jnp.zeros((LANES,), F32)] * (SUBLANES - len(rows)))


def kernel(x_prompt, x_sample, cache_k, cache_v, cache_idx_k, state_mlstm_C, state_mlstm_n, state_mlstm_m, state_rwkv_S, state_rwkv_shift, state_gdn_S, state_gdn_conv, page_table, ffn1_norm, ffn1_w1, ffn1_w3, ffn1_w2, mix_norm, w_in, idx_k_norm, ml_i_bias, ml_f_bias, ml_norm, rw_mu, rw_w0, rw_w2, rw_a0, rw_a2, rw_g2, rw_kk, rw_ka, rw_rk, rw_ln_g, rw_ln_b, gd_conv, gd_A_log, gd_dt_bias, gd_norm, w_branch, w_out, ffn2_norm, ffn2_w1, ffn2_w3, ffn2_w2, final_norm):
    bp, lp_, d = x_prompt.shape
    db, t, _ = x_sample.shape
    assert bp == 1 and t % SUBLANES == 0 and t <= LANES and lp_ % SUBLANES == 0
    depth = w_in.shape[0]
    past = page_table.shape[1] * PAGE_SIZE
    ns = db * t

    x = jnp.concatenate([x_prompt.reshape(lp_, d), x_sample.reshape(ns, d)], axis=0)
    pos = jnp.concatenate([jnp.arange(lp_, dtype=I32), jnp.tile(past + jnp.arange(t, dtype=I32), db)])
    cos, sa, sb = _rope_tables(pos)
    zeros = lambda *s: jnp.zeros(s, F32)

    outs_p, outs_s = [], []
    for l in range(depth):
        lp = dict(idx_k_norm=idx_k_norm[l], ml_i_bias=ml_i_bias[l], ml_f_bias=ml_f_bias[l],
                  rw_mu=rw_mu[l], rw_w0=rw_w0[l], rw_w2=rw_w2[l], rw_a0=rw_a0[l], rw_a2=rw_a2[l], rw_g2=rw_g2[l],
                  rw_kk=rw_kk[l], rw_ka=rw_ka[l], rw_rk=rw_rk[l], rw_ln_g=rw_ln_g[l], rw_ln_b=rw_ln_b[l],
                  gd_conv=gd_conv[l], gd_A_log=gd_A_log[l], gd_dt_bias=gd_dt_bias[l], gd_norm=gd_norm[l])
        x = _ffn(x, ffn1_norm[l], ffn1_w1[l], ffn1_w3[l], ffn1_w2[l])
        aqkv, iq, sm, mslab, rslab, gslab, gate = _proj(x, mix_norm[l], _pad_w_in(w_in[l]), cos, sa, sb, _small_params(lp))

        oa_p = _dsa_prompt(iq, sm, aqkv, lp_)
        oa_s = _dsa_sample(iq[lp_:], sm[lp_:], aqkv[lp_:], cache_k, cache_v, cache_idx_k, page_table, l, db, t)
        om_p, c_p, n_p, m_p = _mlstm(mslab, sm, 0, 1, lp_, zeros(1, HEADS, HD, HD), zeros(1, HEADS, HD), zeros(1, HEADS), ml_norm[l])
        om_s, c_s, n_s, m_s = _mlstm(mslab, sm, lp_, db, t, state_mlstm_C[l], state_mlstm_n[l], state_mlstm_m[l], ml_norm[l])
        or_p, rs_p = _rwkv(rslab, 0, 1, lp_, zeros(1, RWKV_PROJ), zeros(1, HEADS, HD, HD), lp)
        or_s, rs_s = _rwkv(rslab, lp_, db, t, state_rwkv_shift[l], state_rwkv_S[l], lp)
        og_p, gs_p = _gdn(gslab, sm, 0, 1, lp_, zeros(1, CONV_W - 1, GDN_CONV_CH), zeros(1, HEADS, HD, HD), lp)
        og_s, gs_s = _gdn(gslab, sm, lp_, db, t, state_gdn_conv[l], state_gdn_S[l], lp)

        cat = lambda a, b: jnp.concatenate([a, b], axis=0)
        x = _merge(x, cat(oa_p, oa_s), cat(om_p, om_s), cat(or_p, or_s), cat(og_p, og_s), gate, w_branch[l], w_out[l])
        x = _ffn(x, ffn2_norm[l], ffn2_w1[l], ffn2_w3[l], ffn2_w2[l])

        k_all = aqkv[:, BR_W:2 * BR_W]
        v_all = aqkv[:, 2 * BR_W:3 * BR_W]
        ik_all = sm[:, SM_IK:SM_IK + IDX_DIM]
        gq = gslab[:, 0:GDN_CONV_CH]
        outs_p.append((k_all[:lp_].reshape(1, lp_, HEADS, HD), v_all[:lp_].reshape(1, lp_, HEADS, HD),
                       ik_all[:lp_].reshape(1, lp_, IDX_DIM), c_p, n_p, m_p, rs_p, rslab[lp_ - 1:lp_],
                       gs_p, gq[lp_ - (CONV_W - 1):lp_].reshape(1, CONV_W - 1, GDN_CONV_CH)))
        outs_s.append((k_all[lp_:].reshape(db, t, HEADS, HD), v_all[lp_:].reshape(db, t, HEADS, HD),
                       ik_all[lp_:].reshape(db, t, IDX_DIM), c_s, n_s, m_s, rs_s,
                       rslab[lp_:].reshape(db, t, RWKV_PROJ)[:, -1],
                       gs_s, gq[lp_:].reshape(db, t, GDN_CONV_CH)[:, t - (CONV_W - 1):]))

    y = _final_rms(x, final_norm)
    st_p = [jnp.stack(z) for z in zip(*outs_p)]
    st_s = [jnp.stack(z) for z in zip(*outs_s)]
    return (y[:lp_].reshape(1, lp_, d), y[lp_:].reshape(db, t, d), *st_p, *st_s)
```

```python
import functools
import math

import jax
import jax.numpy as jnp
import numpy as np
from jax import lax
from jax.experimental import pallas as pl
from jax.experimental.pallas import tpu as pltpu

F32 = jnp.float32
BF16 = jnp.bfloat16
I32 = jnp.int32
SDS = jax.ShapeDtypeStruct

HD = 64
HEADS = 4
BR_W = HEADS * HD
N_BRANCH = 4
IDX_HEADS = 8
IDX_DIM = 64
TOPK_MAX = 256
TOPM = 12
ROPE_THETA = 10000.0
MLSTM_CHUNK = 64
GDN_CHUNK = 64
CONV_W = 4
RWKV_W_LORA = 64
RWKV_A_LORA = 64
RWKV_G_LORA = 128
RWKV_PROJ = 3 * BR_W + RWKV_W_LORA + RWKV_A_LORA + RWKV_G_LORA
GDN_CONV_CH = 3 * BR_W
EPS = 1e-6
LN_EPS = 1e-5
PAGE_SIZE = 128

LANES = 128
SUBLANES = 8
VMEM_LIMIT_BYTES = 58 * 1024 * 1024

SM_IK = 0
SM_IW = 64
SM_MI = 72
SM_MF = 76
SM_GB = 80
SM_GA = 84
SM_END = 88

PC_AQKV = 0
PC_IQ = 768
PC_SM = 1280
PC_M = 1408
PC_R = 2432
PC_G = 3456
PC_GATE = 4480
PC_END = 8576

HIGHEST = lax.Precision.HIGHEST
INT_MIN = -2147483648
NEG_INF = float("-inf")


def _cp(sem):
    return pltpu.CompilerParams(dimension_semantics=sem, vmem_limit_bytes=VMEM_LIMIT_BYTES)


def _resident(shape):
    return pl.BlockSpec(shape, lambda *_: (0,) * len(shape), pipeline_mode=pl.Buffered(1))


def _rows(tm, w, off_blocks=0):
    return pl.BlockSpec((tm, w), lambda i, *_: (i + off_blocks, 0))


def _dot(a, b):
    return jnp.dot(a, b, preferred_element_type=F32)


def _dot_nt(a, b, precision=None):
    return lax.dot_general(a, b, (((1,), (1,)), ((), ())), preferred_element_type=F32, precision=precision)


def _dot_tn(a, b, precision=None):
    return lax.dot_general(a, b, (((0,), (0,)), ((), ())), preferred_element_type=F32, precision=precision)


def _hdot(a, b):
    return jnp.dot(a, b, preferred_element_type=F32, precision=HIGHEST)


def _rms(x, g):
    return x * lax.rsqrt(jnp.mean(x * x, axis=-1, keepdims=True) + EPS) * g


def _iota(shape, dim):
    return lax.broadcasted_iota(I32, shape, dim)


def _head_masks(rows):
    lane = _iota((rows, BR_W), 1)
    return [(lane >= HD * h) & (lane < HD * (h + 1)) for h in range(HEADS)]


def _seg_sum(x, masks):
    out = jnp.zeros_like(x)
    for m in masks:
        s = jnp.sum(jnp.where(m, x, 0.0), axis=1, keepdims=True)
        out = jnp.where(m, s, out)
    return out


def _col_to_row(col, n):
    eye = _iota((n, n), 0) == _iota((n, n), 1)
    return jnp.sum(jnp.where(eye, jnp.broadcast_to(col, (n, n)), 0.0), axis=0, keepdims=True)


def _row_to_col(row, n):
    eye = _iota((n, n), 0) == _iota((n, n), 1)
    return jnp.sum(jnp.where(eye, jnp.broadcast_to(row, (n, n)), 0.0), axis=1, keepdims=True)


def _cumsum_col(col, n):
    r = _iota((n, n), 0)
    c = _iota((n, n), 1)
    row = jnp.sum(jnp.where(r <= c, jnp.broadcast_to(col, (n, n)), 0.0), axis=0, keepdims=True)
    return row, _row_to_col(row, n)


def _ffn_kernel(x_ref, g_ref, w1_ref, w3_ref, w2_ref, o_ref):
    x = x_ref[...]
    h = _rms(x, g_ref[...]).astype(BF16)
    a = _dot(h, w1_ref[...])
    b = _dot(h, w3_ref[...])
    y = (a * jax.nn.sigmoid(a) * b).astype(BF16)
    o_ref[...] = x + 0.5 * _dot(y, w2_ref[...])


def _ffn(x, g, w1, w3, w2):
    na, d = x.shape
    f = w1.shape[1]
    tm = math.gcd(na, 256)
    return pl.pallas_call(
        _ffn_kernel,
        out_shape=SDS((na, d), F32),
        grid=(na // tm,),
        in_specs=[_rows(tm, d), _resident((1, d)), _resident((d, f)), _resident((d, f)), _resident((f, d))],
        out_specs=_rows(tm, d),
        compiler_params=_cp(("parallel",)),
        name="ffn",
    )(x, g.reshape(1, d), w1.astype(BF16), w3.astype(BF16), w2.astype(BF16))


def _rms_kernel(x_ref, g_ref, o_ref):
    o_ref[...] = _rms(x_ref[...], g_ref[...])


def _final_rms(x, g):
    na, d = x.shape
    tm = math.gcd(na, 512)
    return pl.pallas_call(
        _rms_kernel,
        out_shape=SDS((na, d), F32),
        grid=(na // tm,),
        in_specs=[_rows(tm, d), _resident((1, d))],
        out_specs=_rows(tm, d),
        compiler_params=_cp(("parallel",)),
        name="final_rms",
    )(x, g.reshape(1, d))


def _rope_slab(z, cos, sa, sb):
    return z * cos + pltpu.roll(z, LANES - HD // 2, 1) * sa + pltpu.roll(z, HD // 2, 1) * sb


def _proj_kernel(x_ref, g_ref, w_ref, cos_ref, sa_ref, sb_ref, sp_ref,
                 aqkv_ref, iq_ref, sm_ref, m_ref, r_ref, gq_ref, gate_ref):
    h = _rms(x_ref[...], g_ref[...]).astype(BF16)
    cos, sa, sb = cos_ref[...], sa_ref[...], sb_ref[...]

    def mm(a, b):
        return _dot(h, w_ref[:, a:b])

    za = mm(PC_AQKV, PC_IQ)
    for s in range(4):
        aqkv_ref[:, s * LANES:(s + 1) * LANES] = _rope_slab(za[:, s * LANES:(s + 1) * LANES], cos, sa, sb)
    aqkv_ref[:, 2 * BR_W:3 * BR_W] = za[:, 2 * BR_W:3 * BR_W]
    zi = mm(PC_IQ, PC_SM)
    for s in range(4):
        iq_ref[:, s * LANES:(s + 1) * LANES] = _rope_slab(zi[:, s * LANES:(s + 1) * LANES], cos, sa, sb)

    zs = mm(PC_SM, PC_M)
    lane = _iota(zs.shape, 1)
    gn, bias, alog = sp_ref[0:1, :], sp_ref[1:2, :], sp_ref[2:3, :]
    ms = jnp.sum(jnp.where(lane < IDX_DIM, zs * zs, 0.0), axis=1, keepdims=True) * (1.0 / IDX_DIM)
    ik = _rope_slab(zs * lax.rsqrt(ms + EPS) * gn, cos, sa, sb)
    t = zs + bias
    out = jnp.where(lane < SM_IW, ik,
          jnp.where(lane < SM_MI, zs * (IDX_HEADS ** -0.5),
          jnp.where(lane < SM_MF, t,
          jnp.where(lane < SM_GB, jax.nn.log_sigmoid(t),
          jnp.where(lane < SM_GA, jax.nn.sigmoid(zs),
          jnp.where(lane < SM_END, -jnp.exp(alog) * jax.nn.softplus(t), 0.0))))))
    sm_ref[...] = out

    m_ref[...] = mm(PC_M, PC_R)
    r_ref[...] = mm(PC_R, PC_G)
    gq_ref[...] = mm(PC_G, PC_GATE)
    gate_ref[...] = mm(PC_GATE, PC_END)


def _proj(x, g, w_pad, cos, sa, sb, sp):
    na, d = x.shape
    tm = math.gcd(na, 256)
    widths = (3 * BR_W, IDX_HEADS * IDX_DIM, LANES, 4 * BR_W, RWKV_PROJ, 4 * BR_W, N_BRANCH * d)
    return pl.pallas_call(
        _proj_kernel,
        out_shape=tuple(SDS((na, w), F32) for w in widths),
        grid=(na // tm,),
        in_specs=[_rows(tm, d), _resident((1, d)), _resident((d, PC_END)),
                  _rows(tm, LANES), _rows(tm, LANES), _rows(tm, LANES), _resident((SUBLANES, LANES))],
        out_specs=tuple(_rows(tm, w) for w in widths),
        compiler_params=_cp(("parallel",)),
        name="proj_in",
    )(x, g.reshape(1, d), w_pad, cos, sa, sb, sp)


def _merge_kernel(x_ref, oa_ref, om_ref, or_ref, og_ref, gate_ref, wb_ref, wo_ref, o_ref):
    d = x_ref.shape[1]
    mix = None
    for n, br in enumerate((oa_ref, om_ref, or_ref, og_ref)):
        up = _dot(br[...].astype(BF16), wb_ref[n])
        t = jax.nn.sigmoid(gate_ref[:, n * d:(n + 1) * d]) * up
        mix = t if mix is None else mix + t
    o_ref[...] = x_ref[...] + _dot(mix.astype(BF16), wo_ref[...])


def _merge(x, oa, om, orr, og, gate, w_branch, w_out):
    na, d = x.shape
    tm = math.gcd(na, 256)
    return pl.pallas_call(
        _merge_kernel,
        out_shape=SDS((na, d), F32),
        grid=(na // tm,),
        in_specs=[_rows(tm, d)] + [_rows(tm, BR_W)] * 4 + [_rows(tm, N_BRANCH * d),
                  _resident((N_BRANCH, BR_W, d)), _resident((d, d))],
        out_specs=_rows(tm, d),
        compiler_params=_cp(("parallel",)),
        name="merge",
    )(x, oa, om, orr, og, gate, w_branch.astype(BF16), w_out.astype(BF16))


def _idx_queries(qi, sm, tq):
    qrows = jnp.concatenate([qi[:, h * IDX_DIM:(h + 1) * IDX_DIM] for h in range(IDX_HEADS)], axis=0).astype(BF16)
    wb = [jnp.broadcast_to(sm[:, SM_IW + h:SM_IW + h + 1] * (IDX_DIM ** -0.5), (tq, LANES)) for h in range(IDX_HEADS)]
    return qrows, wb


def _idx_scores(qrows, wb, kib, tq, tk):
    dots = _dot_nt(qrows, kib)
    cols = []
    for j in range(tk // LANES):
        acc = None
        for h in range(IDX_HEADS):
            t = jnp.maximum(dots[h * tq:(h + 1) * tq, j * LANES:(j + 1) * LANES], 0.0) * wb[h]
            acc = t if acc is None else acc + t
        cols.append(acc)
    return cols[0] if len(cols) == 1 else jnp.concatenate(cols, axis=1)


def _to_key(sc):
    b = pltpu.bitcast(jnp.where(sc == 0.0, 0.0, sc), I32)
    return b ^ ((b >> 31) & I32(0x7FFFFFFF))


def _key_to_f32(key):
    return pltpu.bitcast(key ^ ((key >> 31) & I32(0x7FFFFFFF)), F32)


def _kth_largest(count, rows, n_sel, pos_bits, tkey_ref, cut_ref):
    int_min = I32(INT_MIN)

    def bit_step(it, ub):
        cand = ub | lax.shift_left(I32(1), 31 - it)
        cs = cand ^ int_min
        return jnp.where(count(lambda k, p: k >= cs) >= n_sel, cand, ub)

    ub = lax.fori_loop(0, 32, bit_step, jnp.zeros((rows, LANES), I32))
    tkey = ub ^ int_min
    n_gt = count(lambda k, p: k > tkey)
    n_ge = count(lambda k, p: k >= tkey)
    need = n_sel - n_gt
    ambiguous = (n_ge - n_gt) > need
    tkey_ref[...] = tkey
    cut_ref[...] = jnp.full((rows, LANES), 1 << pos_bits, I32)

    @pl.when(jnp.max(jnp.where(ambiguous, 1.0, 0.0)) > 0.0)
    def _():
        def pos_step(it, q):
            cand = q | lax.shift_left(I32(1), pos_bits - 1 - it)
            below = count(lambda k, p: (k == tkey) & (p < cand))
            return jnp.where(below < need, cand, q)

        q = lax.fori_loop(0, pos_bits, pos_step, jnp.zeros((rows, LANES), I32))
        cut_ref[...] = jnp.where(ambiguous, q + 1, 1 << pos_bits)


def _flash_update(s, v, m_s, l_s, acc_s, v_transposed=False):
    m_prev = m_s[...]
    m_new = jnp.maximum(m_prev, jnp.max(s, axis=1, keepdims=True))
    m_safe = jnp.where(m_new == NEG_INF, 0.0, m_new)
    alpha = jnp.exp(m_prev - m_safe)
    p = jnp.exp(s - m_safe)
    l_s[...] = alpha * l_s[...] + jnp.sum(p, axis=1, keepdims=True)
    m_s[...] = m_new
    pv = _dot_nt(p.astype(BF16), v) if v_transposed else _dot(p.astype(BF16), v)
    acc_s[...] = alpha * acc_s[...] + pv


def _stack_heads(q, t):
    masks = _head_masks(t)
    return jnp.concatenate([jnp.where(m, q * (HD ** -0.5), 0.0) for m in masks], axis=0).astype(BF16)


def _unstack_heads(acc, l, t):
    masks = _head_masks(t)
    out = jnp.zeros((t, BR_W), F32)
    for h, m in enumerate(masks):
        out = jnp.where(m, acc[h * t:(h + 1) * t] / l[h * t:(h + 1) * t], out)
    return out


def _dsa_prompt_kernel(qi_ref, smq_ref, q_ref, ki_ref, k_ref, v_ref, o_ref,
                       sc_s, cand_s, tkey_s, cut_s, m_s, l_s, acc_s, *, tq, tk, pf, n_sel, pos_bits):
    i = pl.program_id(0)
    q0 = i * tq
    nch = (q0 + tq - 1) // tk + 1
    qrows, wb = _idx_queries(qi_ref[...], smq_ref[...], tq)
    qpos = q0 + _iota((tq, tk), 0)

    def fill(c, carry):
        k0 = pl.multiple_of(c * tk, tk)
        sc = _idx_scores(qrows, wb, ki_ref[pl.ds(k0, tk), :], tq, tk)
        sc = jnp.where(k0 + _iota((tq, tk), 1) <= qpos, sc, NEG_INF)
        sc_s[c] = jnp.where(sc == 0.0, 0.0, sc)
        return carry

    ngrp = (nch + pf - 1) // pf
    lax.fori_loop(0, ngrp * pf, fill, 0)
    lane = _iota((tq, LANES), 1)

    def count(pred):
        def body(c, acc):
            for j in range(tk // LANES):
                p = c * tk + j * LANES + lane
                acc = acc + jnp.where(pred(_to_key(sc_s[c, :, j * LANES:(j + 1) * LANES]), p), 1.0, 0.0)
            return acc

        acc = lax.fori_loop(0, nch, body, jnp.zeros((tq, LANES), F32))
        return jnp.broadcast_to(jnp.sum(acc, axis=1, keepdims=True), (tq, LANES))

    rg = 2 * SUBLANES
    for g in range(tq // rg):
        def insert(c, best, g=g):
            for j in range(tk // LANES):
                x = sc_s[c, g * rg:(g + 1) * rg, j * LANES:(j + 1) * LANES]
                nxt = []
                for b in best:
                    nxt.append(jnp.maximum(b, x))
                    x = jnp.minimum(b, x)
                best = tuple(nxt)
            return best

        best = lax.fori_loop(0, nch, insert, tuple(jnp.full((rg, LANES), NEG_INF, F32) for _ in range(TOPM)))
        for u in range(TOPM):
            cand_s[u, g * rg:(g + 1) * rg, :] = _to_key(best[u])

    def count_cand(pred):
        acc = jnp.zeros((tq, LANES), F32)
        for u in range(TOPM):
            acc = acc + jnp.where(pred(cand_s[u]), 1.0, 0.0)
        return jnp.broadcast_to(jnp.sum(acc, axis=1, keepdims=True), (tq, LANES))

    int_min = I32(INT_MIN)

    def bit_step(it, ub):
        cand = ub | lax.shift_left(I32(1), 31 - it)
        cs = cand ^ int_min
        return jnp.where(count_cand(lambda k: k >= cs) >= n_sel, cand, ub)

    tk_c = lax.fori_loop(0, 32, bit_step, jnp.zeros((tq, LANES), I32)) ^ int_min
    n_gt = count_cand(lambda k: k > tk_c)
    n_ge = count_cand(lambda k: k >= tk_c)
    dropped = (cand_s[TOPM - 1] >= tk_c) & (nch * (tk // LANES) > TOPM)
    redo = dropped | ((n_ge - n_gt) > (n_sel - n_gt))
    tkey_s[...] = tk_c
    cut_s[...] = jnp.full((tq, LANES), 1 << pos_bits, I32)

    @pl.when(jnp.max(jnp.where(redo, 1.0, 0.0)) > 0.0)
    def _():
        _kth_largest(count, tq, n_sel, pos_bits, tkey_s, cut_s)

    neg_key = _to_key(jnp.full((tq, LANES), NEG_INF, F32))
    qend = q0 + _iota((tq, LANES), 0) + 1
    cut_s[...] = jnp.where(tkey_s[...] == neg_key, jnp.minimum(cut_s[...], qend), cut_s[...])

    thr = _key_to_f32(tkey_s[...])[:, 0:1]
    cut = cut_s[:, 0:1]
    qbd = _stack_heads(q_ref[:, 0:BR_W], tq)
    m_s[...] = jnp.full(m_s.shape, NEG_INF, F32)
    l_s[...] = jnp.zeros(l_s.shape, F32)
    acc_s[...] = jnp.zeros(acc_s.shape, F32)

    def attend(g, carry):
        k0 = pl.multiple_of(g * (pf * tk), pf * tk)
        sc = jnp.concatenate([sc_s[g * pf + u] for u in range(pf)], axis=1) if pf > 1 else sc_s[g]
        sel = (sc > thr) | ((sc == thr) & (k0 + _iota((tq, pf * tk), 1) < cut))
        bias = jnp.where(sel, 0.0, NEG_INF)
        s = _dot_nt(qbd, k_ref[pl.ds(k0, pf * tk), :]) + jnp.concatenate([bias] * HEADS, axis=0)
        _flash_update(s, v_ref[pl.ds(k0, pf * tk), :], m_s, l_s, acc_s)
        return carry

    lax.fori_loop(0, ngrp, attend, 0)
    o_ref[...] = _unstack_heads(acc_s[...], l_s[...], tq)


def _dsa_prompt(iq, sm, aqkv, lp):
    tq = min(128, lp)
    tk = min(512, lp)
    n_sel = min(TOPK_MAX, lp // 4)
    pos_bits = int(lp).bit_length()
    pf = 2 if (lp // tk) % 2 == 0 else 1
    ki = sm[:lp, SM_IK:SM_IK + IDX_DIM].astype(BF16)
    kb = aqkv[:lp, BR_W:2 * BR_W].astype(BF16)
    vb = aqkv[:lp, 2 * BR_W:3 * BR_W].astype(BF16)
    return pl.pallas_call(
        functools.partial(_dsa_prompt_kernel, tq=tq, tk=tk, pf=pf, n_sel=n_sel, pos_bits=pos_bits),
        out_shape=SDS((lp, BR_W), F32),
        grid=(lp // tq,),
        in_specs=[_rows(tq, IDX_HEADS * IDX_DIM), _rows(tq, LANES), _rows(tq, 3 * BR_W),
                  _resident((lp, IDX_DIM)), _resident((lp, BR_W)), _resident((lp, BR_W))],
        out_specs=_rows(tq, BR_W),
        scratch_shapes=[pltpu.VMEM((lp // tk, tq, tk), F32), pltpu.VMEM((TOPM, tq, LANES), I32),
                        pltpu.VMEM((tq, LANES), I32), pltpu.VMEM((tq, LANES), I32),
                        pltpu.VMEM((HEADS * tq, 1), F32), pltpu.VMEM((HEADS * tq, 1), F32),
                        pltpu.VMEM((HEADS * tq, BR_W), F32)],
        compiler_params=_cp(("arbitrary",)),
        name="dsa_prompt",
    )(iq, sm, aqkv, ki, kb, vb)


def _dsa_sample_score_kernel(pt_ref, qr_ref, wb_ref, *refs, t, pg, ng, past, n_sel, pos_bits):
    pages = refs[:pg]
    knew_ref, scp_ref, scn_ref, thr_ref, cut_ref, keys_s, keyn_s, tkey_s = refs[pg:]
    j = pl.program_id(1)
    qrows = qr_ref[...]
    wb = wb_ref[...] * (IDX_DIM ** -0.5)

    def reduce_heads(dots):
        w = jnp.maximum(dots, 0.0) * wb
        s = w[0:t]
        for h in range(1, IDX_HEADS):
            s = s + w[h * t:(h + 1) * t]
        return s

    sc = jnp.concatenate([reduce_heads(_dot(qrows, p[...].astype(BF16))) for p in pages], axis=1)
    scp_ref[...] = sc
    keys_s[j] = _to_key(sc)

    @pl.when(j == ng - 1)
    def _():
        lane = _iota((t, LANES), 1)
        sn = reduce_heads(_dot_nt(qrows, knew_ref[...].astype(BF16)))
        sn = jnp.where(lane <= _iota((t, LANES), 0), sn, NEG_INF)
        scn_ref[...] = sn
        keyn_s[...] = _to_key(sn)

        def count(pred):
            def body(c, acc):
                for jj in range(pg):
                    p = c * (pg * LANES) + jj * LANES + lane
                    acc = acc + jnp.where(pred(keys_s[c, :, jj * LANES:(jj + 1) * LANES], p), 1.0, 0.0)
                return acc

            acc = lax.fori_loop(0, ng, body, jnp.zeros((t, LANES), F32))
            acc = acc + jnp.where(pred(keyn_s[...], past + lane), 1.0, 0.0)
            return jnp.broadcast_to(jnp.sum(acc, axis=1, keepdims=True), (t, LANES))

        _kth_largest(count, t, n_sel, pos_bits, tkey_s, cut_ref)
        thr_ref[...] = _key_to_f32(tkey_s[...])


def _dsa_sample_attn_kernel(pt_ref, q_ref, *refs, t, pg, ng, past):
    kpages = refs[:pg]
    vpages = refs[pg:2 * pg]
    (scp_ref, scn_ref, thr_ref, cut_ref, knew_ref, vnew_ref, o_ref, m_s, l_s, acc_s) = refs[2 * pg:]
    j = pl.program_id(1)

    @pl.when(j == 0)
    def _():
        m_s[...] = jnp.full(m_s.shape, NEG_INF, F32)
        l_s[...] = jnp.zeros(l_s.shape, F32)
        acc_s[...] = jnp.zeros(acc_s.shape, F32)

    qbd = _stack_heads(q_ref[...], t)
    thr = thr_ref[:, 0:1]
    cut = cut_ref[:, 0:1]
    w = pg * LANES
    sc = scp_ref[...]
    kidx = j * w + _iota((t, w), 1)
    bias = jnp.where((sc > thr) | ((sc == thr) & (kidx < cut)), 0.0, NEG_INF)
    kcat = jnp.concatenate([p[...].reshape(BR_W, PAGE_SIZE).astype(BF16) for p in kpages], axis=1)
    vcat = jnp.concatenate([p[...].reshape(BR_W, PAGE_SIZE).astype(BF16) for p in vpages], axis=1)
    s = _dot(qbd, kcat) + jnp.concatenate([bias] * HEADS, axis=0)
    _flash_update(s, vcat, m_s, l_s, acc_s, v_transposed=True)

    @pl.when(j == ng - 1)
    def _():
        lane = _iota((t, LANES), 1)
        scn = scn_ref[...]
        seln = ((scn > thr) | ((scn == thr) & (past + lane < cut))) & (lane <= _iota((t, LANES), 0))
        biasn = jnp.where(seln, 0.0, NEG_INF)
        sn = _dot_nt(qbd, knew_ref[...].astype(BF16)) + jnp.concatenate([biasn] * HEADS, axis=0)
        _flash_update(sn, vnew_ref[...].astype(BF16), m_s, l_s, acc_s)
        o_ref[...] = _unstack_heads(acc_s[...], l_s[...], t)


def _dsa_sample(iq_s, sm_s, aqkv_s, cache_k, cache_v, cache_idx_k, page_table, layer, db, t):
    n_pages = page_table.shape[1]
    past = n_pages * PAGE_SIZE
    n_sel = min(TOPK_MAX, (past + t) // 4)
    pos_bits = int(past + LANES).bit_length()
    pg = math.gcd(n_pages, 8)
    ng = n_pages // pg
    ckt = cache_k.transpose(0, 1, 3, 4, 2)
    cvt = cache_v.transpose(0, 1, 3, 4, 2)
    cit = cache_idx_k.transpose(0, 1, 3, 2)

    qrows = iq_s.reshape(db, t, IDX_HEADS, IDX_DIM).transpose(0, 2, 1, 3).reshape(db, IDX_HEADS * t, IDX_DIM).astype(BF16)
    wcol = sm_s[:, SM_IW:SM_IW + IDX_HEADS].reshape(db, t, IDX_HEADS).transpose(0, 2, 1).reshape(db, IDX_HEADS * t, 1)
    wbs = jnp.broadcast_to(wcol, (db, IDX_HEADS * t, LANES))
    pad = lambda a: jnp.pad(a.reshape(db, t, a.shape[-1]), ((0, 0), (0, LANES - t), (0, 0)))
    ki_new = pad(sm_s[:, SM_IK:SM_IK + IDX_DIM])
    k_new = pad(aqkv_s[:, BR_W:2 * BR_W])
    v_new = pad(aqkv_s[:, 2 * BR_W:3 * BR_W])
    q3 = aqkv_s[:, 0:BR_W].reshape(db, t, BR_W)

    def idx_page(i):
        return pl.BlockSpec((None, None, IDX_DIM, PAGE_SIZE), lambda b, j, pt: (layer, pt[b, j * pg + i], 0, 0))

    def kv_page(i):
        return pl.BlockSpec((None, None, HEADS, HD, PAGE_SIZE), lambda b, j, pt: (layer, pt[b, j * pg + i], 0, 0, 0))

    per_b = lambda r, w: pl.BlockSpec((None, r, w), lambda b, j, pt: (b, 0, 0))
    sc_spec = pl.BlockSpec((None, t, pg * LANES), lambda b, j, pt: (b, 0, j))

    scp, scn, thr, cut = pl.pallas_call(
        functools.partial(_dsa_sample_score_kernel, t=t, pg=pg, ng=ng, past=past, n_sel=n_sel, pos_bits=pos_bits),
        out_shape=(SDS((db, t, past), F32), SDS((db, t, LANES), F32), SDS((db, t, LANES), F32), SDS((db, t, LANES), I32)),
        grid_spec=pltpu.PrefetchScalarGridSpec(
            num_scalar_prefetch=1, grid=(db, ng),
            in_specs=[per_b(IDX_HEADS * t, IDX_DIM), per_b(IDX_HEADS * t, LANES)]
                     + [idx_page(i) for i in range(pg)] + [per_b(LANES, IDX_DIM)],
            out_specs=(sc_spec, per_b(t, LANES), per_b(t, LANES), per_b(t, LANES)),
            scratch_shapes=[pltpu.VMEM((ng, t, pg * LANES), I32), pltpu.VMEM((t, LANES), I32), pltpu.VMEM((t, LANES), I32)]),
        compiler_params=_cp(("arbitrary", "arbitrary")),
        name="dsa_sample_scores",
    )(page_table, qrows, wbs, *([cit] * pg), ki_new)

    out = pl.pallas_call(
        functools.partial(_dsa_sample_attn_kernel, t=t, pg=pg, ng=ng, past=past),
        out_shape=SDS((db, t, BR_W), F32),
        grid_spec=pltpu.PrefetchScalarGridSpec(
            num_scalar_prefetch=1, grid=(db, ng),
            in_specs=[per_b(t, BR_W)] + [kv_page(i) for i in range(pg)] * 2
                     + [sc_spec, per_b(t, LANES), per_b(t, LANES), per_b(t, LANES), per_b(LANES, BR_W), per_b(LANES, BR_W)],
            out_specs=per_b(t, BR_W),
            scratch_shapes=[pltpu.VMEM((HEADS * t, 1), F32), pltpu.VMEM((HEADS * t, 1), F32),
                            pltpu.VMEM((HEADS * t, BR_W), F32)]),
        compiler_params=_cp(("arbitrary", "arbitrary")),
        name="dsa_sample_attention",
    )(page_table, q3, *([ckt] * pg), *([cvt] * pg), scp, scn, thr, cut, k_new, v_new)
    return out.reshape(db * t, BR_W)


def _mlstm_chunk(q, k, v, li, lf, cst, n, m, c):
    r = _iota((c, c), 0)
    cc = _iota((c, c), 1)
    hh = range(len(q))
    qk = [_dot_nt(q[h], k[h], HIGHEST) for h in hh]
    qc = [_dot_nt(q[h], cst[h], HIGHEST) for h in hh]
    b_col, mt, ei, s = [], [], [], []
    for h in hh:
        b_row, bc_ = _cumsum_col(lf[h], c)
        log_d = jnp.where(cc <= r, bc_ - b_row + _col_to_row(li[h], c), NEG_INF)
        inter = bc_ + m[h]
        mt_ = jnp.maximum(inter, jnp.max(log_d, axis=1, keepdims=True))
        b_col.append(bc_)
        mt.append(mt_)
        ei.append(jnp.exp(inter - mt_))
        s.append(qk[h] * jnp.exp(log_d - mt_))
    sv = [_hdot(s[h], v[h]) for h in hh]
    outs, c_new, n_new, m_new, wsv = [], [], [], [], []
    for h in hh:
        den = jnp.sum(s[h], axis=1, keepdims=True) + ei[h] * jnp.sum(q[h] * n[h], axis=1, keepdims=True)
        outs.append((sv[h] + ei[h] * qc[h]) / jnp.maximum(jnp.abs(den), jnp.exp(-mt[h])))
        bc = b_col[h][c - 1:c, :]
        gs = bc - b_col[h] + li[h]
        mn = jnp.maximum(bc + m[h], jnp.max(gs, axis=0, keepdims=True))
        ws = jnp.exp(gs - mn)
        dec = jnp.exp(bc + m[h] - mn)
        wsv.append(ws * v[h])
        c_new.append(dec * cst[h])
        n_new.append(dec * n[h] + jnp.sum(ws * k[h], axis=0, keepdims=True))
        m_new.append(mn)
    c_new = [c_new[h] + _dot_tn(wsv[h], k[h], HIGHEST) for h in hh]
    return outs, c_new, n_new, m_new


def _mlstm_kernel(x_ref, sm_ref, c0_ref, n0_ref, m0_ref, lng_ref, o_ref, c1_ref, n1_ref, m1_ref,
                  c_s, n_s, m_s, *, c, cb):
    i = pl.program_id(1)

    @pl.when(i == 0)
    def _():
        c_s[...] = c0_ref[...]
        n_s[...] = n0_ref[...]
        m_s[...] = m0_ref[...]

    def chunk(ci, carry):
        r0 = pl.multiple_of(ci * c, c)
        hh = range(HEADS)
        col = lambda base, h: x_ref[pl.ds(r0, c), base + HD * h:base + HD * (h + 1)]
        hs, c_new, n_new, m_new = _mlstm_chunk(
            [col(0, h) for h in hh], [col(BR_W, h) * (HD ** -0.5) for h in hh], [col(2 * BR_W, h) for h in hh],
            [sm_ref[pl.ds(r0, c), SM_MI + h:SM_MI + h + 1] for h in hh],
            [sm_ref[pl.ds(r0, c), SM_MF + h:SM_MF + h + 1] for h in hh],
            [c_s[h] for h in hh], [n_s[h] for h in hh], [m_s[h] for h in hh], c)
        outs = []
        for h in hh:
            c_s[h] = c_new[h]
            n_s[h] = n_new[h]
            m_s[h] = m_new[h]
            mu = jnp.mean(hs[h], axis=1, keepdims=True)
            var = jnp.mean(jnp.square(hs[h] - mu), axis=1, keepdims=True)
            y = (hs[h] - mu) * lax.rsqrt(var + LN_EPS) * lng_ref[:, HD * h:HD * (h + 1)]
            outs.append(y * jax.nn.sigmoid(col(3 * BR_W, h)))
        o_ref[pl.ds(r0, c), :] = jnp.concatenate(outs, axis=1)
        return carry

    lax.fori_loop(0, cb, chunk, 0)

    @pl.when(i == pl.num_programs(1) - 1)
    def _():
        c1_ref[...] = c_s[...]
        n1_ref[...] = n_s[...]
        m1_ref[...] = m_s[...]


def _seq_blocks(l, chunk, max_rows):
    nc = l // chunk
    cb = math.gcd(nc, max(1, max_rows // chunk))
    return chunk * cb, cb, nc // cb


def _state_spec(shape):
    return pl.BlockSpec((None,) + shape, lambda b, i: (b,) + (0,) * len(shape))


def _mlstm(mslab, sm, row_off, b, l, c0, n0, m0, ln_g):
    c = math.gcd(l, MLSTM_CHUNK)
    rows, cb, nblk = _seq_blocks(l, c, 512)
    off = row_off // rows
    seq = lambda w: pl.BlockSpec((rows, w), lambda bb, i: (off + bb * nblk + i, 0))
    o, c1, n1, m1 = pl.pallas_call(
        functools.partial(_mlstm_kernel, c=c, cb=cb),
        out_shape=(SDS((b * l, BR_W), F32), SDS((b, HEADS, HD, HD), F32), SDS((b, HEADS, 1, HD), F32),
                   SDS((b, HEADS, 1, 1), F32)),
        grid=(b, nblk),
        in_specs=[seq(4 * BR_W), seq(LANES), _state_spec((HEADS, HD, HD)), _state_spec((HEADS, 1, HD)),
                  _state_spec((HEADS, 1, 1)), pl.BlockSpec((1, BR_W), lambda bb, i: (0, 0))],
        out_specs=(pl.BlockSpec((rows, BR_W), lambda bb, i: (bb * nblk + i, 0)), _state_spec((HEADS, HD, HD)),
                   _state_spec((HEADS, 1, HD)), _state_spec((HEADS, 1, 1))),
        scratch_shapes=[pltpu.VMEM((HEADS, HD, HD), F32), pltpu.VMEM((HEADS, 1, HD), F32), pltpu.VMEM((HEADS, 1, 1), F32)],
        compiler_params=_cp(("parallel", "arbitrary")),
        name="mlstm",
    )(mslab, sm, c0, n0.reshape(b, HEADS, 1, HD), m0.reshape(b, HEADS, 1, 1), ln_g.reshape(1, BR_W))
    return o, c1, n1.reshape(b, HEADS, HD), m1.reshape(b, HEADS)


def _shift_rows(x, halo, s):
    rows = x.shape[0]
    rolled = pltpu.roll(x, s, 0)
    first = jnp.where(_iota((SUBLANES, x.shape[1]), 0) < s, pltpu.roll(halo, s, 0), rolled[0:SUBLANES])
    return first if rows == SUBLANES else jnp.concatenate([first, rolled[SUBLANES:]], axis=0)


def _halo_specs(rows, w, off_rows, nt):
    hb = rows // SUBLANES
    tile = pl.BlockSpec((rows, w), lambda b, i: (off_rows // rows + b * nt + i, 0))
    halo = pl.BlockSpec((SUBLANES, w), lambda b, i: (jnp.maximum(off_rows // SUBLANES + (b * nt + i) * hb - 1, 0), 0))
    state = pl.BlockSpec((None, SUBLANES, w), lambda b, i: (b, 0, 0))
    return tile, halo, state


def _pad_state_rows(st, w):
    b, r, w0 = st.shape
    return jnp.pad(st, ((0, 0), (SUBLANES - r, 0), (0, w - w0)))


def _cumsum_rows(x):
    row = _iota(x.shape, 0)
    sh = 1
    while sh < x.shape[0]:
        x = x + jnp.where(row >= sh, pltpu.roll(x, sh, 0), 0.0)
        sh *= 2
    return x


def _unit_lower_inverse(ms, c):
    r = _iota((c, c), 0)
    cc = _iota((c, c), 1)
    fold = jnp.where((_iota((c, SUBLANES), 0) & 7) == _iota((c, SUBLANES), 1), 1.0, 0.0)
    same8 = (r >> 3) == (cc >> 3)
    m8s = [_hdot(jnp.where(same8, m, 0.0), fold) for m in ms]
    zs = [jnp.where(r == cc, 1.0, 0.0)] * len(ms)

    def pivot(z, s):
        return jnp.broadcast_to(z.reshape(c // SUBLANES, SUBLANES, c)[:, s:s + 1, :],
                                (c // SUBLANES, SUBLANES, c)).reshape(c, c)

    for s in range(SUBLANES - 1):
        zs = [z + m8[:, s:s + 1] * pivot(z, s) for z, m8 in zip(zs, m8s)]
    sh = 3
    while (1 << sh) < c:
        off = ((r >> (sh + 1)) == (cc >> (sh + 1))) & ((r >> sh) != (cc >> sh))
        ts = [_hdot(z, jnp.where(off, m, 0.0)) for z, m in zip(zs, ms)]
        zs = [z + _hdot(t, z) for z, t in zip(zs, ts)]
        sh += 1
    return zs


def _lin_scan(a_ref, y0_ref, phi_ref, psi_ref, s_s, y_s, c, cb):
    def chunk(ci, carry):
        r0 = pl.multiple_of(ci * c, c)
        sts = [s_s[h] for h in range(HEADS)]
        new = [_hdot(sts[h], phi_ref[ci, h]) for h in range(HEADS)]
        ys = [_dot_nt(a_ref[pl.ds(r0, c), HD * h:HD * (h + 1)], sts[h], HIGHEST) for h in range(HEADS)]
        for h in range(HEADS):
            s_s[h] = new[h] + psi_ref[ci, h]
        y_s[pl.ds(r0, c), :] = jnp.concatenate(ys, axis=1) + y0_ref[pl.ds(r0, c), :]
        return carry

    lax.fori_loop(0, cb, chunk, 0)


def _chunk_specs(rows, cb, nblk, widths):
    row_specs = [pl.BlockSpec((rows, w), lambda bb, i: (bb * nblk + i, 0)) for w in widths]
    mat = pl.BlockSpec((cb, HEADS, HD, HD), lambda bb, i: (bb * nblk + i, 0, 0, 0))
    return row_specs, mat


def _rwkv_prep_kernel(p_ref, halo_ref, st_ref, vec_ref, w2_ref, a2_ref, g2_ref, rs_ref, post_ref):
    i = pl.program_id(1)
    p = p_ref[...]
    rows = p.shape[0]
    halo = jnp.where(i == 0, st_ref[...], halo_ref[...])
    prev = _shift_rows(p, halo, 1)
    mu = jnp.concatenate([vec_ref[0:1, :], vec_ref[1:2, :], vec_ref[2:3, :], vec_ref[3:4, :]], axis=1)
    pm = p + (prev - p) * mu
    r = pm[:, 0:BR_W]
    k = pm[:, BR_W:2 * BR_W]
    v = pm[:, 2 * BR_W:3 * BR_W]
    o = 3 * BR_W
    wd = pm[:, o:o + RWKV_W_LORA]
    ad = pm[:, o + RWKV_W_LORA:o + RWKV_W_LORA + RWKV_A_LORA]
    gd = pm[:, o + RWKV_W_LORA + RWKV_A_LORA:RWKV_PROJ]
    w0, a0, kkp, ka, rk = (vec_ref[4:5, :], vec_ref[5:6, :], vec_ref[6:7, :], vec_ref[7:8, :], vec_ref[8:9, :])
    wlog = -jax.nn.softplus(-(w0 + _hdot(jnp.tanh(wd), w2_ref[...]))) - 0.5
    log_decay = -jnp.exp(wlog)
    a = jax.nn.sigmoid(a0 + _hdot(ad, a2_ref[...]))
    g = _hdot(jax.nn.sigmoid(gd), g2_ref[...])
    masks = _head_masks(rows)
    kkr = k * kkp
    kk = kkr * lax.rsqrt(jnp.maximum(_seg_sum(kkr * kkr, masks), 1e-12))
    k2 = k * (1.0 + (a - 1.0) * ka)
    bonus = _seg_sum(r * k2 * rk, masks) * v
    for n, t in enumerate((r, log_decay, k2, v, kk, kk * a)):
        rs_ref[:, n * BR_W:(n + 1) * BR_W] = t
    post_ref[:, 0:BR_W] = bonus
    post_ref[:, BR_W:2 * BR_W] = g


def _rwkv_chunk_kernel(rs_ref, rh_ref, y0_ref, phi_ref, psi_ref, *, c, cb, gc):
    r_i = _iota((c, c), 0)
    c_i = _iota((c, c), 1)
    eye = _iota((HD, HD), 0) == _iota((HD, HD), 1)
    jobs = [(g, h) for g in range(gc) for h in range(HEADS)]
    hs = lambda x, h: x[:, HD * h:HD * (h + 1)]

    def group(gi, carry):
        r0s = [pl.multiple_of((gi * gc + g) * c, c) for g in range(gc)]
        w = []
        for r0 in r0s:
            r, lw, k, v, kk, kka = (rs_ref[pl.ds(r0, c), n * BR_W:(n + 1) * BR_W] for n in range(6))
            cum = _cumsum_rows(lw)
            pc = cum[c - 1:c, :]
            inv = jnp.exp(-cum)
            rest = jnp.exp(pc - cum)
            w.append(dict(at=-kk * jnp.exp(cum - lw),
                          bt=kka * inv, kt=k * inv,
                          rt=r * jnp.exp(cum), bd=kka * rest, kd=k * rest, v=v, epc=jnp.exp(pc)))
        big = [_dot_nt(jnp.concatenate([hs(w[g]["at"], h), hs(w[g]["rt"], h)], axis=0),
                       jnp.concatenate([hs(w[g]["bt"], h), hs(w[g]["kt"], h)], axis=0), HIGHEST) for g, h in jobs]
        mab = [jnp.where(c_i < r_i, x[0:c, 0:c], 0.0) for x in big]
        mak = [jnp.where(c_i < r_i, x[0:c, c:2 * c], 0.0) for x in big]
        nn = [jnp.concatenate([jnp.where(c_i <= r_i, x[c:2 * c, 0:c], 0.0),
                               jnp.where(c_i <= r_i, x[c:2 * c, c:2 * c], 0.0)], axis=1) for x in big]
        tinv = _unit_lower_inverse(mab, c)
        mv = [_hdot(m, hs(w[g]["v"], h)) for m, (g, h) in zip(mak, jobs)]
        hg = [_hdot(t, jnp.concatenate([hs(w[g]["at"], h), x], axis=1)) for t, x, (g, h) in zip(tinv, mv, jobs)]
        out = [_hdot(n2, jnp.concatenate([x, jnp.concatenate([jnp.zeros((c, HD), F32), hs(w[g]["v"], h)], axis=1)], axis=0))
               for n2, x, (g, h) in zip(nn, hg, jobs)]
        hb = [_dot_tn(x, hs(w[g]["bd"], h), HIGHEST) for x, (g, h) in zip(hg, jobs)]
        vk = [_dot_tn(hs(w[g]["v"], h), hs(w[g]["kd"], h), HIGHEST) for g, h in jobs]
        for g in range(gc):
            sel = [n for n, (gg, _) in enumerate(jobs) if gg == g]
            rh_ref[pl.ds(r0s[g], c), :] = w[g]["rt"] + jnp.concatenate([out[n][:, 0:HD] for n in sel], axis=1)
            y0_ref[pl.ds(r0s[g], c), :] = jnp.concatenate([out[n][:, HD:2 * HD] for n in sel], axis=1)
            phi_ref[gi * gc + g] = jnp.stack([jnp.where(eye, jnp.broadcast_to(hs(w[g]["epc"], jobs[n][1]), (HD, HD)), 0.0)
                                              + hb[n][0:HD] for n in sel])
            psi_ref[gi * gc + g] = jnp.stack([hb[n][HD:2 * HD] + vk[n] for n in sel])
        return carry

    lax.fori_loop(0, cb // gc, group, 0)


def _rwkv_scan_kernel(rh_ref, y0_ref, phi_ref, psi_ref, post_ref, s0_ref, ln_ref, o_ref, s1_ref, s_s, y_s, *, c, cb):
    i = pl.program_id(1)

    @pl.when(i == 0)
    def _():
        s_s[...] = s0_ref[...]

    _lin_scan(rh_ref, y0_ref, phi_ref, psi_ref, s_s, y_s, c, cb)
    y = y_s[...]
    masks = _head_masks(c * cb)
    mu = _seg_sum(y, masks) * (1.0 / HD)
    var = _seg_sum(jnp.square(y - mu), masks) * (1.0 / HD)
    y = (y - mu) * lax.rsqrt(var + LN_EPS) * ln_ref[0:1, :] + ln_ref[1:2, :]
    o_ref[...] = (y + post_ref[:, 0:BR_W]) * post_ref[:, BR_W:2 * BR_W]

    @pl.when(i == pl.num_programs(1) - 1)
    def _():
        s1_ref[...] = s_s[...]


def _rwkv(rslab, row_off, b, l, sh0, s0, lp):
    rows = math.gcd(l, 256)
    nt = l // rows
    tile, halo, state = _halo_specs(rows, RWKV_PROJ, row_off, nt)
    quarter = lambda a: a.reshape(4, BR_W)
    vec = jnp.concatenate([quarter(lp["rw_mu"]), lp["rw_w0"][None], lp["rw_a0"][None], lp["rw_kk"][None],
                           lp["rw_ka"][None], lp["rw_rk"].reshape(1, BR_W), jnp.zeros((7, BR_W), F32)], axis=0)
    rs, post = pl.pallas_call(
        _rwkv_prep_kernel,
        out_shape=(SDS((b * l, 6 * BR_W), F32), SDS((b * l, 2 * BR_W), F32)),
        grid=(b, nt),
        in_specs=[tile, halo, state, pl.BlockSpec((16, BR_W), lambda bb, i: (0, 0)),
                  pl.BlockSpec((RWKV_W_LORA, BR_W), lambda bb, i: (0, 0)),
                  pl.BlockSpec((RWKV_A_LORA, BR_W), lambda bb, i: (0, 0)),
                  pl.BlockSpec((RWKV_G_LORA, BR_W), lambda bb, i: (0, 0))],
        out_specs=(pl.BlockSpec((rows, 6 * BR_W), lambda bb, i: (bb * nt + i, 0)),
                   pl.BlockSpec((rows, 2 * BR_W), lambda bb, i: (bb * nt + i, 0))),
        compiler_params=_cp(("parallel", "parallel")),
        name="rwkv_prep",
    )(rslab, rslab, _pad_state_rows(sh0[:, None, :], RWKV_PROJ), vec, lp["rw_w2"], lp["rw_a2"], lp["rw_g2"])

    c = math.gcd(l, 64)
    nc = l // c
    prow, pcb, pblk = _seq_blocks(l, c, 256)
    (rs_spec, rh_spec, y0_spec), mat = _chunk_specs(prow, pcb, pblk, (6 * BR_W, BR_W, BR_W))
    rh, y0, phi, psi = pl.pallas_call(
        functools.partial(_rwkv_chunk_kernel, c=c, cb=pcb, gc=math.gcd(pcb, 2)),
        out_shape=(SDS((b * l, BR_W), F32), SDS((b * l, BR_W), F32),
                   SDS((b * nc, HEADS, HD, HD), F32), SDS((b * nc, HEADS, HD, HD), F32)),
        grid=(b, pblk),
        in_specs=[rs_spec],
        out_specs=(rh_spec, y0_spec, mat, mat),
        compiler_params=_cp(("parallel", "parallel")),
        name="rwkv_chunk",
    )(rs)

    srow, scb, sblk = _seq_blocks(l, c, 512)
    (rh_spec, y0_spec, post_spec, o_spec), mat = _chunk_specs(srow, scb, sblk, (BR_W, BR_W, 2 * BR_W, BR_W))
    ln = jnp.stack([lp["rw_ln_g"], lp["rw_ln_b"]] + [jnp.zeros((BR_W,), F32)] * 6)
    o, s1 = pl.pallas_call(
        functools.partial(_rwkv_scan_kernel, c=c, cb=scb),
        out_shape=(SDS((b * l, BR_W), F32), SDS((b, HEADS, HD, HD), F32)),
        grid=(b, sblk),
        in_specs=[rh_spec, y0_spec, mat, mat, post_spec, _state_spec((HEADS, HD, HD)),
                  pl.BlockSpec((SUBLANES, BR_W), lambda bb, i: (0, 0))],
        out_specs=(o_spec, _state_spec((HEADS, HD, HD))),
        scratch_shapes=[pltpu.VMEM((HEADS, HD, HD), F32), pltpu.VMEM((srow, BR_W), F32)],
        compiler_params=_cp(("parallel", "arbitrary")),
        name="rwkv_scan",
    )(rh, y0, phi, psi, post, s0, ln)
    return o, s1


def _gdn_prep_kernel(x_ref, halo_ref, st_ref, taps_ref, o_ref):
    i = pl.program_id(1)
    x = x_ref[:, 0:GDN_CONV_CH]
    rows = x.shape[0]
    halo = jnp.where(i == 0, st_ref[:, 0:GDN_CONV_CH], halo_ref[:, 0:GDN_CONV_CH])
    conv = x * taps_ref[CONV_W - 1:CONV_W, :]
    for s in range(1, CONV_W):
        conv = conv + _shift_rows(x, halo, s) * taps_ref[CONV_W - 1 - s:CONV_W - s, :]
    conv = conv * jax.nn.sigmoid(conv)
    masks = _head_masks(rows)
    for n in range(3):
        t = conv[:, n * BR_W:(n + 1) * BR_W]
        if n < 2:
            t = t * lax.rsqrt(jnp.maximum(_seg_sum(t * t, masks), 1e-12))
        if n == 0:
            t = t * (HD ** -0.5)
        o_ref[:, n * BR_W:(n + 1) * BR_W] = t


def _gdn_chunk_kernel(x_ref, sm_ref, qh_ref, o0_ref, phi_ref, psi_ref, *, c, cb, gc):
    r_i = _iota((c, c), 0)
    c_i = _iota((c, c), 1)
    eye = _iota((HD, HD), 0) == _iota((HD, HD), 1)
    jobs = [(g, h) for g in range(gc) for h in range(HEADS)]

    def group(gi, carry):
        r0s = [pl.multiple_of((gi * gc + g) * c, c) for g in range(gc)]
        q, k, v, beta, dec, eg, gl, wv = [], [], [], [], [], [], [], []
        for g, h in jobs:
            r0, lo = r0s[g], HD * h
            q.append(x_ref[pl.ds(r0, c), lo:lo + HD])
            k.append(x_ref[pl.ds(r0, c), BR_W + lo:BR_W + lo + HD])
            v.append(x_ref[pl.ds(r0, c), 2 * BR_W + lo:2 * BR_W + lo + HD])
            beta.append(sm_ref[pl.ds(r0, c), SM_GB + h:SM_GB + h + 1])
            gam_row, gam_col = _cumsum_col(sm_ref[pl.ds(r0, c), SM_GA + h:SM_GA + h + 1], c)
            dec.append(jnp.exp(jnp.where(c_i <= r_i, gam_col - gam_row, NEG_INF)))
            eg.append(jnp.exp(gam_col))
            gl.append(gam_col[c - 1:c, :])
            wv.append(jnp.exp(gam_col[c - 1:c, :] - gam_col))
        n = range(len(jobs))
        big = [_dot_nt(jnp.concatenate([k[i], q[i]], axis=0), k[i], HIGHEST) for i in n]
        tinv = _unit_lower_inverse([-(beta[i] * big[i][0:c] * jnp.where(c_i < r_i, dec[i], 0.0)) for i in n], c)
        x = [_hdot(tinv[i], jnp.concatenate([beta[i] * v[i], (beta[i] * eg[i]) * k[i]], axis=1)) for i in n]
        out = [_hdot(big[i][c:2 * c] * dec[i], x[i]) for i in n]
        xb = [_dot_tn(x[i], wv[i] * k[i], HIGHEST) for i in n]
        for g in range(gc):
            sel = [i for i in n if jobs[i][0] == g]
            qh_ref[pl.ds(r0s[g], c), :] = jnp.concatenate([eg[i] * q[i] - out[i][:, HD:2 * HD] for i in sel], axis=1)
            o0_ref[pl.ds(r0s[g], c), :] = jnp.concatenate([out[i][:, 0:HD] for i in sel], axis=1)
            phi_ref[gi * gc + g] = jnp.stack([jnp.where(eye, jnp.broadcast_to(jnp.exp(gl[i]), (HD, HD)), 0.0)
                                              - xb[i][HD:2 * HD] for i in sel])
            psi_ref[gi * gc + g] = jnp.stack([xb[i][0:HD] for i in sel])
        return carry

    lax.fori_loop(0, cb // gc, group, 0)


def _gdn_scan_kernel(qh_ref, o0_ref, phi_ref, psi_ref, z_ref, s0_ref, gn_ref, o_ref, s1_ref, s_s, y_s, *, c, cb):
    i = pl.program_id(1)

    @pl.when(i == 0)
    def _():
        s_s[...] = s0_ref[...]

    _lin_scan(qh_ref, o0_ref, phi_ref, psi_ref, s_s, y_s, c, cb)
    o = y_s[...]
    ms = _seg_sum(o * o, _head_masks(c * cb)) * (1.0 / HD)
    z = z_ref[:, 3 * BR_W:4 * BR_W]
    o_ref[...] = o * lax.rsqrt(ms + EPS) * gn_ref[...] * (z * jax.nn.sigmoid(z))

    @pl.when(i == pl.num_programs(1) - 1)
    def _():
        s1_ref[...] = s_s[...]


def _gdn(gslab, sm, row_off, b, l, cv0, s0, lp):
    rows = math.gcd(l, 256)
    nt = l // rows
    w = 4 * BR_W
    tile, halo, state = _halo_specs(rows, w, row_off, nt)
    taps = jnp.pad(lp["gd_conv"], ((0, SUBLANES - CONV_W), (0, 0)))
    qkv = pl.pallas_call(
        _gdn_prep_kernel,
        out_shape=SDS((b * l, GDN_CONV_CH), F32),
        grid=(b, nt),
        in_specs=[tile, halo, state, pl.BlockSpec((SUBLANES, GDN_CONV_CH), lambda bb, i: (0, 0))],
        out_specs=pl.BlockSpec((rows, GDN_CONV_CH), lambda bb, i: (bb * nt + i, 0)),
        compiler_params=_cp(("parallel", "parallel")),
        name="gdn_prep",
    )(gslab, gslab, _pad_state_rows(cv0, w), taps)

    c = math.gcd(l, GDN_CHUNK)
    nc = l // c
    prow, pcb, pblk = _seq_blocks(l, c, 256)
    (x_spec, qh_spec, o0_spec), mat = _chunk_specs(prow, pcb, pblk, (GDN_CONV_CH, BR_W, BR_W))
    sm_spec = pl.BlockSpec((prow, LANES), lambda bb, i: (row_off // prow + bb * pblk + i, 0))
    qh, o0, phi, psi = pl.pallas_call(
        functools.partial(_gdn_chunk_kernel, c=c, cb=pcb, gc=math.gcd(pcb, 2)),
        out_shape=(SDS((b * l, BR_W), F32), SDS((b * l, BR_W), F32),
                   SDS((b * nc, HEADS, HD, HD), F32), SDS((b * nc, HEADS, HD, HD), F32)),
        grid=(b, pblk),
        in_specs=[x_spec, sm_spec],
        out_specs=(qh_spec, o0_spec, mat, mat),
        compiler_params=_cp(("parallel", "parallel")),
        name="gdn_chunk",
    )(qkv, sm)

    srow, scb, sblk = _seq_blocks(l, c, 512)
    (qh_spec, o0_spec, o_spec), mat = _chunk_specs(srow, scb, sblk, (BR_W, BR_W, BR_W))
    z_spec = pl.BlockSpec((srow, w), lambda bb, i: (row_off // srow + bb * sblk + i, 0))
    o, s1 = pl.pallas_call(
        functools.partial(_gdn_scan_kernel, c=c, cb=scb),
        out_shape=(SDS((b * l, BR_W), F32), SDS((b, HEADS, HD, HD), F32)),
        grid=(b, sblk),
        in_specs=[qh_spec, o0_spec, mat, mat, z_spec, _state_spec((HEADS, HD, HD)),
                  pl.BlockSpec((1, BR_W), lambda bb, i: (0, 0))],
        out_specs=(o_spec, _state_spec((HEADS, HD, HD))),
        scratch_shapes=[pltpu.VMEM((HEADS, HD, HD), F32), pltpu.VMEM((srow, BR_W), F32)],
        compiler_params=_cp(("parallel", "arbitrary")),
        name="gdn_scan",
    )(qh, o0, phi, psi, gslab, s0, jnp.tile(lp["gd_norm"], HEADS).reshape(1, BR_W))
    return o, s1


def _rope_tables(pos):
    half = HD // 2
    inv = ROPE_THETA ** (-jnp.arange(half, dtype=F32) / half)
    ang = pos.astype(F32)[:, None] * inv[None, :]
    cos, sin = jnp.cos(ang), jnp.sin(ang)
    zero = jnp.zeros_like(sin)
    tile = lambda a, b: jnp.concatenate([a, b, a, b], axis=1)
    return tile(cos, cos), tile(-sin, zero), tile(zero, sin)


def _pad_w_in(w):
    sizes = (BR_W, BR_W, BR_W, IDX_HEADS * IDX_DIM, IDX_DIM, IDX_HEADS, BR_W, BR_W, BR_W, HEADS, HEADS, BR_W,
             RWKV_PROJ, GDN_CONV_CH, HEADS, HEADS, BR_W, N_BRANCH * w.shape[0])
    cuts = np.concatenate([[0], np.cumsum(sizes)])
    col = lambda i: w[:, cuts[i]:cuts[i + 1]]
    (a_q, a_k, a_v, a_iq, a_ik, a_iw, m_q, m_k, m_v, m_i, m_f, m_o, r_p, g_qkv, g_b, g_a, g_z, gate) = (
        col(i) for i in range(len(sizes)))
    small = jnp.concatenate([a_ik, a_iw, m_i, m_f, g_b, g_a, jnp.zeros((w.shape[0], LANES - SM_END), w.dtype)], axis=1)
    return jnp.concatenate([a_q, a_k, a_v, a_iq, small, m_q, m_k, m_v, m_o, r_p, g_qkv, g_z, gate], axis=1).astype(BF16)


def _small_params(lp):
    lane_put = lambda v, at: jnp.zeros((LANES,), F32).at[at:at + v.shape[0]].set(v)
    rows = [lane_put(lp["idx_k_norm"], SM_IK),
            lane_put(lp["ml_i_bias"], SM_MI) + lane_put(lp["ml_f_bias"], SM_MF) + lane_put(lp["gd_dt_bias"], SM_GA),
            lane_put(lp["gd_A_log"], SM_GA)]
    return jnp.stack(rows + [jnp.zeros((LANES,), F32)] * (SUBLANES - len(rows)))


def kernel(x_prompt, x_sample, cache_k, cache_v, cache_idx_k, state_mlstm_C, state_mlstm_n, state_mlstm_m, state_rwkv_S, state_rwkv_shift, state_gdn_S, state_gdn_conv, page_table, ffn1_norm, ffn1_w1, ffn1_w3, ffn1_w2, mix_norm, w_in, idx_k_norm, ml_i_bias, ml_f_bias, ml_norm, rw_mu, rw_w0, rw_w2, rw_a0, rw_a2, rw_g2, rw_kk, rw_ka, rw_rk, rw_ln_g, rw_ln_b, gd_conv, gd_A_log, gd_dt_bias, gd_norm, w_branch, w_out, ffn2_norm, ffn2_w1, ffn2_w3, ffn2_w2, final_norm):
    bp, lp_, d = x_prompt.shape
    db, t, _ = x_sample.shape
    assert bp == 1 and t % SUBLANES == 0 and t <= LANES and lp_ % SUBLANES == 0
    depth = w_in.shape[0]
    past = page_table.shape[1] * PAGE_SIZE
    ns = db * t

    x = jnp.concatenate([x_prompt.reshape(lp_, d), x_sample.reshape(ns, d)], axis=0)
    pos = jnp.concatenate([jnp.arange(lp_, dtype=I32), jnp.tile(past + jnp.arange(t, dtype=I32), db)])
    cos, sa, sb = _rope_tables(pos)
    zeros = lambda *s: jnp.zeros(s, F32)

    outs_p, outs_s = [], []
    for l in range(depth):
        lp = dict(idx_k_norm=idx_k_norm[l], ml_i_bias=ml_i_bias[l], ml_f_bias=ml_f_bias[l],
                  rw_mu=rw_mu[l], rw_w0=rw_w0[l], rw_w2=rw_w2[l], rw_a0=rw_a0[l], rw_a2=rw_a2[l], rw_g2=rw_g2[l],
                  rw_kk=rw_kk[l], rw_ka=rw_ka[l], rw_rk=rw_rk[l], rw_ln_g=rw_ln_g[l], rw_ln_b=rw_ln_b[l],
                  gd_conv=gd_conv[l], gd_A_log=gd_A_log[l], gd_dt_bias=gd_dt_bias[l], gd_norm=gd_norm[l])
        x = _ffn(x, ffn1_norm[l], ffn1_w1[l], ffn1_w3[l], ffn1_w2[l])
        aqkv, iq, sm, mslab, rslab, gslab, gate = _proj(x, mix_norm[l], _pad_w_in(w_in[l]), cos, sa, sb, _small_params(lp))

        oa_p = _dsa_prompt(iq, sm, aqkv, lp_)
        oa_s = _dsa_sample(iq[lp_:], sm[lp_:], aqkv[lp_:], cache_k, cache_v, cache_idx_k, page_table, l, db, t)
        om_p, c_p, n_p, m_p = _mlstm(mslab, sm, 0, 1, lp_, zeros(1, HEADS, HD, HD), zeros(1, HEADS, HD), zeros(1, HEADS), ml_norm[l])
        om_s, c_s, n_s, m_s = _mlstm(mslab, sm, lp_, db, t, state_mlstm_C[l], state_mlstm_n[l], state_mlstm_m[l], ml_norm[l])
        or_p, rs_p = _rwkv(rslab, 0, 1, lp_, zeros(1, RWKV_PROJ), zeros(1, HEADS, HD, HD), lp)
        or_s, rs_s = _rwkv(rslab, lp_, db, t, state_rwkv_shift[l], state_rwkv_S[l], lp)
        og_p, gs_p = _gdn(gslab, sm, 0, 1, lp_, zeros(1, CONV_W - 1, GDN_CONV_CH), zeros(1, HEADS, HD, HD), lp)
        og_s, gs_s = _gdn(gslab, sm, lp_, db, t, state_gdn_conv[l], state_gdn_S[l], lp)

        cat = lambda a, b: jnp.concatenate([a, b], axis=0)
        x = _merge(x, cat(oa_p, oa_s), cat(om_p, om_s), cat(or_p, or_s), cat(og_p, og_s), gate, w_branch[l], w_out[l])
        x = _ffn(x, ffn2_norm[l], ffn2_w1[l], ffn2_w3[l], ffn2_w2[l])

        k_all = aqkv[:, BR_W:2 * BR_W]
        v_all = aqkv[:, 2 * BR_W:3 * BR_W]
        ik_all = sm[:, SM_IK:SM_IK + IDX_DIM]
        gq = gslab[:, 0:GDN_CONV_CH]
        outs_p.append((k_all[:lp_].reshape(1, lp_, HEADS, HD), v_all[:lp_].reshape(1, lp_, HEADS, HD),
                       ik_all[:lp_].reshape(1, lp_, IDX_DIM), c_p, n_p, m_p, rs_p, rslab[lp_ - 1:lp_],
                       gs_p, gq[lp_ - (CONV_W - 1):lp_].reshape(1, CONV_W - 1, GDN_CONV_CH)))
        outs_s.append((k_all[lp_:].reshape(db, t, HEADS, HD), v_all[lp_:].reshape(db, t, HEADS, HD),
                       ik_all[lp_:].reshape(db, t, IDX_DIM), c_s, n_s, m_s, rs_s,
                       rslab[lp_:].reshape(db, t, RWKV_PROJ)[:, -1],
                       gs_s, gq[lp_:].reshape(db, t, GDN_CONV_CH)[:, t - (CONV_W - 1):]))

    y = _final_rms(x, final_norm)
    st_p = [jnp.stack(z) for z in zip(*outs_p)]
    st_s = [jnp.stack(z) for z in zip(*outs_s)]
    return (y[:lp_].reshape(1, lp_, d), y[lp_:].reshape(db, t, d), *st_p, *st_s)
```

```python
import functools
import math

import jax
import jax.numpy as jnp
import numpy as np
from jax import lax
from jax.experimental import pallas as pl
from jax.experimental.pallas import tpu as pltpu

F32 = jnp.float32
BF16 = jnp.bfloat16
I32 = jnp.int32
SDS = jax.ShapeDtypeStruct

HD = 64
HEADS = 4
BR_W = HEADS * HD
N_BRANCH = 4
IDX_HEADS = 8
IDX_DIM = 64
TOPK_MAX = 256
TOPM = 12
ROPE_THETA = 10000.0
MLSTM_CHUNK = 64
GDN_CHUNK = 64
CONV_W = 4
RWKV_W_LORA = 64
RWKV_A_LORA = 64
RWKV_G_LORA = 128
RWKV_PROJ = 3 * BR_W + RWKV_W_LORA + RWKV_A_LORA + RWKV_G_LORA
GDN_CONV_CH = 3 * BR_W
EPS = 1e-6
LN_EPS = 1e-5
PAGE_SIZE = 128

LANES = 128
SUBLANES = 8
VMEM_LIMIT_BYTES = 58 * 1024 * 1024

SM_IK = 0
SM_IW = 64
SM_MI = 72
SM_MF = 76
SM_GB = 80
SM_GA = 84
SM_END = 88

PC_AQKV = 0
PC_IQ = 768
PC_SM = 1280
PC_M = 1408
PC_R = 2432
PC_G = 3456
PC_GATE = 4480
PC_END = 8576

HIGHEST = lax.Precision.HIGHEST
INT_MIN = -2147483648
NEG_INF = float("-inf")


def _cp(sem):
    return pltpu.CompilerParams(dimension_semantics=sem, vmem_limit_bytes=VMEM_LIMIT_BYTES)


def _resident(shape):
    return pl.BlockSpec(shape, lambda *_: (0,) * len(shape), pipeline_mode=pl.Buffered(1))


def _rows(tm, w, off_blocks=0):
    return pl.BlockSpec((tm, w), lambda i, *_: (i + off_blocks, 0))


def _dot(a, b):
    return jnp.dot(a, b, preferred_element_type=F32)


def _dot_nt(a, b, precision=None):
    return lax.dot_general(a, b, (((1,), (1,)), ((), ())), preferred_element_type=F32, precision=precision)


def _dot_tn(a, b, precision=None):
    return lax.dot_general(a, b, (((0,), (0,)), ((), ())), preferred_element_type=F32, precision=precision)


def _hdot(a, b):
    return jnp.dot(a, b, preferred_element_type=F32, precision=HIGHEST)


def _rms(x, g):
    return x * lax.rsqrt(jnp.mean(x * x, axis=-1, keepdims=True) + EPS) * g


def _iota(shape, dim):
    return lax.broadcasted_iota(I32, shape, dim)


def _head_masks(rows):
    lane = _iota((rows, BR_W), 1)
    return [(lane >= HD * h) & (lane < HD * (h + 1)) for h in range(HEADS)]


def _seg_sum(x, masks):
    out = jnp.zeros_like(x)
    for m in masks:
        s = jnp.sum(jnp.where(m, x, 0.0), axis=1, keepdims=True)
        out = jnp.where(m, s, out)
    return out


def _col_to_row(col, n):
    eye = _iota((n, n), 0) == _iota((n, n), 1)
    return jnp.sum(jnp.where(eye, jnp.broadcast_to(col, (n, n)), 0.0), axis=0, keepdims=True)


def _row_to_col(row, n):
    eye = _iota((n, n), 0) == _iota((n, n), 1)
    return jnp.sum(jnp.where(eye, jnp.broadcast_to(row, (n, n)), 0.0), axis=1, keepdims=True)


def _cumsum_col(col, n):
    r = _iota((n, n), 0)
    c = _iota((n, n), 1)
    row = jnp.sum(jnp.where(r <= c, jnp.broadcast_to(col, (n, n)), 0.0), axis=0, keepdims=True)
    return row, _row_to_col(row, n)


def _ffn_kernel(x_ref, g_ref, w1_ref, w3_ref, w2_ref, o_ref):
    x = x_ref[...]
    h = _rms(x, g_ref[...]).astype(BF16)
    a = _dot(h, w1_ref[...])
    b = _dot(h, w3_ref[...])
    y = (a * jax.nn.sigmoid(a) * b).astype(BF16)
    o_ref[...] = x + 0.5 * _dot(y, w2_ref[...])


def _ffn(x, g, w1, w3, w2):
    na, d = x.shape
    f = w1.shape[1]
    tm = math.gcd(na, 256)
    return pl.pallas_call(
        _ffn_kernel,
        out_shape=SDS((na, d), F32),
        grid=(na // tm,),
        in_specs=[_rows(tm, d), _resident((1, d)), _resident((d, f)), _resident((d, f)), _resident((f, d))],
        out_specs=_rows(tm, d),
        compiler_params=_cp(("parallel",)),
        name="ffn",
    )(x, g.reshape(1, d), w1.astype(BF16), w3.astype(BF16), w2.astype(BF16))


def _rms_kernel(x_ref, g_ref, o_ref):
    o_ref[...] = _rms(x_ref[...], g_ref[...])


def _final_rms(x, g):
    na, d = x.shape
    tm = math.gcd(na, 512)
    return pl.pallas_call(
        _rms_kernel,
        out_shape=SDS((na, d), F32),
        grid=(na // tm,),
        in_specs=[_rows(tm, d), _resident((1, d))],
        out_specs=_rows(tm, d),
        compiler_params=_cp(("parallel",)),
        name="final_rms",
    )(x, g.reshape(1, d))


def _rope_slab(z, cos, sa, sb):
    return z * cos + pltpu.roll(z, LANES - HD // 2, 1) * sa + pltpu.roll(z, HD // 2, 1) * sb


def _proj_kernel(x_ref, g_ref, w_ref, cos_ref, sa_ref, sb_ref, sp_ref,
                 aqkv_ref, iq_ref, sm_ref, m_ref, r_ref, gq_ref, gate_ref):
    h = _rms(x_ref[...], g_ref[...]).astype(BF16)
    cos, sa, sb = cos_ref[...], sa_ref[...], sb_ref[...]

    def mm(a, b):
        return _dot(h, w_ref[:, a:b])

    za = mm(PC_AQKV, PC_IQ)
    for s in range(4):
        aqkv_ref[:, s * LANES:(s + 1) * LANES] = _rope_slab(za[:, s * LANES:(s + 1) * LANES], cos, sa, sb)
    aqkv_ref[:, 2 * BR_W:3 * BR_W] = za[:, 2 * BR_W:3 * BR_W]
    zi = mm(PC_IQ, PC_SM)
    for s in range(4):
        iq_ref[:, s * LANES:(s + 1) * LANES] = _rope_slab(zi[:, s * LANES:(s + 1) * LANES], cos, sa, sb)

    zs = mm(PC_SM, PC_M)
    lane = _iota(zs.shape, 1)
    gn, bias, alog = sp_ref[0:1, :], sp_ref[1:2, :], sp_ref[2:3, :]
    ms = jnp.sum(jnp.where(lane < IDX_DIM, zs * zs, 0.0), axis=1, keepdims=True) * (1.0 / IDX_DIM)
    ik = _rope_slab(zs * lax.rsqrt(ms + EPS) * gn, cos, sa, sb)
    t = zs + bias
    out = jnp.where(lane < SM_IW, ik,
          jnp.where(lane < SM_MI, zs * (IDX_HEADS ** -0.5),
          jnp.where(lane < SM_MF, t,
          jnp.where(lane < SM_GB, jax.nn.log_sigmoid(t),
          jnp.where(lane < SM_GA, jax.nn.sigmoid(zs),
          jnp.where(lane < SM_END, -jnp.exp(alog) * jax.nn.softplus(t), 0.0))))))
    sm_ref[...] = out

    m_ref[...] = mm(PC_M, PC_R)
    r_ref[...] = mm(PC_R, PC_G)
    gq_ref[...] = mm(PC_G, PC_GATE)
    gate_ref[...] = mm(PC_GATE, PC_END)


def _proj(x, g, w_pad, cos, sa, sb, sp):
    na, d = x.shape
    tm = math.gcd(na, 256)
    widths = (3 * BR_W, IDX_HEADS * IDX_DIM, LANES, 4 * BR_W, RWKV_PROJ, 4 * BR_W, N_BRANCH * d)
    return pl.pallas_call(
        _proj_kernel,
        out_shape=tuple(SDS((na, w), F32) for w in widths),
        grid=(na // tm,),
        in_specs=[_rows(tm, d), _resident((1, d)), _resident((d, PC_END)),
                  _rows(tm, LANES), _rows(tm, LANES), _rows(tm, LANES), _resident((SUBLANES, LANES))],
        out_specs=tuple(_rows(tm, w) for w in widths),
        compiler_params=_cp(("parallel",)),
        name="proj_in",
    )(x, g.reshape(1, d), w_pad, cos, sa, sb, sp)


def _merge_kernel(x_ref, oa_ref, om_ref, or_ref, og_ref, gate_ref, wb_ref, wo_ref, o_ref):
    d = x_ref.shape[1]
    mix = None
    for n, br in enumerate((oa_ref, om_ref, or_ref, og_ref)):
        up = _dot(br[...].astype(BF16), wb_ref[n])
        t = jax.nn.sigmoid(gate_ref[:, n * d:(n + 1) * d]) * up
        mix = t if mix is None else mix + t
    o_ref[...] = x_ref[...] + _dot(mix.astype(BF16), wo_ref[...])


def _merge(x, oa, om, orr, og, gate, w_branch, w_out):
    na, d = x.shape
    tm = math.gcd(na, 256)
    return pl.pallas_call(
        _merge_kernel,
        out_shape=SDS((na, d), F32),
        grid=(na // tm,),
        in_specs=[_rows(tm, d)] + [_rows(tm, BR_W)] * 4 + [_rows(tm, N_BRANCH * d),
                  _resident((N_BRANCH, BR_W, d)), _resident((d, d))],
        out_specs=_rows(tm, d),
        compiler_params=_cp(("parallel",)),
        name="merge",
    )(x, oa, om, orr, og, gate, w_branch.astype(BF16), w_out.astype(BF16))


def _idx_queries(qi, sm, tq):
    qrows = jnp.concatenate([qi[:, h * IDX_DIM:(h + 1) * IDX_DIM] for h in range(IDX_HEADS)], axis=0).astype(BF16)
    wb = [jnp.broadcast_to(sm[:, SM_IW + h:SM_IW + h + 1] * (IDX_DIM ** -0.5), (tq, LANES)) for h in range(IDX_HEADS)]
    return qrows, wb


def _idx_scores(qrows, wb, kib, tq, tk):
    dots = _dot_nt(qrows, kib)
    cols = []
    for j in range(tk // LANES):
        acc = None
        for h in range(IDX_HEADS):
            t = jnp.maximum(dots[h * tq:(h + 1) * tq, j * LANES:(j + 1) * LANES], 0.0) * wb[h]
            acc = t if acc is None else acc + t
        cols.append(acc)
    return cols[0] if len(cols) == 1 else jnp.concatenate(cols, axis=1)


def _to_key(sc):
    b = pltpu.bitcast(jnp.where(sc == 0.0, 0.0, sc), I32)
    return b ^ ((b >> 31) & I32(0x7FFFFFFF))


def _key_to_f32(key):
    return pltpu.bitcast(key ^ ((key >> 31) & I32(0x7FFFFFFF)), F32)


def _kth_largest(count, rows, n_sel, pos_bits, tkey_ref, cut_ref):
    int_min = I32(INT_MIN)

    def bit_step(it, ub):
        cand = ub | lax.shift_left(I32(1), 31 - it)
        cs = cand ^ int_min
        return jnp.where(count(lambda k, p: k >= cs) >= n_sel, cand, ub)

    ub = lax.fori_loop(0, 32, bit_step, jnp.zeros((rows, LANES), I32))
    tkey = ub ^ int_min
    n_gt = count(lambda k, p: k > tkey)
    n_ge = count(lambda k, p: k >= tkey)
    need = n_sel - n_gt
    ambiguous = (n_ge - n_gt) > need
    tkey_ref[...] = tkey
    cut_ref[...] = jnp.full((rows, LANES), 1 << pos_bits, I32)

    @pl.when(jnp.max(jnp.where(ambiguous, 1.0, 0.0)) > 0.0)
    def _():
        def pos_step(it, q):
            cand = q | lax.shift_left(I32(1), pos_bits - 1 - it)
            below = count(lambda k, p: (k == tkey) & (p < cand))
            return jnp.where(below < need, cand, q)

        q = lax.fori_loop(0, pos_bits, pos_step, jnp.zeros((rows, LANES), I32))
        cut_ref[...] = jnp.where(ambiguous, q + 1, 1 << pos_bits)


def _flash_update(s, v, m_s, l_s, acc_s, v_transposed=False):
    m_prev = m_s[...]
    m_new = jnp.maximum(m_prev, jnp.max(s, axis=1, keepdims=True))
    m_safe = jnp.where(m_new == NEG_INF, 0.0, m_new)
    alpha = jnp.exp(m_prev - m_safe)
    p = jnp.exp(s - m_safe)
    l_s[...] = alpha * l_s[...] + jnp.sum(p, axis=1, keepdims=True)
    m_s[...] = m_new
    pv = _dot_nt(p.astype(BF16), v) if v_transposed else _dot(p.astype(BF16), v)
    acc_s[...] = alpha * acc_s[...] + pv


def _stack_heads(q, t):
    masks = _head_masks(t)
    return jnp.concatenate([jnp.where(m, q * (HD ** -0.5), 0.0) for m in masks], axis=0).astype(BF16)


def _unstack_heads(acc, l, t):
    masks = _head_masks(t)
    out = jnp.zeros((t, BR_W), F32)
    for h, m in enumerate(masks):
        out = jnp.where(m, acc[h * t:(h + 1) * t] / l[h * t:(h + 1) * t], out)
    return out


def _dsa_prompt_kernel(qi_ref, smq_ref, q_ref, ki_ref, k_ref, v_ref, o_ref,
                       sc_s, cand_s, tkey_s, cut_s, need_s, neq_s, m_s, l_s, acc_s, *, tq, tk, pf, n_sel, pos_bits):
    i = pl.program_id(0)
    q0 = i * tq
    nch = (q0 + tq - 1) // tk + 1
    qrows, wb = _idx_queries(qi_ref[...], smq_ref[...], tq)
    qpos = q0 + _iota((tq, tk), 0)

    def fill(c, carry):
        k0 = pl.multiple_of(c * tk, tk)
        sc = _idx_scores(qrows, wb, ki_ref[pl.ds(k0, tk), :], tq, tk)
        sc = jnp.where(k0 + _iota((tq, tk), 1) <= qpos, sc, NEG_INF)
        sc_s[c] = jnp.where(sc == 0.0, 0.0, sc)
        return carry

    ngrp = (nch + pf - 1) // pf
    lax.fori_loop(0, ngrp * pf, fill, 0)
    int_min = I32(INT_MIN)
    neg_key = _to_key(jnp.full((tq, LANES), NEG_INF, F32))
    big = 1 << pos_bits

    def count(pred):
        def body(c, acc):
            for j in range(tk // LANES):
                acc = acc + jnp.where(pred(sc_s[c, :, j * LANES:(j + 1) * LANES]), 1.0, 0.0)
            return acc

        acc = lax.fori_loop(0, nch, body, jnp.zeros((tq, LANES), F32))
        return jnp.broadcast_to(jnp.sum(acc, axis=1, keepdims=True), (tq, LANES))

    rg = 2 * SUBLANES
    for g in range(tq // rg):
        def insert(c, best, g=g):
            for j in range(tk // LANES):
                x = sc_s[c, g * rg:(g + 1) * rg, j * LANES:(j + 1) * LANES]
                nxt = []
                for b in best:
                    nxt.append(jnp.maximum(b, x))
                    x = jnp.minimum(b, x)
                best = tuple(nxt)
            return best

        best = lax.fori_loop(0, nch, insert, tuple(jnp.full((rg, LANES), NEG_INF, F32) for _ in range(TOPM)))
        for u in range(TOPM):
            cand_s[u, g * rg:(g + 1) * rg, :] = _to_key(best[u])

    def count_cand(pred):
        acc = jnp.zeros((tq, LANES), F32)
        for u in range(TOPM):
            acc = acc + jnp.where(pred(cand_s[u]), 1.0, 0.0)
        return jnp.broadcast_to(jnp.sum(acc, axis=1, keepdims=True), (tq, LANES))

    def bit_step(it, ub):
        cand = ub | lax.shift_left(I32(1), 31 - it)
        cs = cand ^ int_min
        return jnp.where(count_cand(lambda k: k >= cs) >= n_sel, cand, ub)

    tk_c = lax.fori_loop(0, 32, bit_step, jnp.zeros((tq, LANES), I32)) ^ int_min
    n_gt = count_cand(lambda k: k > tk_c)
    tkey_s[...] = tk_c
    need_s[...] = n_sel - n_gt
    neq_s[...] = count_cand(lambda k: k >= tk_c) - n_gt
    dropped = (cand_s[TOPM - 1] >= tk_c) & (nch * (tk // LANES) > TOPM)

    @pl.when(jnp.max(jnp.where(dropped, 1.0, 0.0)) > 0.0)
    def _():
        def step(it, ub):
            cand = ub | lax.shift_left(I32(1), 31 - it)
            t = _key_to_f32(jnp.maximum(cand ^ int_min, neg_key))
            return jnp.where(count(lambda s: s >= t) >= n_sel, cand, ub)

        tk_f = lax.fori_loop(0, 32, step, jnp.zeros((tq, LANES), I32)) ^ int_min
        t = _key_to_f32(tk_f)
        n_gt_f = count(lambda s: s > t)
        tkey_s[...] = tk_f
        need_s[...] = n_sel - n_gt_f
        neq_s[...] = count(lambda s: s >= t) - n_gt_f

    thr128 = _key_to_f32(tkey_s[...])
    ambiguous = neq_s[...] > need_s[...]
    cut_s[...] = jnp.full((tq, LANES), big, I32)

    @pl.when(jnp.max(jnp.where(ambiguous, 1.0, 0.0)) > 0.0)
    def _():
        thr1 = thr128[:, 0:1]
        need1 = need_s[:, 0:1]
        tri = jnp.where(_iota((tk, tk), 0) <= _iota((tk, tk), 1), 1.0, 0.0).astype(BF16)

        def body(c, carry):
            base, cutf = carry
            eq = sc_s[c] == thr1
            run = base + _dot(jnp.where(eq, 1.0, 0.0).astype(BF16), tri)
            posf = (c * tk + 1 + _iota((tq, tk), 1)).astype(F32)
            hit = jnp.where(eq & (run == need1), posf, float(big))
            return run[:, tk - 1:tk], jnp.minimum(cutf, jnp.min(hit, axis=1, keepdims=True))

        _, cutf = lax.fori_loop(0, nch, body, (jnp.zeros((tq, 1), F32), jnp.full((tq, 1), float(big), F32)))
        cut_s[...] = jnp.where(ambiguous, jnp.broadcast_to(cutf, (tq, LANES)).astype(I32), big)

    qend = q0 + _iota((tq, LANES), 0) + 1
    cut_s[...] = jnp.where(tkey_s[...] == neg_key, jnp.minimum(cut_s[...], qend), cut_s[...])

    thr = thr128[:, 0:1]
    cut = cut_s[:, 0:1]
    qbd = _stack_heads(q_ref[:, 0:BR_W], tq)
    m_s[...] = jnp.full(m_s.shape, NEG_INF, F32)
    l_s[...] = jnp.zeros(l_s.shape, F32)
    acc_s[...] = jnp.zeros(acc_s.shape, F32)

    def attend(g, carry):
        k0 = pl.multiple_of(g * (pf * tk), pf * tk)
        sc = jnp.concatenate([sc_s[g * pf + u] for u in range(pf)], axis=1) if pf > 1 else sc_s[g]
        sel = (sc > thr) | ((sc == thr) & (k0 + _iota((tq, pf * tk), 1) < cut))
        bias = jnp.where(sel, 0.0, NEG_INF)
        s = _dot_nt(qbd, k_ref[pl.ds(k0, pf * tk), :]) + jnp.concatenate([bias] * HEADS, axis=0)
        _flash_update(s, v_ref[pl.ds(k0, pf * tk), :], m_s, l_s, acc_s)
        return carry

    lax.fori_loop(0, ngrp, attend, 0)
    o_ref[...] = _unstack_heads(acc_s[...], l_s[...], tq)


def _dsa_prompt(iq, sm, aqkv, lp):
    tq = min(128, lp)
    tk = min(512, lp)
    n_sel = min(TOPK_MAX, lp // 4)
    pos_bits = int(lp).bit_length()
    pf = 2 if (lp // tk) % 2 == 0 else 1
    ki = sm[:lp, SM_IK:SM_IK + IDX_DIM].astype(BF16)
    kb = aqkv[:lp, BR_W:2 * BR_W].astype(BF16)
    vb = aqkv[:lp, 2 * BR_W:3 * BR_W].astype(BF16)
    return pl.pallas_call(
        functools.partial(_dsa_prompt_kernel, tq=tq, tk=tk, pf=pf, n_sel=n_sel, pos_bits=pos_bits),
        out_shape=SDS((lp, BR_W), F32),
        grid=(lp // tq,),
        in_specs=[_rows(tq, IDX_HEADS * IDX_DIM), _rows(tq, LANES), _rows(tq, 3 * BR_W),
                  _resident((lp, IDX_DIM)), _resident((lp, BR_W)), _resident((lp, BR_W))],
        out_specs=_rows(tq, BR_W),
        scratch_shapes=[pltpu.VMEM((lp // tk, tq, tk), F32), pltpu.VMEM((TOPM, tq, LANES), I32),
                        pltpu.VMEM((tq, LANES), I32), pltpu.VMEM((tq, LANES), I32),
                        pltpu.VMEM((tq, LANES), F32), pltpu.VMEM((tq, LANES), F32),
                        pltpu.VMEM((HEADS * tq, 1), F32), pltpu.VMEM((HEADS * tq, 1), F32),
                        pltpu.VMEM((HEADS * tq, BR_W), F32)],
        compiler_params=_cp(("arbitrary",)),
        name="dsa_prompt",
    )(iq, sm, aqkv, ki, kb, vb)


def _dsa_sample_score_kernel(pt_ref, qr_ref, wb_ref, *refs, t, pg, ng, past, n_sel, pos_bits):
    pages = refs[:pg]
    knew_ref, scp_ref, scn_ref, thr_ref, cut_ref, keys_s, keyn_s, tkey_s = refs[pg:]
    j = pl.program_id(1)
    qrows = qr_ref[...]
    wb = wb_ref[...] * (IDX_DIM ** -0.5)

    def reduce_heads(dots):
        w = jnp.maximum(dots, 0.0) * wb
        s = w[0:t]
        for h in range(1, IDX_HEADS):
            s = s + w[h * t:(h + 1) * t]
        return s

    sc = jnp.concatenate([reduce_heads(_dot(qrows, p[...].astype(BF16))) for p in pages], axis=1)
    scp_ref[...] = sc
    keys_s[j] = _to_key(sc)

    @pl.when(j == ng - 1)
    def _():
        lane = _iota((t, LANES), 1)
        sn = reduce_heads(_dot_nt(qrows, knew_ref[...].astype(BF16)))
        sn = jnp.where(lane <= _iota((t, LANES), 0), sn, NEG_INF)
        scn_ref[...] = sn
        keyn_s[...] = _to_key(sn)

        def count(pred):
            def body(c, acc):
                for jj in range(pg):
                    p = c * (pg * LANES) + jj * LANES + lane
                    acc = acc + jnp.where(pred(keys_s[c, :, jj * LANES:(jj + 1) * LANES], p), 1.0, 0.0)
                return acc

            acc = lax.fori_loop(0, ng, body, jnp.zeros((t, LANES), F32))
            acc = acc + jnp.where(pred(keyn_s[...], past + lane), 1.0, 0.0)
            return jnp.broadcast_to(jnp.sum(acc, axis=1, keepdims=True), (t, LANES))

        _kth_largest(count, t, n_sel, pos_bits, tkey_s, cut_ref)
        thr_ref[...] = _key_to_f32(tkey_s[...])


def _dsa_sample_attn_kernel(pt_ref, q_ref, *refs, t, pg, ng, past):
    kpages = refs[:pg]
    vpages = refs[pg:2 * pg]
    (scp_ref, scn_ref, thr_ref, cut_ref, knew_ref, vnew_ref, o_ref, m_s, l_s, acc_s) = refs[2 * pg:]
    j = pl.program_id(1)

    @pl.when(j == 0)
    def _():
        m_s[...] = jnp.full(m_s.shape, NEG_INF, F32)
        l_s[...] = jnp.zeros(l_s.shape, F32)
        acc_s[...] = jnp.zeros(acc_s.shape, F32)

    qbd = _stack_heads(q_ref[...], t)
    thr = thr_ref[:, 0:1]
    cut = cut_ref[:, 0:1]
    w = pg * LANES
    sc = scp_ref[...]
    kidx = j * w + _iota((t, w), 1)
    bias = jnp.where((sc > thr) | ((sc == thr) & (kidx < cut)), 0.0, NEG_INF)
    kcat = jnp.concatenate([p[...].reshape(BR_W, PAGE_SIZE).astype(BF16) for p in kpages], axis=1)
    vcat = jnp.concatenate([p[...].reshape(BR_W, PAGE_SIZE).astype(BF16) for p in vpages], axis=1)
    s = _dot(qbd, kcat) + jnp.concatenate([bias] * HEADS, axis=0)
    _flash_update(s, vcat, m_s, l_s, acc_s, v_transposed=True)

    @pl.when(j == ng - 1)
    def _():
        lane = _iota((t, LANES), 1)
        scn = scn_ref[...]
        seln = ((scn > thr) | ((scn == thr) & (past + lane < cut))) & (lane <= _iota((t, LANES), 0))
        biasn = jnp.where(seln, 0.0, NEG_INF)
        sn = _dot_nt(qbd, knew_ref[...].astype(BF16)) + jnp.concatenate([biasn] * HEADS, axis=0)
        _flash_update(sn, vnew_ref[...].astype(BF16), m_s, l_s, acc_s)
        o_ref[...] = _unstack_heads(acc_s[...], l_s[...], t)


def _dsa_sample(iq_s, sm_s, aqkv_s, cache_k, cache_v, cache_idx_k, page_table, layer, db, t):
    n_pages = page_table.shape[1]
    past = n_pages * PAGE_SIZE
    n_sel = min(TOPK_MAX, (past + t) // 4)
    pos_bits = int(past + LANES).bit_length()
    pg = math.gcd(n_pages, 8)
    ng = n_pages // pg
    ckt = cache_k.transpose(0, 1, 3, 4, 2)
    cvt = cache_v.transpose(0, 1, 3, 4, 2)
    cit = cache_idx_k.transpose(0, 1, 3, 2)

    qrows = iq_s.reshape(db, t, IDX_HEADS, IDX_DIM).transpose(0, 2, 1, 3).reshape(db, IDX_HEADS * t, IDX_DIM).astype(BF16)
    wcol = sm_s[:, SM_IW:SM_IW + IDX_HEADS].reshape(db, t, IDX_HEADS).transpose(0, 2, 1).reshape(db, IDX_HEADS * t, 1)
    wbs = jnp.broadcast_to(wcol, (db, IDX_HEADS * t, LANES))
    pad = lambda a: jnp.pad(a.reshape(db, t, a.shape[-1]), ((0, 0), (0, LANES - t), (0, 0)))
    ki_new = pad(sm_s[:, SM_IK:SM_IK + IDX_DIM])
    k_new = pad(aqkv_s[:, BR_W:2 * BR_W])
    v_new = pad(aqkv_s[:, 2 * BR_W:3 * BR_W])
    q3 = aqkv_s[:, 0:BR_W].reshape(db, t, BR_W)

    def idx_page(i):
        return pl.BlockSpec((None, None, IDX_DIM, PAGE_SIZE), lambda b, j, pt: (layer, pt[b, j * pg + i], 0, 0))

    def kv_page(i):
        return pl.BlockSpec((None, None, HEADS, HD, PAGE_SIZE), lambda b, j, pt: (layer, pt[b, j * pg + i], 0, 0, 0))

    per_b = lambda r, w: pl.BlockSpec((None, r, w), lambda b, j, pt: (b, 0, 0))
    sc_spec = pl.BlockSpec((None, t, pg * LANES), lambda b, j, pt: (b, 0, j))

    scp, scn, thr, cut = pl.pallas_call(
        functools.partial(_dsa_sample_score_kernel, t=t, pg=pg, ng=ng, past=past, n_sel=n_sel, pos_bits=pos_bits),
        out_shape=(SDS((db, t, past), F32), SDS((db, t, LANES), F32), SDS((db, t, LANES), F32), SDS((db, t, LANES), I32)),
        grid_spec=pltpu.PrefetchScalarGridSpec(
            num_scalar_prefetch=1, grid=(db, ng),
            in_specs=[per_b(IDX_HEADS * t, IDX_DIM), per_b(IDX_HEADS * t, LANES)]
                     + [idx_page(i) for i in range(pg)] + [per_b(LANES, IDX_DIM)],
            out_specs=(sc_spec, per_b(t, LANES), per_b(t, LANES), per_b(t, LANES)),
            scratch_shapes=[pltpu.VMEM((ng, t, pg * LANES), I32), pltpu.VMEM((t, LANES), I32), pltpu.VMEM((t, LANES), I32)]),
        compiler_params=_cp(("arbitrary", "arbitrary")),
        name="dsa_sample_scores",
    )(page_table, qrows, wbs, *([cit] * pg), ki_new)

    out = pl.pallas_call(
        functools.partial(_dsa_sample_attn_kernel, t=t, pg=pg, ng=ng, past=past),
        out_shape=SDS((db, t, BR_W), F32),
        grid_spec=pltpu.PrefetchScalarGridSpec(
            num_scalar_prefetch=1, grid=(db, ng),
            in_specs=[per_b(t, BR_W)] + [kv_page(i) for i in range(pg)] * 2
                     + [sc_spec, per_b(t, LANES), per_b(t, LANES), per_b(t, LANES), per_b(LANES, BR_W), per_b(LANES, BR_W)],
            out_specs=per_b(t, BR_W),
            scratch_shapes=[pltpu.VMEM((HEADS * t, 1), F32), pltpu.VMEM((HEADS * t, 1), F32),
                            pltpu.VMEM((HEADS * t, BR_W), F32)]),
        compiler_params=_cp(("arbitrary", "arbitrary")),
        name="dsa_sample_attention",
    )(page_table, q3, *([ckt] * pg), *([cvt] * pg), scp, scn, thr, cut, k_new, v_new)
    return out.reshape(db * t, BR_W)


def _mlstm_chunk(q, k, v, li, lf, cst, n, m, c):
    r = _iota((c, c), 0)
    cc = _iota((c, c), 1)
    hh = range(len(q))
    qk = [_dot_nt(q[h], k[h], HIGHEST) for h in hh]
    qc = [_dot_nt(q[h], cst[h], HIGHEST) for h in hh]
    b_col, mt, ei, s = [], [], [], []
    for h in hh:
        b_row, bc_ = _cumsum_col(lf[h], c)
        log_d = jnp.where(cc <= r, bc_ - b_row + _col_to_row(li[h], c), NEG_INF)
        inter = bc_ + m[h]
        mt_ = jnp.maximum(inter, jnp.max(log_d, axis=1, keepdims=True))
        b_col.append(bc_)
        mt.append(mt_)
        ei.append(jnp.exp(inter - mt_))
        s.append(qk[h] * jnp.exp(log_d - mt_))
    sv = [_hdot(s[h], v[h]) for h in hh]
    outs, c_new, n_new, m_new, wsv = [], [], [], [], []
    for h in hh:
        den = jnp.sum(s[h], axis=1, keepdims=True) + ei[h] * jnp.sum(q[h] * n[h], axis=1, keepdims=True)
        outs.append((sv[h] + ei[h] * qc[h]) / jnp.maximum(jnp.abs(den), jnp.exp(-mt[h])))
        bc = b_col[h][c - 1:c, :]
        gs = bc - b_col[h] + li[h]
        mn = jnp.maximum(bc + m[h], jnp.max(gs, axis=0, keepdims=True))
        ws = jnp.exp(gs - mn)
        dec = jnp.exp(bc + m[h] - mn)
        wsv.append(ws * v[h])
        c_new.append(dec * cst[h])
        n_new.append(dec * n[h] + jnp.sum(ws * k[h], axis=0, keepdims=True))
        m_new.append(mn)
    c_new = [c_new[h] + _dot_tn(wsv[h], k[h], HIGHEST) for h in hh]
    return outs, c_new, n_new, m_new


def _mlstm_kernel(x_ref, sm_ref, c0_ref, n0_ref, m0_ref, lng_ref, o_ref, c1_ref, n1_ref, m1_ref,
                  c_s, n_s, m_s, *, c, cb):
    i = pl.program_id(1)

    @pl.when(i == 0)
    def _():
        c_s[...] = c0_ref[...]
        n_s[...] = n0_ref[...]
        m_s[...] = m0_ref[...]

    def chunk(ci, carry):
        r0 = pl.multiple_of(ci * c, c)
        hh = range(HEADS)
        col = lambda base, h: x_ref[pl.ds(r0, c), base + HD * h:base + HD * (h + 1)]
        hs, c_new, n_new, m_new = _mlstm_chunk(
            [col(0, h) for h in hh], [col(BR_W, h) * (HD ** -0.5) for h in hh], [col(2 * BR_W, h) for h in hh],
            [sm_ref[pl.ds(r0, c), SM_MI + h:SM_MI + h + 1] for h in hh],
            [sm_ref[pl.ds(r0, c), SM_MF + h:SM_MF + h + 1] for h in hh],
            [c_s[h] for h in hh], [n_s[h] for h in hh], [m_s[h] for h in hh], c)
        outs = []
        for h in hh:
            c_s[h] = c_new[h]
            n_s[h] = n_new[h]
            m_s[h] = m_new[h]
            mu = jnp.mean(hs[h], axis=1, keepdims=True)
            var = jnp.mean(jnp.square(hs[h] - mu), axis=1, keepdims=True)
            y = (hs[h] - mu) * lax.rsqrt(var + LN_EPS) * lng_ref[:, HD * h:HD * (h + 1)]
            outs.append(y * jax.nn.sigmoid(col(3 * BR_W, h)))
        o_ref[pl.ds(r0, c), :] = jnp.concatenate(outs, axis=1)
        return carry

    lax.fori_loop(0, cb, chunk, 0)

    @pl.when(i == pl.num_programs(1) - 1)
    def _():
        c1_ref[...] = c_s[...]
        n1_ref[...] = n_s[...]
        m1_ref[...] = m_s[...]


def _seq_blocks(l, chunk, max_rows):
    nc = l // chunk
    cb = math.gcd(nc, max(1, max_rows // chunk))
    return chunk * cb, cb, nc // cb


def _state_spec(shape):
    return pl.BlockSpec((None,) + shape, lambda b, i: (b,) + (0,) * len(shape))


def _mlstm(mslab, sm, row_off, b, l, c0, n0, m0, ln_g):
    c = math.gcd(l, MLSTM_CHUNK)
    rows, cb, nblk = _seq_blocks(l, c, 512)
    off = row_off // rows
    seq = lambda w: pl.BlockSpec((rows, w), lambda bb, i: (off + bb * nblk + i, 0))
    o, c1, n1, m1 = pl.pallas_call(
        functools.partial(_mlstm_kernel, c=c, cb=cb),
        out_shape=(SDS((b * l, BR_W), F32), SDS((b, HEADS, HD, HD), F32), SDS((b, HEADS, 1, HD), F32),
                   SDS((b, HEADS, 1, 1), F32)),
        grid=(b, nblk),
        in_specs=[seq(4 * BR_W), seq(LANES), _state_spec((HEADS, HD, HD)), _state_spec((HEADS, 1, HD)),
                  _state_spec((HEADS, 1, 1)), pl.BlockSpec((1, BR_W), lambda bb, i: (0, 0))],
        out_specs=(pl.BlockSpec((rows, BR_W), lambda bb, i: (bb * nblk + i, 0)), _state_spec((HEADS, HD, HD)),
                   _state_spec((HEADS, 1, HD)), _state_spec((HEADS, 1, 1))),
        scratch_shapes=[pltpu.VMEM((HEADS, HD, HD), F32), pltpu.VMEM((HEADS, 1, HD), F32), pltpu.VMEM((HEADS, 1, 1), F32)],
        compiler_params=_cp(("parallel", "arbitrary")),
        name="mlstm",
    )(mslab, sm, c0, n0.reshape(b, HEADS, 1, HD), m0.reshape(b, HEADS, 1, 1), ln_g.reshape(1, BR_W))
    return o, c1, n1.reshape(b, HEADS, HD), m1.reshape(b, HEADS)


def _shift_rows(x, halo, s):
    rows = x.shape[0]
    rolled = pltpu.roll(x, s, 0)
    first = jnp.where(_iota((SUBLANES, x.shape[1]), 0) < s, pltpu.roll(halo, s, 0), rolled[0:SUBLANES])
    return first if rows == SUBLANES else jnp.concatenate([first, rolled[SUBLANES:]], axis=0)


def _halo_specs(rows, w, off_rows, nt):
    hb = rows // SUBLANES
    tile = pl.BlockSpec((rows, w), lambda b, i: (off_rows // rows + b * nt + i, 0))
    halo = pl.BlockSpec((SUBLANES, w), lambda b, i: (jnp.maximum(off_rows // SUBLANES + (b * nt + i) * hb - 1, 0), 0))
    state = pl.BlockSpec((None, SUBLANES, w), lambda b, i: (b, 0, 0))
    return tile, halo, state


def _pad_state_rows(st, w):
    b, r, w0 = st.shape
    return jnp.pad(st, ((0, 0), (SUBLANES - r, 0), (0, w - w0)))


def _cumsum_rows(x):
    row = _iota(x.shape, 0)
    sh = 1
    while sh < x.shape[0]:
        x = x + jnp.where(row >= sh, pltpu.roll(x, sh, 0), 0.0)
        sh *= 2
    return x


def _unit_lower_inverse(ms, c):
    r = _iota((c, c), 0)
    cc = _iota((c, c), 1)
    fold = jnp.where((_iota((c, SUBLANES), 0) & 7) == _iota((c, SUBLANES), 1), 1.0, 0.0)
    same8 = (r >> 3) == (cc >> 3)
    m8s = [_hdot(jnp.where(same8, m, 0.0), fold) for m in ms]
    zs = [jnp.where(r == cc, 1.0, 0.0)] * len(ms)

    def pivot(z, s):
        return jnp.broadcast_to(z.reshape(c // SUBLANES, SUBLANES, c)[:, s:s + 1, :],
                                (c // SUBLANES, SUBLANES, c)).reshape(c, c)

    for s in range(SUBLANES - 1):
        zs = [z + m8[:, s:s + 1] * pivot(z, s) for z, m8 in zip(zs, m8s)]
    sh = 3
    while (1 << sh) < c:
        off = ((r >> (sh + 1)) == (cc >> (sh + 1))) & ((r >> sh) != (cc >> sh))
        ts = [_hdot(z, jnp.where(off, m, 0.0)) for z, m in zip(zs, ms)]
        zs = [z + _hdot(t, z) for z, t in zip(zs, ts)]
        sh += 1
    return zs


def _lin_scan(a_ref, y0_ref, phi_ref, psi_ref, s_s, y_s, c, cb):
    def chunk(ci, carry):
        r0 = pl.multiple_of(ci * c, c)
        sts = [s_s[h] for h in range(HEADS)]
        new = [_hdot(sts[h], phi_ref[ci, h]) for h in range(HEADS)]
        ys = [_dot_nt(a_ref[pl.ds(r0, c), HD * h:HD * (h + 1)], sts[h], HIGHEST) for h in range(HEADS)]
        for h in range(HEADS):
            s_s[h] = new[h] + psi_ref[ci, h]
        y_s[pl.ds(r0, c), :] = jnp.concatenate(ys, axis=1) + y0_ref[pl.ds(r0, c), :]
        return carry

    lax.fori_loop(0, cb, chunk, 0)


def _chunk_specs(rows, cb, nblk, widths):
    row_specs = [pl.BlockSpec((rows, w), lambda bb, i: (bb * nblk + i, 0)) for w in widths]
    mat = pl.BlockSpec((cb, HEADS, HD, HD), lambda bb, i: (bb * nblk + i, 0, 0, 0))
    return row_specs, mat


def _rwkv_prep_kernel(p_ref, halo_ref, st_ref, vec_ref, w2_ref, a2_ref, g2_ref, rs_ref, post_ref):
    i = pl.program_id(1)
    p = p_ref[...]
    rows = p.shape[0]
    halo = jnp.where(i == 0, st_ref[...], halo_ref[...])
    prev = _shift_rows(p, halo, 1)
    mu = jnp.concatenate([vec_ref[0:1, :], vec_ref[1:2, :], vec_ref[2:3, :], vec_ref[3:4, :]], axis=1)
    pm = p + (prev - p) * mu
    r = pm[:, 0:BR_W]
    k = pm[:, BR_W:2 * BR_W]
    v = pm[:, 2 * BR_W:3 * BR_W]
    o = 3 * BR_W
    wd = pm[:, o:o + RWKV_W_LORA]
    ad = pm[:, o + RWKV_W_LORA:o + RWKV_W_LORA + RWKV_A_LORA]
    gd = pm[:, o + RWKV_W_LORA + RWKV_A_LORA:RWKV_PROJ]
    w0, a0, kkp, ka, rk = (vec_ref[4:5, :], vec_ref[5:6, :], vec_ref[6:7, :], vec_ref[7:8, :], vec_ref[8:9, :])
    wlog = -jax.nn.softplus(-(w0 + _hdot(jnp.tanh(wd), w2_ref[...]))) - 0.5
    log_decay = -jnp.exp(wlog)
    a = jax.nn.sigmoid(a0 + _hdot(ad, a2_ref[...]))
    g = _hdot(jax.nn.sigmoid(gd), g2_ref[...])
    masks = _head_masks(rows)
    kkr = k * kkp
    kk = kkr * lax.rsqrt(jnp.maximum(_seg_sum(kkr * kkr, masks), 1e-12))
    k2 = k * (1.0 + (a - 1.0) * ka)
    bonus = _seg_sum(r * k2 * rk, masks) * v
    for n, t in enumerate((r, log_decay, k2, v, kk, kk * a)):
        rs_ref[:, n * BR_W:(n + 1) * BR_W] = t
    post_ref[:, 0:BR_W] = bonus
    post_ref[:, BR_W:2 * BR_W] = g


def _rwkv_chunk_kernel(rs_ref, rh_ref, y0_ref, phi_ref, psi_ref, *, c, cb, gc):
    r_i = _iota((c, c), 0)
    c_i = _iota((c, c), 1)
    eye = _iota((HD, HD), 0) == _iota((HD, HD), 1)
    jobs = [(g, h) for g in range(gc) for h in range(HEADS)]
    hs = lambda x, h: x[:, HD * h:HD * (h + 1)]

    def group(gi, carry):
        r0s = [pl.multiple_of((gi * gc + g) * c, c) for g in range(gc)]
        w = []
        for r0 in r0s:
            r, lw, k, v, kk, kka = (rs_ref[pl.ds(r0, c), n * BR_W:(n + 1) * BR_W] for n in range(6))
            cum = _cumsum_rows(lw)
            pc = cum[c - 1:c, :]
            inv = jnp.exp(-cum)
            rest = jnp.exp(pc - cum)
            w.append(dict(at=-kk * jnp.exp(cum - lw),
                          bt=kka * inv, kt=k * inv,
                          rt=r * jnp.exp(cum), bd=kka * rest, kd=k * rest, v=v, epc=jnp.exp(pc)))
        big = [_dot_nt(jnp.concatenate([hs(w[g]["at"], h), hs(w[g]["rt"], h)], axis=0),
                       jnp.concatenate([hs(w[g]["bt"], h), hs(w[g]["kt"], h)], axis=0), HIGHEST) for g, h in jobs]
        mab = [jnp.where(c_i < r_i, x[0:c, 0:c], 0.0) for x in big]
        mak = [jnp.where(c_i < r_i, x[0:c, c:2 * c], 0.0) for x in big]
        nn = [jnp.concatenate([jnp.where(c_i <= r_i, x[c:2 * c, 0:c], 0.0),
                               jnp.where(c_i <= r_i, x[c:2 * c, c:2 * c], 0.0)], axis=1) for x in big]
        tinv = _unit_lower_inverse(mab, c)
        mv = [_hdot(m, hs(w[g]["v"], h)) for m, (g, h) in zip(mak, jobs)]
        hg = [_hdot(t, jnp.concatenate([hs(w[g]["at"], h), x], axis=1)) for t, x, (g, h) in zip(tinv, mv, jobs)]
        out = [_hdot(n2, jnp.concatenate([x, jnp.concatenate([jnp.zeros((c, HD), F32), hs(w[g]["v"], h)], axis=1)], axis=0))
               for n2, x, (g, h) in zip(nn, hg, jobs)]
        hb = [_dot_tn(x, hs(w[g]["bd"], h), HIGHEST) for x, (g, h) in zip(hg, jobs)]
        vk = [_dot_tn(hs(w[g]["v"], h), hs(w[g]["kd"], h), HIGHEST) for g, h in jobs]
        for g in range(gc):
            sel = [n for n, (gg, _) in enumerate(jobs) if gg == g]
            rh_ref[pl.ds(r0s[g], c), :] = w[g]["rt"] + jnp.concatenate([out[n][:, 0:HD] for n in sel], axis=1)
            y0_ref[pl.ds(r0s[g], c), :] = jnp.concatenate([out[n][:, HD:2 * HD] for n in sel], axis=1)
            phi_ref[gi * gc + g] = jnp.stack([jnp.where(eye, jnp.broadcast_to(hs(w[g]["epc"], jobs[n][1]), (HD, HD)), 0.0)
                                              + hb[n][0:HD] for n in sel])
            psi_ref[gi * gc + g] = jnp.stack([hb[n][HD:2 * HD] + vk[n] for n in sel])
        return carry

    lax.fori_loop(0, cb // gc, group, 0)


def _rwkv_scan_kernel(rh_ref, y0_ref, phi_ref, psi_ref, post_ref, s0_ref, ln_ref, o_ref, s1_ref, s_s, y_s, *, c, cb):
    i = pl.program_id(1)

    @pl.when(i == 0)
    def _():
        s_s[...] = s0_ref[...]

    _lin_scan(rh_ref, y0_ref, phi_ref, psi_ref, s_s, y_s, c, cb)
    y = y_s[...]
    masks = _head_masks(c * cb)
    mu = _seg_sum(y, masks) * (1.0 / HD)
    var = _seg_sum(jnp.square(y - mu), masks) * (1.0 / HD)
    y = (y - mu) * lax.rsqrt(var + LN_EPS) * ln_ref[0:1, :] + ln_ref[1:2, :]
    o_ref[...] = (y + post_ref[:, 0:BR_W]) * post_ref[:, BR_W:2 * BR_W]

    @pl.when(i == pl.num_programs(1) - 1)
    def _():
        s1_ref[...] = s_s[...]


def _rwkv(rslab, row_off, b, l, sh0, s0, lp):
    rows = math.gcd(l, 256)
    nt = l // rows
    tile, halo, state = _halo_specs(rows, RWKV_PROJ, row_off, nt)
    quarter = lambda a: a.reshape(4, BR_W)
    vec = jnp.concatenate([quarter(lp["rw_mu"]), lp["rw_w0"][None], lp["rw_a0"][None], lp["rw_kk"][None],
                           lp["rw_ka"][None], lp["rw_rk"].reshape(1, BR_W), jnp.zeros((7, BR_W), F32)], axis=0)
    rs, post = pl.pallas_call(
        _rwkv_prep_kernel,
        out_shape=(SDS((b * l, 6 * BR_W), F32), SDS((b * l, 2 * BR_W), F32)),
        grid=(b, nt),
        in_specs=[tile, halo, state, pl.BlockSpec((16, BR_W), lambda bb, i: (0, 0)),
                  pl.BlockSpec((RWKV_W_LORA, BR_W), lambda bb, i: (0, 0)),
                  pl.BlockSpec((RWKV_A_LORA, BR_W), lambda bb, i: (0, 0)),
                  pl.BlockSpec((RWKV_G_LORA, BR_W), lambda bb, i: (0, 0))],
        out_specs=(pl.BlockSpec((rows, 6 * BR_W), lambda bb, i: (bb * nt + i, 0)),
                   pl.BlockSpec((rows, 2 * BR_W), lambda bb, i: (bb * nt + i, 0))),
        compiler_params=_cp(("parallel", "parallel")),
        name="rwkv_prep",
    )(rslab, rslab, _pad_state_rows(sh0[:, None, :], RWKV_PROJ), vec, lp["rw_w2"], lp["rw_a2"], lp["rw_g2"])

    c = math.gcd(l, 64)
    nc = l // c
    prow, pcb, pblk = _seq_blocks(l, c, 256)
    (rs_spec, rh_spec, y0_spec), mat = _chunk_specs(prow, pcb, pblk, (6 * BR_W, BR_W, BR_W))
    rh, y0, phi, psi = pl.pallas_call(
        functools.partial(_rwkv_chunk_kernel, c=c, cb=pcb, gc=math.gcd(pcb, 2)),
        out_shape=(SDS((b * l, BR_W), F32), SDS((b * l, BR_W), F32),
                   SDS((b * nc, HEADS, HD, HD), F32), SDS((b * nc, HEADS, HD, HD), F32)),
        grid=(b, pblk),
        in_specs=[rs_spec],
        out_specs=(rh_spec, y0_spec, mat, mat),
        compiler_params=_cp(("parallel", "parallel")),
        name="rwkv_chunk",
    )(rs)

    srow, scb, sblk = _seq_blocks(l, c, 512)
    (rh_spec, y0_spec, post_spec, o_spec), mat = _chunk_specs(srow, scb, sblk, (BR_W, BR_W, 2 * BR_W, BR_W))
    ln = jnp.stack([lp["rw_ln_g"], lp["rw_ln_b"]] + [jnp.zeros((BR_W,), F32)] * 6)
    o, s1 = pl.pallas_call(
        functools.partial(_rwkv_scan_kernel, c=c, cb=scb),
        out_shape=(SDS((b * l, BR_W), F32), SDS((b, HEADS, HD, HD), F32)),
        grid=(b, sblk),
        in_specs=[rh_spec, y0_spec, mat, mat, post_spec, _state_spec((HEADS, HD, HD)),
                  pl.BlockSpec((SUBLANES, BR_W), lambda bb, i: (0, 0))],
        out_specs=(o_spec, _state_spec((HEADS, HD, HD))),
        scratch_shapes=[pltpu.VMEM((HEADS, HD, HD), F32), pltpu.VMEM((srow, BR_W), F32)],
        compiler_params=_cp(("parallel", "arbitrary")),
        name="rwkv_scan",
    )(rh, y0, phi, psi, post, s0, ln)
    return o, s1


def _gdn_prep_kernel(x_ref, halo_ref, st_ref, taps_ref, o_ref):
    i = pl.program_id(1)
    x = x_ref[:, 0:GDN_CONV_CH]
    rows = x.shape[0]
    halo = jnp.where(i == 0, st_ref[:, 0:GDN_CONV_CH], halo_ref[:, 0:GDN_CONV_CH])
    conv = x * taps_ref[CONV_W - 1:CONV_W, :]
    for s in range(1, CONV_W):
        conv = conv + _shift_rows(x, halo, s) * taps_ref[CONV_W - 1 - s:CONV_W - s, :]
    conv = conv * jax.nn.sigmoid(conv)
    masks = _head_masks(rows)
    for n in range(3):
        t = conv[:, n * BR_W:(n + 1) * BR_W]
        if n < 2:
            t = t * lax.rsqrt(jnp.maximum(_seg_sum(t * t, masks), 1e-12))
        if n == 0:
            t = t * (HD ** -0.5)
        o_ref[:, n * BR_W:(n + 1) * BR_W] = t


def _gdn_chunk_kernel(x_ref, sm_ref, qh_ref, o0_ref, phi_ref, psi_ref, *, c, cb, gc):
    r_i = _iota((c, c), 0)
    c_i = _iota((c, c), 1)
    eye = _iota((HD, HD), 0) == _iota((HD, HD), 1)
    jobs = [(g, h) for g in range(gc) for h in range(HEADS)]

    def group(gi, carry):
        r0s = [pl.multiple_of((gi * gc + g) * c, c) for g in range(gc)]
        q, k, v, beta, dec, eg, gl, wv = [], [], [], [], [], [], [], []
        for g, h in jobs:
            r0, lo = r0s[g], HD * h
            q.append(x_ref[pl.ds(r0, c), lo:lo + HD])
            k.append(x_ref[pl.ds(r0, c), BR_W + lo:BR_W + lo + HD])
            v.append(x_ref[pl.ds(r0, c), 2 * BR_W + lo:2 * BR_W + lo + HD])
            beta.append(sm_ref[pl.ds(r0, c), SM_GB + h:SM_GB + h + 1])
            gam_row, gam_col = _cumsum_col(sm_ref[pl.ds(r0, c), SM_GA + h:SM_GA + h + 1], c)
            dec.append(jnp.exp(jnp.where(c_i <= r_i, gam_col - gam_row, NEG_INF)))
            eg.append(jnp.exp(gam_col))
            gl.append(gam_col[c - 1:c, :])
            wv.append(jnp.exp(gam_col[c - 1:c, :] - gam_col))
        n = range(len(jobs))
        big = [_dot_nt(jnp.concatenate([k[i], q[i]], axis=0), k[i], HIGHEST) for i in n]
        tinv = _unit_lower_inverse([-(beta[i] * big[i][0:c] * jnp.where(c_i < r_i, dec[i], 0.0)) for i in n], c)
        x = [_hdot(tinv[i], jnp.concatenate([beta[i] * v[i], (beta[i] * eg[i]) * k[i]], axis=1)) for i in n]
        out = [_hdot(big[i][c:2 * c] * dec[i], x[i]) for i in n]
        xb = [_dot_tn(x[i], wv[i] * k[i], HIGHEST) for i in n]
        for g in range(gc):
            sel = [i for i in n if jobs[i][0] == g]
            qh_ref[pl.ds(r0s[g], c), :] = jnp.concatenate([eg[i] * q[i] - out[i][:, HD:2 * HD] for i in sel], axis=1)
            o0_ref[pl.ds(r0s[g], c), :] = jnp.concatenate([out[i][:, 0:HD] for i in sel], axis=1)
            phi_ref[gi * gc + g] = jnp.stack([jnp.where(eye, jnp.broadcast_to(jnp.exp(gl[i]), (HD, HD)), 0.0)
                                              - xb[i][HD:2 * HD] for i in sel])
            psi_ref[gi * gc + g] = jnp.stack([xb[i][0:HD] for i in sel])
        return carry

    lax.fori_loop(0, cb // gc, group, 0)


def _gdn_scan_kernel(qh_ref, o0_ref, phi_ref, psi_ref, z_ref, s0_ref, gn_ref, o_ref, s1_ref, s_s, y_s, *, c, cb):
    i = pl.program_id(1)

    @pl.when(i == 0)
    def _():
        s_s[...] = s0_ref[...]

    _lin_scan(qh_ref, o0_ref, phi_ref, psi_ref, s_s, y_s, c, cb)
    o = y_s[...]
    ms = _seg_sum(o * o, _head_masks(c * cb)) * (1.0 / HD)
    z = z_ref[:, 3 * BR_W:4 * BR_W]
    o_ref[...] = o * lax.rsqrt(ms + EPS) * gn_ref[...] * (z * jax.nn.sigmoid(z))

    @pl.when(i == pl.num_programs(1) - 1)
    def _():
        s1_ref[...] = s_s[...]


def _gdn(gslab, sm, row_off, b, l, cv0, s0, lp):
    rows = math.gcd(l, 256)
    nt = l // rows
    w = 4 * BR_W
    tile, halo, state = _halo_specs(rows, w, row_off, nt)
    taps = jnp.pad(lp["gd_conv"], ((0, SUBLANES - CONV_W), (0, 0)))
    qkv = pl.pallas_call(
        _gdn_prep_kernel,
        out_shape=SDS((b * l, GDN_CONV_CH), F32),
        grid=(b, nt),
        in_specs=[tile, halo, state, pl.BlockSpec((SUBLANES, GDN_CONV_CH), lambda bb, i: (0, 0))],
        out_specs=pl.BlockSpec((rows, GDN_CONV_CH), lambda bb, i: (bb * nt + i, 0)),
        compiler_params=_cp(("parallel", "parallel")),
        name="gdn_prep",
    )(gslab, gslab, _pad_state_rows(cv0, w), taps)

    c = math.gcd(l, GDN_CHUNK)
    nc = l // c
    prow, pcb, pblk = _seq_blocks(l, c, 256)
    (x_spec, qh_spec, o0_spec), mat = _chunk_specs(prow, pcb, pblk, (GDN_CONV_CH, BR_W, BR_W))
    sm_spec = pl.BlockSpec((prow, LANES), lambda bb, i: (row_off // prow + bb * pblk + i, 0))
    qh, o0, phi, psi = pl.pallas_call(
        functools.partial(_gdn_chunk_kernel, c=c, cb=pcb, gc=math.gcd(pcb, 2)),
        out_shape=(SDS((b * l, BR_W), F32), SDS((b * l, BR_W), F32),
                   SDS((b * nc, HEADS, HD, HD), F32), SDS((b * nc, HEADS, HD, HD), F32)),
        grid=(b, pblk),
        in_specs=[x_spec, sm_spec],
        out_specs=(qh_spec, o0_spec, mat, mat),
        compiler_params=_cp(("parallel", "parallel")),
        name="gdn_chunk",
    )(qkv, sm)

    srow, scb, sblk = _seq_blocks(l, c, 512)
    (qh_spec, o0_spec, o_spec), mat = _chunk_specs(srow, scb, sblk, (BR_W, BR_W, BR_W))
    z_spec = pl.BlockSpec((srow, w), lambda bb, i: (row_off // srow + bb * sblk + i, 0))
    o, s1 = pl.pallas_call(
        functools.partial(_gdn_scan_kernel, c=c, cb=scb),
        out_shape=(SDS((b * l, BR_W), F32), SDS((b, HEADS, HD, HD), F32)),
        grid=(b, sblk),
        in_specs=[qh_spec, o0_spec, mat, mat, z_spec, _state_spec((HEADS, HD, HD)),
                  pl.BlockSpec((1, BR_W), lambda bb, i: (0, 0))],
        out_specs=(o_spec, _state_spec((HEADS, HD, HD))),
        scratch_shapes=[pltpu.VMEM((HEADS, HD, HD), F32), pltpu.VMEM((srow, BR_W), F32)],
        compiler_params=_cp(("parallel", "arbitrary")),
        name="gdn_scan",
    )(qh, o0, phi, psi, gslab, s0, jnp.tile(lp["gd_norm"], HEADS).reshape(1, BR_W))
    return o, s1


def _rope_tables(pos):
    half = HD // 2
    inv = ROPE_THETA ** (-jnp.arange(half, dtype=F32) / half)
    ang = pos.astype(F32)[:, None] * inv[None, :]
    cos, sin = jnp.cos(ang), jnp.sin(ang)
    zero = jnp.zeros_like(sin)
    tile = lambda a, b: jnp.concatenate([a, b, a, b], axis=1)
    return tile(cos, cos), tile(-sin, zero), tile(zero, sin)


def _pad_w_in(w):
    sizes = (BR_W, BR_W, BR_W, IDX_HEADS * IDX_DIM, IDX_DIM, IDX_HEADS, BR_W, BR_W, BR_W, HEADS, HEADS, BR_W,
             RWKV_PROJ, GDN_CONV_CH, HEADS, HEADS, BR_W, N_BRANCH * w.shape[0])
    cuts = np.concatenate([[0], np.cumsum(sizes)])
    col = lambda i: w[:, cuts[i]:cuts[i + 1]]
    (a_q, a_k, a_v, a_iq, a_ik, a_iw, m_q, m_k, m_v, m_i, m_f, m_o, r_p, g_qkv, g_b, g_a, g_z, gate) = (
        col(i) for i in range(len(sizes)))
    small = jnp.concatenate([a_ik, a_iw, m_i, m_f, g_b, g_a, jnp.zeros((w.shape[0], LANES - SM_END), w.dtype)], axis=1)
    return jnp.concatenate([a_q, a_k, a_v, a_iq, small, m_q, m_k, m_v, m_o, r_p, g_qkv, g_z, gate], axis=1).astype(BF16)


def _small_params(lp):
    lane_put = lambda v, at: jnp.zeros((LANES,), F32).at[at:at + v.shape[0]].set(v)
    rows = [lane_put(lp["idx_k_norm"], SM_IK),
            lane_put(lp["ml_i_bias"], SM_MI) + lane_put(lp["ml_f_bias"], SM_MF) + lane_put(lp["gd_dt_bias"], SM_GA),
            lane_put(lp["gd_A_log"], SM_GA)]
    return jnp.stack(rows + [jnp.zeros((LANES,), F32)] * (SUBLANES - len(rows)))


def kernel(x_prompt, x_sample, cache_k, cache_v, cache_idx_k, state_mlstm_C, state_mlstm_n, state_mlstm_m, state_rwkv_S, state_rwkv_shift, state_gdn_S, state_gdn_conv, page_table, ffn1_norm, ffn1_w1, ffn1_w3, ffn1_w2, mix_norm, w_in, idx_k_norm, ml_i_bias, ml_f_bias, ml_norm, rw_mu, rw_w0, rw_w2, rw_a0, rw_a2, rw_g2, rw_kk, rw_ka, rw_rk, rw_ln_g, rw_ln_b, gd_conv, gd_A_log, gd_dt_bias, gd_norm, w_branch, w_out, ffn2_norm, ffn2_w1, ffn2_w3, ffn2_w2, final_norm):
    bp, lp_, d = x_prompt.shape
    db, t, _ = x_sample.shape
    assert bp == 1 and t % SUBLANES == 0 and t <= LANES and lp_ % SUBLANES == 0
    depth = w_in.shape[0]
    past = page_table.shape[1] * PAGE_SIZE
    ns = db * t

    x = jnp.concatenate([x_prompt.reshape(lp_, d), x_sample.reshape(ns, d)], axis=0)
    pos = jnp.concatenate([jnp.arange(lp_, dtype=I32), jnp.tile(past + jnp.arange(t, dtype=I32), db)])
    cos, sa, sb = _rope_tables(pos)
    zeros = lambda *s: jnp.zeros(s, F32)

    outs_p, outs_s = [], []
    for l in range(depth):
        lp = dict(idx_k_norm=idx_k_norm[l], ml_i_bias=ml_i_bias[l], ml_f_bias=ml_f_bias[l],
                  rw_mu=rw_mu[l], rw_w0=rw_w0[l], rw_w2=rw_w2[l], rw_a0=rw_a0[l], rw_a2=rw_a2[l], rw_g2=rw_g2[l],
                  rw_kk=rw_kk[l], rw_ka=rw_ka[l], rw_rk=rw_rk[l], rw_ln_g=rw_ln_g[l], rw_ln_b=rw_ln_b[l],
                  gd_conv=gd_conv[l], gd_A_log=gd_A_log[l], gd_dt_bias=gd_dt_bias[l], gd_norm=gd_norm[l])
        x = _ffn(x, ffn1_norm[l], ffn1_w1[l], ffn1_w3[l], ffn1_w2[l])
        aqkv, iq, sm, mslab, rslab, gslab, gate = _proj(x, mix_norm[l], _pad_w_in(w_in[l]), cos, sa, sb, _small_params(lp))

        oa_p = _dsa_prompt(iq, sm, aqkv, lp_)
        oa_s = _dsa_sample(iq[lp_:], sm[lp_:], aqkv[lp_:], cache_k, cache_v, cache_idx_k, page_table, l, db, t)
        om_p, c_p, n_p, m_p = _mlstm(mslab, sm, 0, 1, lp_, zeros(1, HEADS, HD, HD), zeros(1, HEADS, HD), zeros(1, HEADS), ml_norm[l])
        om_s, c_s, n_s, m_s = _mlstm(mslab, sm, lp_, db, t, state_mlstm_C[l], state_mlstm_n[l], state_mlstm_m[l], ml_norm[l])
        or_p, rs_p = _rwkv(rslab, 0, 1, lp_, zeros(1, RWKV_PROJ), zeros(1, HEADS, HD, HD), lp)
        or_s, rs_s = _rwkv(rslab, lp_, db, t, state_rwkv_shift[l], state_rwkv_S[l], lp)
        og_p, gs_p = _gdn(gslab, sm, 0, 1, lp_, zeros(1, CONV_W - 1, GDN_CONV_CH), zeros(1, HEADS, HD, HD), lp)
        og_s, gs_s = _gdn(gslab, sm, lp_, db, t, state_gdn_conv[l], state_gdn_S[l], lp)

        cat = lambda a, b: jnp.concatenate([a, b], axis=0)
        x = _merge(x, cat(oa_p, oa_s), cat(om_p, om_s), cat(or_p, or_s), cat(og_p, og_s), gate, w_branch[l], w_out[l])
        x = _ffn(x, ffn2_norm[l], ffn2_w1[l], ffn2_w3[l], ffn2_w2[l])

        k_all = aqkv[:, BR_W:2 * BR_W]
        v_all = aqkv[:, 2 * BR_W:3 * BR_W]
        ik_all = sm[:, SM_IK:SM_IK + IDX_DIM]
        gq = gslab[:, 0:GDN_CONV_CH]
        outs_p.append((k_all[:lp_].reshape(1, lp_, HEADS, HD), v_all[:lp_].reshape(1, lp_, HEADS, HD),
                       ik_all[:lp_].reshape(1, lp_, IDX_DIM), c_p, n_p, m_p, rs_p, rslab[lp_ - 1:lp_],
                       gs_p, gq[lp_ - (CONV_W - 1):lp_].reshape(1, CONV_W - 1, GDN_CONV_CH)))
        outs_s.append((k_all[lp_:].reshape(db, t, HEADS, HD), v_all[lp_:].reshape(db, t, HEADS, HD),
                       ik_all[lp_:].reshape(db, t, IDX_DIM), c_s, n_s, m_s, rs_s,
                       rslab[lp_:].reshape(db, t, RWKV_PROJ)[:, -1],
                       gs_s, gq[lp_:].reshape(db, t, GDN_CONV_CH)[:, t - (CONV_W - 1):]))

    y = _final_rms(x, final_norm)
    st_p = [jnp.stack(z) for z in zip(*outs_p)]
    st_s = [jnp.stack(z) for z in zip(*outs_s)]
    return (y[:lp_].reshape(1, lp_, d), y[lp_:].reshape(db, t, d), *st_p, *st_s)
```

```python
import functools
import math

import jax
import jax.numpy as jnp
import numpy as np
from jax import lax
from jax.experimental import pallas as pl
from jax.experimental.pallas import tpu as pltpu

F32 = jnp.float32
BF16 = jnp.bfloat16
I32 = jnp.int32
SDS = jax.ShapeDtypeStruct

HD = 64
HEADS = 4
BR_W = HEADS * HD
N_BRANCH = 4
IDX_HEADS = 8
IDX_DIM = 64
TOPK_MAX = 256
TOPM = 12
ROPE_THETA = 10000.0
MLSTM_CHUNK = 64
GDN_CHUNK = 64
CONV_W = 4
RWKV_W_LORA = 64
RWKV_A_LORA = 64
RWKV_G_LORA = 128
RWKV_PROJ = 3 * BR_W + RWKV_W_LORA + RWKV_A_LORA + RWKV_G_LORA
GDN_CONV_CH = 3 * BR_W
EPS = 1e-6
LN_EPS = 1e-5
PAGE_SIZE = 128

LANES = 128
SUBLANES = 8
VMEM_LIMIT_BYTES = 58 * 1024 * 1024

SM_IK = 0
SM_IW = 64
SM_MI = 72
SM_MF = 76
SM_GB = 80
SM_GA = 84
SM_END = 88

PC_AQKV = 0
PC_IQ = 768
PC_SM = 1280
PC_M = 1408
PC_R = 2432
PC_G = 3456
PC_GATE = 4480
PC_END = 8576

HIGHEST = lax.Precision.HIGHEST
INT_MIN = -2147483648
NEG_INF = float("-inf")


def _cp(sem):
    return pltpu.CompilerParams(dimension_semantics=sem, vmem_limit_bytes=VMEM_LIMIT_BYTES)


def _resident(shape):
    return pl.BlockSpec(shape, lambda *_: (0,) * len(shape), pipeline_mode=pl.Buffered(1))


def _rows(tm, w, off_blocks=0):
    return pl.BlockSpec((tm, w), lambda i, *_: (i + off_blocks, 0))


def _dot(a, b):
    return jnp.dot(a, b, preferred_element_type=F32)


def _dot_nt(a, b, precision=None):
    return lax.dot_general(a, b, (((1,), (1,)), ((), ())), preferred_element_type=F32, precision=precision)


def _dot_tn(a, b, precision=None):
    return lax.dot_general(a, b, (((0,), (0,)), ((), ())), preferred_element_type=F32, precision=precision)


def _hdot(a, b):
    return jnp.dot(a, b, preferred_element_type=F32, precision=HIGHEST)


def _rms(x, g):
    return x * lax.rsqrt(jnp.mean(x * x, axis=-1, keepdims=True) + EPS) * g


def _iota(shape, dim):
    return lax.broadcasted_iota(I32, shape, dim)


def _head_masks(rows):
    lane = _iota((rows, BR_W), 1)
    return [(lane >= HD * h) & (lane < HD * (h + 1)) for h in range(HEADS)]


def _seg_sum(x, masks):
    out = jnp.zeros_like(x)
    for m in masks:
        s = jnp.sum(jnp.where(m, x, 0.0), axis=1, keepdims=True)
        out = jnp.where(m, s, out)
    return out


def _col_to_row(col, n):
    eye = _iota((n, n), 0) == _iota((n, n), 1)
    return jnp.sum(jnp.where(eye, jnp.broadcast_to(col, (n, n)), 0.0), axis=0, keepdims=True)


def _row_to_col(row, n):
    eye = _iota((n, n), 0) == _iota((n, n), 1)
    return jnp.sum(jnp.where(eye, jnp.broadcast_to(row, (n, n)), 0.0), axis=1, keepdims=True)


def _cumsum_col(col, n):
    r = _iota((n, n), 0)
    c = _iota((n, n), 1)
    row = jnp.sum(jnp.where(r <= c, jnp.broadcast_to(col, (n, n)), 0.0), axis=0, keepdims=True)
    return row, _row_to_col(row, n)


def _ffn_kernel(x_ref, g_ref, w1_ref, w3_ref, w2_ref, o_ref):
    x = x_ref[...]
    h = _rms(x, g_ref[...]).astype(BF16)
    a = _dot(h, w1_ref[...])
    b = _dot(h, w3_ref[...])
    y = (a * jax.nn.sigmoid(a) * b).astype(BF16)
    o_ref[...] = x + 0.5 * _dot(y, w2_ref[...])


def _ffn(x, g, w1, w3, w2):
    na, d = x.shape
    f = w1.shape[1]
    tm = math.gcd(na, 256)
    return pl.pallas_call(
        _ffn_kernel,
        out_shape=SDS((na, d), F32),
        grid=(na // tm,),
        in_specs=[_rows(tm, d), _resident((1, d)), _resident((d, f)), _resident((d, f)), _resident((f, d))],
        out_specs=_rows(tm, d),
        compiler_params=_cp(("parallel",)),
        name="ffn",
    )(x, g.reshape(1, d), w1.astype(BF16), w3.astype(BF16), w2.astype(BF16))


def _rms_kernel(x_ref, g_ref, o_ref):
    o_ref[...] = _rms(x_ref[...], g_ref[...])


def _final_rms(x, g):
    na, d = x.shape
    tm = math.gcd(na, 512)
    return pl.pallas_call(
        _rms_kernel,
        out_shape=SDS((na, d), F32),
        grid=(na // tm,),
        in_specs=[_rows(tm, d), _resident((1, d))],
        out_specs=_rows(tm, d),
        compiler_params=_cp(("parallel",)),
        name="final_rms",
    )(x, g.reshape(1, d))


def _rope_slab(z, cos, sa, sb):
    return z * cos + pltpu.roll(z, LANES - HD // 2, 1) * sa + pltpu.roll(z, HD // 2, 1) * sb


def _proj_kernel(x_ref, g_ref, w_ref, cos_ref, sa_ref, sb_ref, sp_ref,
                 aqkv_ref, iq_ref, sm_ref, m_ref, r_ref, gq_ref, gate_ref):
    h = _rms(x_ref[...], g_ref[...]).astype(BF16)
    cos, sa, sb = cos_ref[...], sa_ref[...], sb_ref[...]

    def mm(a, b):
        return _dot(h, w_ref[:, a:b])

    za = mm(PC_AQKV, PC_IQ)
    for s in range(4):
        aqkv_ref[:, s * LANES:(s + 1) * LANES] = _rope_slab(za[:, s * LANES:(s + 1) * LANES], cos, sa, sb)
    aqkv_ref[:, 2 * BR_W:3 * BR_W] = za[:, 2 * BR_W:3 * BR_W]
    zi = mm(PC_IQ, PC_SM)
    for s in range(4):
        iq_ref[:, s * LANES:(s + 1) * LANES] = _rope_slab(zi[:, s * LANES:(s + 1) * LANES], cos, sa, sb)

    zs = mm(PC_SM, PC_M)
    lane = _iota(zs.shape, 1)
    gn, bias, alog = sp_ref[0:1, :], sp_ref[1:2, :], sp_ref[2:3, :]
    ms = jnp.sum(jnp.where(lane < IDX_DIM, zs * zs, 0.0), axis=1, keepdims=True) * (1.0 / IDX_DIM)
    ik = _rope_slab(zs * lax.rsqrt(ms + EPS) * gn, cos, sa, sb)
    t = zs + bias
    out = jnp.where(lane < SM_IW, ik,
          jnp.where(lane < SM_MI, zs * (IDX_HEADS ** -0.5),
          jnp.where(lane < SM_MF, t,
          jnp.where(lane < SM_GB, jax.nn.log_sigmoid(t),
          jnp.where(lane < SM_GA, jax.nn.sigmoid(zs),
          jnp.where(lane < SM_END, -jnp.exp(alog) * jax.nn.softplus(t), 0.0))))))
    sm_ref[...] = out

    m_ref[...] = mm(PC_M, PC_R)
    r_ref[...] = mm(PC_R, PC_G)
    gq_ref[...] = mm(PC_G, PC_GATE)
    gate_ref[...] = mm(PC_GATE, PC_END)


def _proj(x, g, w_pad, cos, sa, sb, sp):
    na, d = x.shape
    tm = math.gcd(na, 256)
    widths = (3 * BR_W, IDX_HEADS * IDX_DIM, LANES, 4 * BR_W, RWKV_PROJ, 4 * BR_W, N_BRANCH * d)
    return pl.pallas_call(
        _proj_kernel,
        out_shape=tuple(SDS((na, w), F32) for w in widths),
        grid=(na // tm,),
        in_specs=[_rows(tm, d), _resident((1, d)), _resident((d, PC_END)),
                  _rows(tm, LANES), _rows(tm, LANES), _rows(tm, LANES), _resident((SUBLANES, LANES))],
        out_specs=tuple(_rows(tm, w) for w in widths),
        compiler_params=_cp(("parallel",)),
        name="proj_in",
    )(x, g.reshape(1, d), w_pad, cos, sa, sb, sp)


def _merge_kernel(x_ref, oa_ref, om_ref, or_ref, og_ref, gate_ref, wb_ref, wo_ref, o_ref):
    d = x_ref.shape[1]
    mix = None
    for n, br in enumerate((oa_ref, om_ref, or_ref, og_ref)):
        up = _dot(br[...].astype(BF16), wb_ref[n])
        t = jax.nn.sigmoid(gate_ref[:, n * d:(n + 1) * d]) * up
        mix = t if mix is None else mix + t
    o_ref[...] = x_ref[...] + _dot(mix.astype(BF16), wo_ref[...])


def _merge(x, oa, om, orr, og, gate, w_branch, w_out):
    na, d = x.shape
    tm = math.gcd(na, 256)
    return pl.pallas_call(
        _merge_kernel,
        out_shape=SDS((na, d), F32),
        grid=(na // tm,),
        in_specs=[_rows(tm, d)] + [_rows(tm, BR_W)] * 4 + [_rows(tm, N_BRANCH * d),
                  _resident((N_BRANCH, BR_W, d)), _resident((d, d))],
        out_specs=_rows(tm, d),
        compiler_params=_cp(("parallel",)),
        name="merge",
    )(x, oa, om, orr, og, gate, w_branch.astype(BF16), w_out.astype(BF16))


def _idx_queries(qi, sm, tq):
    qrows = jnp.concatenate([qi[:, h * IDX_DIM:(h + 1) * IDX_DIM] for h in range(IDX_HEADS)], axis=0).astype(BF16)
    wb = [jnp.broadcast_to(sm[:, SM_IW + h:SM_IW + h + 1] * (IDX_DIM ** -0.5), (tq, LANES)) for h in range(IDX_HEADS)]
    return qrows, wb


def _idx_scores(qrows, wb, kib, tq, tk):
    dots = _dot_nt(qrows, kib)
    cols = []
    for j in range(tk // LANES):
        acc = None
        for h in range(IDX_HEADS):
            t = jnp.maximum(dots[h * tq:(h + 1) * tq, j * LANES:(j + 1) * LANES], 0.0) * wb[h]
            acc = t if acc is None else acc + t
        cols.append(acc)
    return cols[0] if len(cols) == 1 else jnp.concatenate(cols, axis=1)


def _to_key(sc):
    b = pltpu.bitcast(jnp.where(sc == 0.0, 0.0, sc), I32)
    return b ^ ((b >> 31) & I32(0x7FFFFFFF))


def _key_to_f32(key):
    return pltpu.bitcast(key ^ ((key >> 31) & I32(0x7FFFFFFF)), F32)


def _kth_largest(count, rows, n_sel, pos_bits, tkey_ref, cut_ref):
    int_min = I32(INT_MIN)

    def bit_step(it, ub):
        cand = ub | lax.shift_left(I32(1), 31 - it)
        cs = cand ^ int_min
        return jnp.where(count(lambda k, p: k >= cs) >= n_sel, cand, ub)

    ub = lax.fori_loop(0, 32, bit_step, jnp.zeros((rows, LANES), I32))
    tkey = ub ^ int_min
    n_gt = count(lambda k, p: k > tkey)
    n_ge = count(lambda k, p: k >= tkey)
    need = n_sel - n_gt
    ambiguous = (n_ge - n_gt) > need
    tkey_ref[...] = tkey
    cut_ref[...] = jnp.full((rows, LANES), 1 << pos_bits, I32)

    @pl.when(jnp.max(jnp.where(ambiguous, 1.0, 0.0)) > 0.0)
    def _():
        def pos_step(it, q):
            cand = q | lax.shift_left(I32(1), pos_bits - 1 - it)
            below = count(lambda k, p: (k == tkey) & (p < cand))
            return jnp.where(below < need, cand, q)

        q = lax.fori_loop(0, pos_bits, pos_step, jnp.zeros((rows, LANES), I32))
        cut_ref[...] = jnp.where(ambiguous, q + 1, 1 << pos_bits)


def _flash_update(s, v, m_s, l_s, acc_s, v_transposed=False):
    m_prev = m_s[...]
    m_new = jnp.maximum(m_prev, jnp.max(s, axis=1, keepdims=True))
    m_safe = jnp.where(m_new == NEG_INF, 0.0, m_new)
    alpha = jnp.exp(m_prev - m_safe)
    p = jnp.exp(s - m_safe)
    l_s[...] = alpha * l_s[...] + jnp.sum(p, axis=1, keepdims=True)
    m_s[...] = m_new
    pv = _dot_nt(p.astype(BF16), v) if v_transposed else _dot(p.astype(BF16), v)
    acc_s[...] = alpha * acc_s[...] + pv


def _stack_heads(q, t):
    masks = _head_masks(t)
    return jnp.concatenate([jnp.where(m, q * (HD ** -0.5), 0.0) for m in masks], axis=0).astype(BF16)


def _unstack_heads(acc, l, t):
    masks = _head_masks(t)
    out = jnp.zeros((t, BR_W), F32)
    for h, m in enumerate(masks):
        out = jnp.where(m, acc[h * t:(h + 1) * t] / l[h * t:(h + 1) * t], out)
    return out


def _dsa_prompt_kernel(qi_ref, smq_ref, q_ref, ki_ref, k_ref, v_ref, o_ref,
                       sc_s, cand_s, tkey_s, cut_s, need_s, neq_s, m_s, l_s, acc_s, sa_s, sb_s,
                       *, tq, tk, pf, n_sel, pos_bits):
    i = pl.program_id(0)
    q0 = i * tq
    nch = (q0 + tq - 1) // tk + 1
    qrows, wb = _idx_queries(qi_ref[...], smq_ref[...], tq)
    qpos = q0 + _iota((tq, tk), 0)

    def fill(c, carry):
        k0 = pl.multiple_of(c * tk, tk)
        sc = _idx_scores(qrows, wb, ki_ref[pl.ds(k0, tk), :], tq, tk)
        sc = jnp.where(k0 + _iota((tq, tk), 1) <= qpos, sc, NEG_INF)
        sc_s[c] = jnp.where(sc == 0.0, 0.0, sc)
        return carry

    ngrp = (nch + pf - 1) // pf
    lax.fori_loop(0, ngrp * pf, fill, 0)
    int_min = I32(INT_MIN)
    neg_key = _to_key(jnp.full((tq, LANES), NEG_INF, F32))
    big = 1 << pos_bits

    def count(pred):
        def body(c, acc):
            for j in range(tk // LANES):
                acc = acc + jnp.where(pred(sc_s[c, :, j * LANES:(j + 1) * LANES]), 1.0, 0.0)
            return acc

        acc = lax.fori_loop(0, nch, body, jnp.zeros((tq, LANES), F32))
        return jnp.broadcast_to(jnp.sum(acc, axis=1, keepdims=True), (tq, LANES))

    rg = 2 * SUBLANES
    for g in range(tq // rg):
        def insert(c, best, g=g):
            for j in range(tk // LANES):
                x = sc_s[c, g * rg:(g + 1) * rg, j * LANES:(j + 1) * LANES]
                nxt = []
                for b in best:
                    nxt.append(jnp.maximum(b, x))
                    x = jnp.minimum(b, x)
                best = tuple(nxt)
            return best

        best = lax.fori_loop(0, nch, insert, tuple(jnp.full((rg, LANES), NEG_INF, F32) for _ in range(TOPM)))
        for u in range(TOPM):
            cand_s[u, g * rg:(g + 1) * rg, :] = _to_key(best[u])

    def count_cand(pred):
        acc = jnp.zeros((tq, LANES), F32)
        for u in range(TOPM):
            acc = acc + jnp.where(pred(cand_s[u]), 1.0, 0.0)
        return jnp.broadcast_to(jnp.sum(acc, axis=1, keepdims=True), (tq, LANES))

    def bit_step(it, ub):
        cand = ub | lax.shift_left(I32(1), 31 - it)
        cs = cand ^ int_min
        return jnp.where(count_cand(lambda k: k >= cs) >= n_sel, cand, ub)

    tk_c = lax.fori_loop(0, 32, bit_step, jnp.zeros((tq, LANES), I32)) ^ int_min
    n_gt = count_cand(lambda k: k > tk_c)
    tkey_s[...] = tk_c
    need_s[...] = n_sel - n_gt
    neq_s[...] = count_cand(lambda k: k >= tk_c) - n_gt
    dropped = (cand_s[TOPM - 1] >= tk_c) & (nch * (tk // LANES) > TOPM)

    @pl.when(jnp.max(jnp.where(dropped, 1.0, 0.0)) > 0.0)
    def _():
        def step(it, ub):
            cand = ub | lax.shift_left(I32(1), 31 - it)
            t = _key_to_f32(jnp.maximum(cand ^ int_min, neg_key))
            return jnp.where(count(lambda s: s >= t) >= n_sel, cand, ub)

        tk_f = lax.fori_loop(0, 32, step, jnp.zeros((tq, LANES), I32)) ^ int_min
        t = _key_to_f32(tk_f)
        n_gt_f = count(lambda s: s > t)
        tkey_s[...] = tk_f
        need_s[...] = n_sel - n_gt_f
        neq_s[...] = count(lambda s: s >= t) - n_gt_f

    thr128 = _key_to_f32(tkey_s[...])
    ambiguous = neq_s[...] > need_s[...]
    cut_s[...] = jnp.full((tq, LANES), big, I32)

    @pl.when(jnp.max(jnp.where(ambiguous, 1.0, 0.0)) > 0.0)
    def _():
        thr1 = thr128[:, 0:1]
        need1 = need_s[:, 0:1]
        tri = jnp.where(_iota((tk, tk), 0) <= _iota((tk, tk), 1), 1.0, 0.0).astype(BF16)

        def body(c, carry):
            base, cutf = carry
            eq = sc_s[c] == thr1
            run = base + _dot(jnp.where(eq, 1.0, 0.0).astype(BF16), tri)
            posf = (c * tk + 1 + _iota((tq, tk), 1)).astype(F32)
            hit = jnp.where(eq & (run == need1), posf, float(big))
            return run[:, tk - 1:tk], jnp.minimum(cutf, jnp.min(hit, axis=1, keepdims=True))

        _, cutf = lax.fori_loop(0, nch, body, (jnp.zeros((tq, 1), F32), jnp.full((tq, 1), float(big), F32)))
        cut_s[...] = jnp.where(ambiguous, jnp.broadcast_to(cutf, (tq, LANES)).astype(I32), big)

    qend = q0 + _iota((tq, LANES), 0) + 1
    cut_s[...] = jnp.where(tkey_s[...] == neg_key, jnp.minimum(cut_s[...], qend), cut_s[...])

    thr = thr128[:, 0:1]
    cut = cut_s[:, 0:1]
    qbd = _stack_heads(q_ref[:, 0:BR_W], tq)
    m_s[...] = jnp.full(m_s.shape, NEG_INF, F32)
    l_s[...] = jnp.zeros(l_s.shape, F32)
    acc_s[...] = jnp.zeros(acc_s.shape, F32)

    def logits_into(buf, g):
        gc = jnp.minimum(g, ngrp - 1)
        k0 = pl.multiple_of(gc * (pf * tk), pf * tk)
        sc = jnp.concatenate([sc_s[gc * pf + u] for u in range(pf)], axis=1) if pf > 1 else sc_s[gc]
        sel = ((sc > thr) | ((sc == thr) & (k0 + _iota((tq, pf * tk), 1) < cut))) & (g < ngrp)
        bias = jnp.where(sel, 0.0, NEG_INF)
        buf[...] = _dot_nt(qbd, k_ref[pl.ds(k0, pf * tk), :]) + jnp.concatenate([bias] * HEADS, axis=0)

    def consume(buf, g):
        k0 = pl.multiple_of(jnp.minimum(g, ngrp - 1) * (pf * tk), pf * tk)
        _flash_update(buf[...], v_ref[pl.ds(k0, pf * tk), :], m_s, l_s, acc_s)

    logits_into(sa_s, 0)

    def pair(j, carry):
        logits_into(sb_s, 2 * j + 1)
        consume(sa_s, 2 * j)
        logits_into(sa_s, 2 * j + 2)
        consume(sb_s, 2 * j + 1)
        return carry

    lax.fori_loop(0, (ngrp + 1) // 2, pair, 0)
    o_ref[...] = _unstack_heads(acc_s[...], l_s[...], tq)


def _dsa_prompt(iq, sm, aqkv, lp):
    tq = min(128, lp)
    tk = min(512, lp)
    n_sel = min(TOPK_MAX, lp // 4)
    pos_bits = int(lp).bit_length()
    pf = 2 if (lp // tk) % 2 == 0 else 1
    ki = sm[:lp, SM_IK:SM_IK + IDX_DIM].astype(BF16)
    kb = aqkv[:lp, BR_W:2 * BR_W].astype(BF16)
    vb = aqkv[:lp, 2 * BR_W:3 * BR_W].astype(BF16)
    return pl.pallas_call(
        functools.partial(_dsa_prompt_kernel, tq=tq, tk=tk, pf=pf, n_sel=n_sel, pos_bits=pos_bits),
        out_shape=SDS((lp, BR_W), F32),
        grid=(lp // tq,),
        in_specs=[_rows(tq, IDX_HEADS * IDX_DIM), _rows(tq, LANES), _rows(tq, 3 * BR_W),
                  _resident((lp, IDX_DIM)), _resident((lp, BR_W)), _resident((lp, BR_W))],
        out_specs=_rows(tq, BR_W),
        scratch_shapes=[pltpu.VMEM((lp // tk, tq, tk), F32), pltpu.VMEM((TOPM, tq, LANES), I32),
                        pltpu.VMEM((tq, LANES), I32), pltpu.VMEM((tq, LANES), I32),
                        pltpu.VMEM((tq, LANES), F32), pltpu.VMEM((tq, LANES), F32),
                        pltpu.VMEM((HEADS * tq, 1), F32), pltpu.VMEM((HEADS * tq, 1), F32),
                        pltpu.VMEM((HEADS * tq, BR_W), F32),
                        pltpu.VMEM((HEADS * tq, pf * tk), F32), pltpu.VMEM((HEADS * tq, pf * tk), F32)],
        compiler_params=_cp(("arbitrary",)),
        name="dsa_prompt",
    )(iq, sm, aqkv, ki, kb, vb)


def _dsa_sample_score_kernel(pt_ref, qr_ref, wb_ref, *refs, t, pg, ng, past, n_sel, pos_bits):
    pages = refs[:pg]
    knew_ref, scp_ref, scn_ref, thr_ref, cut_ref, keys_s, keyn_s, tkey_s = refs[pg:]
    j = pl.program_id(1)
    qrows = qr_ref[...]
    wb = wb_ref[...] * (IDX_DIM ** -0.5)

    def reduce_heads(dots):
        w = jnp.maximum(dots, 0.0) * wb
        s = w[0:t]
        for h in range(1, IDX_HEADS):
            s = s + w[h * t:(h + 1) * t]
        return s

    sc = jnp.concatenate([reduce_heads(_dot(qrows, p[...].astype(BF16))) for p in pages], axis=1)
    scp_ref[...] = sc
    keys_s[j] = _to_key(sc)

    @pl.when(j == ng - 1)
    def _():
        lane = _iota((t, LANES), 1)
        sn = reduce_heads(_dot_nt(qrows, knew_ref[...].astype(BF16)))
        sn = jnp.where(lane <= _iota((t, LANES), 0), sn, NEG_INF)
        scn_ref[...] = sn
        keyn_s[...] = _to_key(sn)

        def count(pred):
            def body(c, acc):
                for jj in range(pg):
                    p = c * (pg * LANES) + jj * LANES + lane
                    acc = acc + jnp.where(pred(keys_s[c, :, jj * LANES:(jj + 1) * LANES], p), 1.0, 0.0)
                return acc

            acc = lax.fori_loop(0, ng, body, jnp.zeros((t, LANES), F32))
            acc = acc + jnp.where(pred(keyn_s[...], past + lane), 1.0, 0.0)
            return jnp.broadcast_to(jnp.sum(acc, axis=1, keepdims=True), (t, LANES))

        _kth_largest(count, t, n_sel, pos_bits, tkey_s, cut_ref)
        thr_ref[...] = _key_to_f32(tkey_s[...])


def _dsa_sample_attn_kernel(pt_ref, q_ref, *refs, t, pg, ng, past):
    kpages = refs[:pg]
    vpages = refs[pg:2 * pg]
    (scp_ref, scn_ref, thr_ref, cut_ref, knew_ref, vnew_ref, o_ref, m_s, l_s, acc_s) = refs[2 * pg:]
    j = pl.program_id(1)

    @pl.when(j == 0)
    def _():
        m_s[...] = jnp.full(m_s.shape, NEG_INF, F32)
        l_s[...] = jnp.zeros(l_s.shape, F32)
        acc_s[...] = jnp.zeros(acc_s.shape, F32)

    qbd = _stack_heads(q_ref[...], t)
    thr = thr_ref[:, 0:1]
    cut = cut_ref[:, 0:1]
    w = pg * LANES
    sc = scp_ref[...]
    kidx = j * w + _iota((t, w), 1)
    bias = jnp.where((sc > thr) | ((sc == thr) & (kidx < cut)), 0.0, NEG_INF)
    kcat = jnp.concatenate([p[...].reshape(BR_W, PAGE_SIZE).astype(BF16) for p in kpages], axis=1)
    vcat = jnp.concatenate([p[...].reshape(BR_W, PAGE_SIZE).astype(BF16) for p in vpages], axis=1)
    s = _dot(qbd, kcat) + jnp.concatenate([bias] * HEADS, axis=0)
    _flash_update(s, vcat, m_s, l_s, acc_s, v_transposed=True)

    @pl.when(j == ng - 1)
    def _():
        lane = _iota((t, LANES), 1)
        scn = scn_ref[...]
        seln = ((scn > thr) | ((scn == thr) & (past + lane < cut))) & (lane <= _iota((t, LANES), 0))
        biasn = jnp.where(seln, 0.0, NEG_INF)
        sn = _dot_nt(qbd, knew_ref[...].astype(BF16)) + jnp.concatenate([biasn] * HEADS, axis=0)
        _flash_update(sn, vnew_ref[...].astype(BF16), m_s, l_s, acc_s)
        o_ref[...] = _unstack_heads(acc_s[...], l_s[...], t)


def _dsa_sample(iq_s, sm_s, aqkv_s, cache_k, cache_v, cache_idx_k, page_table, layer, db, t):
    n_pages = page_table.shape[1]
    past = n_pages * PAGE_SIZE
    n_sel = min(TOPK_MAX, (past + t) // 4)
    pos_bits = int(past + LANES).bit_length()
    pg = math.gcd(n_pages, 16)
    ng = n_pages // pg
    ckt = cache_k.transpose(0, 1, 3, 4, 2)
    cvt = cache_v.transpose(0, 1, 3, 4, 2)
    cit = cache_idx_k.transpose(0, 1, 3, 2)

    qrows = iq_s.reshape(db, t, IDX_HEADS, IDX_DIM).transpose(0, 2, 1, 3).reshape(db, IDX_HEADS * t, IDX_DIM).astype(BF16)
    wcol = sm_s[:, SM_IW:SM_IW + IDX_HEADS].reshape(db, t, IDX_HEADS).transpose(0, 2, 1).reshape(db, IDX_HEADS * t, 1)
    wbs = jnp.broadcast_to(wcol, (db, IDX_HEADS * t, LANES))
    pad = lambda a: jnp.pad(a.reshape(db, t, a.shape[-1]), ((0, 0), (0, LANES - t), (0, 0)))
    ki_new = pad(sm_s[:, SM_IK:SM_IK + IDX_DIM])
    k_new = pad(aqkv_s[:, BR_W:2 * BR_W])
    v_new = pad(aqkv_s[:, 2 * BR_W:3 * BR_W])
    q3 = aqkv_s[:, 0:BR_W].reshape(db, t, BR_W)

    def idx_page(i):
        return pl.BlockSpec((None, None, IDX_DIM, PAGE_SIZE), lambda b, j, pt: (layer, pt[b, j * pg + i], 0, 0))

    def kv_page(i):
        return pl.BlockSpec((None, None, HEADS, HD, PAGE_SIZE), lambda b, j, pt: (layer, pt[b, j * pg + i], 0, 0, 0))

    per_b = lambda r, w: pl.BlockSpec((None, r, w), lambda b, j, pt: (b, 0, 0))
    sc_spec = pl.BlockSpec((None, t, pg * LANES), lambda b, j, pt: (b, 0, j))

    scp, scn, thr, cut = pl.pallas_call(
        functools.partial(_dsa_sample_score_kernel, t=t, pg=pg, ng=ng, past=past, n_sel=n_sel, pos_bits=pos_bits),
        out_shape=(SDS((db, t, past), F32), SDS((db, t, LANES), F32), SDS((db, t, LANES), F32), SDS((db, t, LANES), I32)),
        grid_spec=pltpu.PrefetchScalarGridSpec(
            num_scalar_prefetch=1, grid=(db, ng),
            in_specs=[per_b(IDX_HEADS * t, IDX_DIM), per_b(IDX_HEADS * t, LANES)]
                     + [idx_page(i) for i in range(pg)] + [per_b(LANES, IDX_DIM)],
            out_specs=(sc_spec, per_b(t, LANES), per_b(t, LANES), per_b(t, LANES)),
            scratch_shapes=[pltpu.VMEM((ng, t, pg * LANES), I32), pltpu.VMEM((t, LANES), I32), pltpu.VMEM((t, LANES), I32)]),
        compiler_params=_cp(("arbitrary", "arbitrary")),
        name="dsa_sample_scores",
    )(page_table, qrows, wbs, *([cit] * pg), ki_new)

    out = pl.pallas_call(
        functools.partial(_dsa_sample_attn_kernel, t=t, pg=pg, ng=ng, past=past),
        out_shape=SDS((db, t, BR_W), F32),
        grid_spec=pltpu.PrefetchScalarGridSpec(
            num_scalar_prefetch=1, grid=(db, ng),
            in_specs=[per_b(t, BR_W)] + [kv_page(i) for i in range(pg)] * 2
                     + [sc_spec, per_b(t, LANES), per_b(t, LANES), per_b(t, LANES), per_b(LANES, BR_W), per_b(LANES, BR_W)],
            out_specs=per_b(t, BR_W),
            scratch_shapes=[pltpu.VMEM((HEADS * t, 1), F32), pltpu.VMEM((HEADS * t, 1), F32),
                            pltpu.VMEM((HEADS * t, BR_W), F32)]),
        compiler_params=_cp(("arbitrary", "arbitrary")),
        name="dsa_sample_attention",
    )(page_table, q3, *([ckt] * pg), *([cvt] * pg), scp, scn, thr, cut, k_new, v_new)
    return out.reshape(db * t, BR_W)


def _mlstm_chunk(q, k, v, li, lf, cst, n, m, c):
    r = _iota((c, c), 0)
    cc = _iota((c, c), 1)
    hh = range(len(q))
    qk = [_dot_nt(q[h], k[h], HIGHEST) for h in hh]
    qc = [_dot_nt(q[h], cst[h], HIGHEST) for h in hh]
    b_col, mt, ei, s = [], [], [], []
    for h in hh:
        b_row, bc_ = _cumsum_col(lf[h], c)
        log_d = jnp.where(cc <= r, bc_ - b_row + _col_to_row(li[h], c), NEG_INF)
        inter = bc_ + m[h]
        mt_ = jnp.maximum(inter, jnp.max(log_d, axis=1, keepdims=True))
        b_col.append(bc_)
        mt.append(mt_)
        ei.append(jnp.exp(inter - mt_))
        s.append(qk[h] * jnp.exp(log_d - mt_))
    sv = [_hdot(s[h], v[h]) for h in hh]
    outs, c_new, n_new, m_new, wsv = [], [], [], [], []
    for h in hh:
        den = jnp.sum(s[h], axis=1, keepdims=True) + ei[h] * jnp.sum(q[h] * n[h], axis=1, keepdims=True)
        outs.append((sv[h] + ei[h] * qc[h]) / jnp.maximum(jnp.abs(den), jnp.exp(-mt[h])))
        bc = b_col[h][c - 1:c, :]
        gs = bc - b_col[h] + li[h]
        mn = jnp.maximum(bc + m[h], jnp.max(gs, axis=0, keepdims=True))
        ws = jnp.exp(gs - mn)
        dec = jnp.exp(bc + m[h] - mn)
        wsv.append(ws * v[h])
        c_new.append(dec * cst[h])
        n_new.append(dec * n[h] + jnp.sum(ws * k[h], axis=0, keepdims=True))
        m_new.append(mn)
    c_new = [c_new[h] + _dot_tn(wsv[h], k[h], HIGHEST) for h in hh]
    return outs, c_new, n_new, m_new


def _mlstm_kernel(x_ref, sm_ref, c0_ref, n0_ref, m0_ref, lng_ref, o_ref, c1_ref, n1_ref, m1_ref,
                  c_s, n_s, m_s, *, c, cb):
    i = pl.program_id(1)

    @pl.when(i == 0)
    def _():
        c_s[...] = c0_ref[...]
        n_s[...] = n0_ref[...]
        m_s[...] = m0_ref[...]

    def chunk(ci, carry):
        r0 = pl.multiple_of(ci * c, c)
        hh = range(HEADS)
        col = lambda base, h: x_ref[pl.ds(r0, c), base + HD * h:base + HD * (h + 1)]
        hs, c_new, n_new, m_new = _mlstm_chunk(
            [col(0, h) for h in hh], [col(BR_W, h) * (HD ** -0.5) for h in hh], [col(2 * BR_W, h) for h in hh],
            [sm_ref[pl.ds(r0, c), SM_MI + h:SM_MI + h + 1] for h in hh],
            [sm_ref[pl.ds(r0, c), SM_MF + h:SM_MF + h + 1] for h in hh],
            [c_s[h] for h in hh], [n_s[h] for h in hh], [m_s[h] for h in hh], c)
        outs = []
        for h in hh:
            c_s[h] = c_new[h]
            n_s[h] = n_new[h]
            m_s[h] = m_new[h]
            mu = jnp.mean(hs[h], axis=1, keepdims=True)
            var = jnp.mean(jnp.square(hs[h] - mu), axis=1, keepdims=True)
            y = (hs[h] - mu) * lax.rsqrt(var + LN_EPS) * lng_ref[:, HD * h:HD * (h + 1)]
            outs.append(y * jax.nn.sigmoid(col(3 * BR_W, h)))
        o_ref[pl.ds(r0, c), :] = jnp.concatenate(outs, axis=1)
        return carry

    lax.fori_loop(0, cb, chunk, 0)

    @pl.when(i == pl.num_programs(1) - 1)
    def _():
        c1_ref[...] = c_s[...]
        n1_ref[...] = n_s[...]
        m1_ref[...] = m_s[...]


def _seq_blocks(l, chunk, max_rows):
    nc = l // chunk
    cb = math.gcd(nc, max(1, max_rows // chunk))
    return chunk * cb, cb, nc // cb


def _state_spec(shape):
    return pl.BlockSpec((None,) + shape, lambda b, i: (b,) + (0,) * len(shape))


def _mlstm(mslab, sm, row_off, b, l, c0, n0, m0, ln_g):
    c = math.gcd(l, MLSTM_CHUNK)
    rows, cb, nblk = _seq_blocks(l, c, 512)
    off = row_off // rows
    seq = lambda w: pl.BlockSpec((rows, w), lambda bb, i: (off + bb * nblk + i, 0))
    o, c1, n1, m1 = pl.pallas_call(
        functools.partial(_mlstm_kernel, c=c, cb=cb),
        out_shape=(SDS((b * l, BR_W), F32), SDS((b, HEADS, HD, HD), F32), SDS((b, HEADS, 1, HD), F32),
                   SDS((b, HEADS, 1, 1), F32)),
        grid=(b, nblk),
        in_specs=[seq(4 * BR_W), seq(LANES), _state_spec((HEADS, HD, HD)), _state_spec((HEADS, 1, HD)),
                  _state_spec((HEADS, 1, 1)), pl.BlockSpec((1, BR_W), lambda bb, i: (0, 0))],
        out_specs=(pl.BlockSpec((rows, BR_W), lambda bb, i: (bb * nblk + i, 0)), _state_spec((HEADS, HD, HD)),
                   _state_spec((HEADS, 1, HD)), _state_spec((HEADS, 1, 1))),
        scratch_shapes=[pltpu.VMEM((HEADS, HD, HD), F32), pltpu.VMEM((HEADS, 1, HD), F32), pltpu.VMEM((HEADS, 1, 1), F32)],
        compiler_params=_cp(("parallel", "arbitrary")),
        name="mlstm",
    )(mslab, sm, c0, n0.reshape(b, HEADS, 1, HD), m0.reshape(b, HEADS, 1, 1), ln_g.reshape(1, BR_W))
    return o, c1, n1.reshape(b, HEADS, HD), m1.reshape(b, HEADS)


def _shift_rows(x, halo, s):
    rows = x.shape[0]
    rolled = pltpu.roll(x, s, 0)
    first = jnp.where(_iota((SUBLANES, x.shape[1]), 0) < s, pltpu.roll(halo, s, 0), rolled[0:SUBLANES])
    return first if rows == SUBLANES else jnp.concatenate([first, rolled[SUBLANES:]], axis=0)


def _halo_specs(rows, w, off_rows, nt):
    hb = rows // SUBLANES
    tile = pl.BlockSpec((rows, w), lambda b, i: (off_rows // rows + b * nt + i, 0))
    halo = pl.BlockSpec((SUBLANES, w), lambda b, i: (jnp.maximum(off_rows // SUBLANES + (b * nt + i) * hb - 1, 0), 0))
    state = pl.BlockSpec((None, SUBLANES, w), lambda b, i: (b, 0, 0))
    return tile, halo, state


def _pad_state_rows(st, w):
    b, r, w0 = st.shape
    return jnp.pad(st, ((0, 0), (SUBLANES - r, 0), (0, w - w0)))


def _cumsum_rows(x):
    row = _iota(x.shape, 0)
    sh = 1
    while sh < x.shape[0]:
        x = x + jnp.where(row >= sh, pltpu.roll(x, sh, 0), 0.0)
        sh *= 2
    return x


def _unit_lower_inverse(ms, c):
    r = _iota((c, c), 0)
    cc = _iota((c, c), 1)
    fold = jnp.where((_iota((c, SUBLANES), 0) & 7) == _iota((c, SUBLANES), 1), 1.0, 0.0)
    same8 = (r >> 3) == (cc >> 3)
    m8s = [_hdot(jnp.where(same8, m, 0.0), fold) for m in ms]
    zs = [jnp.where(r == cc, 1.0, 0.0)] * len(ms)

    def pivot(z, s):
        return jnp.broadcast_to(z.reshape(c // SUBLANES, SUBLANES, c)[:, s:s + 1, :],
                                (c // SUBLANES, SUBLANES, c)).reshape(c, c)

    for s in range(SUBLANES - 1):
        zs = [z + m8[:, s:s + 1] * pivot(z, s) for z, m8 in zip(zs, m8s)]
    sh = 3
    while (1 << sh) < c:
        off = ((r >> (sh + 1)) == (cc >> (sh + 1))) & ((r >> sh) != (cc >> sh))
        ts = [_hdot(z, jnp.where(off, m, 0.0)) for z, m in zip(zs, ms)]
        zs = [z + _hdot(t, z) for z, t in zip(zs, ts)]
        sh += 1
    return zs


def _lin_scan(a_ref, y0_ref, phi_ref, psi_ref, s_s, y_s, c, cb):
    def chunk(ci, carry):
        r0 = pl.multiple_of(ci * c, c)
        sts = [s_s[h] for h in range(HEADS)]
        new = [_hdot(sts[h], phi_ref[ci, h]) for h in range(HEADS)]
        ys = [_dot_nt(a_ref[pl.ds(r0, c), HD * h:HD * (h + 1)], sts[h], HIGHEST) for h in range(HEADS)]
        for h in range(HEADS):
            s_s[h] = new[h] + psi_ref[ci, h]
        y_s[pl.ds(r0, c), :] = jnp.concatenate(ys, axis=1) + y0_ref[pl.ds(r0, c), :]
        return carry

    lax.fori_loop(0, cb, chunk, 0)


def _chunk_specs(rows, cb, nblk, widths):
    row_specs = [pl.BlockSpec((rows, w), lambda bb, i: (bb * nblk + i, 0)) for w in widths]
    mat = pl.BlockSpec((cb, HEADS, HD, HD), lambda bb, i: (bb * nblk + i, 0, 0, 0))
    return row_specs, mat


def _rwkv_prep_kernel(p_ref, halo_ref, st_ref, vec_ref, w2_ref, a2_ref, g2_ref, rs_ref, post_ref):
    i = pl.program_id(1)
    p = p_ref[...]
    rows = p.shape[0]
    halo = jnp.where(i == 0, st_ref[...], halo_ref[...])
    prev = _shift_rows(p, halo, 1)
    mu = jnp.concatenate([vec_ref[0:1, :], vec_ref[1:2, :], vec_ref[2:3, :], vec_ref[3:4, :]], axis=1)
    pm = p + (prev - p) * mu
    r = pm[:, 0:BR_W]
    k = pm[:, BR_W:2 * BR_W]
    v = pm[:, 2 * BR_W:3 * BR_W]
    o = 3 * BR_W
    wd = pm[:, o:o + RWKV_W_LORA]
    ad = pm[:, o + RWKV_W_LORA:o + RWKV_W_LORA + RWKV_A_LORA]
    gd = pm[:, o + RWKV_W_LORA + RWKV_A_LORA:RWKV_PROJ]
    w0, a0, kkp, ka, rk = (vec_ref[4:5, :], vec_ref[5:6, :], vec_ref[6:7, :], vec_ref[7:8, :], vec_ref[8:9, :])
    wlog = -jax.nn.softplus(-(w0 + _hdot(jnp.tanh(wd), w2_ref[...]))) - 0.5
    log_decay = -jnp.exp(wlog)
    a = jax.nn.sigmoid(a0 + _hdot(ad, a2_ref[...]))
    g = _hdot(jax.nn.sigmoid(gd), g2_ref[...])
    masks = _head_masks(rows)
    kkr = k * kkp
    kk = kkr * lax.rsqrt(jnp.maximum(_seg_sum(kkr * kkr, masks), 1e-12))
    k2 = k * (1.0 + (a - 1.0) * ka)
    bonus = _seg_sum(r * k2 * rk, masks) * v
    for n, t in enumerate((r, log_decay, k2, v, kk, kk * a)):
        rs_ref[:, n * BR_W:(n + 1) * BR_W] = t
    post_ref[:, 0:BR_W] = bonus
    post_ref[:, BR_W:2 * BR_W] = g


def _rwkv_chunk_kernel(rs_ref, rh_ref, y0_ref, phi_ref, psi_ref, *, c, cb, gc):
    r_i = _iota((c, c), 0)
    c_i = _iota((c, c), 1)
    eye = _iota((HD, HD), 0) == _iota((HD, HD), 1)
    jobs = [(g, h) for g in range(gc) for h in range(HEADS)]
    hs = lambda x, h: x[:, HD * h:HD * (h + 1)]

    def group(gi, carry):
        r0s = [pl.multiple_of((gi * gc + g) * c, c) for g in range(gc)]
        w = []
        for r0 in r0s:
            r, lw, k, v, kk, kka = (rs_ref[pl.ds(r0, c), n * BR_W:(n + 1) * BR_W] for n in range(6))
            cum = _cumsum_rows(lw)
            pc = cum[c - 1:c, :]
            inv = jnp.exp(-cum)
            rest = jnp.exp(pc - cum)
            w.append(dict(at=-kk * jnp.exp(cum - lw),
                          bt=kka * inv, kt=k * inv,
                          rt=r * jnp.exp(cum), bd=kka * rest, kd=k * rest, v=v, epc=jnp.exp(pc)))
        big = [_dot_nt(jnp.concatenate([hs(w[g]["at"], h), hs(w[g]["rt"], h)], axis=0),
                       jnp.concatenate([hs(w[g]["bt"], h), hs(w[g]["kt"], h)], axis=0), HIGHEST) for g, h in jobs]
        mab = [jnp.where(c_i < r_i, x[0:c, 0:c], 0.0) for x in big]
        mak = [jnp.where(c_i < r_i, x[0:c, c:2 * c], 0.0) for x in big]
        nn = [jnp.concatenate([jnp.where(c_i <= r_i, x[c:2 * c, 0:c], 0.0),
                               jnp.where(c_i <= r_i, x[c:2 * c, c:2 * c], 0.0)], axis=1) for x in big]
        tinv = _unit_lower_inverse(mab, c)
        mv = [_hdot(m, hs(w[g]["v"], h)) for m, (g, h) in zip(mak, jobs)]
        hg = [_hdot(t, jnp.concatenate([hs(w[g]["at"], h), x], axis=1)) for t, x, (g, h) in zip(tinv, mv, jobs)]
        out = [_hdot(n2, jnp.concatenate([x, jnp.concatenate([jnp.zeros((c, HD), F32), hs(w[g]["v"], h)], axis=1)], axis=0))
               for n2, x, (g, h) in zip(nn, hg, jobs)]
        hb = [_dot_tn(x, hs(w[g]["bd"], h), HIGHEST) for x, (g, h) in zip(hg, jobs)]
        vk = [_dot_tn(hs(w[g]["v"], h), hs(w[g]["kd"], h), HIGHEST) for g, h in jobs]
        for g in range(gc):
            sel = [n for n, (gg, _) in enumerate(jobs) if gg == g]
            rh_ref[pl.ds(r0s[g], c), :] = w[g]["rt"] + jnp.concatenate([out[n][:, 0:HD] for n in sel], axis=1)
            y0_ref[pl.ds(r0s[g], c), :] = jnp.concatenate([out[n][:, HD:2 * HD] for n in sel], axis=1)
            phi_ref[gi * gc + g] = jnp.stack([jnp.where(eye, jnp.broadcast_to(hs(w[g]["epc"], jobs[n][1]), (HD, HD)), 0.0)
                                              + hb[n][0:HD] for n in sel])
            psi_ref[gi * gc + g] = jnp.stack([hb[n][HD:2 * HD] + vk[n] for n in sel])
        return carry

    lax.fori_loop(0, cb // gc, group, 0)


def _rwkv_scan_kernel(rh_ref, y0_ref, phi_ref, psi_ref, post_ref, s0_ref, ln_ref, o_ref, s1_ref, s_s, y_s, *, c, cb):
    i = pl.program_id(1)

    @pl.when(i == 0)
    def _():
        s_s[...] = s0_ref[...]

    _lin_scan(rh_ref, y0_ref, phi_ref, psi_ref, s_s, y_s, c, cb)
    y = y_s[...]
    masks = _head_masks(c * cb)
    mu = _seg_sum(y, masks) * (1.0 / HD)
    var = _seg_sum(jnp.square(y - mu), masks) * (1.0 / HD)
    y = (y - mu) * lax.rsqrt(var + LN_EPS) * ln_ref[0:1, :] + ln_ref[1:2, :]
    o_ref[...] = (y + post_ref[:, 0:BR_W]) * post_ref[:, BR_W:2 * BR_W]

    @pl.when(i == pl.num_programs(1) - 1)
    def _():
        s1_ref[...] = s_s[...]


def _rwkv(rslab, row_off, b, l, sh0, s0, lp):
    rows = math.gcd(l, 256)
    nt = l // rows
    tile, halo, state = _halo_specs(rows, RWKV_PROJ, row_off, nt)
    quarter = lambda a: a.reshape(4, BR_W)
    vec = jnp.concatenate([quarter(lp["rw_mu"]), lp["rw_w0"][None], lp["rw_a0"][None], lp["rw_kk"][None],
                           lp["rw_ka"][None], lp["rw_rk"].reshape(1, BR_W), jnp.zeros((7, BR_W), F32)], axis=0)
    rs, post = pl.pallas_call(
        _rwkv_prep_kernel,
        out_shape=(SDS((b * l, 6 * BR_W), F32), SDS((b * l, 2 * BR_W), F32)),
        grid=(b, nt),
        in_specs=[tile, halo, state, pl.BlockSpec((16, BR_W), lambda bb, i: (0, 0)),
                  pl.BlockSpec((RWKV_W_LORA, BR_W), lambda bb, i: (0, 0)),
                  pl.BlockSpec((RWKV_A_LORA, BR_W), lambda bb, i: (0, 0)),
                  pl.BlockSpec((RWKV_G_LORA, BR_W), lambda bb, i: (0, 0))],
        out_specs=(pl.BlockSpec((rows, 6 * BR_W), lambda bb, i: (bb * nt + i, 0)),
                   pl.BlockSpec((rows, 2 * BR_W), lambda bb, i: (bb * nt + i, 0))),
        compiler_params=_cp(("parallel", "parallel")),
        name="rwkv_prep",
    )(rslab, rslab, _pad_state_rows(sh0[:, None, :], RWKV_PROJ), vec, lp["rw_w2"], lp["rw_a2"], lp["rw_g2"])

    c = math.gcd(l, 64)
    nc = l // c
    prow, pcb, pblk = _seq_blocks(l, c, 256)
    (rs_spec, rh_spec, y0_spec), mat = _chunk_specs(prow, pcb, pblk, (6 * BR_W, BR_W, BR_W))
    rh, y0, phi, psi = pl.pallas_call(
        functools.partial(_rwkv_chunk_kernel, c=c, cb=pcb, gc=math.gcd(pcb, 2)),
        out_shape=(SDS((b * l, BR_W), F32), SDS((b * l, BR_W), F32),
                   SDS((b * nc, HEADS, HD, HD), F32), SDS((b * nc, HEADS, HD, HD), F32)),
        grid=(b, pblk),
        in_specs=[rs_spec],
        out_specs=(rh_spec, y0_spec, mat, mat),
        compiler_params=_cp(("parallel", "parallel")),
        name="rwkv_chunk",
    )(rs)

    srow, scb, sblk = _seq_blocks(l, c, 512)
    (rh_spec, y0_spec, post_spec, o_spec), mat = _chunk_specs(srow, scb, sblk, (BR_W, BR_W, 2 * BR_W, BR_W))
    ln = jnp.stack([lp["rw_ln_g"], lp["rw_ln_b"]] + [jnp.zeros((BR_W,), F32)] * 6)
    o, s1 = pl.pallas_call(
        functools.partial(_rwkv_scan_kernel, c=c, cb=scb),
        out_shape=(SDS((b * l, BR_W), F32), SDS((b, HEADS, HD, HD), F32)),
        grid=(b, sblk),
        in_specs=[rh_spec, y0_spec, mat, mat, post_spec, _state_spec((HEADS, HD, HD)),
                  pl.BlockSpec((SUBLANES, BR_W), lambda bb, i: (0, 0))],
        out_specs=(o_spec, _state_spec((HEADS, HD, HD))),
        scratch_shapes=[pltpu.VMEM((HEADS, HD, HD), F32), pltpu.VMEM((srow, BR_W), F32)],
        compiler_params=_cp(("parallel", "arbitrary")),
        name="rwkv_scan",
    )(rh, y0, phi, psi, post, s0, ln)
    return o, s1


def _gdn_prep_kernel(x_ref, halo_ref, st_ref, taps_ref, o_ref):
    i = pl.program_id(1)
    x = x_ref[:, 0:GDN_CONV_CH]
    rows = x.shape[0]
    halo = jnp.where(i == 0, st_ref[:, 0:GDN_CONV_CH], halo_ref[:, 0:GDN_CONV_CH])
    conv = x * taps_ref[CONV_W - 1:CONV_W, :]
    for s in range(1, CONV_W):
        conv = conv + _shift_rows(x, halo, s) * taps_ref[CONV_W - 1 - s:CONV_W - s, :]
    conv = conv * jax.nn.sigmoid(conv)
    masks = _head_masks(rows)
    for n in range(3):
        t = conv[:, n * BR_W:(n + 1) * BR_W]
        if n < 2:
            t = t * lax.rsqrt(jnp.maximum(_seg_sum(t * t, masks), 1e-12))
        if n == 0:
            t = t * (HD ** -0.5)
        o_ref[:, n * BR_W:(n + 1) * BR_W] = t


def _gdn_chunk_kernel(x_ref, sm_ref, qh_ref, o0_ref, phi_ref, psi_ref, *, c, cb, gc):
    r_i = _iota((c, c), 0)
    c_i = _iota((c, c), 1)
    eye = _iota((HD, HD), 0) == _iota((HD, HD), 1)
    jobs = [(g, h) for g in range(gc) for h in range(HEADS)]

    def group(gi, carry):
        r0s = [pl.multiple_of((gi * gc + g) * c, c) for g in range(gc)]
        q, k, v, beta, dec, eg, gl, wv = [], [], [], [], [], [], [], []
        for g, h in jobs:
            r0, lo = r0s[g], HD * h
            q.append(x_ref[pl.ds(r0, c), lo:lo + HD])
            k.append(x_ref[pl.ds(r0, c), BR_W + lo:BR_W + lo + HD])
            v.append(x_ref[pl.ds(r0, c), 2 * BR_W + lo:2 * BR_W + lo + HD])
            beta.append(sm_ref[pl.ds(r0, c), SM_GB + h:SM_GB + h + 1])
            gam_row, gam_col = _cumsum_col(sm_ref[pl.ds(r0, c), SM_GA + h:SM_GA + h + 1], c)
            dec.append(jnp.exp(jnp.where(c_i <= r_i, gam_col - gam_row, NEG_INF)))
            eg.append(jnp.exp(gam_col))
            gl.append(gam_col[c - 1:c, :])
            wv.append(jnp.exp(gam_col[c - 1:c, :] - gam_col))
        n = range(len(jobs))
        big = [_dot_nt(jnp.concatenate([k[i], q[i]], axis=0), k[i], HIGHEST) for i in n]
        tinv = _unit_lower_inverse([-(beta[i] * big[i][0:c] * jnp.where(c_i < r_i, dec[i], 0.0)) for i in n], c)
        x = [_hdot(tinv[i], jnp.concatenate([beta[i] * v[i], (beta[i] * eg[i]) * k[i]], axis=1)) for i in n]
        out = [_hdot(big[i][c:2 * c] * dec[i], x[i]) for i in n]
        xb = [_dot_tn(x[i], wv[i] * k[i], HIGHEST) for i in n]
        for g in range(gc):
            sel = [i for i in n if jobs[i][0] == g]
            qh_ref[pl.ds(r0s[g], c), :] = jnp.concatenate([eg[i] * q[i] - out[i][:, HD:2 * HD] for i in sel], axis=1)
            o0_ref[pl.ds(r0s[g], c), :] = jnp.concatenate([out[i][:, 0:HD] for i in sel], axis=1)
            phi_ref[gi * gc + g] = jnp.stack([jnp.where(eye, jnp.broadcast_to(jnp.exp(gl[i]), (HD, HD)), 0.0)
                                              - xb[i][HD:2 * HD] for i in sel])
            psi_ref[gi * gc + g] = jnp.stack([xb[i][0:HD] for i in sel])
        return carry

    lax.fori_loop(0, cb // gc, group, 0)


def _gdn_scan_kernel(qh_ref, o0_ref, phi_ref, psi_ref, z_ref, s0_ref, gn_ref, o_ref, s1_ref, s_s, y_s, *, c, cb):
    i = pl.program_id(1)

    @pl.when(i == 0)
    def _():
        s_s[...] = s0_ref[...]

    _lin_scan(qh_ref, o0_ref, phi_ref, psi_ref, s_s, y_s, c, cb)
    o = y_s[...]
    ms = _seg_sum(o * o, _head_masks(c * cb)) * (1.0 / HD)
    z = z_ref[:, 3 * BR_W:4 * BR_W]
    o_ref[...] = o * lax.rsqrt(ms + EPS) * gn_ref[...] * (z * jax.nn.sigmoid(z))

    @pl.when(i == pl.num_programs(1) - 1)
    def _():
        s1_ref[...] = s_s[...]


def _gdn(gslab, sm, row_off, b, l, cv0, s0, lp):
    rows = math.gcd(l, 256)
    nt = l // rows
    w = 4 * BR_W
    tile, halo, state = _halo_specs(rows, w, row_off, nt)
    taps = jnp.pad(lp["gd_conv"], ((0, SUBLANES - CONV_W), (0, 0)))
    qkv = pl.pallas_call(
        _gdn_prep_kernel,
        out_shape=SDS((b * l, GDN_CONV_CH), F32),
        grid=(b, nt),
        in_specs=[tile, halo, state, pl.BlockSpec((SUBLANES, GDN_CONV_CH), lambda bb, i: (0, 0))],
        out_specs=pl.BlockSpec((rows, GDN_CONV_CH), lambda bb, i: (bb * nt + i, 0)),
        compiler_params=_cp(("parallel", "parallel")),
        name="gdn_prep",
    )(gslab, gslab, _pad_state_rows(cv0, w), taps)

    c = math.gcd(l, GDN_CHUNK)
    nc = l // c
    prow, pcb, pblk = _seq_blocks(l, c, 256)
    (x_spec, qh_spec, o0_spec), mat = _chunk_specs(prow, pcb, pblk, (GDN_CONV_CH, BR_W, BR_W))
    sm_spec = pl.BlockSpec((prow, LANES), lambda bb, i: (row_off // prow + bb * pblk + i, 0))
    qh, o0, phi, psi = pl.pallas_call(
        functools.partial(_gdn_chunk_kernel, c=c, cb=pcb, gc=math.gcd(pcb, 2)),
        out_shape=(SDS((b * l, BR_W), F32), SDS((b * l, BR_W), F32),
                   SDS((b * nc, HEADS, HD, HD), F32), SDS((b * nc, HEADS, HD, HD), F32)),
        grid=(b, pblk),
        in_specs=[x_spec, sm_spec],
        out_specs=(qh_spec, o0_spec, mat, mat),
        compiler_params=_cp(("parallel", "parallel")),
        name="gdn_chunk",
    )(qkv, sm)

    srow, scb, sblk = _seq_blocks(l, c, 512)
    (qh_spec, o0_spec, o_spec), mat = _chunk_specs(srow, scb, sblk, (BR_W, BR_W, BR_W))
    z_spec = pl.BlockSpec((srow, w), lambda bb, i: (row_off // srow + bb * sblk + i, 0))
    o, s1 = pl.pallas_call(
        functools.partial(_gdn_scan_kernel, c=c, cb=scb),
        out_shape=(SDS((b * l, BR_W), F32), SDS((b, HEADS, HD, HD), F32)),
        grid=(b, sblk),
        in_specs=[qh_spec, o0_spec, mat, mat, z_spec, _state_spec((HEADS, HD, HD)),
                  pl.BlockSpec((1, BR_W), lambda bb, i: (0, 0))],
        out_specs=(o_spec, _state_spec((HEADS, HD, HD))),
        scratch_shapes=[pltpu.VMEM((HEADS, HD, HD), F32), pltpu.VMEM((srow, BR_W), F32)],
        compiler_params=_cp(("parallel", "arbitrary")),
        name="gdn_scan",
    )(qh, o0, phi, psi, gslab, s0, jnp.tile(lp["gd_norm"], HEADS).reshape(1, BR_W))
    return o, s1


def _rope_tables(pos):
    half = HD // 2
    inv = ROPE_THETA ** (-jnp.arange(half, dtype=F32) / half)
    ang = pos.astype(F32)[:, None] * inv[None, :]
    cos, sin = jnp.cos(ang), jnp.sin(ang)
    zero = jnp.zeros_like(sin)
    tile = lambda a, b: jnp.concatenate([a, b, a, b], axis=1)
    return tile(cos, cos), tile(-sin, zero), tile(zero, sin)


def _pad_w_in(w):
    sizes = (BR_W, BR_W, BR_W, IDX_HEADS * IDX_DIM, IDX_DIM, IDX_HEADS, BR_W, BR_W, BR_W, HEADS, HEADS, BR_W,
             RWKV_PROJ, GDN_CONV_CH, HEADS, HEADS, BR_W, N_BRANCH * w.shape[0])
    cuts = np.concatenate([[0], np.cumsum(sizes)])
    col = lambda i: w[:, cuts[i]:cuts[i + 1]]
    (a_q, a_k, a_v, a_iq, a_ik, a_iw, m_q, m_k, m_v, m_i, m_f, m_o, r_p, g_qkv, g_b, g_a, g_z, gate) = (
        col(i) for i in range(len(sizes)))
    small = jnp.concatenate([a_ik, a_iw, m_i, m_f, g_b, g_a, jnp.zeros((w.shape[0], LANES - SM_END), w.dtype)], axis=1)
    return jnp.concatenate([a_q, a_k, a_v, a_iq, small, m_q, m_k, m_v, m_o, r_p, g_qkv, g_z, gate], axis=1).astype(BF16)


def _small_params(lp):
    lane_put = lambda v, at: jnp.zeros((LANES,), F32).at[at:at + v.shape[0]].set(v)
    rows = [lane_put(lp["idx_k_norm"], SM_IK),
            lane_put(lp["ml_i_bias"], SM_MI) + lane_put(lp["ml_f_bias"], SM_MF) + lane_put(lp["gd_dt_bias"], SM_GA),
            lane_put(lp["gd_A_log"], SM_GA)]
    return jnp.stack(rows + [jnp.zeros((LANES,), F32)] * (SUBLANES - len(rows)))


def kernel(x_prompt, x_sample, cache_k, cache_v, cache_idx_k, state_mlstm_C, state_mlstm_n, state_mlstm_m, state_rwkv_S, state_rwkv_shift, state_gdn_S, state_gdn_conv, page_table, ffn1_norm, ffn1_w1, ffn1_w3, ffn1_w2, mix_norm, w_in, idx_k_norm, ml_i_bias, ml_f_bias, ml_norm, rw_mu, rw_w0, rw_w2, rw_a0, rw_a2, rw_g2, rw_kk, rw_ka, rw_rk, rw_ln_g, rw_ln_b, gd_conv, gd_A_log, gd_dt_bias, gd_norm, w_branch, w_out, ffn2_norm, ffn2_w1, ffn2_w3, ffn2_w2, final_norm):
    bp, lp_, d = x_prompt.shape
    db, t, _ = x_sample.shape
    assert bp == 1 and t % SUBLANES == 0 and t <= LANES and lp_ % SUBLANES == 0
    depth = w_in.shape[0]
    past = page_table.shape[1] * PAGE_SIZE
    ns = db * t

    x = jnp.concatenate([x_prompt.reshape(lp_, d), x_sample.reshape(ns, d)], axis=0)
    pos = jnp.concatenate([jnp.arange(lp_, dtype=I32), jnp.tile(past + jnp.arange(t, dtype=I32), db)])
    cos, sa, sb = _rope_tables(pos)
    zeros = lambda *s: jnp.zeros(s, F32)

    outs_p, outs_s = [], []
    for l in range(depth):
        lp = dict(idx_k_norm=idx_k_norm[l], ml_i_bias=ml_i_bias[l], ml_f_bias=ml_f_bias[l],
                  rw_mu=rw_mu[l], rw_w0=rw_w0[l], rw_w2=rw_w2[l], rw_a0=rw_a0[l], rw_a2=rw_a2[l], rw_g2=rw_g2[l],
                  rw_kk=rw_kk[l], rw_ka=rw_ka[l], rw_rk=rw_rk[l], rw_ln_g=rw_ln_g[l], rw_ln_b=rw_ln_b[l],
                  gd_conv=gd_conv[l], gd_A_log=gd_A_log[l], gd_dt_bias=gd_dt_bias[l], gd_norm=gd_norm[l])
        x = _ffn(x, ffn1_norm[l], ffn1_w1[l], ffn1_w3[l], ffn1_w2[l])
        aqkv, iq, sm, mslab, rslab, gslab, gate = _proj(x, mix_norm[l], _pad_w_in(w_in[l]), cos, sa, sb, _small_params(lp))

        oa_p = _dsa_prompt(iq, sm, aqkv, lp_)
        oa_s = _dsa_sample(iq[lp_:], sm[lp_:], aqkv[lp_:], cache_k, cache_v, cache_idx_k, page_table, l, db, t)
        om_p, c_p, n_p, m_p = _mlstm(mslab, sm, 0, 1, lp_, zeros(1, HEADS, HD, HD), zeros(1, HEADS, HD), zeros(1, HEADS), ml_norm[l])
        om_s, c_s, n_s, m_s = _mlstm(mslab, sm, lp_, db, t, state_mlstm_C[l], state_mlstm_n[l], state_mlstm_m[l], ml_norm[l])
        or_p, rs_p = _rwkv(rslab, 0, 1, lp_, zeros(1, RWKV_PROJ), zeros(1, HEADS, HD, HD), lp)
        or_s, rs_s = _rwkv(rslab, lp_, db, t, state_rwkv_shift[l], state_rwkv_S[l], lp)
        og_p, gs_p = _gdn(gslab, sm, 0, 1, lp_, zeros(1, CONV_W - 1, GDN_CONV_CH), zeros(1, HEADS, HD, HD), lp)
        og_s, gs_s = _gdn(gslab, sm, lp_, db, t, state_gdn_conv[l], state_gdn_S[l], lp)

        cat = lambda a, b: jnp.concatenate([a, b], axis=0)
        x = _merge(x, cat(oa_p, oa_s), cat(om_p, om_s), cat(or_p, or_s), cat(og_p, og_s), gate, w_branch[l], w_out[l])
        x = _ffn(x, ffn2_norm[l], ffn2_w1[l], ffn2_w3[l], ffn2_w2[l])

        k_all = aqkv[:, BR_W:2 * BR_W]
        v_all = aqkv[:, 2 * BR_W:3 * BR_W]
        ik_all = sm[:, SM_IK:SM_IK + IDX_DIM]
        gq = gslab[:, 0:GDN_CONV_CH]
        outs_p.append((k_all[:lp_].reshape(1, lp_, HEADS, HD), v_all[:lp_].reshape(1, lp_, HEADS, HD),
                       ik_all[:lp_].reshape(1, lp_, IDX_DIM), c_p, n_p, m_p, rs_p, rslab[lp_ - 1:lp_],
                       gs_p, gq[lp_ - (CONV_W - 1):lp_].reshape(1, CONV_W - 1, GDN_CONV_CH)))
        outs_s.append((k_all[lp_:].reshape(db, t, HEADS, HD), v_all[lp_:].reshape(db, t, HEADS, HD),
                       ik_all[lp_:].reshape(db, t, IDX_DIM), c_s, n_s, m_s, rs_s,
                       rslab[lp_:].reshape(db, t, RWKV_PROJ)[:, -1],
                       gs_s, gq[lp_:].reshape(db, t, GDN_CONV_CH)[:, t - (CONV_W - 1):]))

    y = _final_rms(x, final_norm)
    st_p = [jnp.stack(z) for z in zip(*outs_p)]
    st_s = [jnp.stack(z) for z in zip(*outs_s)]
    return (y[:lp_].reshape(1, lp_, d), y[lp_:].reshape(db, t, d), *st_p, *st_s)
```

```python
import functools
import math

import jax
import jax.numpy as jnp
import numpy as np
from jax import lax
from jax.experimental import pallas as pl
from jax.experimental.pallas import tpu as pltpu

F32 = jnp.float32
BF16 = jnp.bfloat16
I32 = jnp.int32
SDS = jax.ShapeDtypeStruct

HD = 64
HEADS = 4
BR_W = HEADS * HD
N_BRANCH = 4
IDX_HEADS = 8
IDX_DIM = 64
TOPK_MAX = 256
TOPM = 12
ROPE_THETA = 10000.0
MLSTM_CHUNK = 256
GDN_CHUNK = 64
CONV_W = 4
RWKV_W_LORA = 64
RWKV_A_LORA = 64
RWKV_G_LORA = 128
RWKV_PROJ = 3 * BR_W + RWKV_W_LORA + RWKV_A_LORA + RWKV_G_LORA
GDN_CONV_CH = 3 * BR_W
EPS = 1e-6
LN_EPS = 1e-5
PAGE_SIZE = 128

LANES = 128
SUBLANES = 8
VMEM_LIMIT_BYTES = 58 * 1024 * 1024

SM_IK = 0
SM_IW = 64
SM_MI = 72
SM_MF = 76
SM_GB = 80
SM_GA = 84
SM_END = 88

PC_AQKV = 0
PC_IQ = 768
PC_SM = 1280
PC_M = 1408
PC_R = 2432
PC_G = 3456
PC_GATE = 4480
PC_END = 8576

HIGHEST = lax.Precision.HIGHEST
INT_MIN = -2147483648
NEG_INF = float("-inf")


def _cp(sem):
    return pltpu.CompilerParams(dimension_semantics=sem, vmem_limit_bytes=VMEM_LIMIT_BYTES)


def _resident(shape):
    return pl.BlockSpec(shape, lambda *_: (0,) * len(shape), pipeline_mode=pl.Buffered(1))


def _rows(tm, w, off_blocks=0):
    return pl.BlockSpec((tm, w), lambda i, *_: (i + off_blocks, 0))


def _dot(a, b):
    return jnp.dot(a, b, preferred_element_type=F32)


def _dot_nt(a, b, precision=None):
    return lax.dot_general(a, b, (((1,), (1,)), ((), ())), preferred_element_type=F32, precision=precision)


def _dot_tn(a, b, precision=None):
    return lax.dot_general(a, b, (((0,), (0,)), ((), ())), preferred_element_type=F32, precision=precision)


def _hdot(a, b):
    return jnp.dot(a, b, preferred_element_type=F32, precision=HIGHEST)


def _dot3(a, b, dims=(((1,), (0,)), ((), ()))):
    ah, bh = a.astype(BF16), b.astype(BF16)
    al, bl = (a - ah.astype(F32)).astype(BF16), (b - bh.astype(F32)).astype(BF16)
    d = lambda x, y: lax.dot_general(x, y, dims, preferred_element_type=F32)
    return d(ah, bh) + (d(ah, bl) + d(al, bh))


_NT = (((1,), (1,)), ((), ()))
_TN = (((0,), (0,)), ((), ()))


def _rms(x, g):
    return x * lax.rsqrt(jnp.mean(x * x, axis=-1, keepdims=True) + EPS) * g


def _iota(shape, dim):
    return lax.broadcasted_iota(I32, shape, dim)


def _head_masks(rows):
    lane = _iota((rows, BR_W), 1)
    return [(lane >= HD * h) & (lane < HD * (h + 1)) for h in range(HEADS)]


def _seg_sum(x, masks):
    out = jnp.zeros_like(x)
    for m in masks:
        s = jnp.sum(jnp.where(m, x, 0.0), axis=1, keepdims=True)
        out = jnp.where(m, s, out)
    return out


def _col_to_row(col, n):
    eye = _iota((n, n), 0) == _iota((n, n), 1)
    return jnp.sum(jnp.where(eye, jnp.broadcast_to(col, (n, n)), 0.0), axis=0, keepdims=True)


def _row_to_col(row, n):
    eye = _iota((n, n), 0) == _iota((n, n), 1)
    return jnp.sum(jnp.where(eye, jnp.broadcast_to(row, (n, n)), 0.0), axis=1, keepdims=True)


def _cumsum_col(col, n):
    r = _iota((n, n), 0)
    c = _iota((n, n), 1)
    row = jnp.sum(jnp.where(r <= c, jnp.broadcast_to(col, (n, n)), 0.0), axis=0, keepdims=True)
    return row, _row_to_col(row, n)


def _ffn_kernel(x_ref, g_ref, w1_ref, w3_ref, w2_ref, o_ref):
    x = x_ref[...]
    h = _rms(x, g_ref[...]).astype(BF16)
    a = _dot(h, w1_ref[...])
    b = _dot(h, w3_ref[...])
    y = (a * jax.nn.sigmoid(a) * b).astype(BF16)
    o_ref[...] = x + 0.5 * _dot(y, w2_ref[...])


def _ffn(x, g, w1, w3, w2):
    na, d = x.shape
    f = w1.shape[1]
    tm = math.gcd(na, 256)
    return pl.pallas_call(
        _ffn_kernel,
        out_shape=SDS((na, d), F32),
        grid=(na // tm,),
        in_specs=[_rows(tm, d), _resident((1, d)), _resident((d, f)), _resident((d, f)), _resident((f, d))],
        out_specs=_rows(tm, d),
        compiler_params=_cp(("parallel",)),
        name="ffn",
    )(x, g.reshape(1, d), w1.astype(BF16), w3.astype(BF16), w2.astype(BF16))


def _rms_kernel(x_ref, g_ref, o_ref):
    o_ref[...] = _rms(x_ref[...], g_ref[...])


def _final_rms(x, g):
    na, d = x.shape
    tm = math.gcd(na, 512)
    return pl.pallas_call(
        _rms_kernel,
        out_shape=SDS((na, d), F32),
        grid=(na // tm,),
        in_specs=[_rows(tm, d), _resident((1, d))],
        out_specs=_rows(tm, d),
        compiler_params=_cp(("parallel",)),
        name="final_rms",
    )(x, g.reshape(1, d))


def _rope_slab(z, cos, sa, sb):
    return z * cos + pltpu.roll(z, LANES - HD // 2, 1) * sa + pltpu.roll(z, HD // 2, 1) * sb


def _proj_kernel(x_ref, g_ref, w_ref, cos_ref, sa_ref, sb_ref, sp_ref,
                 aqkv_ref, iq_ref, sm_ref, m_ref, r_ref, gq_ref, gate_ref):
    h = _rms(x_ref[...], g_ref[...]).astype(BF16)
    cos, sa, sb = cos_ref[...], sa_ref[...], sb_ref[...]

    def mm(a, b):
        return _dot(h, w_ref[:, a:b])

    za = mm(PC_AQKV, PC_IQ)
    for s in range(4):
        aqkv_ref[:, s * LANES:(s + 1) * LANES] = _rope_slab(za[:, s * LANES:(s + 1) * LANES], cos, sa, sb)
    aqkv_ref[:, 2 * BR_W:3 * BR_W] = za[:, 2 * BR_W:3 * BR_W]
    zi = mm(PC_IQ, PC_SM)
    for s in range(4):
        iq_ref[:, s * LANES:(s + 1) * LANES] = _rope_slab(zi[:, s * LANES:(s + 1) * LANES], cos, sa, sb)

    zs = mm(PC_SM, PC_M)
    lane = _iota(zs.shape, 1)
    gn, bias, alog = sp_ref[0:1, :], sp_ref[1:2, :], sp_ref[2:3, :]
    ms = jnp.sum(jnp.where(lane < IDX_DIM, zs * zs, 0.0), axis=1, keepdims=True) * (1.0 / IDX_DIM)
    ik = _rope_slab(zs * lax.rsqrt(ms + EPS) * gn, cos, sa, sb)
    t = zs + bias
    out = jnp.where(lane < SM_IW, ik,
          jnp.where(lane < SM_MI, zs * (IDX_HEADS ** -0.5),
          jnp.where(lane < SM_MF, t,
          jnp.where(lane < SM_GB, jax.nn.log_sigmoid(t),
          jnp.where(lane < SM_GA, jax.nn.sigmoid(zs),
          jnp.where(lane < SM_END, -jnp.exp(alog) * jax.nn.softplus(t), 0.0))))))
    sm_ref[...] = out

    m_ref[...] = mm(PC_M, PC_R)
    r_ref[...] = mm(PC_R, PC_G)
    gq_ref[...] = mm(PC_G, PC_GATE)
    gate_ref[...] = mm(PC_GATE, PC_END)


def _proj(x, g, w_pad, cos, sa, sb, sp):
    na, d = x.shape
    tm = math.gcd(na, 256)
    widths = (3 * BR_W, IDX_HEADS * IDX_DIM, LANES, 4 * BR_W, RWKV_PROJ, 4 * BR_W, N_BRANCH * d)
    return pl.pallas_call(
        _proj_kernel,
        out_shape=tuple(SDS((na, w), F32) for w in widths),
        grid=(na // tm,),
        in_specs=[_rows(tm, d), _resident((1, d)), _resident((d, PC_END)),
                  _rows(tm, LANES), _rows(tm, LANES), _rows(tm, LANES), _resident((SUBLANES, LANES))],
        out_specs=tuple(_rows(tm, w) for w in widths),
        compiler_params=_cp(("parallel",)),
        name="proj_in",
    )(x, g.reshape(1, d), w_pad, cos, sa, sb, sp)


def _merge_kernel(x_ref, oa_ref, om_ref, or_ref, og_ref, gate_ref, wb_ref, wo_ref, o_ref):
    d = x_ref.shape[1]
    mix = None
    for n, br in enumerate((oa_ref, om_ref, or_ref, og_ref)):
        up = _dot(br[...].astype(BF16), wb_ref[n])
        t = jax.nn.sigmoid(gate_ref[:, n * d:(n + 1) * d]) * up
        mix = t if mix is None else mix + t
    o_ref[...] = x_ref[...] + _dot(mix.astype(BF16), wo_ref[...])


def _merge(x, oa, om, orr, og, gate, w_branch, w_out):
    na, d = x.shape
    tm = math.gcd(na, 256)
    return pl.pallas_call(
        _merge_kernel,
        out_shape=SDS((na, d), F32),
        grid=(na // tm,),
        in_specs=[_rows(tm, d)] + [_rows(tm, BR_W)] * 4 + [_rows(tm, N_BRANCH * d),
                  _resident((N_BRANCH, BR_W, d)), _resident((d, d))],
        out_specs=_rows(tm, d),
        compiler_params=_cp(("parallel",)),
        name="merge",
    )(x, oa, om, orr, og, gate, w_branch.astype(BF16), w_out.astype(BF16))


def _idx_queries(qi, sm, tq):
    qrows = jnp.concatenate([qi[:, h * IDX_DIM:(h + 1) * IDX_DIM] for h in range(IDX_HEADS)], axis=0).astype(BF16)
    wb = [jnp.broadcast_to(sm[:, SM_IW + h:SM_IW + h + 1] * (IDX_DIM ** -0.5), (tq, LANES)) for h in range(IDX_HEADS)]
    return qrows, wb


def _idx_scores(qrows, wb, kib, tq, tk):
    dots = _dot_nt(qrows, kib)
    cols = []
    for j in range(tk // LANES):
        acc = None
        for h in range(IDX_HEADS):
            t = jnp.maximum(dots[h * tq:(h + 1) * tq, j * LANES:(j + 1) * LANES], 0.0) * wb[h]
            acc = t if acc is None else acc + t
        cols.append(acc)
    return cols[0] if len(cols) == 1 else jnp.concatenate(cols, axis=1)


def _to_key(sc):
    b = pltpu.bitcast(jnp.where(sc == 0.0, 0.0, sc), I32)
    return b ^ ((b >> 31) & I32(0x7FFFFFFF))


def _key_to_f32(key):
    return pltpu.bitcast(key ^ ((key >> 31) & I32(0x7FFFFFFF)), F32)


def _kth_largest(count, rows, n_sel, pos_bits, tkey_ref, cut_ref):
    int_min = I32(INT_MIN)

    def bit_step(it, ub):
        cand = ub | lax.shift_left(I32(1), 31 - it)
        cs = cand ^ int_min
        return jnp.where(count(lambda k, p: k >= cs) >= n_sel, cand, ub)

    ub = lax.fori_loop(0, 32, bit_step, jnp.zeros((rows, LANES), I32))
    tkey = ub ^ int_min
    n_gt = count(lambda k, p: k > tkey)
    n_ge = count(lambda k, p: k >= tkey)
    need = n_sel - n_gt
    ambiguous = (n_ge - n_gt) > need
    tkey_ref[...] = tkey
    cut_ref[...] = jnp.full((rows, LANES), 1 << pos_bits, I32)

    @pl.when(jnp.max(jnp.where(ambiguous, 1.0, 0.0)) > 0.0)
    def _():
        def pos_step(it, q):
            cand = q | lax.shift_left(I32(1), pos_bits - 1 - it)
            below = count(lambda k, p: (k == tkey) & (p < cand))
            return jnp.where(below < need, cand, q)

        q = lax.fori_loop(0, pos_bits, pos_step, jnp.zeros((rows, LANES), I32))
        cut_ref[...] = jnp.where(ambiguous, q + 1, 1 << pos_bits)


def _flash_update(s, v, m_s, l_s, acc_s, v_transposed=False):
    m_prev = m_s[...]
    m_new = jnp.maximum(m_prev, jnp.max(s, axis=1, keepdims=True))
    m_safe = jnp.where(m_new == NEG_INF, 0.0, m_new)
    alpha = jnp.exp(m_prev - m_safe)
    p = jnp.exp(s - m_safe)
    l_s[...] = alpha * l_s[...] + jnp.sum(p, axis=1, keepdims=True)
    m_s[...] = m_new
    pv = _dot_nt(p.astype(BF16), v) if v_transposed else _dot(p.astype(BF16), v)
    acc_s[...] = alpha * acc_s[...] + pv


def _stack_heads(q, t):
    masks = _head_masks(t)
    return jnp.concatenate([jnp.where(m, q * (HD ** -0.5), 0.0) for m in masks], axis=0).astype(BF16)


def _unstack_heads(acc, l, t):
    masks = _head_masks(t)
    out = jnp.zeros((t, BR_W), F32)
    for h, m in enumerate(masks):
        out = jnp.where(m, acc[h * t:(h + 1) * t] / l[h * t:(h + 1) * t], out)
    return out


def _dsa_prompt_kernel(qi_ref, smq_ref, q_ref, ki_ref, k_ref, v_ref, o_ref,
                       sc_s, cand_s, tkey_s, cut_s, need_s, neq_s, m_s, l_s, acc_s, sa_s, sb_s,
                       *, tq, tk, pf, n_sel, pos_bits):
    i = pl.program_id(0)
    q0 = i * tq
    nch = (q0 + tq - 1) // tk + 1
    qrows, wb = _idx_queries(qi_ref[...], smq_ref[...], tq)
    qpos = q0 + _iota((tq, tk), 0)

    def fill(c, carry):
        k0 = pl.multiple_of(c * tk, tk)
        sc = _idx_scores(qrows, wb, ki_ref[pl.ds(k0, tk), :], tq, tk)
        sc = jnp.where(k0 + _iota((tq, tk), 1) <= qpos, sc, NEG_INF)
        sc_s[c] = jnp.where(sc == 0.0, 0.0, sc)
        return carry

    ngrp = (nch + pf - 1) // pf
    lax.fori_loop(0, ngrp * pf, fill, 0)
    int_min = I32(INT_MIN)
    neg_key = _to_key(jnp.full((tq, LANES), NEG_INF, F32))
    big = 1 << pos_bits

    def count(pred):
        def body(c, acc):
            for j in range(tk // LANES):
                acc = acc + jnp.where(pred(sc_s[c, :, j * LANES:(j + 1) * LANES]), 1.0, 0.0)
            return acc

        acc = lax.fori_loop(0, nch, body, jnp.zeros((tq, LANES), F32))
        return jnp.broadcast_to(jnp.sum(acc, axis=1, keepdims=True), (tq, LANES))

    rg = 2 * SUBLANES
    for g in range(tq // rg):
        def insert(c, best, g=g):
            for j in range(tk // LANES):
                x = sc_s[c, g * rg:(g + 1) * rg, j * LANES:(j + 1) * LANES]
                nxt = []
                for b in best:
                    nxt.append(jnp.maximum(b, x))
                    x = jnp.minimum(b, x)
                best = tuple(nxt)
            return best

        best = lax.fori_loop(0, nch, insert, tuple(jnp.full((rg, LANES), NEG_INF, F32) for _ in range(TOPM)))
        for u in range(TOPM):
            cand_s[u, g * rg:(g + 1) * rg, :] = _to_key(best[u])

    def count_cand(pred):
        acc = jnp.zeros((tq, LANES), F32)
        for u in range(TOPM):
            acc = acc + jnp.where(pred(cand_s[u]), 1.0, 0.0)
        return jnp.broadcast_to(jnp.sum(acc, axis=1, keepdims=True), (tq, LANES))

    def bit_step(it, ub):
        cand = ub | lax.shift_left(I32(1), 31 - it)
        cs = cand ^ int_min
        return jnp.where(count_cand(lambda k: k >= cs) >= n_sel, cand, ub)

    tk_c = lax.fori_loop(0, 32, bit_step, jnp.zeros((tq, LANES), I32)) ^ int_min
    n_gt = count_cand(lambda k: k > tk_c)
    tkey_s[...] = tk_c
    need_s[...] = n_sel - n_gt
    neq_s[...] = count_cand(lambda k: k >= tk_c) - n_gt
    dropped = (cand_s[TOPM - 1] >= tk_c) & (nch * (tk // LANES) > TOPM)

    @pl.when(jnp.max(jnp.where(dropped, 1.0, 0.0)) > 0.0)
    def _():
        def step(it, ub):
            cand = ub | lax.shift_left(I32(1), 31 - it)
            t = _key_to_f32(jnp.maximum(cand ^ int_min, neg_key))
            return jnp.where(count(lambda s: s >= t) >= n_sel, cand, ub)

        tk_f = lax.fori_loop(0, 32, step, jnp.zeros((tq, LANES), I32)) ^ int_min
        t = _key_to_f32(tk_f)
        n_gt_f = count(lambda s: s > t)
        tkey_s[...] = tk_f
        need_s[...] = n_sel - n_gt_f
        neq_s[...] = count(lambda s: s >= t) - n_gt_f

    thr128 = _key_to_f32(tkey_s[...])
    ambiguous = neq_s[...] > need_s[...]
    cut_s[...] = jnp.full((tq, LANES), big, I32)

    @pl.when(jnp.max(jnp.where(ambiguous, 1.0, 0.0)) > 0.0)
    def _():
        thr1 = thr128[:, 0:1]
        need1 = need_s[:, 0:1]
        tri = jnp.where(_iota((tk, tk), 0) <= _iota((tk, tk), 1), 1.0, 0.0).astype(BF16)

        def body(c, carry):
            base, cutf = carry
            eq = sc_s[c] == thr1
            run = base + _dot(jnp.where(eq, 1.0, 0.0).astype(BF16), tri)
            posf = (c * tk + 1 + _iota((tq, tk), 1)).astype(F32)
            hit = jnp.where(eq & (run == need1), posf, float(big))
            return run[:, tk - 1:tk], jnp.minimum(cutf, jnp.min(hit, axis=1, keepdims=True))

        _, cutf = lax.fori_loop(0, nch, body, (jnp.zeros((tq, 1), F32), jnp.full((tq, 1), float(big), F32)))
        cut_s[...] = jnp.where(ambiguous, jnp.broadcast_to(cutf, (tq, LANES)).astype(I32), big)

    qend = q0 + _iota((tq, LANES), 0) + 1
    cut_s[...] = jnp.where(tkey_s[...] == neg_key, jnp.minimum(cut_s[...], qend), cut_s[...])

    thr = thr128[:, 0:1]
    cut = cut_s[:, 0:1]
    qbd = _stack_heads(q_ref[:, 0:BR_W], tq)
    m_s[...] = jnp.full(m_s.shape, NEG_INF, F32)
    l_s[...] = jnp.zeros(l_s.shape, F32)
    acc_s[...] = jnp.zeros(acc_s.shape, F32)

    def logits_into(buf, g):
        gc = jnp.minimum(g, ngrp - 1)
        k0 = pl.multiple_of(gc * (pf * tk), pf * tk)
        sc = jnp.concatenate([sc_s[gc * pf + u] for u in range(pf)], axis=1) if pf > 1 else sc_s[gc]
        sel = ((sc > thr) | ((sc == thr) & (k0 + _iota((tq, pf * tk), 1) < cut))) & (g < ngrp)
        bias = jnp.where(sel, 0.0, NEG_INF)
        buf[...] = _dot_nt(qbd, k_ref[pl.ds(k0, pf * tk), :]) + jnp.concatenate([bias] * HEADS, axis=0)

    def consume(buf, g):
        k0 = pl.multiple_of(jnp.minimum(g, ngrp - 1) * (pf * tk), pf * tk)
        _flash_update(buf[...], v_ref[pl.ds(k0, pf * tk), :], m_s, l_s, acc_s)

    logits_into(sa_s, 0)

    def pair(j, carry):
        logits_into(sb_s, 2 * j + 1)
        consume(sa_s, 2 * j)
        logits_into(sa_s, 2 * j + 2)
        consume(sb_s, 2 * j + 1)
        return carry

    lax.fori_loop(0, (ngrp + 1) // 2, pair, 0)
    o_ref[...] = _unstack_heads(acc_s[...], l_s[...], tq)


def _dsa_prompt(iq, sm, aqkv, lp):
    tq = min(128, lp)
    tk = min(512, lp)
    n_sel = min(TOPK_MAX, lp // 4)
    pos_bits = int(lp).bit_length()
    pf = 2 if (lp // tk) % 2 == 0 else 1
    ki = sm[:lp, SM_IK:SM_IK + IDX_DIM].astype(BF16)
    kb = aqkv[:lp, BR_W:2 * BR_W].astype(BF16)
    vb = aqkv[:lp, 2 * BR_W:3 * BR_W].astype(BF16)
    return pl.pallas_call(
        functools.partial(_dsa_prompt_kernel, tq=tq, tk=tk, pf=pf, n_sel=n_sel, pos_bits=pos_bits),
        out_shape=SDS((lp, BR_W), F32),
        grid=(lp // tq,),
        in_specs=[_rows(tq, IDX_HEADS * IDX_DIM), _rows(tq, LANES), _rows(tq, 3 * BR_W),
                  _resident((lp, IDX_DIM)), _resident((lp, BR_W)), _resident((lp, BR_W))],
        out_specs=_rows(tq, BR_W),
        scratch_shapes=[pltpu.VMEM((lp // tk, tq, tk), F32), pltpu.VMEM((TOPM, tq, LANES), I32),
                        pltpu.VMEM((tq, LANES), I32), pltpu.VMEM((tq, LANES), I32),
                        pltpu.VMEM((tq, LANES), F32), pltpu.VMEM((tq, LANES), F32),
                        pltpu.VMEM((HEADS * tq, 1), F32), pltpu.VMEM((HEADS * tq, 1), F32),
                        pltpu.VMEM((HEADS * tq, BR_W), F32),
                        pltpu.VMEM((HEADS * tq, pf * tk), F32), pltpu.VMEM((HEADS * tq, pf * tk), F32)],
        compiler_params=_cp(("arbitrary",)),
        name="dsa_prompt",
    )(iq, sm, aqkv, ki, kb, vb)


def _dsa_sample_score_kernel(pt_ref, qr_ref, wb_ref, *refs, t, pg, ng, past, n_sel, pos_bits):
    pages = refs[:pg]
    knew_ref, scp_ref, scn_ref, thr_ref, cut_ref, keys_s, keyn_s, tkey_s = refs[pg:]
    j = pl.program_id(1)
    qrows = qr_ref[...]
    wb = wb_ref[...] * (IDX_DIM ** -0.5)

    def reduce_heads(dots):
        w = jnp.maximum(dots, 0.0) * wb
        s = w[0:t]
        for h in range(1, IDX_HEADS):
            s = s + w[h * t:(h + 1) * t]
        return s

    sc = jnp.concatenate([reduce_heads(_dot(qrows, p[...].astype(BF16))) for p in pages], axis=1)
    scp_ref[...] = sc
    keys_s[j] = _to_key(sc)

    @pl.when(j == ng - 1)
    def _():
        lane = _iota((t, LANES), 1)
        sn = reduce_heads(_dot_nt(qrows, knew_ref[...].astype(BF16)))
        sn = jnp.where(lane <= _iota((t, LANES), 0), sn, NEG_INF)
        scn_ref[...] = sn
        keyn_s[...] = _to_key(sn)

        def count(pred):
            def body(c, acc):
                for jj in range(pg):
                    p = c * (pg * LANES) + jj * LANES + lane
                    acc = acc + jnp.where(pred(keys_s[c, :, jj * LANES:(jj + 1) * LANES], p), 1.0, 0.0)
                return acc

            acc = lax.fori_loop(0, ng, body, jnp.zeros((t, LANES), F32))
            acc = acc + jnp.where(pred(keyn_s[...], past + lane), 1.0, 0.0)
            return jnp.broadcast_to(jnp.sum(acc, axis=1, keepdims=True), (t, LANES))

        _kth_largest(count, t, n_sel, pos_bits, tkey_s, cut_ref)
        thr_ref[...] = _key_to_f32(tkey_s[...])


def _dsa_sample_attn_kernel(pt_ref, q_ref, *refs, t, pg, ng, past):
    kpages = refs[:pg]
    vpages = refs[pg:2 * pg]
    (scp_ref, scn_ref, thr_ref, cut_ref, knew_ref, vnew_ref, o_ref, m_s, l_s, acc_s) = refs[2 * pg:]
    j = pl.program_id(1)

    @pl.when(j == 0)
    def _():
        m_s[...] = jnp.full(m_s.shape, NEG_INF, F32)
        l_s[...] = jnp.zeros(l_s.shape, F32)
        acc_s[...] = jnp.zeros(acc_s.shape, F32)

    qbd = _stack_heads(q_ref[...], t)
    thr = thr_ref[:, 0:1]
    cut = cut_ref[:, 0:1]
    w = pg * LANES
    sc = scp_ref[...]
    kidx = j * w + _iota((t, w), 1)
    bias = jnp.where((sc > thr) | ((sc == thr) & (kidx < cut)), 0.0, NEG_INF)
    kcat = jnp.concatenate([p[...].reshape(BR_W, PAGE_SIZE).astype(BF16) for p in kpages], axis=1)
    vcat = jnp.concatenate([p[...].reshape(BR_W, PAGE_SIZE).astype(BF16) for p in vpages], axis=1)
    s = _dot(qbd, kcat) + jnp.concatenate([bias] * HEADS, axis=0)
    _flash_update(s, vcat, m_s, l_s, acc_s, v_transposed=True)

    @pl.when(j == ng - 1)
    def _():
        lane = _iota((t, LANES), 1)
        scn = scn_ref[...]
        seln = ((scn > thr) | ((scn == thr) & (past + lane < cut))) & (lane <= _iota((t, LANES), 0))
        biasn = jnp.where(seln, 0.0, NEG_INF)
        sn = _dot_nt(qbd, knew_ref[...].astype(BF16)) + jnp.concatenate([biasn] * HEADS, axis=0)
        _flash_update(sn, vnew_ref[...].astype(BF16), m_s, l_s, acc_s)
        o_ref[...] = _unstack_heads(acc_s[...], l_s[...], t)


def _dsa_sample(iq_s, sm_s, aqkv_s, cache_k, cache_v, cache_idx_k, page_table, layer, db, t):
    n_pages = page_table.shape[1]
    past = n_pages * PAGE_SIZE
    n_sel = min(TOPK_MAX, (past + t) // 4)
    pos_bits = int(past + LANES).bit_length()
    pg = math.gcd(n_pages, 32)
    ng = n_pages // pg
    ckt = cache_k.transpose(0, 1, 3, 4, 2)
    cvt = cache_v.transpose(0, 1, 3, 4, 2)
    cit = cache_idx_k.transpose(0, 1, 3, 2)

    qrows = iq_s.reshape(db, t, IDX_HEADS, IDX_DIM).transpose(0, 2, 1, 3).reshape(db, IDX_HEADS * t, IDX_DIM).astype(BF16)
    wcol = sm_s[:, SM_IW:SM_IW + IDX_HEADS].reshape(db, t, IDX_HEADS).transpose(0, 2, 1).reshape(db, IDX_HEADS * t, 1)
    wbs = jnp.broadcast_to(wcol, (db, IDX_HEADS * t, LANES))
    pad = lambda a: jnp.pad(a.reshape(db, t, a.shape[-1]), ((0, 0), (0, LANES - t), (0, 0)))
    ki_new = pad(sm_s[:, SM_IK:SM_IK + IDX_DIM])
    k_new = pad(aqkv_s[:, BR_W:2 * BR_W])
    v_new = pad(aqkv_s[:, 2 * BR_W:3 * BR_W])
    q3 = aqkv_s[:, 0:BR_W].reshape(db, t, BR_W)

    def idx_page(i):
        return pl.BlockSpec((None, None, IDX_DIM, PAGE_SIZE), lambda b, j, pt: (layer, pt[b, j * pg + i], 0, 0))

    def kv_page(i):
        return pl.BlockSpec((None, None, HEADS, HD, PAGE_SIZE), lambda b, j, pt: (layer, pt[b, j * pg + i], 0, 0, 0))

    per_b = lambda r, w: pl.BlockSpec((None, r, w), lambda b, j, pt: (b, 0, 0))
    sc_spec = pl.BlockSpec((None, t, pg * LANES), lambda b, j, pt: (b, 0, j))

    scp, scn, thr, cut = pl.pallas_call(
        functools.partial(_dsa_sample_score_kernel, t=t, pg=pg, ng=ng, past=past, n_sel=n_sel, pos_bits=pos_bits),
        out_shape=(SDS((db, t, past), F32), SDS((db, t, LANES), F32), SDS((db, t, LANES), F32), SDS((db, t, LANES), I32)),
        grid_spec=pltpu.PrefetchScalarGridSpec(
            num_scalar_prefetch=1, grid=(db, ng),
            in_specs=[per_b(IDX_HEADS * t, IDX_DIM), per_b(IDX_HEADS * t, LANES)]
                     + [idx_page(i) for i in range(pg)] + [per_b(LANES, IDX_DIM)],
            out_specs=(sc_spec, per_b(t, LANES), per_b(t, LANES), per_b(t, LANES)),
            scratch_shapes=[pltpu.VMEM((ng, t, pg * LANES), I32), pltpu.VMEM((t, LANES), I32), pltpu.VMEM((t, LANES), I32)]),
        compiler_params=_cp(("arbitrary", "arbitrary")),
        name="dsa_sample_scores",
    )(page_table, qrows, wbs, *([cit] * pg), ki_new)

    out = pl.pallas_call(
        functools.partial(_dsa_sample_attn_kernel, t=t, pg=pg, ng=ng, past=past),
        out_shape=SDS((db, t, BR_W), F32),
        grid_spec=pltpu.PrefetchScalarGridSpec(
            num_scalar_prefetch=1, grid=(db, ng),
            in_specs=[per_b(t, BR_W)] + [kv_page(i) for i in range(pg)] * 2
                     + [sc_spec, per_b(t, LANES), per_b(t, LANES), per_b(t, LANES), per_b(LANES, BR_W), per_b(LANES, BR_W)],
            out_specs=per_b(t, BR_W),
            scratch_shapes=[pltpu.VMEM((HEADS * t, 1), F32), pltpu.VMEM((HEADS * t, 1), F32),
                            pltpu.VMEM((HEADS * t, BR_W), F32)]),
        compiler_params=_cp(("arbitrary", "arbitrary")),
        name="dsa_sample_attention",
    )(page_table, q3, *([ckt] * pg), *([cvt] * pg), scp, scn, thr, cut, k_new, v_new)
    return out.reshape(db * t, BR_W)


def _mlstm_chunk(q, k, v, li, lf, cst, n, m, c):
    r = _iota((c, c), 0)
    cc = _iota((c, c), 1)
    hh = range(len(q))
    qk = [_dot_nt(q[h], k[h], HIGHEST) for h in hh]
    qc = [_dot_nt(q[h], cst[h], HIGHEST) for h in hh]
    b_col, mt, ei, s = [], [], [], []
    for h in hh:
        b_row, bc_ = _cumsum_col(lf[h], c)
        log_d = jnp.where(cc <= r, bc_ - b_row + _col_to_row(li[h], c), NEG_INF)
        inter = bc_ + m[h]
        mt_ = jnp.maximum(inter, jnp.max(log_d, axis=1, keepdims=True))
        b_col.append(bc_)
        mt.append(mt_)
        ei.append(jnp.exp(inter - mt_))
        s.append(qk[h] * jnp.exp(log_d - mt_))
    sv = [_hdot(s[h], v[h]) for h in hh]
    outs, c_new, n_new, m_new, wsv = [], [], [], [], []
    for h in hh:
        den = jnp.sum(s[h], axis=1, keepdims=True) + ei[h] * jnp.sum(q[h] * n[h], axis=1, keepdims=True)
        outs.append((sv[h] + ei[h] * qc[h]) / jnp.maximum(jnp.abs(den), jnp.exp(-mt[h])))
        bc = b_col[h][c - 1:c, :]
        gs = bc - b_col[h] + li[h]
        mn = jnp.maximum(bc + m[h], jnp.max(gs, axis=0, keepdims=True))
        ws = jnp.exp(gs - mn)
        dec = jnp.exp(bc + m[h] - mn)
        wsv.append(ws * v[h])
        c_new.append(dec * cst[h])
        n_new.append(dec * n[h] + jnp.sum(ws * k[h], axis=0, keepdims=True))
        m_new.append(mn)
    c_new = [c_new[h] + _dot_tn(wsv[h], k[h], HIGHEST) for h in hh]
    return outs, c_new, n_new, m_new


def _mlstm_kernel(x_ref, sm_ref, c0_ref, n0_ref, m0_ref, lng_ref, o_ref, c1_ref, n1_ref, m1_ref,
                  c_s, n_s, m_s, *, c, cb):
    i = pl.program_id(1)

    @pl.when(i == 0)
    def _():
        c_s[...] = c0_ref[...]
        n_s[...] = n0_ref[...]
        m_s[...] = m0_ref[...]

    def chunk(ci, carry):
        r0 = pl.multiple_of(ci * c, c)
        hh = range(HEADS)
        col = lambda base, h: x_ref[pl.ds(r0, c), base + HD * h:base + HD * (h + 1)]
        hs, c_new, n_new, m_new = _mlstm_chunk(
            [col(0, h) for h in hh], [col(BR_W, h) * (HD ** -0.5) for h in hh], [col(2 * BR_W, h) for h in hh],
            [sm_ref[pl.ds(r0, c), SM_MI + h:SM_MI + h + 1] for h in hh],
            [sm_ref[pl.ds(r0, c), SM_MF + h:SM_MF + h + 1] for h in hh],
            [c_s[h] for h in hh], [n_s[h] for h in hh], [m_s[h] for h in hh], c)
        outs = []
        for h in hh:
            c_s[h] = c_new[h]
            n_s[h] = n_new[h]
            m_s[h] = m_new[h]
            mu = jnp.mean(hs[h], axis=1, keepdims=True)
            var = jnp.mean(jnp.square(hs[h] - mu), axis=1, keepdims=True)
            y = (hs[h] - mu) * lax.rsqrt(var + LN_EPS) * lng_ref[:, HD * h:HD * (h + 1)]
            outs.append(y * jax.nn.sigmoid(col(3 * BR_W, h)))
        o_ref[pl.ds(r0, c), :] = jnp.concatenate(outs, axis=1)
        return carry

    lax.fori_loop(0, cb, chunk, 0)

    @pl.when(i == pl.num_programs(1) - 1)
    def _():
        c1_ref[...] = c_s[...]
        n1_ref[...] = n_s[...]
        m1_ref[...] = m_s[...]


def _seq_blocks(l, chunk, max_rows):
    nc = l // chunk
    cb = math.gcd(nc, max(1, max_rows // chunk))
    return chunk * cb, cb, nc // cb


def _state_spec(shape):
    return pl.BlockSpec((None,) + shape, lambda b, i: (b,) + (0,) * len(shape))


def _mlstm(mslab, sm, row_off, b, l, c0, n0, m0, ln_g):
    c = math.gcd(l, MLSTM_CHUNK)
    rows, cb, nblk = _seq_blocks(l, c, 512)
    off = row_off // rows
    seq = lambda w: pl.BlockSpec((rows, w), lambda bb, i: (off + bb * nblk + i, 0))
    o, c1, n1, m1 = pl.pallas_call(
        functools.partial(_mlstm_kernel, c=c, cb=cb),
        out_shape=(SDS((b * l, BR_W), F32), SDS((b, HEADS, HD, HD), F32), SDS((b, HEADS, 1, HD), F32),
                   SDS((b, HEADS, 1, 1), F32)),
        grid=(b, nblk),
        in_specs=[seq(4 * BR_W), seq(LANES), _state_spec((HEADS, HD, HD)), _state_spec((HEADS, 1, HD)),
                  _state_spec((HEADS, 1, 1)), pl.BlockSpec((1, BR_W), lambda bb, i: (0, 0))],
        out_specs=(pl.BlockSpec((rows, BR_W), lambda bb, i: (bb * nblk + i, 0)), _state_spec((HEADS, HD, HD)),
                   _state_spec((HEADS, 1, HD)), _state_spec((HEADS, 1, 1))),
        scratch_shapes=[pltpu.VMEM((HEADS, HD, HD), F32), pltpu.VMEM((HEADS, 1, HD), F32), pltpu.VMEM((HEADS, 1, 1), F32)],
        compiler_params=_cp(("parallel", "arbitrary")),
        name="mlstm",
    )(mslab, sm, c0, n0.reshape(b, HEADS, 1, HD), m0.reshape(b, HEADS, 1, 1), ln_g.reshape(1, BR_W))
    return o, c1, n1.reshape(b, HEADS, HD), m1.reshape(b, HEADS)


def _shift_rows(x, halo, s):
    rows = x.shape[0]
    rolled = pltpu.roll(x, s, 0)
    first = jnp.where(_iota((SUBLANES, x.shape[1]), 0) < s, pltpu.roll(halo, s, 0), rolled[0:SUBLANES])
    return first if rows == SUBLANES else jnp.concatenate([first, rolled[SUBLANES:]], axis=0)


def _halo_specs(rows, w, off_rows, nt):
    hb = rows // SUBLANES
    tile = pl.BlockSpec((rows, w), lambda b, i: (off_rows // rows + b * nt + i, 0))
    halo = pl.BlockSpec((SUBLANES, w), lambda b, i: (jnp.maximum(off_rows // SUBLANES + (b * nt + i) * hb - 1, 0), 0))
    state = pl.BlockSpec((None, SUBLANES, w), lambda b, i: (b, 0, 0))
    return tile, halo, state


def _pad_state_rows(st, w):
    b, r, w0 = st.shape
    return jnp.pad(st, ((0, 0), (SUBLANES - r, 0), (0, w - w0)))


def _cumsum_rows(x):
    row = _iota(x.shape, 0)
    sh = 1
    while sh < x.shape[0]:
        x = x + jnp.where(row >= sh, pltpu.roll(x, sh, 0), 0.0)
        sh *= 2
    return x


def _unit_lower_inverse(ms, c):
    r = _iota((c, c), 0)
    cc = _iota((c, c), 1)
    fold = jnp.where((_iota((c, SUBLANES), 0) & 7) == _iota((c, SUBLANES), 1), 1.0, 0.0)
    same8 = (r >> 3) == (cc >> 3)
    m8s = [_dot3(jnp.where(same8, m, 0.0), fold) for m in ms]
    zs = [jnp.where(r == cc, 1.0, 0.0)] * len(ms)

    def pivot(z, s):
        return jnp.broadcast_to(z.reshape(c // SUBLANES, SUBLANES, c)[:, s:s + 1, :],
                                (c // SUBLANES, SUBLANES, c)).reshape(c, c)

    for s in range(SUBLANES - 1):
        zs = [z + m8[:, s:s + 1] * pivot(z, s) for z, m8 in zip(zs, m8s)]
    sh = 3
    while (1 << sh) < c:
        off = ((r >> (sh + 1)) == (cc >> (sh + 1))) & ((r >> sh) != (cc >> sh))
        ts = [_dot3(z, jnp.where(off, m, 0.0)) for z, m in zip(zs, ms)]
        zs = [z + _dot3(t, z) for z, t in zip(zs, ts)]
        sh += 1
    return zs


def _lin_scan(a_ref, y0_ref, phi_ref, psi_ref, s_s, y_s, c, cb):
    def chunk(ci, carry):
        r0 = pl.multiple_of(ci * c, c)
        sts = [s_s[h] for h in range(HEADS)]
        new = [_dot3(sts[h], phi_ref[ci, h]) for h in range(HEADS)]
        ys = [_dot3(a_ref[pl.ds(r0, c), HD * h:HD * (h + 1)], sts[h], _NT) for h in range(HEADS)]
        for h in range(HEADS):
            s_s[h] = new[h] + psi_ref[ci, h]
        y_s[pl.ds(r0, c), :] = jnp.concatenate(ys, axis=1) + y0_ref[pl.ds(r0, c), :]
        return carry

    lax.fori_loop(0, cb, chunk, 0)


def _chunk_specs(rows, cb, nblk, widths):
    row_specs = [pl.BlockSpec((rows, w), lambda bb, i: (bb * nblk + i, 0)) for w in widths]
    mat = pl.BlockSpec((cb, HEADS, HD, HD), lambda bb, i: (bb * nblk + i, 0, 0, 0))
    return row_specs, mat


def _rwkv_prep_kernel(p_ref, halo_ref, st_ref, vec_ref, w2_ref, a2_ref, g2_ref, rs_ref, post_ref):
    i = pl.program_id(1)
    p = p_ref[...]
    rows = p.shape[0]
    halo = jnp.where(i == 0, st_ref[...], halo_ref[...])
    prev = _shift_rows(p, halo, 1)
    mu = jnp.concatenate([vec_ref[0:1, :], vec_ref[1:2, :], vec_ref[2:3, :], vec_ref[3:4, :]], axis=1)
    pm = p + (prev - p) * mu
    r = pm[:, 0:BR_W]
    k = pm[:, BR_W:2 * BR_W]
    v = pm[:, 2 * BR_W:3 * BR_W]
    o = 3 * BR_W
    wd = pm[:, o:o + RWKV_W_LORA]
    ad = pm[:, o + RWKV_W_LORA:o + RWKV_W_LORA + RWKV_A_LORA]
    gd = pm[:, o + RWKV_W_LORA + RWKV_A_LORA:RWKV_PROJ]
    w0, a0, kkp, ka, rk = (vec_ref[4:5, :], vec_ref[5:6, :], vec_ref[6:7, :], vec_ref[7:8, :], vec_ref[8:9, :])
    wlog = -jax.nn.softplus(-(w0 + _hdot(jnp.tanh(wd), w2_ref[...]))) - 0.5
    log_decay = -jnp.exp(wlog)
    a = jax.nn.sigmoid(a0 + _hdot(ad, a2_ref[...]))
    g = _hdot(jax.nn.sigmoid(gd), g2_ref[...])
    masks = _head_masks(rows)
    kkr = k * kkp
    kk = kkr * lax.rsqrt(jnp.maximum(_seg_sum(kkr * kkr, masks), 1e-12))
    k2 = k * (1.0 + (a - 1.0) * ka)
    bonus = _seg_sum(r * k2 * rk, masks) * v
    for n, t in enumerate((r, log_decay, k2, v, kk, kk * a)):
        rs_ref[:, n * BR_W:(n + 1) * BR_W] = t
    post_ref[:, 0:BR_W] = bonus
    post_ref[:, BR_W:2 * BR_W] = g


def _rwkv_chunk_kernel(rs_ref, rh_ref, y0_ref, phi_ref, psi_ref, *, c, cb, gc):
    r_i = _iota((c, c), 0)
    c_i = _iota((c, c), 1)
    eye = _iota((HD, HD), 0) == _iota((HD, HD), 1)
    jobs = [(g, h) for g in range(gc) for h in range(HEADS)]
    hs = lambda x, h: x[:, HD * h:HD * (h + 1)]

    def group(gi, carry):
        r0s = [pl.multiple_of((gi * gc + g) * c, c) for g in range(gc)]
        w = []
        for r0 in r0s:
            r, lw, k, v, kk, kka = (rs_ref[pl.ds(r0, c), n * BR_W:(n + 1) * BR_W] for n in range(6))
            cum = _cumsum_rows(lw)
            pc = cum[c - 1:c, :]
            inv = jnp.exp(-cum)
            rest = jnp.exp(pc - cum)
            w.append(dict(at=-kk * jnp.exp(cum - lw),
                          bt=kka * inv, kt=k * inv,
                          rt=r * jnp.exp(cum), bd=kka * rest, kd=k * rest, v=v, epc=jnp.exp(pc)))
        big = [_dot3(jnp.concatenate([hs(w[g]["at"], h), hs(w[g]["rt"], h)], axis=0),
                     jnp.concatenate([hs(w[g]["bt"], h), hs(w[g]["kt"], h)], axis=0), _NT) for g, h in jobs]
        mab = [jnp.where(c_i < r_i, x[0:c, 0:c], 0.0) for x in big]
        mak = [jnp.where(c_i < r_i, x[0:c, c:2 * c], 0.0) for x in big]
        nn = [jnp.concatenate([jnp.where(c_i <= r_i, x[c:2 * c, 0:c], 0.0),
                               jnp.where(c_i <= r_i, x[c:2 * c, c:2 * c], 0.0)], axis=1) for x in big]
        tinv = _unit_lower_inverse(mab, c)
        mv = [_dot3(m, hs(w[g]["v"], h)) for m, (g, h) in zip(mak, jobs)]
        hg = [_dot3(t, jnp.concatenate([hs(w[g]["at"], h), x], axis=1)) for t, x, (g, h) in zip(tinv, mv, jobs)]
        out = [_dot3(n2, jnp.concatenate([x, jnp.concatenate([jnp.zeros((c, HD), F32), hs(w[g]["v"], h)], axis=1)], axis=0))
               for n2, x, (g, h) in zip(nn, hg, jobs)]
        hb = [_dot3(x, hs(w[g]["bd"], h), _TN) for x, (g, h) in zip(hg, jobs)]
        vk = [_dot3(hs(w[g]["v"], h), hs(w[g]["kd"], h), _TN) for g, h in jobs]
        for g in range(gc):
            sel = [n for n, (gg, _) in enumerate(jobs) if gg == g]
            rh_ref[pl.ds(r0s[g], c), :] = w[g]["rt"] + jnp.concatenate([out[n][:, 0:HD] for n in sel], axis=1)
            y0_ref[pl.ds(r0s[g], c), :] = jnp.concatenate([out[n][:, HD:2 * HD] for n in sel], axis=1)
            phi_ref[gi * gc + g] = jnp.stack([jnp.where(eye, jnp.broadcast_to(hs(w[g]["epc"], jobs[n][1]), (HD, HD)), 0.0)
                                              + hb[n][0:HD] for n in sel])
            psi_ref[gi * gc + g] = jnp.stack([hb[n][HD:2 * HD] + vk[n] for n in sel])
        return carry

    lax.fori_loop(0, cb // gc, group, 0)


def _rwkv_scan_kernel(rh_ref, y0_ref, phi_ref, psi_ref, post_ref, s0_ref, ln_ref, o_ref, s1_ref, s_s, y_s, *, c, cb):
    i = pl.program_id(1)

    @pl.when(i == 0)
    def _():
        s_s[...] = s0_ref[...]

    _lin_scan(rh_ref, y0_ref, phi_ref, psi_ref, s_s, y_s, c, cb)
    y = y_s[...]
    masks = _head_masks(c * cb)
    mu = _seg_sum(y, masks) * (1.0 / HD)
    var = _seg_sum(jnp.square(y - mu), masks) * (1.0 / HD)
    y = (y - mu) * lax.rsqrt(var + LN_EPS) * ln_ref[0:1, :] + ln_ref[1:2, :]
    o_ref[...] = (y + post_ref[:, 0:BR_W]) * post_ref[:, BR_W:2 * BR_W]

    @pl.when(i == pl.num_programs(1) - 1)
    def _():
        s1_ref[...] = s_s[...]


def _rwkv(rslab, row_off, b, l, sh0, s0, lp):
    rows = math.gcd(l, 256)
    nt = l // rows
    tile, halo, state = _halo_specs(rows, RWKV_PROJ, row_off, nt)
    quarter = lambda a: a.reshape(4, BR_W)
    vec = jnp.concatenate([quarter(lp["rw_mu"]), lp["rw_w0"][None], lp["rw_a0"][None], lp["rw_kk"][None],
                           lp["rw_ka"][None], lp["rw_rk"].reshape(1, BR_W), jnp.zeros((7, BR_W), F32)], axis=0)
    rs, post = pl.pallas_call(
        _rwkv_prep_kernel,
        out_shape=(SDS((b * l, 6 * BR_W), F32), SDS((b * l, 2 * BR_W), F32)),
        grid=(b, nt),
        in_specs=[tile, halo, state, pl.BlockSpec((16, BR_W), lambda bb, i: (0, 0)),
                  pl.BlockSpec((RWKV_W_LORA, BR_W), lambda bb, i: (0, 0)),
                  pl.BlockSpec((RWKV_A_LORA, BR_W), lambda bb, i: (0, 0)),
                  pl.BlockSpec((RWKV_G_LORA, BR_W), lambda bb, i: (0, 0))],
        out_specs=(pl.BlockSpec((rows, 6 * BR_W), lambda bb, i: (bb * nt + i, 0)),
                   pl.BlockSpec((rows, 2 * BR_W), lambda bb, i: (bb * nt + i, 0))),
        compiler_params=_cp(("parallel", "parallel")),
        name="rwkv_prep",
    )(rslab, rslab, _pad_state_rows(sh0[:, None, :], RWKV_PROJ), vec, lp["rw_w2"], lp["rw_a2"], lp["rw_g2"])

    c = math.gcd(l, 64)
    nc = l // c
    prow, pcb, pblk = _seq_blocks(l, c, 256)
    (rs_spec, rh_spec, y0_spec), mat = _chunk_specs(prow, pcb, pblk, (6 * BR_W, BR_W, BR_W))
    rh, y0, phi, psi = pl.pallas_call(
        functools.partial(_rwkv_chunk_kernel, c=c, cb=pcb, gc=math.gcd(pcb, 2)),
        out_shape=(SDS((b * l, BR_W), F32), SDS((b * l, BR_W), F32),
                   SDS((b * nc, HEADS, HD, HD), F32), SDS((b * nc, HEADS, HD, HD), F32)),
        grid=(b, pblk),
        in_specs=[rs_spec],
        out_specs=(rh_spec, y0_spec, mat, mat),
        compiler_params=_cp(("parallel", "parallel")),
        name="rwkv_chunk",
    )(rs)

    srow, scb, sblk = _seq_blocks(l, c, 512)
    (rh_spec, y0_spec, post_spec, o_spec), mat = _chunk_specs(srow, scb, sblk, (BR_W, BR_W, 2 * BR_W, BR_W))
    ln = jnp.stack([lp["rw_ln_g"], lp["rw_ln_b"]] + [jnp.zeros((BR_W,), F32)] * 6)
    o, s1 = pl.pallas_call(
        functools.partial(_rwkv_scan_kernel, c=c, cb=scb),
        out_shape=(SDS((b * l, BR_W), F32), SDS((b, HEADS, HD, HD), F32)),
        grid=(b, sblk),
        in_specs=[rh_spec, y0_spec, mat, mat, post_spec, _state_spec((HEADS, HD, HD)),
                  pl.BlockSpec((SUBLANES, BR_W), lambda bb, i: (0, 0))],
        out_specs=(o_spec, _state_spec((HEADS, HD, HD))),
        scratch_shapes=[pltpu.VMEM((HEADS, HD, HD), F32), pltpu.VMEM((srow, BR_W), F32)],
        compiler_params=_cp(("parallel", "arbitrary")),
        name="rwkv_scan",
    )(rh, y0, phi, psi, post, s0, ln)
    return o, s1


def _gdn_prep_kernel(x_ref, halo_ref, st_ref, taps_ref, o_ref):
    i = pl.program_id(1)
    x = x_ref[:, 0:GDN_CONV_CH]
    rows = x.shape[0]
    halo = jnp.where(i == 0, st_ref[:, 0:GDN_CONV_CH], halo_ref[:, 0:GDN_CONV_CH])
    conv = x * taps_ref[CONV_W - 1:CONV_W, :]
    for s in range(1, CONV_W):
        conv = conv + _shift_rows(x, halo, s) * taps_ref[CONV_W - 1 - s:CONV_W - s, :]
    conv = conv * jax.nn.sigmoid(conv)
    masks = _head_masks(rows)
    for n in range(3):
        t = conv[:, n * BR_W:(n + 1) * BR_W]
        if n < 2:
            t = t * lax.rsqrt(jnp.maximum(_seg_sum(t * t, masks), 1e-12))
        if n == 0:
            t = t * (HD ** -0.5)
        o_ref[:, n * BR_W:(n + 1) * BR_W] = t


def _gdn_chunk_kernel(x_ref, sm_ref, qh_ref, o0_ref, phi_ref, psi_ref, *, c, cb, gc):
    r_i = _iota((c, c), 0)
    c_i = _iota((c, c), 1)
    eye = _iota((HD, HD), 0) == _iota((HD, HD), 1)
    jobs = [(g, h) for g in range(gc) for h in range(HEADS)]

    def group(gi, carry):
        r0s = [pl.multiple_of((gi * gc + g) * c, c) for g in range(gc)]
        q, k, v, beta, dec, eg, gl, wv = [], [], [], [], [], [], [], []
        for g, h in jobs:
            r0, lo = r0s[g], HD * h
            q.append(x_ref[pl.ds(r0, c), lo:lo + HD])
            k.append(x_ref[pl.ds(r0, c), BR_W + lo:BR_W + lo + HD])
            v.append(x_ref[pl.ds(r0, c), 2 * BR_W + lo:2 * BR_W + lo + HD])
            beta.append(sm_ref[pl.ds(r0, c), SM_GB + h:SM_GB + h + 1])
            gam_row, gam_col = _cumsum_col(sm_ref[pl.ds(r0, c), SM_GA + h:SM_GA + h + 1], c)
            dec.append(jnp.exp(jnp.where(c_i <= r_i, gam_col - gam_row, NEG_INF)))
            eg.append(jnp.exp(gam_col))
            gl.append(gam_col[c - 1:c, :])
            wv.append(jnp.exp(gam_col[c - 1:c, :] - gam_col))
        n = range(len(jobs))
        big = [_dot3(jnp.concatenate([k[i], q[i]], axis=0), k[i], _NT) for i in n]
        tinv = _unit_lower_inverse([-(beta[i] * big[i][0:c] * jnp.where(c_i < r_i, dec[i], 0.0)) for i in n], c)
        x = [_dot3(tinv[i], jnp.concatenate([beta[i] * v[i], (beta[i] * eg[i]) * k[i]], axis=1)) for i in n]
        out = [_dot3(big[i][c:2 * c] * dec[i], x[i]) for i in n]
        xb = [_dot3(x[i], wv[i] * k[i], _TN) for i in n]
        for g in range(gc):
            sel = [i for i in n if jobs[i][0] == g]
            qh_ref[pl.ds(r0s[g], c), :] = jnp.concatenate([eg[i] * q[i] - out[i][:, HD:2 * HD] for i in sel], axis=1)
            o0_ref[pl.ds(r0s[g], c), :] = jnp.concatenate([out[i][:, 0:HD] for i in sel], axis=1)
            phi_ref[gi * gc + g] = jnp.stack([jnp.where(eye, jnp.broadcast_to(jnp.exp(gl[i]), (HD, HD)), 0.0)
                                              - xb[i][HD:2 * HD] for i in sel])
            psi_ref[gi * gc + g] = jnp.stack([xb[i][0:HD] for i in sel])
        return carry

    lax.fori_loop(0, cb // gc, group, 0)


def _gdn_scan_kernel(qh_ref, o0_ref, phi_ref, psi_ref, z_ref, s0_ref, gn_ref, o_ref, s1_ref, s_s, y_s, *, c, cb):
    i = pl.program_id(1)

    @pl.when(i == 0)
    def _():
        s_s[...] = s0_ref[...]

    _lin_scan(qh_ref, o0_ref, phi_ref, psi_ref, s_s, y_s, c, cb)
    o = y_s[...]
    ms = _seg_sum(o * o, _head_masks(c * cb)) * (1.0 / HD)
    z = z_ref[:, 3 * BR_W:4 * BR_W]
    o_ref[...] = o * lax.rsqrt(ms + EPS) * gn_ref[...] * (z * jax.nn.sigmoid(z))

    @pl.when(i == pl.num_programs(1) - 1)
    def _():
        s1_ref[...] = s_s[...]


def _gdn(gslab, sm, row_off, b, l, cv0, s0, lp):
    rows = math.gcd(l, 256)
    nt = l // rows
    w = 4 * BR_W
    tile, halo, state = _halo_specs(rows, w, row_off, nt)
    taps = jnp.pad(lp["gd_conv"], ((0, SUBLANES - CONV_W), (0, 0)))
    qkv = pl.pallas_call(
        _gdn_prep_kernel,
        out_shape=SDS((b * l, GDN_CONV_CH), F32),
        grid=(b, nt),
        in_specs=[tile, halo, state, pl.BlockSpec((SUBLANES, GDN_CONV_CH), lambda bb, i: (0, 0))],
        out_specs=pl.BlockSpec((rows, GDN_CONV_CH), lambda bb, i: (bb * nt + i, 0)),
        compiler_params=_cp(("parallel", "parallel")),
        name="gdn_prep",
    )(gslab, gslab, _pad_state_rows(cv0, w), taps)

    c = math.gcd(l, GDN_CHUNK)
    nc = l // c
    prow, pcb, pblk = _seq_blocks(l, c, 256)
    (x_spec, qh_spec, o0_spec), mat = _chunk_specs(prow, pcb, pblk, (GDN_CONV_CH, BR_W, BR_W))
    sm_spec = pl.BlockSpec((prow, LANES), lambda bb, i: (row_off // prow + bb * pblk + i, 0))
    qh, o0, phi, psi = pl.pallas_call(
        functools.partial(_gdn_chunk_kernel, c=c, cb=pcb, gc=math.gcd(pcb, 2)),
        out_shape=(SDS((b * l, BR_W), F32), SDS((b * l, BR_W), F32),
                   SDS((b * nc, HEADS, HD, HD), F32), SDS((b * nc, HEADS, HD, HD), F32)),
        grid=(b, pblk),
        in_specs=[x_spec, sm_spec],
        out_specs=(qh_spec, o0_spec, mat, mat),
        compiler_params=_cp(("parallel", "parallel")),
        name="gdn_chunk",
    )(qkv, sm)

    srow, scb, sblk = _seq_blocks(l, c, 512)
    (qh_spec, o0_spec, o_spec), mat = _chunk_specs(srow, scb, sblk, (BR_W, BR_W, BR_W))
    z_spec = pl.BlockSpec((srow, w), lambda bb, i: (row_off // srow + bb * sblk + i, 0))
    o, s1 = pl.pallas_call(
        functools.partial(_gdn_scan_kernel, c=c, cb=scb),
        out_shape=(SDS((b * l, BR_W), F32), SDS((b, HEADS, HD, HD), F32)),
        grid=(b, sblk),
        in_specs=[qh_spec, o0_spec, mat, mat, z_spec, _state_spec((HEADS, HD, HD)),
                  pl.BlockSpec((1, BR_W), lambda bb, i: (0, 0))],
        out_specs=(o_spec, _state_spec((HEADS, HD, HD))),
        scratch_shapes=[pltpu.VMEM((HEADS, HD, HD), F32), pltpu.VMEM((srow, BR_W), F32)],
        compiler_params=_cp(("parallel", "arbitrary")),
        name="gdn_scan",
    )(qh, o0, phi, psi, gslab, s0, jnp.tile(lp["gd_norm"], HEADS).reshape(1, BR_W))
    return o, s1


def _rope_tables(pos):
    half = HD // 2
    inv = ROPE_THETA ** (-jnp.arange(half, dtype=F32) / half)
    ang = pos.astype(F32)[:, None] * inv[None, :]
    cos, sin = jnp.cos(ang), jnp.sin(ang)
    zero = jnp.zeros_like(sin)
    tile = lambda a, b: jnp.concatenate([a, b, a, b], axis=1)
    return tile(cos, cos), tile(-sin, zero), tile(zero, sin)


def _pad_w_in(w):
    sizes = (BR_W, BR_W, BR_W, IDX_HEADS * IDX_DIM, IDX_DIM, IDX_HEADS, BR_W, BR_W, BR_W, HEADS, HEADS, BR_W,
             RWKV_PROJ, GDN_CONV_CH, HEADS, HEADS, BR_W, N_BRANCH * w.shape[0])
    cuts = np.concatenate([[0], np.cumsum(sizes)])
    col = lambda i: w[:, cuts[i]:cuts[i + 1]]
    (a_q, a_k, a_v, a_iq, a_ik, a_iw, m_q, m_k, m_v, m_i, m_f, m_o, r_p, g_qkv, g_b, g_a, g_z, gate) = (
        col(i) for i in range(len(sizes)))
    small = jnp.concatenate([a_ik, a_iw, m_i, m_f, g_b, g_a, jnp.zeros((w.shape[0], LANES - SM_END), w.dtype)], axis=1)
    return jnp.concatenate([a_q, a_k, a_v, a_iq, small, m_q, m_k, m_v, m_o, r_p, g_qkv, g_z, gate], axis=1).astype(BF16)


def _small_params(lp):
    lane_put = lambda v, at: jnp.zeros((LANES,), F32).at[at:at + v.shape[0]].set(v)
    rows = [lane_put(lp["idx_k_norm"], SM_IK),
            lane_put(lp["ml_i_bias"], SM_MI) + lane_put(lp["ml_f_bias"], SM_MF) + lane_put(lp["gd_dt_bias"], SM_GA),
            lane_put(lp["gd_A_log"], SM_GA)]
    return jnp.stack(rows + [jnp.zeros((LANES,), F32)] * (SUBLANES - len(rows)))


def kernel(x_prompt, x_sample, cache_k, cache_v, cache_idx_k, state_mlstm_C, state_mlstm_n, state_mlstm_m, state_rwkv_S, state_rwkv_shift, state_gdn_S, state_gdn_conv, page_table, ffn1_norm, ffn1_w1, ffn1_w3, ffn1_w2, mix_norm, w_in, idx_k_norm, ml_i_bias, ml_f_bias, ml_norm, rw_mu, rw_w0, rw_w2, rw_a0, rw_a2, rw_g2, rw_kk, rw_ka, rw_rk, rw_ln_g, rw_ln_b, gd_conv, gd_A_log, gd_dt_bias, gd_norm, w_branch, w_out, ffn2_norm, ffn2_w1, ffn2_w3, ffn2_w2, final_norm):
    bp, lp_, d = x_prompt.shape
    db, t, _ = x_sample.shape
    assert bp == 1 and t % SUBLANES == 0 and t <= LANES and lp_ % SUBLANES == 0
    depth = w_in.shape[0]
    past = page_table.shape[1] * PAGE_SIZE
    ns = db * t

    x = jnp.concatenate([x_prompt.reshape(lp_, d), x_sample.reshape(ns, d)], axis=0)
    pos = jnp.concatenate([jnp.arange(lp_, dtype=I32), jnp.tile(past + jnp.arange(t, dtype=I32), db)])
    cos, sa, sb = _rope_tables(pos)
    zeros = lambda *s: jnp.zeros(s, F32)

    outs_p, outs_s = [], []
    for l in range(depth):
        lp = dict(idx_k_norm=idx_k_norm[l], ml_i_bias=ml_i_bias[l], ml_f_bias=ml_f_bias[l],
                  rw_mu=rw_mu[l], rw_w0=rw_w0[l], rw_w2=rw_w2[l], rw_a0=rw_a0[l], rw_a2=rw_a2[l], rw_g2=rw_g2[l],
                  rw_kk=rw_kk[l], rw_ka=rw_ka[l], rw_rk=rw_rk[l], rw_ln_g=rw_ln_g[l], rw_ln_b=rw_ln_b[l],
                  gd_conv=gd_conv[l], gd_A_log=gd_A_log[l], gd_dt_bias=gd_dt_bias[l], gd_norm=gd_norm[l])
        x = _ffn(x, ffn1_norm[l], ffn1_w1[l], ffn1_w3[l], ffn1_w2[l])
        aqkv, iq, sm, mslab, rslab, gslab, gate = _proj(x, mix_norm[l], _pad_w_in(w_in[l]), cos, sa, sb, _small_params(lp))

        oa_p = _dsa_prompt(iq, sm, aqkv, lp_)
        oa_s = _dsa_sample(iq[lp_:], sm[lp_:], aqkv[lp_:], cache_k, cache_v, cache_idx_k, page_table, l, db, t)
        om_p, c_p, n_p, m_p = _mlstm(mslab, sm, 0, 1, lp_, zeros(1, HEADS, HD, HD), zeros(1, HEADS, HD), zeros(1, HEADS), ml_norm[l])
        om_s, c_s, n_s, m_s = _mlstm(mslab, sm, lp_, db, t, state_mlstm_C[l], state_mlstm_n[l], state_mlstm_m[l], ml_norm[l])
        or_p, rs_p = _rwkv(rslab, 0, 1, lp_, zeros(1, RWKV_PROJ), zeros(1, HEADS, HD, HD), lp)
        or_s, rs_s = _rwkv(rslab, lp_, db, t, state_rwkv_shift[l], state_rwkv_S[l], lp)
        og_p, gs_p = _gdn(gslab, sm, 0, 1, lp_, zeros(1, CONV_W - 1, GDN_CONV_CH), zeros(1, HEADS, HD, HD), lp)
        og_s, gs_s = _gdn(gslab, sm, lp_, db, t, state_gdn_conv[l], state_gdn_S[l], lp)

        cat = lambda a, b: jnp.concatenate([a, b], axis=0)
        x = _merge(x, cat(oa_p, oa_s), cat(om_p, om_s), cat(or_p, or_s), cat(og_p, og_s), gate, w_branch[l], w_out[l])
        x = _ffn(x, ffn2_norm[l], ffn2_w1[l], ffn2_w3[l], ffn2_w2[l])

        k_all = aqkv[:, BR_W:2 * BR_W]
        v_all = aqkv[:, 2 * BR_W:3 * BR_W]
        ik_all = sm[:, SM_IK:SM_IK + IDX_DIM]
        gq = gslab[:, 0:GDN_CONV_CH]
        outs_p.append((k_all[:lp_].reshape(1, lp_, HEADS, HD), v_all[:lp_].reshape(1, lp_, HEADS, HD),
                       ik_all[:lp_].reshape(1, lp_, IDX_DIM), c_p, n_p, m_p, rs_p, rslab[lp_ - 1:lp_],
                       gs_p, gq[lp_ - (CONV_W - 1):lp_].reshape(1, CONV_W - 1, GDN_CONV_CH)))
        outs_s.append((k_all[lp_:].reshape(db, t, HEADS, HD), v_all[lp_:].reshape(db, t, HEADS, HD),
                       ik_all[lp_:].reshape(db, t, IDX_DIM), c_s, n_s, m_s, rs_s,
                       rslab[lp_:].reshape(db, t, RWKV_PROJ)[:, -1],
                       gs_s, gq[lp_:].reshape(db, t, GDN_CONV_CH)[:, t - (CONV_W - 1):]))

    y = _final_rms(x, final_norm)
    st_p = [jnp.stack(z) for z in zip(*outs_p)]
    st_s = [jnp.stack(z) for z in zip(*outs_s)]
    return (y[:lp_].reshape(1, lp_, d), y[lp_:].reshape(db, t, d), *st_p, *st_s)
```

```python
import functools
import math

import jax
import jax.numpy as jnp
import numpy as np
from jax import lax
from jax.experimental import pallas as pl
from jax.experimental.pallas import tpu as pltpu

F32 = jnp.float32
BF16 = jnp.bfloat16
I32 = jnp.int32
SDS = jax.ShapeDtypeStruct

HD = 64
HEADS = 4
BR_W = HEADS * HD
N_BRANCH = 4
IDX_HEADS = 8
IDX_DIM = 64
TOPK_MAX = 256
TOPM = 12
ROPE_THETA = 10000.0
MLSTM_CHUNK = 256
GDN_CHUNK = 64
CONV_W = 4
RWKV_W_LORA = 64
RWKV_A_LORA = 64
RWKV_G_LORA = 128
RWKV_PROJ = 3 * BR_W + RWKV_W_LORA + RWKV_A_LORA + RWKV_G_LORA
GDN_CONV_CH = 3 * BR_W
EPS = 1e-6
LN_EPS = 1e-5
PAGE_SIZE = 128

LANES = 128
SUBLANES = 8
VMEM_LIMIT_BYTES = 58 * 1024 * 1024

SM_IK = 0
SM_IW = 64
SM_MI = 72
SM_MF = 76
SM_GB = 80
SM_GA = 84
SM_END = 88

PC_AQKV = 0
PC_IQ = 768
PC_SM = 1280
PC_M = 1408
PC_R = 2432
PC_G = 3456
PC_GATE = 4480
PC_END = 8576

HIGHEST = lax.Precision.HIGHEST
INT_MIN = -2147483648
NEG_INF = float("-inf")


def _cp(sem):
    return pltpu.CompilerParams(dimension_semantics=sem, vmem_limit_bytes=VMEM_LIMIT_BYTES)


def _resident(shape):
    return pl.BlockSpec(shape, lambda *_: (0,) * len(shape), pipeline_mode=pl.Buffered(1))


def _rows(tm, w, off_blocks=0):
    return pl.BlockSpec((tm, w), lambda i, *_: (i + off_blocks, 0))


def _dot(a, b):
    return jnp.dot(a, b, preferred_element_type=F32)


def _dot_nt(a, b, precision=None):
    return lax.dot_general(a, b, (((1,), (1,)), ((), ())), preferred_element_type=F32, precision=precision)


def _dot_tn(a, b, precision=None):
    return lax.dot_general(a, b, (((0,), (0,)), ((), ())), preferred_element_type=F32, precision=precision)


def _hdot(a, b):
    return jnp.dot(a, b, preferred_element_type=F32, precision=HIGHEST)


def _dot3(a, b, dims=(((1,), (0,)), ((), ()))):
    ah, bh = a.astype(BF16), b.astype(BF16)
    al, bl = (a - ah.astype(F32)).astype(BF16), (b - bh.astype(F32)).astype(BF16)
    d = lambda x, y: lax.dot_general(x, y, dims, preferred_element_type=F32)
    return d(ah, bh) + (d(ah, bl) + d(al, bh))


_NT = (((1,), (1,)), ((), ()))
_TN = (((0,), (0,)), ((), ()))


def _rms(x, g):
    return x * lax.rsqrt(jnp.mean(x * x, axis=-1, keepdims=True) + EPS) * g


def _iota(shape, dim):
    return lax.broadcasted_iota(I32, shape, dim)


def _head_masks(rows):
    lane = _iota((rows, BR_W), 1)
    return [(lane >= HD * h) & (lane < HD * (h + 1)) for h in range(HEADS)]


def _seg_sum(x, masks):
    out = jnp.zeros_like(x)
    for m in masks:
        s = jnp.sum(jnp.where(m, x, 0.0), axis=1, keepdims=True)
        out = jnp.where(m, s, out)
    return out


def _col_to_row(col, n):
    eye = _iota((n, n), 0) == _iota((n, n), 1)
    return jnp.sum(jnp.where(eye, jnp.broadcast_to(col, (n, n)), 0.0), axis=0, keepdims=True)


def _row_to_col(row, n):
    eye = _iota((n, n), 0) == _iota((n, n), 1)
    return jnp.sum(jnp.where(eye, jnp.broadcast_to(row, (n, n)), 0.0), axis=1, keepdims=True)


def _cumsum_col(col, n):
    r = _iota((n, n), 0)
    c = _iota((n, n), 1)
    row = jnp.sum(jnp.where(r <= c, jnp.broadcast_to(col, (n, n)), 0.0), axis=0, keepdims=True)
    return row, _row_to_col(row, n)


def _ffn_kernel(x_ref, g_ref, w1_ref, w3_ref, w2_ref, o_ref):
    x = x_ref[...]
    h = _rms(x, g_ref[...]).astype(BF16)
    a = _dot(h, w1_ref[...])
    b = _dot(h, w3_ref[...])
    y = (a * jax.nn.sigmoid(a) * b).astype(BF16)
    o_ref[...] = x + 0.5 * _dot(y, w2_ref[...])


def _ffn(x, g, w1, w3, w2):
    na, d = x.shape
    f = w1.shape[1]
    tm = math.gcd(na, 256)
    return pl.pallas_call(
        _ffn_kernel,
        out_shape=SDS((na, d), F32),
        grid=(na // tm,),
        in_specs=[_rows(tm, d), _resident((1, d)), _resident((d, f)), _resident((d, f)), _resident((f, d))],
        out_specs=_rows(tm, d),
        compiler_params=_cp(("parallel",)),
        name="ffn",
    )(x, g.reshape(1, d), w1.astype(BF16), w3.astype(BF16), w2.astype(BF16))


def _rms_kernel(x_ref, g_ref, o_ref):
    o_ref[...] = _rms(x_ref[...], g_ref[...])


def _final_rms(x, g):
    na, d = x.shape
    tm = math.gcd(na, 512)
    return pl.pallas_call(
        _rms_kernel,
        out_shape=SDS((na, d), F32),
        grid=(na // tm,),
        in_specs=[_rows(tm, d), _resident((1, d))],
        out_specs=_rows(tm, d),
        compiler_params=_cp(("parallel",)),
        name="final_rms",
    )(x, g.reshape(1, d))


def _rope_slab(z, cos, sa, sb):
    return z * cos + pltpu.roll(z, LANES - HD // 2, 1) * sa + pltpu.roll(z, HD // 2, 1) * sb


def _proj_kernel(x_ref, g_ref, w_ref, cos_ref, sa_ref, sb_ref, sp_ref,
                 aqkv_ref, iq_ref, sm_ref, m_ref, r_ref, gq_ref, gate_ref):
    h = _rms(x_ref[...], g_ref[...]).astype(BF16)
    cos, sa, sb = cos_ref[...], sa_ref[...], sb_ref[...]

    def mm(a, b):
        return _dot(h, w_ref[:, a:b])

    za = mm(PC_AQKV, PC_IQ)
    for s in range(4):
        aqkv_ref[:, s * LANES:(s + 1) * LANES] = _rope_slab(za[:, s * LANES:(s + 1) * LANES], cos, sa, sb)
    aqkv_ref[:, 2 * BR_W:3 * BR_W] = za[:, 2 * BR_W:3 * BR_W]
    zi = mm(PC_IQ, PC_SM)
    for s in range(4):
        iq_ref[:, s * LANES:(s + 1) * LANES] = _rope_slab(zi[:, s * LANES:(s + 1) * LANES], cos, sa, sb)

    zs = mm(PC_SM, PC_M)
    lane = _iota(zs.shape, 1)
    gn, bias, alog = sp_ref[0:1, :], sp_ref[1:2, :], sp_ref[2:3, :]
    ms = jnp.sum(jnp.where(lane < IDX_DIM, zs * zs, 0.0), axis=1, keepdims=True) * (1.0 / IDX_DIM)
    ik = _rope_slab(zs * lax.rsqrt(ms + EPS) * gn, cos, sa, sb)
    t = zs + bias
    out = jnp.where(lane < SM_IW, ik,
          jnp.where(lane < SM_MI, zs * (IDX_HEADS ** -0.5),
          jnp.where(lane < SM_MF, t,
          jnp.where(lane < SM_GB, jax.nn.log_sigmoid(t),
          jnp.where(lane < SM_GA, jax.nn.sigmoid(zs),
          jnp.where(lane < SM_END, -jnp.exp(alog) * jax.nn.softplus(t), 0.0))))))
    sm_ref[...] = out

    m_ref[...] = mm(PC_M, PC_R)
    r_ref[...] = mm(PC_R, PC_G)
    gq_ref[...] = mm(PC_G, PC_GATE)
    gate_ref[...] = mm(PC_GATE, PC_END)


def _proj(x, g, w_pad, cos, sa, sb, sp):
    na, d = x.shape
    tm = math.gcd(na, 256)
    widths = (3 * BR_W, IDX_HEADS * IDX_DIM, LANES, 4 * BR_W, RWKV_PROJ, 4 * BR_W, N_BRANCH * d)
    return pl.pallas_call(
        _proj_kernel,
        out_shape=tuple(SDS((na, w), F32) for w in widths),
        grid=(na // tm,),
        in_specs=[_rows(tm, d), _resident((1, d)), _resident((d, PC_END)),
                  _rows(tm, LANES), _rows(tm, LANES), _rows(tm, LANES), _resident((SUBLANES, LANES))],
        out_specs=tuple(_rows(tm, w) for w in widths),
        compiler_params=_cp(("parallel",)),
        name="proj_in",
    )(x, g.reshape(1, d), w_pad, cos, sa, sb, sp)


def _merge_kernel(x_ref, oa_ref, om_ref, or_ref, og_ref, gate_ref, wb_ref, wo_ref, o_ref):
    d = x_ref.shape[1]
    mix = None
    for n, br in enumerate((oa_ref, om_ref, or_ref, og_ref)):
        up = _dot(br[...].astype(BF16), wb_ref[n])
        t = jax.nn.sigmoid(gate_ref[:, n * d:(n + 1) * d]) * up
        mix = t if mix is None else mix + t
    o_ref[...] = x_ref[...] + _dot(mix.astype(BF16), wo_ref[...])


def _merge(x, oa, om, orr, og, gate, w_branch, w_out):
    na, d = x.shape
    tm = math.gcd(na, 256)
    return pl.pallas_call(
        _merge_kernel,
        out_shape=SDS((na, d), F32),
        grid=(na // tm,),
        in_specs=[_rows(tm, d)] + [_rows(tm, BR_W)] * 4 + [_rows(tm, N_BRANCH * d),
                  _resident((N_BRANCH, BR_W, d)), _resident((d, d))],
        out_specs=_rows(tm, d),
        compiler_params=_cp(("parallel",)),
        name="merge",
    )(x, oa, om, orr, og, gate, w_branch.astype(BF16), w_out.astype(BF16))


def _idx_queries(qi, sm, tq):
    qrows = jnp.concatenate([qi[:, h * IDX_DIM:(h + 1) * IDX_DIM] for h in range(IDX_HEADS)], axis=0).astype(BF16)
    wb = [jnp.broadcast_to(sm[:, SM_IW + h:SM_IW + h + 1] * (IDX_DIM ** -0.5), (tq, LANES)) for h in range(IDX_HEADS)]
    return qrows, wb


def _head_sum(dots, wb, tq, tk):
    cols = []
    for j in range(tk // LANES):
        acc = None
        for h in range(IDX_HEADS):
            t = jnp.maximum(dots[h * tq:(h + 1) * tq, j * LANES:(j + 1) * LANES], 0.0) * wb[h]
            acc = t if acc is None else acc + t
        cols.append(acc)
    return cols[0] if len(cols) == 1 else jnp.concatenate(cols, axis=1)


def _to_key(sc):
    b = pltpu.bitcast(jnp.where(sc == 0.0, 0.0, sc), I32)
    return b ^ ((b >> 31) & I32(0x7FFFFFFF))


def _key_to_f32(key):
    return pltpu.bitcast(key ^ ((key >> 31) & I32(0x7FFFFFFF)), F32)


def _kth_largest(count, rows, n_sel, pos_bits, tkey_ref, cut_ref):
    int_min = I32(INT_MIN)

    def bit_step(it, ub):
        cand = ub | lax.shift_left(I32(1), 31 - it)
        cs = cand ^ int_min
        return jnp.where(count(lambda k, p: k >= cs) >= n_sel, cand, ub)

    ub = lax.fori_loop(0, 32, bit_step, jnp.zeros((rows, LANES), I32))
    tkey = ub ^ int_min
    n_gt = count(lambda k, p: k > tkey)
    n_ge = count(lambda k, p: k >= tkey)
    need = n_sel - n_gt
    ambiguous = (n_ge - n_gt) > need
    tkey_ref[...] = tkey
    cut_ref[...] = jnp.full((rows, LANES), 1 << pos_bits, I32)

    @pl.when(jnp.max(jnp.where(ambiguous, 1.0, 0.0)) > 0.0)
    def _():
        def pos_step(it, q):
            cand = q | lax.shift_left(I32(1), pos_bits - 1 - it)
            below = count(lambda k, p: (k == tkey) & (p < cand))
            return jnp.where(below < need, cand, q)

        q = lax.fori_loop(0, pos_bits, pos_step, jnp.zeros((rows, LANES), I32))
        cut_ref[...] = jnp.where(ambiguous, q + 1, 1 << pos_bits)


def _flash_update(s, v, m_s, l_s, acc_s, v_transposed=False):
    m_prev = m_s[...]
    m_new = jnp.maximum(m_prev, jnp.max(s, axis=1, keepdims=True))
    m_safe = jnp.where(m_new == NEG_INF, 0.0, m_new)
    alpha = jnp.exp(m_prev - m_safe)
    p = jnp.exp(s - m_safe)
    l_s[...] = alpha * l_s[...] + jnp.sum(p, axis=1, keepdims=True)
    m_s[...] = m_new
    pv = _dot_nt(p.astype(BF16), v) if v_transposed else _dot(p.astype(BF16), v)
    acc_s[...] = alpha * acc_s[...] + pv


def _stack_heads(q, t):
    masks = _head_masks(t)
    return jnp.concatenate([jnp.where(m, q * (HD ** -0.5), 0.0) for m in masks], axis=0).astype(BF16)


def _unstack_heads(acc, l, t):
    masks = _head_masks(t)
    out = jnp.zeros((t, BR_W), F32)
    for h, m in enumerate(masks):
        out = jnp.where(m, acc[h * t:(h + 1) * t] / l[h * t:(h + 1) * t], out)
    return out


def _dsa_prompt_kernel(qi_ref, smq_ref, q_ref, ki_ref, k_ref, v_ref, o_ref,
                       sc_s, cand_s, tkey_s, cut_s, need_s, neq_s, m_s, l_s, acc_s, sa_s, sb_s, da_s, db_s,
                       *, tq, tk, pf, n_sel, pos_bits):
    i = pl.program_id(0)
    q0 = i * tq
    nch = (q0 + tq - 1) // tk + 1
    qrows, wb = _idx_queries(qi_ref[...], smq_ref[...], tq)
    qpos = q0 + _iota((tq, tk), 0)

    ngrp = (nch + pf - 1) // pf
    nfill = ngrp * pf

    def dots_into(buf, c):
        k0 = pl.multiple_of(jnp.minimum(c, nfill - 1) * tk, tk)
        buf[...] = _dot_nt(qrows, ki_ref[pl.ds(k0, tk), :])

    def scores_from(buf, c):
        cc = jnp.minimum(c, nfill - 1)
        sc = _head_sum(buf, wb, tq, tk)
        sc = jnp.where(cc * tk + _iota((tq, tk), 1) <= qpos, sc, NEG_INF)
        sc_s[cc] = jnp.where(sc == 0.0, 0.0, sc)

    dots_into(da_s, 0)

    def fill2(j, carry):
        dots_into(db_s, 2 * j + 1)
        scores_from(da_s, 2 * j)
        dots_into(da_s, 2 * j + 2)
        scores_from(db_s, 2 * j + 1)
        return carry

    lax.fori_loop(0, (nfill + 1) // 2, fill2, 0)
    int_min = I32(INT_MIN)
    neg_key = _to_key(jnp.full((tq, LANES), NEG_INF, F32))
    big = 1 << pos_bits

    def count(pred):
        def body(c, acc):
            for j in range(tk // LANES):
                acc = acc + jnp.where(pred(sc_s[c, :, j * LANES:(j + 1) * LANES]), 1.0, 0.0)
            return acc

        acc = lax.fori_loop(0, nch, body, jnp.zeros((tq, LANES), F32))
        return jnp.broadcast_to(jnp.sum(acc, axis=1, keepdims=True), (tq, LANES))

    rg = 2 * SUBLANES
    for g in range(tq // rg):
        def insert(c, best, g=g):
            for j in range(tk // LANES):
                x = sc_s[c, g * rg:(g + 1) * rg, j * LANES:(j + 1) * LANES]
                nxt = []
                for b in best:
                    nxt.append(jnp.maximum(b, x))
                    x = jnp.minimum(b, x)
                best = tuple(nxt)
            return best

        best = lax.fori_loop(0, nch, insert, tuple(jnp.full((rg, LANES), NEG_INF, F32) for _ in range(TOPM)))
        for u in range(TOPM):
            cand_s[u, g * rg:(g + 1) * rg, :] = _to_key(best[u])

    def count_cand(pred):
        terms = [jnp.where(pred(cand_s[u]), 1.0, 0.0) for u in range(TOPM)]
        while len(terms) > 1:
            terms = [terms[n] + terms[n + 1] for n in range(0, len(terms) - 1, 2)] + terms[len(terms) & ~1:]
        return jnp.broadcast_to(jnp.sum(terms[0], axis=1, keepdims=True), (tq, LANES))

    def bit_step(it, ub):
        cand = ub | lax.shift_left(I32(1), 31 - it)
        cs = cand ^ int_min
        return jnp.where(count_cand(lambda k: k >= cs) >= n_sel, cand, ub)

    first_bit, ub0 = 0, jnp.zeros((tq, LANES), I32)
    if 2 * LANES >= n_sel > LANES and TOPM >= 2:
        second = _key_to_f32(cand_s[1])
        lo = _to_key(jnp.broadcast_to(jnp.min(second, axis=1, keepdims=True), (tq, LANES))) ^ int_min
        hi = _to_key(jnp.broadcast_to(jnp.max(second, axis=1, keepdims=True), (tq, LANES))) ^ int_min
        shared = lax.clz(lo ^ hi).astype(F32)
        first_bit = jnp.min(shared).astype(I32)
        keep = jnp.where(first_bit == 0, 0, lax.shift_left(I32(-1), jnp.minimum(32 - first_bit, 31)))
        keep = jnp.where(first_bit == 32, I32(-1), keep)
        ub0 = lo & keep
    tk_c = lax.fori_loop(first_bit, 32, bit_step, ub0) ^ int_min
    n_gt = count_cand(lambda k: k > tk_c)
    tkey_s[...] = tk_c
    need_s[...] = n_sel - n_gt
    neq_s[...] = count_cand(lambda k: k >= tk_c) - n_gt
    dropped = (cand_s[TOPM - 1] >= tk_c) & (nch * (tk // LANES) > TOPM)

    @pl.when(jnp.max(jnp.where(dropped, 1.0, 0.0)) > 0.0)
    def _():
        def step(it, ub):
            cand = ub | lax.shift_left(I32(1), 31 - it)
            t = _key_to_f32(jnp.maximum(cand ^ int_min, neg_key))
            return jnp.where(count(lambda s: s >= t) >= n_sel, cand, ub)

        tk_f = lax.fori_loop(0, 32, step, jnp.zeros((tq, LANES), I32)) ^ int_min
        t = _key_to_f32(tk_f)
        n_gt_f = count(lambda s: s > t)
        tkey_s[...] = tk_f
        need_s[...] = n_sel - n_gt_f
        neq_s[...] = count(lambda s: s >= t) - n_gt_f

    thr128 = _key_to_f32(tkey_s[...])
    ambiguous = neq_s[...] > need_s[...]
    cut_s[...] = jnp.full((tq, LANES), big, I32)

    @pl.when(jnp.max(jnp.where(ambiguous, 1.0, 0.0)) > 0.0)
    def _():
        thr1 = thr128[:, 0:1]
        need1 = need_s[:, 0:1]
        tri = jnp.where(_iota((tk, tk), 0) <= _iota((tk, tk), 1), 1.0, 0.0).astype(BF16)

        def body(c, carry):
            base, cutf = carry
            eq = sc_s[c] == thr1
            run = base + _dot(jnp.where(eq, 1.0, 0.0).astype(BF16), tri)
            posf = (c * tk + 1 + _iota((tq, tk), 1)).astype(F32)
            hit = jnp.where(eq & (run == need1), posf, float(big))
            return run[:, tk - 1:tk], jnp.minimum(cutf, jnp.min(hit, axis=1, keepdims=True))

        _, cutf = lax.fori_loop(0, nch, body, (jnp.zeros((tq, 1), F32), jnp.full((tq, 1), float(big), F32)))
        cut_s[...] = jnp.where(ambiguous, jnp.broadcast_to(cutf, (tq, LANES)).astype(I32), big)

    qend = q0 + _iota((tq, LANES), 0) + 1
    cut_s[...] = jnp.where(tkey_s[...] == neg_key, jnp.minimum(cut_s[...], qend), cut_s[...])

    thr = thr128[:, 0:1]
    cut = cut_s[:, 0:1]
    qbd = _stack_heads(q_ref[:, 0:BR_W], tq)
    m_s[...] = jnp.full(m_s.shape, NEG_INF, F32)
    l_s[...] = jnp.zeros(l_s.shape, F32)
    acc_s[...] = jnp.zeros(acc_s.shape, F32)

    def logits_into(buf, g):
        gc = jnp.minimum(g, ngrp - 1)
        k0 = pl.multiple_of(gc * (pf * tk), pf * tk)
        sc = jnp.concatenate([sc_s[gc * pf + u] for u in range(pf)], axis=1) if pf > 1 else sc_s[gc]
        sel = ((sc > thr) | ((sc == thr) & (k0 + _iota((tq, pf * tk), 1) < cut))) & (g < ngrp)
        bias = jnp.where(sel, 0.0, NEG_INF)
        buf[...] = _dot_nt(qbd, k_ref[pl.ds(k0, pf * tk), :]) + jnp.concatenate([bias] * HEADS, axis=0)

    def consume(buf, g):
        k0 = pl.multiple_of(jnp.minimum(g, ngrp - 1) * (pf * tk), pf * tk)
        _flash_update(buf[...], v_ref[pl.ds(k0, pf * tk), :], m_s, l_s, acc_s)

    logits_into(sa_s, 0)

    def pair(j, carry):
        logits_into(sb_s, 2 * j + 1)
        consume(sa_s, 2 * j)
        logits_into(sa_s, 2 * j + 2)
        consume(sb_s, 2 * j + 1)
        return carry

    lax.fori_loop(0, (ngrp + 1) // 2, pair, 0)
    o_ref[...] = _unstack_heads(acc_s[...], l_s[...], tq)


def _dsa_prompt(iq, sm, aqkv, lp):
    tq = min(128, lp)
    tk = min(512, lp)
    n_sel = min(TOPK_MAX, lp // 4)
    pos_bits = int(lp).bit_length()
    pf = 2 if (lp // tk) % 2 == 0 else 1
    ki = sm[:lp, SM_IK:SM_IK + IDX_DIM].astype(BF16)
    kb = aqkv[:lp, BR_W:2 * BR_W].astype(BF16)
    vb = aqkv[:lp, 2 * BR_W:3 * BR_W].astype(BF16)
    return pl.pallas_call(
        functools.partial(_dsa_prompt_kernel, tq=tq, tk=tk, pf=pf, n_sel=n_sel, pos_bits=pos_bits),
        out_shape=SDS((lp, BR_W), F32),
        grid=(lp // tq,),
        in_specs=[_rows(tq, IDX_HEADS * IDX_DIM), _rows(tq, LANES), _rows(tq, 3 * BR_W),
                  _resident((lp, IDX_DIM)), _resident((lp, BR_W)), _resident((lp, BR_W))],
        out_specs=_rows(tq, BR_W),
        scratch_shapes=[pltpu.VMEM((lp // tk, tq, tk), F32), pltpu.VMEM((TOPM, tq, LANES), I32),
                        pltpu.VMEM((tq, LANES), I32), pltpu.VMEM((tq, LANES), I32),
                        pltpu.VMEM((tq, LANES), F32), pltpu.VMEM((tq, LANES), F32),
                        pltpu.VMEM((HEADS * tq, 1), F32), pltpu.VMEM((HEADS * tq, 1), F32),
                        pltpu.VMEM((HEADS * tq, BR_W), F32),
                        pltpu.VMEM((HEADS * tq, pf * tk), F32), pltpu.VMEM((HEADS * tq, pf * tk), F32),
                        pltpu.VMEM((IDX_HEADS * tq, tk), F32), pltpu.VMEM((IDX_HEADS * tq, tk), F32)],
        compiler_params=_cp(("arbitrary",)),
        name="dsa_prompt",
    )(iq, sm, aqkv, ki, kb, vb)


def _dsa_sample_score_kernel(pt_ref, qr_ref, wb_ref, *refs, t, pg, ng, past, n_sel, pos_bits):
    pages = refs[:pg]
    knew_ref, scp_ref, scn_ref, thr_ref, cut_ref, keys_s, keyn_s, tkey_s = refs[pg:]
    j = pl.program_id(1)
    qrows = qr_ref[...]
    wb = wb_ref[...] * (IDX_DIM ** -0.5)

    def reduce_heads(dots):
        w = jnp.maximum(dots, 0.0) * wb
        s = w[0:t]
        for h in range(1, IDX_HEADS):
            s = s + w[h * t:(h + 1) * t]
        return s

    sc = jnp.concatenate([reduce_heads(_dot(qrows, p[...].astype(BF16))) for p in pages], axis=1)
    scp_ref[...] = sc
    keys_s[j] = _to_key(sc)

    @pl.when(j == ng - 1)
    def _():
        lane = _iota((t, LANES), 1)
        sn = reduce_heads(_dot_nt(qrows, knew_ref[...].astype(BF16)))
        sn = jnp.where(lane <= _iota((t, LANES), 0), sn, NEG_INF)
        scn_ref[...] = sn
        keyn_s[...] = _to_key(sn)

        def count(pred):
            def body(c, acc):
                for jj in range(pg):
                    p = c * (pg * LANES) + jj * LANES + lane
                    acc = acc + jnp.where(pred(keys_s[c, :, jj * LANES:(jj + 1) * LANES], p), 1.0, 0.0)
                return acc

            acc = lax.fori_loop(0, ng, body, jnp.zeros((t, LANES), F32))
            acc = acc + jnp.where(pred(keyn_s[...], past + lane), 1.0, 0.0)
            return jnp.broadcast_to(jnp.sum(acc, axis=1, keepdims=True), (t, LANES))

        _kth_largest(count, t, n_sel, pos_bits, tkey_s, cut_ref)
        thr_ref[...] = _key_to_f32(tkey_s[...])


def _dsa_sample_attn_kernel(pt_ref, q_ref, *refs, t, pg, ng, past):
    kpages = refs[:pg]
    vpages = refs[pg:2 * pg]
    (scp_ref, scn_ref, thr_ref, cut_ref, knew_ref, vnew_ref, o_ref, m_s, l_s, acc_s) = refs[2 * pg:]
    j = pl.program_id(1)

    @pl.when(j == 0)
    def _():
        m_s[...] = jnp.full(m_s.shape, NEG_INF, F32)
        l_s[...] = jnp.zeros(l_s.shape, F32)
        acc_s[...] = jnp.zeros(acc_s.shape, F32)

    qbd = _stack_heads(q_ref[...], t)
    thr = thr_ref[:, 0:1]
    cut = cut_ref[:, 0:1]
    w = pg * LANES
    sc = scp_ref[...]
    kidx = j * w + _iota((t, w), 1)
    bias = jnp.where((sc > thr) | ((sc == thr) & (kidx < cut)), 0.0, NEG_INF)
    kcat = jnp.concatenate([p[...].reshape(BR_W, PAGE_SIZE).astype(BF16) for p in kpages], axis=1)
    vcat = jnp.concatenate([p[...].reshape(BR_W, PAGE_SIZE).astype(BF16) for p in vpages], axis=1)
    s = _dot(qbd, kcat) + jnp.concatenate([bias] * HEADS, axis=0)
    _flash_update(s, vcat, m_s, l_s, acc_s, v_transposed=True)

    @pl.when(j == ng - 1)
    def _():
        lane = _iota((t, LANES), 1)
        scn = scn_ref[...]
        seln = ((scn > thr) | ((scn == thr) & (past + lane < cut))) & (lane <= _iota((t, LANES), 0))
        biasn = jnp.where(seln, 0.0, NEG_INF)
        sn = _dot_nt(qbd, knew_ref[...].astype(BF16)) + jnp.concatenate([biasn] * HEADS, axis=0)
        _flash_update(sn, vnew_ref[...].astype(BF16), m_s, l_s, acc_s)
        o_ref[...] = _unstack_heads(acc_s[...], l_s[...], t)


def _dsa_sample(iq_s, sm_s, aqkv_s, cache_k, cache_v, cache_idx_k, page_table, layer, db, t):
    n_pages = page_table.shape[1]
    past = n_pages * PAGE_SIZE
    n_sel = min(TOPK_MAX, (past + t) // 4)
    pos_bits = int(past + LANES).bit_length()
    pg = math.gcd(n_pages, 32)
    ng = n_pages // pg
    ckt = cache_k.transpose(0, 1, 3, 4, 2)
    cvt = cache_v.transpose(0, 1, 3, 4, 2)
    cit = cache_idx_k.transpose(0, 1, 3, 2)

    qrows = iq_s.reshape(db, t, IDX_HEADS, IDX_DIM).transpose(0, 2, 1, 3).reshape(db, IDX_HEADS * t, IDX_DIM).astype(BF16)
    wcol = sm_s[:, SM_IW:SM_IW + IDX_HEADS].reshape(db, t, IDX_HEADS).transpose(0, 2, 1).reshape(db, IDX_HEADS * t, 1)
    wbs = jnp.broadcast_to(wcol, (db, IDX_HEADS * t, LANES))
    pad = lambda a: jnp.pad(a.reshape(db, t, a.shape[-1]), ((0, 0), (0, LANES - t), (0, 0)))
    ki_new = pad(sm_s[:, SM_IK:SM_IK + IDX_DIM])
    k_new = pad(aqkv_s[:, BR_W:2 * BR_W])
    v_new = pad(aqkv_s[:, 2 * BR_W:3 * BR_W])
    q3 = aqkv_s[:, 0:BR_W].reshape(db, t, BR_W)

    def idx_page(i):
        return pl.BlockSpec((None, None, IDX_DIM, PAGE_SIZE), lambda b, j, pt: (layer, pt[b, j * pg + i], 0, 0))

    def kv_page(i):
        return pl.BlockSpec((None, None, HEADS, HD, PAGE_SIZE), lambda b, j, pt: (layer, pt[b, j * pg + i], 0, 0, 0))

    per_b = lambda r, w: pl.BlockSpec((None, r, w), lambda b, j, pt: (b, 0, 0))
    sc_spec = pl.BlockSpec((None, t, pg * LANES), lambda b, j, pt: (b, 0, j))

    scp, scn, thr, cut = pl.pallas_call(
        functools.partial(_dsa_sample_score_kernel, t=t, pg=pg, ng=ng, past=past, n_sel=n_sel, pos_bits=pos_bits),
        out_shape=(SDS((db, t, past), F32), SDS((db, t, LANES), F32), SDS((db, t, LANES), F32), SDS((db, t, LANES), I32)),
        grid_spec=pltpu.PrefetchScalarGridSpec(
            num_scalar_prefetch=1, grid=(db, ng),
            in_specs=[per_b(IDX_HEADS * t, IDX_DIM), per_b(IDX_HEADS * t, LANES)]
                     + [idx_page(i) for i in range(pg)] + [per_b(LANES, IDX_DIM)],
            out_specs=(sc_spec, per_b(t, LANES), per_b(t, LANES), per_b(t, LANES)),
            scratch_shapes=[pltpu.VMEM((ng, t, pg * LANES), I32), pltpu.VMEM((t, LANES), I32), pltpu.VMEM((t, LANES), I32)]),
        compiler_params=_cp(("arbitrary", "arbitrary")),
        name="dsa_sample_scores",
    )(page_table, qrows, wbs, *([cit] * pg), ki_new)

    out = pl.pallas_call(
        functools.partial(_dsa_sample_attn_kernel, t=t, pg=pg, ng=ng, past=past),
        out_shape=SDS((db, t, BR_W), F32),
        grid_spec=pltpu.PrefetchScalarGridSpec(
            num_scalar_prefetch=1, grid=(db, ng),
            in_specs=[per_b(t, BR_W)] + [kv_page(i) for i in range(pg)] * 2
                     + [sc_spec, per_b(t, LANES), per_b(t, LANES), per_b(t, LANES), per_b(LANES, BR_W), per_b(LANES, BR_W)],
            out_specs=per_b(t, BR_W),
            scratch_shapes=[pltpu.VMEM((HEADS * t, 1), F32), pltpu.VMEM((HEADS * t, 1), F32),
                            pltpu.VMEM((HEADS * t, BR_W), F32)]),
        compiler_params=_cp(("arbitrary", "arbitrary")),
        name="dsa_sample_attention",
    )(page_table, q3, *([ckt] * pg), *([cvt] * pg), scp, scn, thr, cut, k_new, v_new)
    return out.reshape(db * t, BR_W)


def _mlstm_chunk(q, k, v, li, lf, cst, n, m, c):
    r = _iota((c, c), 0)
    cc = _iota((c, c), 1)
    hh = range(len(q))
    qk = [_dot_nt(q[h], k[h], HIGHEST) for h in hh]
    qc = [_dot_nt(q[h], cst[h], HIGHEST) for h in hh]
    b_col, mt, ei, s = [], [], [], []
    for h in hh:
        b_row, bc_ = _cumsum_col(lf[h], c)
        log_d = jnp.where(cc <= r, bc_ - b_row + _col_to_row(li[h], c), NEG_INF)
        inter = bc_ + m[h]
        mt_ = jnp.maximum(inter, jnp.max(log_d, axis=1, keepdims=True))
        b_col.append(bc_)
        mt.append(mt_)
        ei.append(jnp.exp(inter - mt_))
        s.append(qk[h] * jnp.exp(log_d - mt_))
    sv = [_hdot(s[h], v[h]) for h in hh]
    outs, c_new, n_new, m_new, wsv = [], [], [], [], []
    for h in hh:
        den = jnp.sum(s[h], axis=1, keepdims=True) + ei[h] * jnp.sum(q[h] * n[h], axis=1, keepdims=True)
        outs.append((sv[h] + ei[h] * qc[h]) / jnp.maximum(jnp.abs(den), jnp.exp(-mt[h])))
        bc = b_col[h][c - 1:c, :]
        gs = bc - b_col[h] + li[h]
        mn = jnp.maximum(bc + m[h], jnp.max(gs, axis=0, keepdims=True))
        ws = jnp.exp(gs - mn)
        dec = jnp.exp(bc + m[h] - mn)
        wsv.append(ws * v[h])
        c_new.append(dec * cst[h])
        n_new.append(dec * n[h] + jnp.sum(ws * k[h], axis=0, keepdims=True))
        m_new.append(mn)
    c_new = [c_new[h] + _dot_tn(wsv[h], k[h], HIGHEST) for h in hh]
    return outs, c_new, n_new, m_new


def _mlstm_kernel(x_ref, sm_ref, c0_ref, n0_ref, m0_ref, lng_ref, o_ref, c1_ref, n1_ref, m1_ref,
                  c_s, n_s, m_s, *, c, cb):
    i = pl.program_id(1)

    @pl.when(i == 0)
    def _():
        c_s[...] = c0_ref[...]
        n_s[...] = n0_ref[...]
        m_s[...] = m0_ref[...]

    def chunk(ci, carry):
        r0 = pl.multiple_of(ci * c, c)
        hh = range(HEADS)
        col = lambda base, h: x_ref[pl.ds(r0, c), base + HD * h:base + HD * (h + 1)]
        hs, c_new, n_new, m_new = _mlstm_chunk(
            [col(0, h) for h in hh], [col(BR_W, h) * (HD ** -0.5) for h in hh], [col(2 * BR_W, h) for h in hh],
            [sm_ref[pl.ds(r0, c), SM_MI + h:SM_MI + h + 1] for h in hh],
            [sm_ref[pl.ds(r0, c), SM_MF + h:SM_MF + h + 1] for h in hh],
            [c_s[h] for h in hh], [n_s[h] for h in hh], [m_s[h] for h in hh], c)
        outs = []
        for h in hh:
            c_s[h] = c_new[h]
            n_s[h] = n_new[h]
            m_s[h] = m_new[h]
            mu = jnp.mean(hs[h], axis=1, keepdims=True)
            var = jnp.mean(jnp.square(hs[h] - mu), axis=1, keepdims=True)
            y = (hs[h] - mu) * lax.rsqrt(var + LN_EPS) * lng_ref[:, HD * h:HD * (h + 1)]
            outs.append(y * jax.nn.sigmoid(col(3 * BR_W, h)))
        o_ref[pl.ds(r0, c), :] = jnp.concatenate(outs, axis=1)
        return carry

    lax.fori_loop(0, cb, chunk, 0)

    @pl.when(i == pl.num_programs(1) - 1)
    def _():
        c1_ref[...] = c_s[...]
        n1_ref[...] = n_s[...]
        m1_ref[...] = m_s[...]


def _seq_blocks(l, chunk, max_rows):
    nc = l // chunk
    cb = math.gcd(nc, max(1, max_rows // chunk))
    return chunk * cb, cb, nc // cb


def _state_spec(shape):
    return pl.BlockSpec((None,) + shape, lambda b, i: (b,) + (0,) * len(shape))


def _mlstm(mslab, sm, row_off, b, l, c0, n0, m0, ln_g):
    c = math.gcd(l, MLSTM_CHUNK)
    rows, cb, nblk = _seq_blocks(l, c, 512)
    off = row_off // rows
    seq = lambda w: pl.BlockSpec((rows, w), lambda bb, i: (off + bb * nblk + i, 0))
    o, c1, n1, m1 = pl.pallas_call(
        functools.partial(_mlstm_kernel, c=c, cb=cb),
        out_shape=(SDS((b * l, BR_W), F32), SDS((b, HEADS, HD, HD), F32), SDS((b, HEADS, 1, HD), F32),
                   SDS((b, HEADS, 1, 1), F32)),
        grid=(b, nblk),
        in_specs=[seq(4 * BR_W), seq(LANES), _state_spec((HEADS, HD, HD)), _state_spec((HEADS, 1, HD)),
                  _state_spec((HEADS, 1, 1)), pl.BlockSpec((1, BR_W), lambda bb, i: (0, 0))],
        out_specs=(pl.BlockSpec((rows, BR_W), lambda bb, i: (bb * nblk + i, 0)), _state_spec((HEADS, HD, HD)),
                   _state_spec((HEADS, 1, HD)), _state_spec((HEADS, 1, 1))),
        scratch_shapes=[pltpu.VMEM((HEADS, HD, HD), F32), pltpu.VMEM((HEADS, 1, HD), F32), pltpu.VMEM((HEADS, 1, 1), F32)],
        compiler_params=_cp(("parallel", "arbitrary")),
        name="mlstm",
    )(mslab, sm, c0, n0.reshape(b, HEADS, 1, HD), m0.reshape(b, HEADS, 1, 1), ln_g.reshape(1, BR_W))
    return o, c1, n1.reshape(b, HEADS, HD), m1.reshape(b, HEADS)


def _shift_rows(x, halo, s):
    rows = x.shape[0]
    rolled = pltpu.roll(x, s, 0)
    first = jnp.where(_iota((SUBLANES, x.shape[1]), 0) < s, pltpu.roll(halo, s, 0), rolled[0:SUBLANES])
    return first if rows == SUBLANES else jnp.concatenate([first, rolled[SUBLANES:]], axis=0)


def _halo_specs(rows, w, off_rows, nt):
    hb = rows // SUBLANES
    tile = pl.BlockSpec((rows, w), lambda b, i: (off_rows // rows + b * nt + i, 0))
    halo = pl.BlockSpec((SUBLANES, w), lambda b, i: (jnp.maximum(off_rows // SUBLANES + (b * nt + i) * hb - 1, 0), 0))
    state = pl.BlockSpec((None, SUBLANES, w), lambda b, i: (b, 0, 0))
    return tile, halo, state


def _pad_state_rows(st, w):
    b, r, w0 = st.shape
    return jnp.pad(st, ((0, 0), (SUBLANES - r, 0), (0, w - w0)))


def _cumsum_rows(x):
    row = _iota(x.shape, 0)
    sh = 1
    while sh < x.shape[0]:
        x = x + jnp.where(row >= sh, pltpu.roll(x, sh, 0), 0.0)
        sh *= 2
    return x


def _unit_lower_inverse(ms, c):
    r = _iota((c, c), 0)
    cc = _iota((c, c), 1)
    fold = jnp.where((_iota((c, SUBLANES), 0) & 7) == _iota((c, SUBLANES), 1), 1.0, 0.0)
    same8 = (r >> 3) == (cc >> 3)
    m8s = [_dot3(jnp.where(same8, m, 0.0), fold) for m in ms]
    zs = [jnp.where(r == cc, 1.0, 0.0)] * len(ms)

    def pivot(z, s):
        return jnp.broadcast_to(z.reshape(c // SUBLANES, SUBLANES, c)[:, s:s + 1, :],
                                (c // SUBLANES, SUBLANES, c)).reshape(c, c)

    for s in range(SUBLANES - 1):
        zs = [z + m8[:, s:s + 1] * pivot(z, s) for z, m8 in zip(zs, m8s)]
    sh = 3
    while (1 << sh) < c:
        off = ((r >> (sh + 1)) == (cc >> (sh + 1))) & ((r >> sh) != (cc >> sh))
        ts = [_dot3(z, jnp.where(off, m, 0.0)) for z, m in zip(zs, ms)]
        zs = [z + _dot3(t, z) for z, t in zip(zs, ts)]
        sh += 1
    return zs


def _lin_scan(a_ref, y0_ref, phi_ref, psi_ref, s_s, y_s, c, cb):
    def chunk(ci, carry):
        r0 = pl.multiple_of(ci * c, c)
        sts = [s_s[h] for h in range(HEADS)]
        new = [_dot3(sts[h], phi_ref[ci, h]) for h in range(HEADS)]
        ys = [_dot3(a_ref[pl.ds(r0, c), HD * h:HD * (h + 1)], sts[h], _NT) for h in range(HEADS)]
        for h in range(HEADS):
            s_s[h] = new[h] + psi_ref[ci, h]
        y_s[pl.ds(r0, c), :] = jnp.concatenate(ys, axis=1) + y0_ref[pl.ds(r0, c), :]
        return carry

    lax.fori_loop(0, cb, chunk, 0)


def _chunk_specs(rows, cb, nblk, widths):
    row_specs = [pl.BlockSpec((rows, w), lambda bb, i: (bb * nblk + i, 0)) for w in widths]
    mat = pl.BlockSpec((cb, HEADS, HD, HD), lambda bb, i: (bb * nblk + i, 0, 0, 0))
    return row_specs, mat


def _rwkv_prep_kernel(p_ref, halo_ref, st_ref, vec_ref, w2_ref, a2_ref, g2_ref, rs_ref, post_ref):
    i = pl.program_id(1)
    p = p_ref[...]
    rows = p.shape[0]
    halo = jnp.where(i == 0, st_ref[...], halo_ref[...])
    prev = _shift_rows(p, halo, 1)
    mu = jnp.concatenate([vec_ref[0:1, :], vec_ref[1:2, :], vec_ref[2:3, :], vec_ref[3:4, :]], axis=1)
    pm = p + (prev - p) * mu
    r = pm[:, 0:BR_W]
    k = pm[:, BR_W:2 * BR_W]
    v = pm[:, 2 * BR_W:3 * BR_W]
    o = 3 * BR_W
    wd = pm[:, o:o + RWKV_W_LORA]
    ad = pm[:, o + RWKV_W_LORA:o + RWKV_W_LORA + RWKV_A_LORA]
    gd = pm[:, o + RWKV_W_LORA + RWKV_A_LORA:RWKV_PROJ]
    w0, a0, kkp, ka, rk = (vec_ref[4:5, :], vec_ref[5:6, :], vec_ref[6:7, :], vec_ref[7:8, :], vec_ref[8:9, :])
    wlog = -jax.nn.softplus(-(w0 + _hdot(jnp.tanh(wd), w2_ref[...]))) - 0.5
    log_decay = -jnp.exp(wlog)
    a = jax.nn.sigmoid(a0 + _hdot(ad, a2_ref[...]))
    g = _hdot(jax.nn.sigmoid(gd), g2_ref[...])
    masks = _head_masks(rows)
    kkr = k * kkp
    kk = kkr * lax.rsqrt(jnp.maximum(_seg_sum(kkr * kkr, masks), 1e-12))
    k2 = k * (1.0 + (a - 1.0) * ka)
    bonus = _seg_sum(r * k2 * rk, masks) * v
    for n, t in enumerate((r, log_decay, k2, v, kk, kk * a)):
        rs_ref[:, n * BR_W:(n + 1) * BR_W] = t
    post_ref[:, 0:BR_W] = bonus
    post_ref[:, BR_W:2 * BR_W] = g


def _rwkv_chunk_kernel(rs_ref, rh_ref, y0_ref, phi_ref, psi_ref, *, c, cb, gc):
    r_i = _iota((c, c), 0)
    c_i = _iota((c, c), 1)
    eye = _iota((HD, HD), 0) == _iota((HD, HD), 1)
    jobs = [(g, h) for g in range(gc) for h in range(HEADS)]
    hs = lambda x, h: x[:, HD * h:HD * (h + 1)]

    def group(gi, carry):
        r0s = [pl.multiple_of((gi * gc + g) * c, c) for g in range(gc)]
        w = []
        for r0 in r0s:
            r, lw, k, v, kk, kka = (rs_ref[pl.ds(r0, c), n * BR_W:(n + 1) * BR_W] for n in range(6))
            cum = _cumsum_rows(lw)
            pc = cum[c - 1:c, :]
            inv = jnp.exp(-cum)
            rest = jnp.exp(pc - cum)
            w.append(dict(at=-kk * jnp.exp(cum - lw),
                          bt=kka * inv, kt=k * inv,
                          rt=r * jnp.exp(cum), bd=kka * rest, kd=k * rest, v=v, epc=jnp.exp(pc)))
        big = [_dot3(jnp.concatenate([hs(w[g]["at"], h), hs(w[g]["rt"], h)], axis=0),
                     jnp.concatenate([hs(w[g]["bt"], h), hs(w[g]["kt"], h)], axis=0), _NT) for g, h in jobs]
        mab = [jnp.where(c_i < r_i, x[0:c, 0:c], 0.0) for x in big]
        mak = [jnp.where(c_i < r_i, x[0:c, c:2 * c], 0.0) for x in big]
        nn = [jnp.concatenate([jnp.where(c_i <= r_i, x[c:2 * c, 0:c], 0.0),
                               jnp.where(c_i <= r_i, x[c:2 * c, c:2 * c], 0.0)], axis=1) for x in big]
        tinv = _unit_lower_inverse(mab, c)
        mv = [_dot3(m, hs(w[g]["v"], h)) for m, (g, h) in zip(mak, jobs)]
        hg = [_dot3(t, jnp.concatenate([hs(w[g]["at"], h), x], axis=1)) for t, x, (g, h) in zip(tinv, mv, jobs)]
        out = [_dot3(n2, jnp.concatenate([x, jnp.concatenate([jnp.zeros((c, HD), F32), hs(w[g]["v"], h)], axis=1)], axis=0))
               for n2, x, (g, h) in zip(nn, hg, jobs)]
        hb = [_dot3(x, hs(w[g]["bd"], h), _TN) for x, (g, h) in zip(hg, jobs)]
        vk = [_dot3(hs(w[g]["v"], h), hs(w[g]["kd"], h), _TN) for g, h in jobs]
        for g in range(gc):
            sel = [n for n, (gg, _) in enumerate(jobs) if gg == g]
            rh_ref[pl.ds(r0s[g], c), :] = w[g]["rt"] + jnp.concatenate([out[n][:, 0:HD] for n in sel], axis=1)
            y0_ref[pl.ds(r0s[g], c), :] = jnp.concatenate([out[n][:, HD:2 * HD] for n in sel], axis=1)
            phi_ref[gi * gc + g] = jnp.stack([jnp.where(eye, jnp.broadcast_to(hs(w[g]["epc"], jobs[n][1]), (HD, HD)), 0.0)
                                              + hb[n][0:HD] for n in sel])
            psi_ref[gi * gc + g] = jnp.stack([hb[n][HD:2 * HD] + vk[n] for n in sel])
        return carry

    lax.fori_loop(0, cb // gc, group, 0)


def _rwkv_scan_kernel(rh_ref, y0_ref, phi_ref, psi_ref, post_ref, s0_ref, ln_ref, o_ref, s1_ref, s_s, y_s, *, c, cb):
    i = pl.program_id(1)

    @pl.when(i == 0)
    def _():
        s_s[...] = s0_ref[...]

    _lin_scan(rh_ref, y0_ref, phi_ref, psi_ref, s_s, y_s, c, cb)
    y = y_s[...]
    masks = _head_masks(c * cb)
    mu = _seg_sum(y, masks) * (1.0 / HD)
    var = _seg_sum(jnp.square(y - mu), masks) * (1.0 / HD)
    y = (y - mu) * lax.rsqrt(var + LN_EPS) * ln_ref[0:1, :] + ln_ref[1:2, :]
    o_ref[...] = (y + post_ref[:, 0:BR_W]) * post_ref[:, BR_W:2 * BR_W]

    @pl.when(i == pl.num_programs(1) - 1)
    def _():
        s1_ref[...] = s_s[...]


def _rwkv(rslab, row_off, b, l, sh0, s0, lp):
    rows = math.gcd(l, 256)
    nt = l // rows
    tile, halo, state = _halo_specs(rows, RWKV_PROJ, row_off, nt)
    quarter = lambda a: a.reshape(4, BR_W)
    vec = jnp.concatenate([quarter(lp["rw_mu"]), lp["rw_w0"][None], lp["rw_a0"][None], lp["rw_kk"][None],
                           lp["rw_ka"][None], lp["rw_rk"].reshape(1, BR_W), jnp.zeros((7, BR_W), F32)], axis=0)
    rs, post = pl.pallas_call(
        _rwkv_prep_kernel,
        out_shape=(SDS((b * l, 6 * BR_W), F32), SDS((b * l, 2 * BR_W), F32)),
        grid=(b, nt),
        in_specs=[tile, halo, state, pl.BlockSpec((16, BR_W), lambda bb, i: (0, 0)),
                  pl.BlockSpec((RWKV_W_LORA, BR_W), lambda bb, i: (0, 0)),
                  pl.BlockSpec((RWKV_A_LORA, BR_W), lambda bb, i: (0, 0)),
                  pl.BlockSpec((RWKV_G_LORA, BR_W), lambda bb, i: (0, 0))],
        out_specs=(pl.BlockSpec((rows, 6 * BR_W), lambda bb, i: (bb * nt + i, 0)),
                   pl.BlockSpec((rows, 2 * BR_W), lambda bb, i: (bb * nt + i, 0))),
        compiler_params=_cp(("parallel", "parallel")),
        name="rwkv_prep",
    )(rslab, rslab, _pad_state_rows(sh0[:, None, :], RWKV_PROJ), vec, lp["rw_w2"], lp["rw_a2"], lp["rw_g2"])

    c = math.gcd(l, 64)
    nc = l // c
    prow, pcb, pblk = _seq_blocks(l, c, 256)
    (rs_spec, rh_spec, y0_spec), mat = _chunk_specs(prow, pcb, pblk, (6 * BR_W, BR_W, BR_W))
    rh, y0, phi, psi = pl.pallas_call(
        functools.partial(_rwkv_chunk_kernel, c=c, cb=pcb, gc=math.gcd(pcb, 2)),
        out_shape=(SDS((b * l, BR_W), F32), SDS((b * l, BR_W), F32),
                   SDS((b * nc, HEADS, HD, HD), F32), SDS((b * nc, HEADS, HD, HD), F32)),
        grid=(b, pblk),
        in_specs=[rs_spec],
        out_specs=(rh_spec, y0_spec, mat, mat),
        compiler_params=_cp(("parallel", "parallel")),
        name="rwkv_chunk",
    )(rs)

    srow, scb, sblk = _seq_blocks(l, c, 512)
    (rh_spec, y0_spec, post_spec, o_spec), mat = _chunk_specs(srow, scb, sblk, (BR_W, BR_W, 2 * BR_W, BR_W))
    ln = jnp.stack([lp["rw_ln_g"], lp["rw_ln_b"]] + [jnp.zeros((BR_W,), F32)] * 6)
    o, s1 = pl.pallas_call(
        functools.partial(_rwkv_scan_kernel, c=c, cb=scb),
        out_shape=(SDS((b * l, BR_W), F32), SDS((b, HEADS, HD, HD), F32)),
        grid=(b, sblk),
        in_specs=[rh_spec, y0_spec, mat, mat, post_spec, _state_spec((HEADS, HD, HD)),
                  pl.BlockSpec((SUBLANES, BR_W), lambda bb, i: (0, 0))],
        out_specs=(o_spec, _state_spec((HEADS, HD, HD))),
        scratch_shapes=[pltpu.VMEM((HEADS, HD, HD), F32), pltpu.VMEM((srow, BR_W), F32)],
        compiler_params=_cp(("parallel", "arbitrary")),
        name="rwkv_scan",
    )(rh, y0, phi, psi, post, s0, ln)
    return o, s1


def _gdn_prep_kernel(x_ref, halo_ref, st_ref, taps_ref, o_ref):
    i = pl.program_id(1)
    x = x_ref[:, 0:GDN_CONV_CH]
    rows = x.shape[0]
    halo = jnp.where(i == 0, st_ref[:, 0:GDN_CONV_CH], halo_ref[:, 0:GDN_CONV_CH])
    conv = x * taps_ref[CONV_W - 1:CONV_W, :]
    for s in range(1, CONV_W):
        conv = conv + _shift_rows(x, halo, s) * taps_ref[CONV_W - 1 - s:CONV_W - s, :]
    conv = conv * jax.nn.sigmoid(conv)
    masks = _head_masks(rows)
    for n in range(3):
        t = conv[:, n * BR_W:(n + 1) * BR_W]
        if n < 2:
            t = t * lax.rsqrt(jnp.maximum(_seg_sum(t * t, masks), 1e-12))
        if n == 0:
            t = t * (HD ** -0.5)
        o_ref[:, n * BR_W:(n + 1) * BR_W] = t


def _gdn_chunk_kernel(x_ref, sm_ref, qh_ref, o0_ref, phi_ref, psi_ref, *, c, cb, gc):
    r_i = _iota((c, c), 0)
    c_i = _iota((c, c), 1)
    eye = _iota((HD, HD), 0) == _iota((HD, HD), 1)
    jobs = [(g, h) for g in range(gc) for h in range(HEADS)]

    def group(gi, carry):
        r0s = [pl.multiple_of((gi * gc + g) * c, c) for g in range(gc)]
        q, k, v, beta, dec, eg, gl, wv = [], [], [], [], [], [], [], []
        for g, h in jobs:
            r0, lo = r0s[g], HD * h
            q.append(x_ref[pl.ds(r0, c), lo:lo + HD])
            k.append(x_ref[pl.ds(r0, c), BR_W + lo:BR_W + lo + HD])
            v.append(x_ref[pl.ds(r0, c), 2 * BR_W + lo:2 * BR_W + lo + HD])
            beta.append(sm_ref[pl.ds(r0, c), SM_GB + h:SM_GB + h + 1])
            gam_row, gam_col = _cumsum_col(sm_ref[pl.ds(r0, c), SM_GA + h:SM_GA + h + 1], c)
            dec.append(jnp.exp(jnp.where(c_i <= r_i, gam_col - gam_row, NEG_INF)))
            eg.append(jnp.exp(gam_col))
            gl.append(gam_col[c - 1:c, :])
            wv.append(jnp.exp(gam_col[c - 1:c, :] - gam_col))
        n = range(len(jobs))
        big = [_dot3(jnp.concatenate([k[i], q[i]], axis=0), k[i], _NT) for i in n]
        tinv = _unit_lower_inverse([-(beta[i] * big[i][0:c] * jnp.where(c_i < r_i, dec[i], 0.0)) for i in n], c)
        x = [_dot3(tinv[i], jnp.concatenate([beta[i] * v[i], (beta[i] * eg[i]) * k[i]], axis=1)) for i in n]
        out = [_dot3(big[i][c:2 * c] * dec[i], x[i]) for i in n]
        xb = [_dot3(x[i], wv[i] * k[i], _TN) for i in n]
        for g in range(gc):
            sel = [i for i in n if jobs[i][0] == g]
            qh_ref[pl.ds(r0s[g], c), :] = jnp.concatenate([eg[i] * q[i] - out[i][:, HD:2 * HD] for i in sel], axis=1)
            o0_ref[pl.ds(r0s[g], c), :] = jnp.concatenate([out[i][:, 0:HD] for i in sel], axis=1)
            phi_ref[gi * gc + g] = jnp.stack([jnp.where(eye, jnp.broadcast_to(jnp.exp(gl[i]), (HD, HD)), 0.0)
                                              - xb[i][HD:2 * HD] for i in sel])
            psi_ref[gi * gc + g] = jnp.stack([xb[i][0:HD] for i in sel])
        return carry

    lax.fori_loop(0, cb // gc, group, 0)


def _gdn_scan_kernel(qh_ref, o0_ref, phi_ref, psi_ref, z_ref, s0_ref, gn_ref, o_ref, s1_ref, s_s, y_s, *, c, cb):
    i = pl.program_id(1)

    @pl.when(i == 0)
    def _():
        s_s[...] = s0_ref[...]

    _lin_scan(qh_ref, o0_ref, phi_ref, psi_ref, s_s, y_s, c, cb)
    o = y_s[...]
    ms = _seg_sum(o * o, _head_masks(c * cb)) * (1.0 / HD)
    z = z_ref[:, 3 * BR_W:4 * BR_W]
    o_ref[...] = o * lax.rsqrt(ms + EPS) * gn_ref[...] * (z * jax.nn.sigmoid(z))

    @pl.when(i == pl.num_programs(1) - 1)
    def _():
        s1_ref[...] = s_s[...]


def _gdn(gslab, sm, row_off, b, l, cv0, s0, lp):
    rows = math.gcd(l, 256)
    nt = l // rows
    w = 4 * BR_W
    tile, halo, state = _halo_specs(rows, w, row_off, nt)
    taps = jnp.pad(lp["gd_conv"], ((0, SUBLANES - CONV_W), (0, 0)))
    qkv = pl.pallas_call(
        _gdn_prep_kernel,
        out_shape=SDS((b * l, GDN_CONV_CH), F32),
        grid=(b, nt),
        in_specs=[tile, halo, state, pl.BlockSpec((SUBLANES, GDN_CONV_CH), lambda bb, i: (0, 0))],
        out_specs=pl.BlockSpec((rows, GDN_CONV_CH), lambda bb, i: (bb * nt + i, 0)),
        compiler_params=_cp(("parallel", "parallel")),
        name="gdn_prep",
    )(gslab, gslab, _pad_state_rows(cv0, w), taps)

    c = math.gcd(l, GDN_CHUNK)
    nc = l // c
    prow, pcb, pblk = _seq_blocks(l, c, 256)
    (x_spec, qh_spec, o0_spec), mat = _chunk_specs(prow, pcb, pblk, (GDN_CONV_CH, BR_W, BR_W))
    sm_spec = pl.BlockSpec((prow, LANES), lambda bb, i: (row_off // prow + bb * pblk + i, 0))
    qh, o0, phi, psi = pl.pallas_call(
        functools.partial(_gdn_chunk_kernel, c=c, cb=pcb, gc=math.gcd(pcb, 2)),
        out_shape=(SDS((b * l, BR_W), F32), SDS((b * l, BR_W), F32),
                   SDS((b * nc, HEADS, HD, HD), F32), SDS((b * nc, HEADS, HD, HD), F32)),
        grid=(b, pblk),
        in_specs=[x_spec, sm_spec],
        out_specs=(qh_spec, o0_spec, mat, mat),
        compiler_params=_cp(("parallel", "parallel")),
        name="gdn_chunk",
    )(qkv, sm)

    srow, scb, sblk = _seq_blocks(l, c, 512)
    (qh_spec, o0_spec, o_spec), mat = _chunk_specs(srow, scb, sblk, (BR_W, BR_W, BR_W))
    z_spec = pl.BlockSpec((srow, w), lambda bb, i: (row_off // srow + bb * sblk + i, 0))
    o, s1 = pl.pallas_call(
        functools.partial(_gdn_scan_kernel, c=c, cb=scb),
        out_shape=(SDS((b * l, BR_W), F32), SDS((b, HEADS, HD, HD), F32)),
        grid=(b, sblk),
        in_specs=[qh_spec, o0_spec, mat, mat, z_spec, _state_spec((HEADS, HD, HD)),
                  pl.BlockSpec((1, BR_W), lambda bb, i: (0, 0))],
        out_specs=(o_spec, _state_spec((HEADS, HD, HD))),
        scratch_shapes=[pltpu.VMEM((HEADS, HD, HD), F32), pltpu.VMEM((srow, BR_W), F32)],
        compiler_params=_cp(("parallel", "arbitrary")),
        name="gdn_scan",
    )(qh, o0, phi, psi, gslab, s0, jnp.tile(lp["gd_norm"], HEADS).reshape(1, BR_W))
    return o, s1


def _rope_tables(pos):
    half = HD // 2
    inv = ROPE_THETA ** (-jnp.arange(half, dtype=F32) / half)
    ang = pos.astype(F32)[:, None] * inv[None, :]
    cos, sin = jnp.cos(ang), jnp.sin(ang)
    zero = jnp.zeros_like(sin)
    tile = lambda a, b: jnp.concatenate([a, b, a, b], axis=1)
    return tile(cos, cos), tile(-sin, zero), tile(zero, sin)


def _pad_w_in(w):
    sizes = (BR_W, BR_W, BR_W, IDX_HEADS * IDX_DIM, IDX_DIM, IDX_HEADS, BR_W, BR_W, BR_W, HEADS, HEADS, BR_W,
             RWKV_PROJ, GDN_CONV_CH, HEADS, HEADS, BR_W, N_BRANCH * w.shape[0])
    cuts = np.concatenate([[0], np.cumsum(sizes)])
    col = lambda i: w[:, cuts[i]:cuts[i + 1]]
    (a_q, a_k, a_v, a_iq, a_ik, a_iw, m_q, m_k, m_v, m_i, m_f, m_o, r_p, g_qkv, g_b, g_a, g_z, gate) = (
        col(i) for i in range(len(sizes)))
    small = jnp.concatenate([a_ik, a_iw, m_i, m_f, g_b, g_a, jnp.zeros((w.shape[0], LANES - SM_END), w.dtype)], axis=1)
    return jnp.concatenate([a_q, a_k, a_v, a_iq, small, m_q, m_k, m_v, m_o, r_p, g_qkv, g_z, gate], axis=1).astype(BF16)


def _small_params(lp):
    lane_put = lambda v, at: jnp.zeros((LANES,), F32).at[at:at + v.shape[0]].set(v)
    rows = [lane_put(lp["idx_k_norm"], SM_IK),
            lane_put(lp["ml_i_bias"], SM_MI) + lane_put(lp["ml_f_bias"], SM_MF) + lane_put(lp["gd_dt_bias"], SM_GA),
            lane_put(lp["gd_A_log"], SM_GA)]
    return jnp.stack(rows + [jnp.zeros((LANES,), F32)] * (SUBLANES - len(rows)))


def kernel(x_prompt, x_sample, cache_k, cache_v, cache_idx_k, state_mlstm_C, state_mlstm_n, state_mlstm_m, state_rwkv_S, state_rwkv_shift, state_gdn_S, state_gdn_conv, page_table, ffn1_norm, ffn1_w1, ffn1_w3, ffn1_w2, mix_norm, w_in, idx_k_norm, ml_i_bias, ml_f_bias, ml_norm, rw_mu, rw_w0, rw_w2, rw_a0, rw_a2, rw_g2, rw_kk, rw_ka, rw_rk, rw_ln_g, rw_ln_b, gd_conv, gd_A_log, gd_dt_bias, gd_norm, w_branch, w_out, ffn2_norm, ffn2_w1, ffn2_w3, ffn2_w2, final_norm):
    bp, lp_, d = x_prompt.shape
    db, t, _ = x_sample.shape
    assert bp == 1 and t % SUBLANES == 0 and t <= LANES and lp_ % SUBLANES == 0
    depth = w_in.shape[0]
    past = page_table.shape[1] * PAGE_SIZE
    ns = db * t

    x = jnp.concatenate([x_prompt.reshape(lp_, d), x_sample.reshape(ns, d)], axis=0)
    pos = jnp.concatenate([jnp.arange(lp_, dtype=I32), jnp.tile(past + jnp.arange(t, dtype=I32), db)])
    cos, sa, sb = _rope_tables(pos)
    zeros = lambda *s: jnp.zeros(s, F32)

    outs_p, outs_s = [], []
    for l in range(depth):
        lp = dict(idx_k_norm=idx_k_norm[l], ml_i_bias=ml_i_bias[l], ml_f_bias=ml_f_bias[l],
                  rw_mu=rw_mu[l], rw_w0=rw_w0[l], rw_w2=rw_w2[l], rw_a0=rw_a0[l], rw_a2=rw_a2[l], rw_g2=rw_g2[l],
                  rw_kk=rw_kk[l], rw_ka=rw_ka[l], rw_rk=rw_rk[l], rw_ln_g=rw_ln_g[l], rw_ln_b=rw_ln_b[l],
                  gd_conv=gd_conv[l], gd_A_log=gd_A_log[l], gd_dt_bias=gd_dt_bias[l], gd_norm=gd_norm[l])
        x = _ffn(x, ffn1_norm[l], ffn1_w1[l], ffn1_w3[l], ffn1_w2[l])
        aqkv, iq, sm, mslab, rslab, gslab, gate = _proj(x, mix_norm[l], _pad_w_in(w_in[l]), cos, sa, sb, _small_params(lp))

        oa_p = _dsa_prompt(iq, sm, aqkv, lp_)
        oa_s = _dsa_sample(iq[lp_:], sm[lp_:], aqkv[lp_:], cache_k, cache_v, cache_idx_k, page_table, l, db, t)
        om_p, c_p, n_p, m_p = _mlstm(mslab, sm, 0, 1, lp_, zeros(1, HEADS, HD, HD), zeros(1, HEADS, HD), zeros(1, HEADS), ml_norm[l])
        om_s, c_s, n_s, m_s = _mlstm(mslab, sm, lp_, db, t, state_mlstm_C[l], state_mlstm_n[l], state_mlstm_m[l], ml_norm[l])
        or_p, rs_p = _rwkv(rslab, 0, 1, lp_, zeros(1, RWKV_PROJ), zeros(1, HEADS, HD, HD), lp)
        or_s, rs_s = _rwkv(rslab, lp_, db, t, state_rwkv_shift[l], state_rwkv_S[l], lp)
        og_p, gs_p = _gdn(gslab, sm, 0, 1, lp_, zeros(1, CONV_W - 1, GDN_CONV_CH), zeros(1, HEADS, HD, HD), lp)
        og_s, gs_s = _gdn(gslab, sm, lp_, db, t, state_gdn_conv[l], state_gdn_S[l], lp)

        cat = lambda a, b: jnp.concatenate([a, b], axis=0)
        x = _merge(x, cat(oa_p, oa_s), cat(om_p, om_s), cat(or_p, or_s), cat(og_p, og_s), gate, w_branch[l], w_out[l])
        x = _ffn(x, ffn2_norm[l], ffn2_w1[l], ffn2_w3[l], ffn2_w2[l])

        k_all = aqkv[:, BR_W:2 * BR_W]
        v_all = aqkv[:, 2 * BR_W:3 * BR_W]
        ik_all = sm[:, SM_IK:SM_IK + IDX_DIM]
        gq = gslab[:, 0:GDN_CONV_CH]
        outs_p.append((k_all[:lp_].reshape(1, lp_, HEADS, HD), v_all[:lp_].reshape(1, lp_, HEADS, HD),
                       ik_all[:lp_].reshape(1, lp_, IDX_DIM), c_p, n_p, m_p, rs_p, rslab[lp_ - 1:lp_],
                       gs_p, gq[lp_ - (CONV_W - 1):lp_].reshape(1, CONV_W - 1, GDN_CONV_CH)))
        outs_s.append((k_all[lp_:].reshape(db, t, HEADS, HD), v_all[lp_:].reshape(db, t, HEADS, HD),
                       ik_all[lp_:].reshape(db, t, IDX_DIM), c_s, n_s, m_s, rs_s,
                       rslab[lp_:].reshape(db, t, RWKV_PROJ)[:, -1],
                       gs_s, gq[lp_:].reshape(db, t, GDN_CONV_CH)[:, t - (CONV_W - 1):]))

    y = _final_rms(x, final_norm)
    st_p = [jnp.stack(z) for z in zip(*outs_p)]
    st_s = [jnp.stack(z) for z in zip(*outs_s)]
    return (y[:lp_].reshape(1, lp_, d), y[lp_:].reshape(db, t, d), *st_p, *st_s)
```

```python
import functools
import math

import jax
import jax.numpy as jnp
import numpy as np
from jax import lax
from jax.experimental import pallas as pl
from jax.experimental.pallas import tpu as pltpu

F32 = jnp.float32
BF16 = jnp.bfloat16
I32 = jnp.int32
SDS = jax.ShapeDtypeStruct

HD = 64
HEADS = 4
BR_W = HEADS * HD
N_BRANCH = 4
IDX_HEADS = 8
IDX_DIM = 64
TOPK_MAX = 256
TOPM = 12
ROPE_THETA = 10000.0
MLSTM_CHUNK = 256
GDN_CHUNK = 64
CONV_W = 4
RWKV_W_LORA = 64
RWKV_A_LORA = 64
RWKV_G_LORA = 128
RWKV_PROJ = 3 * BR_W + RWKV_W_LORA + RWKV_A_LORA + RWKV_G_LORA
GDN_CONV_CH = 3 * BR_W
EPS = 1e-6
LN_EPS = 1e-5
PAGE_SIZE = 128

LANES = 128
SUBLANES = 8
VMEM_LIMIT_BYTES = 58 * 1024 * 1024

SM_IK = 0
SM_IW = 64
SM_MI = 72
SM_MF = 76
SM_GB = 80
SM_GA = 84
SM_END = 88

PC_AQKV = 0
PC_IQ = 768
PC_SM = 1280
PC_M = 1408
PC_R = 2432
PC_G = 3456
PC_GATE = 4480
PC_END = 8576

HIGHEST = lax.Precision.HIGHEST
INT_MIN = -2147483648
NEG_INF = float("-inf")


def _cp(sem):
    return pltpu.CompilerParams(dimension_semantics=sem, vmem_limit_bytes=VMEM_LIMIT_BYTES)


def _resident(shape):
    return pl.BlockSpec(shape, lambda *_: (0,) * len(shape), pipeline_mode=pl.Buffered(1))


def _rows(tm, w, off_blocks=0):
    return pl.BlockSpec((tm, w), lambda i, *_: (i + off_blocks, 0))


def _dot(a, b):
    return jnp.dot(a, b, preferred_element_type=F32)


def _dot_nt(a, b, precision=None):
    return lax.dot_general(a, b, (((1,), (1,)), ((), ())), preferred_element_type=F32, precision=precision)


def _dot_tn(a, b, precision=None):
    return lax.dot_general(a, b, (((0,), (0,)), ((), ())), preferred_element_type=F32, precision=precision)


def _hdot(a, b):
    return jnp.dot(a, b, preferred_element_type=F32, precision=HIGHEST)


def _dot3(a, b, dims=(((1,), (0,)), ((), ()))):
    ah, bh = a.astype(BF16), b.astype(BF16)
    al, bl = (a - ah.astype(F32)).astype(BF16), (b - bh.astype(F32)).astype(BF16)
    d = lambda x, y: lax.dot_general(x, y, dims, preferred_element_type=F32)
    return d(ah, bh) + (d(ah, bl) + d(al, bh))


_NT = (((1,), (1,)), ((), ()))
_TN = (((0,), (0,)), ((), ()))


def _rms(x, g):
    return x * lax.rsqrt(jnp.mean(x * x, axis=-1, keepdims=True) + EPS) * g


def _iota(shape, dim):
    return lax.broadcasted_iota(I32, shape, dim)


def _head_masks(rows):
    lane = _iota((rows, BR_W), 1)
    return [(lane >= HD * h) & (lane < HD * (h + 1)) for h in range(HEADS)]


def _seg_sum(x, masks):
    out = jnp.zeros_like(x)
    for m in masks:
        s = jnp.sum(jnp.where(m, x, 0.0), axis=1, keepdims=True)
        out = jnp.where(m, s, out)
    return out


def _col_to_row(col, n):
    eye = _iota((n, n), 0) == _iota((n, n), 1)
    return jnp.sum(jnp.where(eye, jnp.broadcast_to(col, (n, n)), 0.0), axis=0, keepdims=True)


def _row_to_col(row, n):
    eye = _iota((n, n), 0) == _iota((n, n), 1)
    return jnp.sum(jnp.where(eye, jnp.broadcast_to(row, (n, n)), 0.0), axis=1, keepdims=True)


def _cumsum_col(col, n):
    r = _iota((n, n), 0)
    c = _iota((n, n), 1)
    row = jnp.sum(jnp.where(r <= c, jnp.broadcast_to(col, (n, n)), 0.0), axis=0, keepdims=True)
    return row, _row_to_col(row, n)


def _ffn_kernel(x_ref, g_ref, w1_ref, w3_ref, w2_ref, o_ref):
    x = x_ref[...]
    h = _rms(x, g_ref[...]).astype(BF16)
    a = _dot(h, w1_ref[...])
    b = _dot(h, w3_ref[...])
    y = (a * jax.nn.sigmoid(a) * b).astype(BF16)
    o_ref[...] = x + 0.5 * _dot(y, w2_ref[...])


def _ffn(x, g, w1, w3, w2):
    na, d = x.shape
    f = w1.shape[1]
    tm = math.gcd(na, 256)
    return pl.pallas_call(
        _ffn_kernel,
        out_shape=SDS((na, d), F32),
        grid=(na // tm,),
        in_specs=[_rows(tm, d), _resident((1, d)), _resident((d, f)), _resident((d, f)), _resident((f, d))],
        out_specs=_rows(tm, d),
        compiler_params=_cp(("parallel",)),
        name="ffn",
    )(x, g.reshape(1, d), w1.astype(BF16), w3.astype(BF16), w2.astype(BF16))


def _rms_kernel(x_ref, g_ref, o_ref):
    o_ref[...] = _rms(x_ref[...], g_ref[...])


def _final_rms(x, g):
    na, d = x.shape
    tm = math.gcd(na, 512)
    return pl.pallas_call(
        _rms_kernel,
        out_shape=SDS((na, d), F32),
        grid=(na // tm,),
        in_specs=[_rows(tm, d), _resident((1, d))],
        out_specs=_rows(tm, d),
        compiler_params=_cp(("parallel",)),
        name="final_rms",
    )(x, g.reshape(1, d))


def _rope_slab(z, cos, sa, sb):
    return z * cos + pltpu.roll(z, LANES - HD // 2, 1) * sa + pltpu.roll(z, HD // 2, 1) * sb


def _proj_kernel(x_ref, g_ref, w_ref, cos_ref, sa_ref, sb_ref, sp_ref,
                 aqkv_ref, iq_ref, sm_ref, m_ref, r_ref, gq_ref, gate_ref):
    h = _rms(x_ref[...], g_ref[...]).astype(BF16)
    cos, sa, sb = cos_ref[...], sa_ref[...], sb_ref[...]

    def mm(a, b):
        return _dot(h, w_ref[:, a:b])

    za = mm(PC_AQKV, PC_IQ)
    for s in range(4):
        aqkv_ref[:, s * LANES:(s + 1) * LANES] = _rope_slab(za[:, s * LANES:(s + 1) * LANES], cos, sa, sb)
    aqkv_ref[:, 2 * BR_W:3 * BR_W] = za[:, 2 * BR_W:3 * BR_W]
    zi = mm(PC_IQ, PC_SM)
    for s in range(4):
        iq_ref[:, s * LANES:(s + 1) * LANES] = _rope_slab(zi[:, s * LANES:(s + 1) * LANES], cos, sa, sb)

    zs = mm(PC_SM, PC_M)
    lane = _iota(zs.shape, 1)
    gn, bias, alog = sp_ref[0:1, :], sp_ref[1:2, :], sp_ref[2:3, :]
    ms = jnp.sum(jnp.where(lane < IDX_DIM, zs * zs, 0.0), axis=1, keepdims=True) * (1.0 / IDX_DIM)
    ik = _rope_slab(zs * lax.rsqrt(ms + EPS) * gn, cos, sa, sb)
    t = zs + bias
    out = jnp.where(lane < SM_IW, ik,
          jnp.where(lane < SM_MI, zs * (IDX_HEADS ** -0.5),
          jnp.where(lane < SM_MF, t,
          jnp.where(lane < SM_GB, jax.nn.log_sigmoid(t),
          jnp.where(lane < SM_GA, jax.nn.sigmoid(zs),
          jnp.where(lane < SM_END, -jnp.exp(alog) * jax.nn.softplus(t), 0.0))))))
    sm_ref[...] = out

    m_ref[...] = mm(PC_M, PC_R)
    r_ref[...] = mm(PC_R, PC_G)
    gq_ref[...] = mm(PC_G, PC_GATE)
    gate_ref[...] = mm(PC_GATE, PC_END)


def _proj(x, g, w_pad, cos, sa, sb, sp):
    na, d = x.shape
    tm = math.gcd(na, 256)
    widths = (3 * BR_W, IDX_HEADS * IDX_DIM, LANES, 4 * BR_W, RWKV_PROJ, 4 * BR_W, N_BRANCH * d)
    return pl.pallas_call(
        _proj_kernel,
        out_shape=tuple(SDS((na, w), F32) for w in widths),
        grid=(na // tm,),
        in_specs=[_rows(tm, d), _resident((1, d)), _resident((d, PC_END)),
                  _rows(tm, LANES), _rows(tm, LANES), _rows(tm, LANES), _resident((SUBLANES, LANES))],
        out_specs=tuple(_rows(tm, w) for w in widths),
        compiler_params=_cp(("parallel",)),
        name="proj_in",
    )(x, g.reshape(1, d), w_pad, cos, sa, sb, sp)


def _merge_kernel(x_ref, oa_ref, om_ref, or_ref, og_ref, gate_ref, wb_ref, wo_ref, o_ref):
    d = x_ref.shape[1]
    mix = None
    for n, br in enumerate((oa_ref, om_ref, or_ref, og_ref)):
        up = _dot(br[...].astype(BF16), wb_ref[n])
        t = jax.nn.sigmoid(gate_ref[:, n * d:(n + 1) * d]) * up
        mix = t if mix is None else mix + t
    o_ref[...] = x_ref[...] + _dot(mix.astype(BF16), wo_ref[...])


def _merge(x, oa, om, orr, og, gate, w_branch, w_out):
    na, d = x.shape
    tm = math.gcd(na, 256)
    return pl.pallas_call(
        _merge_kernel,
        out_shape=SDS((na, d), F32),
        grid=(na // tm,),
        in_specs=[_rows(tm, d)] + [_rows(tm, BR_W)] * 4 + [_rows(tm, N_BRANCH * d),
                  _resident((N_BRANCH, BR_W, d)), _resident((d, d))],
        out_specs=_rows(tm, d),
        compiler_params=_cp(("parallel",)),
        name="merge",
    )(x, oa, om, orr, og, gate, w_branch.astype(BF16), w_out.astype(BF16))


def _idx_queries(qi, sm, tq):
    qrows = jnp.concatenate([qi[:, h * IDX_DIM:(h + 1) * IDX_DIM] for h in range(IDX_HEADS)], axis=0).astype(BF16)
    wb = [jnp.broadcast_to(sm[:, SM_IW + h:SM_IW + h + 1] * (IDX_DIM ** -0.5), (tq, LANES)) for h in range(IDX_HEADS)]
    return qrows, wb


def _head_sum(dots, wb, tq, tk):
    cols = []
    for j in range(tk // LANES):
        acc = None
        for h in range(IDX_HEADS):
            t = jnp.maximum(dots[h * tq:(h + 1) * tq, j * LANES:(j + 1) * LANES], 0.0) * wb[h]
            acc = t if acc is None else acc + t
        cols.append(acc)
    return cols[0] if len(cols) == 1 else jnp.concatenate(cols, axis=1)


def _to_key(sc):
    b = pltpu.bitcast(jnp.where(sc == 0.0, 0.0, sc), I32)
    return b ^ ((b >> 31) & I32(0x7FFFFFFF))


def _key_to_f32(key):
    return pltpu.bitcast(key ^ ((key >> 31) & I32(0x7FFFFFFF)), F32)


def _kth_largest(count, rows, n_sel, pos_bits, tkey_ref, cut_ref):
    int_min = I32(INT_MIN)

    def bit_step(it, ub):
        cand = ub | lax.shift_left(I32(1), 31 - it)
        cs = cand ^ int_min
        return jnp.where(count(lambda k, p: k >= cs) >= n_sel, cand, ub)

    ub = lax.fori_loop(0, 32, bit_step, jnp.zeros((rows, LANES), I32))
    tkey = ub ^ int_min
    n_gt = count(lambda k, p: k > tkey)
    n_ge = count(lambda k, p: k >= tkey)
    need = n_sel - n_gt
    ambiguous = (n_ge - n_gt) > need
    tkey_ref[...] = tkey
    cut_ref[...] = jnp.full((rows, LANES), 1 << pos_bits, I32)

    @pl.when(jnp.max(jnp.where(ambiguous, 1.0, 0.0)) > 0.0)
    def _():
        def pos_step(it, q):
            cand = q | lax.shift_left(I32(1), pos_bits - 1 - it)
            below = count(lambda k, p: (k == tkey) & (p < cand))
            return jnp.where(below < need, cand, q)

        q = lax.fori_loop(0, pos_bits, pos_step, jnp.zeros((rows, LANES), I32))
        cut_ref[...] = jnp.where(ambiguous, q + 1, 1 << pos_bits)


def _flash_update(s, v, m_s, l_s, acc_s, v_transposed=False):
    m_prev = m_s[...]
    m_new = jnp.maximum(m_prev, jnp.max(s, axis=1, keepdims=True))
    m_safe = jnp.where(m_new == NEG_INF, 0.0, m_new)
    alpha = jnp.exp(m_prev - m_safe)
    p = jnp.exp(s - m_safe)
    l_s[...] = alpha * l_s[...] + jnp.sum(p, axis=1, keepdims=True)
    m_s[...] = m_new
    pv = _dot_nt(p.astype(BF16), v) if v_transposed else _dot(p.astype(BF16), v)
    acc_s[...] = alpha * acc_s[...] + pv


def _stack_heads(q, t):
    masks = _head_masks(t)
    return jnp.concatenate([jnp.where(m, q * (HD ** -0.5), 0.0) for m in masks], axis=0).astype(BF16)


def _unstack_heads(acc, l, t):
    masks = _head_masks(t)
    out = jnp.zeros((t, BR_W), F32)
    for h, m in enumerate(masks):
        out = jnp.where(m, acc[h * t:(h + 1) * t] / l[h * t:(h + 1) * t], out)
    return out


def _dsa_prompt_kernel(qi_ref, smq_ref, q_ref, ki_ref, k_ref, v_ref, o_ref,
                       sc_s, cand_s, tkey_s, cut_s, need_s, neq_s, m_s, l_s, acc_s, sa_s, sb_s, da_s, db_s,
                       *, tq, tk, pf, n_sel, pos_bits):
    i = pl.program_id(0)
    q0 = i * tq
    nch = (q0 + tq - 1) // tk + 1
    qrows, wb = _idx_queries(qi_ref[...], smq_ref[...], tq)
    qpos = q0 + _iota((tq, tk), 0)

    ngrp = (nch + pf - 1) // pf
    nfill = ngrp * pf

    def dots_into(buf, c):
        k0 = pl.multiple_of(jnp.minimum(c, nfill - 1) * tk, tk)
        buf[...] = _dot_nt(qrows, ki_ref[pl.ds(k0, tk), :])

    def scores_from(buf, c):
        cc = jnp.minimum(c, nfill - 1)
        sc = _head_sum(buf, wb, tq, tk)
        sc = jnp.where(cc * tk + _iota((tq, tk), 1) <= qpos, sc, NEG_INF)
        sc_s[cc] = jnp.where(sc == 0.0, 0.0, sc)

    dots_into(da_s, 0)

    def fill2(j, carry):
        dots_into(db_s, 2 * j + 1)
        scores_from(da_s, 2 * j)
        dots_into(da_s, 2 * j + 2)
        scores_from(db_s, 2 * j + 1)
        return carry

    lax.fori_loop(0, (nfill + 1) // 2, fill2, 0)
    int_min = I32(INT_MIN)
    neg_key = _to_key(jnp.full((tq, LANES), NEG_INF, F32))
    big = 1 << pos_bits

    def count(pred):
        def body(c, acc):
            for j in range(tk // LANES):
                acc = acc + jnp.where(pred(sc_s[c, :, j * LANES:(j + 1) * LANES]), 1.0, 0.0)
            return acc

        acc = lax.fori_loop(0, nch, body, jnp.zeros((tq, LANES), F32))
        return jnp.broadcast_to(jnp.sum(acc, axis=1, keepdims=True), (tq, LANES))

    rg = 2 * SUBLANES
    for g in range(tq // rg):
        def insert(c, best, g=g):
            for j in range(tk // LANES):
                x = sc_s[c, g * rg:(g + 1) * rg, j * LANES:(j + 1) * LANES]
                nxt = []
                for b in best:
                    nxt.append(jnp.maximum(b, x))
                    x = jnp.minimum(b, x)
                best = tuple(nxt)
            return best

        best = lax.fori_loop(0, nch, insert, tuple(jnp.full((rg, LANES), NEG_INF, F32) for _ in range(TOPM)))
        for u in range(TOPM):
            cand_s[u, g * rg:(g + 1) * rg, :] = _to_key(best[u])

    def count_cand(pred):
        terms = [jnp.where(pred(cand_s[u]), 1.0, 0.0) for u in range(TOPM)]
        while len(terms) > 1:
            terms = [terms[n] + terms[n + 1] for n in range(0, len(terms) - 1, 2)] + terms[len(terms) & ~1:]
        return jnp.broadcast_to(jnp.sum(terms[0], axis=1, keepdims=True), (tq, LANES))

    def bit_step(it, ub):
        cand = ub | lax.shift_left(I32(1), 31 - it)
        cs = cand ^ int_min
        return jnp.where(count_cand(lambda k: k >= cs) >= n_sel, cand, ub)

    first_bit, ub0 = 0, jnp.zeros((tq, LANES), I32)
    if 2 * LANES >= n_sel > LANES and TOPM >= 2:
        second = _key_to_f32(cand_s[1])
        lo = _to_key(jnp.broadcast_to(jnp.min(second, axis=1, keepdims=True), (tq, LANES))) ^ int_min
        hi = _to_key(jnp.broadcast_to(jnp.max(second, axis=1, keepdims=True), (tq, LANES))) ^ int_min
        shared = lax.clz(lo ^ hi).astype(F32)
        first_bit = jnp.min(shared).astype(I32)
        keep = jnp.where(first_bit == 0, 0, lax.shift_left(I32(-1), jnp.minimum(32 - first_bit, 31)))
        keep = jnp.where(first_bit == 32, I32(-1), keep)
        ub0 = lo & keep
    tk_c = lax.fori_loop(first_bit, 32, bit_step, ub0) ^ int_min
    n_gt = count_cand(lambda k: k > tk_c)
    tkey_s[...] = tk_c
    need_s[...] = n_sel - n_gt
    neq_s[...] = count_cand(lambda k: k >= tk_c) - n_gt
    dropped = (cand_s[TOPM - 1] >= tk_c) & (nch * (tk // LANES) > TOPM)

    @pl.when(jnp.max(jnp.where(dropped, 1.0, 0.0)) > 0.0)
    def _():
        def step(it, ub):
            cand = ub | lax.shift_left(I32(1), 31 - it)
            t = _key_to_f32(jnp.maximum(cand ^ int_min, neg_key))
            return jnp.where(count(lambda s: s >= t) >= n_sel, cand, ub)

        tk_f = lax.fori_loop(0, 32, step, jnp.zeros((tq, LANES), I32)) ^ int_min
        t = _key_to_f32(tk_f)
        n_gt_f = count(lambda s: s > t)
        tkey_s[...] = tk_f
        need_s[...] = n_sel - n_gt_f
        neq_s[...] = count(lambda s: s >= t) - n_gt_f

    thr128 = _key_to_f32(tkey_s[...])
    ambiguous = neq_s[...] > need_s[...]
    cut_s[...] = jnp.full((tq, LANES), big, I32)

    @pl.when(jnp.max(jnp.where(ambiguous, 1.0, 0.0)) > 0.0)
    def _():
        thr1 = thr128[:, 0:1]
        need1 = need_s[:, 0:1]
        tri = jnp.where(_iota((tk, tk), 0) <= _iota((tk, tk), 1), 1.0, 0.0).astype(BF16)

        def body(c, carry):
            base, cutf = carry
            eq = sc_s[c] == thr1
            run = base + _dot(jnp.where(eq, 1.0, 0.0).astype(BF16), tri)
            posf = (c * tk + 1 + _iota((tq, tk), 1)).astype(F32)
            hit = jnp.where(eq & (run == need1), posf, float(big))
            return run[:, tk - 1:tk], jnp.minimum(cutf, jnp.min(hit, axis=1, keepdims=True))

        _, cutf = lax.fori_loop(0, nch, body, (jnp.zeros((tq, 1), F32), jnp.full((tq, 1), float(big), F32)))
        cut_s[...] = jnp.where(ambiguous, jnp.broadcast_to(cutf, (tq, LANES)).astype(I32), big)

    qend = q0 + _iota((tq, LANES), 0) + 1
    cut_s[...] = jnp.where(tkey_s[...] == neg_key, jnp.minimum(cut_s[...], qend), cut_s[...])

    thr = thr128[:, 0:1]
    cut = cut_s[:, 0:1]
    qbd = _stack_heads(q_ref[:, 0:BR_W], tq)
    m_s[...] = jnp.full(m_s.shape, NEG_INF, F32)
    l_s[...] = jnp.zeros(l_s.shape, F32)
    acc_s[...] = jnp.zeros(acc_s.shape, F32)

    def logits_into(buf, g):
        gc = jnp.minimum(g, ngrp - 1)
        k0 = pl.multiple_of(gc * (pf * tk), pf * tk)
        sc = jnp.concatenate([sc_s[gc * pf + u] for u in range(pf)], axis=1) if pf > 1 else sc_s[gc]
        sel = ((sc > thr) | ((sc == thr) & (k0 + _iota((tq, pf * tk), 1) < cut))) & (g < ngrp)
        bias = jnp.where(sel, 0.0, NEG_INF)
        buf[...] = _dot_nt(qbd, k_ref[pl.ds(k0, pf * tk), :]) + jnp.concatenate([bias] * HEADS, axis=0)

    def consume(buf, g):
        k0 = pl.multiple_of(jnp.minimum(g, ngrp - 1) * (pf * tk), pf * tk)
        _flash_update(buf[...], v_ref[pl.ds(k0, pf * tk), :], m_s, l_s, acc_s)

    logits_into(sa_s, 0)

    def pair(j, carry):
        logits_into(sb_s, 2 * j + 1)
        consume(sa_s, 2 * j)
        logits_into(sa_s, 2 * j + 2)
        consume(sb_s, 2 * j + 1)
        return carry

    lax.fori_loop(0, (ngrp + 1) // 2, pair, 0)
    o_ref[...] = _unstack_heads(acc_s[...], l_s[...], tq)


def _dsa_prompt(iq, sm, aqkv, lp):
    tq = min(128, lp)
    tk = min(512, lp)
    n_sel = min(TOPK_MAX, lp // 4)
    pos_bits = int(lp).bit_length()
    pf = 2 if (lp // tk) % 2 == 0 else 1
    ki = sm[:lp, SM_IK:SM_IK + IDX_DIM].astype(BF16)
    kb = aqkv[:lp, BR_W:2 * BR_W].astype(BF16)
    vb = aqkv[:lp, 2 * BR_W:3 * BR_W].astype(BF16)
    return pl.pallas_call(
        functools.partial(_dsa_prompt_kernel, tq=tq, tk=tk, pf=pf, n_sel=n_sel, pos_bits=pos_bits),
        out_shape=SDS((lp, BR_W), F32),
        grid=(lp // tq,),
        in_specs=[_rows(tq, IDX_HEADS * IDX_DIM), _rows(tq, LANES), _rows(tq, 3 * BR_W),
                  _resident((lp, IDX_DIM)), _resident((lp, BR_W)), _resident((lp, BR_W))],
        out_specs=_rows(tq, BR_W),
        scratch_shapes=[pltpu.VMEM((lp // tk, tq, tk), F32), pltpu.VMEM((TOPM, tq, LANES), I32),
                        pltpu.VMEM((tq, LANES), I32), pltpu.VMEM((tq, LANES), I32),
                        pltpu.VMEM((tq, LANES), F32), pltpu.VMEM((tq, LANES), F32),
                        pltpu.VMEM((HEADS * tq, 1), F32), pltpu.VMEM((HEADS * tq, 1), F32),
                        pltpu.VMEM((HEADS * tq, BR_W), F32),
                        pltpu.VMEM((HEADS * tq, pf * tk), F32), pltpu.VMEM((HEADS * tq, pf * tk), F32),
                        pltpu.VMEM((IDX_HEADS * tq, tk), F32), pltpu.VMEM((IDX_HEADS * tq, tk), F32)],
        compiler_params=_cp(("arbitrary",)),
        name="dsa_prompt",
    )(iq, sm, aqkv, ki, kb, vb)


def _dsa_sample_score_kernel(pt_ref, qr_ref, wb_ref, *refs, t, pg, ng, past, n_sel, pos_bits):
    pages = refs[:pg]
    knew_ref, scp_ref, scn_ref, thr_ref, cut_ref, keys_s, keyn_s, tkey_s = refs[pg:]
    j = pl.program_id(1)
    qrows = qr_ref[...]
    wb = wb_ref[...] * (IDX_DIM ** -0.5)

    def reduce_heads(dots):
        w = jnp.maximum(dots, 0.0) * wb
        s = w[0:t]
        for h in range(1, IDX_HEADS):
            s = s + w[h * t:(h + 1) * t]
        return s

    sc = jnp.concatenate([reduce_heads(_dot(qrows, p[...].astype(BF16))) for p in pages], axis=1)
    scp_ref[...] = sc
    keys_s[j] = _to_key(sc)

    @pl.when(j == ng - 1)
    def _():
        lane = _iota((t, LANES), 1)
        sn = reduce_heads(_dot_nt(qrows, knew_ref[...].astype(BF16)))
        sn = jnp.where(lane <= _iota((t, LANES), 0), sn, NEG_INF)
        scn_ref[...] = sn
        keyn_s[...] = _to_key(sn)

        def count(pred):
            def body(c, acc):
                for jj in range(pg):
                    p = c * (pg * LANES) + jj * LANES + lane
                    acc = acc + jnp.where(pred(keys_s[c, :, jj * LANES:(jj + 1) * LANES], p), 1.0, 0.0)
                return acc

            acc = lax.fori_loop(0, ng, body, jnp.zeros((t, LANES), F32))
            acc = acc + jnp.where(pred(keyn_s[...], past + lane), 1.0, 0.0)
            return jnp.broadcast_to(jnp.sum(acc, axis=1, keepdims=True), (t, LANES))

        _kth_largest(count, t, n_sel, pos_bits, tkey_s, cut_ref)
        thr_ref[...] = _key_to_f32(tkey_s[...])


def _dsa_sample_attn_kernel(pt_ref, q_ref, *refs, t, pg, ng, past):
    kpages = refs[:pg]
    vpages = refs[pg:2 * pg]
    (scp_ref, scn_ref, thr_ref, cut_ref, knew_ref, vnew_ref, o_ref, m_s, l_s, acc_s) = refs[2 * pg:]
    j = pl.program_id(1)

    @pl.when(j == 0)
    def _():
        m_s[...] = jnp.full(m_s.shape, NEG_INF, F32)
        l_s[...] = jnp.zeros(l_s.shape, F32)
        acc_s[...] = jnp.zeros(acc_s.shape, F32)

    qbd = _stack_heads(q_ref[...], t)
    thr = thr_ref[:, 0:1]
    cut = cut_ref[:, 0:1]
    w = pg * LANES
    sc = scp_ref[...]
    kidx = j * w + _iota((t, w), 1)
    bias = jnp.where((sc > thr) | ((sc == thr) & (kidx < cut)), 0.0, NEG_INF)
    kcat = jnp.concatenate([p[...].reshape(BR_W, PAGE_SIZE).astype(BF16) for p in kpages], axis=1)
    vcat = jnp.concatenate([p[...].reshape(BR_W, PAGE_SIZE).astype(BF16) for p in vpages], axis=1)
    s = _dot(qbd, kcat) + jnp.concatenate([bias] * HEADS, axis=0)
    _flash_update(s, vcat, m_s, l_s, acc_s, v_transposed=True)

    @pl.when(j == ng - 1)
    def _():
        lane = _iota((t, LANES), 1)
        scn = scn_ref[...]
        seln = ((scn > thr) | ((scn == thr) & (past + lane < cut))) & (lane <= _iota((t, LANES), 0))
        biasn = jnp.where(seln, 0.0, NEG_INF)
        sn = _dot_nt(qbd, knew_ref[...].astype(BF16)) + jnp.concatenate([biasn] * HEADS, axis=0)
        _flash_update(sn, vnew_ref[...].astype(BF16), m_s, l_s, acc_s)
        o_ref[...] = _unstack_heads(acc_s[...], l_s[...], t)


def _dsa_sample(iq_s, sm_s, aqkv_s, cache_k, cache_v, cache_idx_k, page_table, layer, db, t):
    n_pages = page_table.shape[1]
    past = n_pages * PAGE_SIZE
    n_sel = min(TOPK_MAX, (past + t) // 4)
    pos_bits = int(past + LANES).bit_length()
    pg = math.gcd(n_pages, 32)
    ng = n_pages // pg
    ckt = cache_k.transpose(0, 1, 3, 4, 2)
    cvt = cache_v.transpose(0, 1, 3, 4, 2)
    cit = cache_idx_k.transpose(0, 1, 3, 2)

    qrows = iq_s.reshape(db, t, IDX_HEADS, IDX_DIM).transpose(0, 2, 1, 3).reshape(db, IDX_HEADS * t, IDX_DIM).astype(BF16)
    wcol = sm_s[:, SM_IW:SM_IW + IDX_HEADS].reshape(db, t, IDX_HEADS).transpose(0, 2, 1).reshape(db, IDX_HEADS * t, 1)
    wbs = jnp.broadcast_to(wcol, (db, IDX_HEADS * t, LANES))
    pad = lambda a: jnp.pad(a.reshape(db, t, a.shape[-1]), ((0, 0), (0, LANES - t), (0, 0)))
    ki_new = pad(sm_s[:, SM_IK:SM_IK + IDX_DIM])
    k_new = pad(aqkv_s[:, BR_W:2 * BR_W])
    v_new = pad(aqkv_s[:, 2 * BR_W:3 * BR_W])
    q3 = aqkv_s[:, 0:BR_W].reshape(db, t, BR_W)

    def idx_page(i):
        return pl.BlockSpec((None, None, IDX_DIM, PAGE_SIZE), lambda b, j, pt: (layer, pt[b, j * pg + i], 0, 0))

    def kv_page(i):
        return pl.BlockSpec((None, None, HEADS, HD, PAGE_SIZE), lambda b, j, pt: (layer, pt[b, j * pg + i], 0, 0, 0))

    per_b = lambda r, w: pl.BlockSpec((None, r, w), lambda b, j, pt: (b, 0, 0))
    sc_spec = pl.BlockSpec((None, t, pg * LANES), lambda b, j, pt: (b, 0, j))

    scp, scn, thr, cut = pl.pallas_call(
        functools.partial(_dsa_sample_score_kernel, t=t, pg=pg, ng=ng, past=past, n_sel=n_sel, pos_bits=pos_bits),
        out_shape=(SDS((db, t, past), F32), SDS((db, t, LANES), F32), SDS((db, t, LANES), F32), SDS((db, t, LANES), I32)),
        grid_spec=pltpu.PrefetchScalarGridSpec(
            num_scalar_prefetch=1, grid=(db, ng),
            in_specs=[per_b(IDX_HEADS * t, IDX_DIM), per_b(IDX_HEADS * t, LANES)]
                     + [idx_page(i) for i in range(pg)] + [per_b(LANES, IDX_DIM)],
            out_specs=(sc_spec, per_b(t, LANES), per_b(t, LANES), per_b(t, LANES)),
            scratch_shapes=[pltpu.VMEM((ng, t, pg * LANES), I32), pltpu.VMEM((t, LANES), I32), pltpu.VMEM((t, LANES), I32)]),
        compiler_params=_cp(("arbitrary", "arbitrary")),
        name="dsa_sample_scores",
    )(page_table, qrows, wbs, *([cit] * pg), ki_new)

    out = pl.pallas_call(
        functools.partial(_dsa_sample_attn_kernel, t=t, pg=pg, ng=ng, past=past),
        out_shape=SDS((db, t, BR_W), F32),
        grid_spec=pltpu.PrefetchScalarGridSpec(
            num_scalar_prefetch=1, grid=(db, ng),
            in_specs=[per_b(t, BR_W)] + [kv_page(i) for i in range(pg)] * 2
                     + [sc_spec, per_b(t, LANES), per_b(t, LANES), per_b(t, LANES), per_b(LANES, BR_W), per_b(LANES, BR_W)],
            out_specs=per_b(t, BR_W),
            scratch_shapes=[pltpu.VMEM((HEADS * t, 1), F32), pltpu.VMEM((HEADS * t, 1), F32),
                            pltpu.VMEM((HEADS * t, BR_W), F32)]),
        compiler_params=_cp(("arbitrary", "arbitrary")),
        name="dsa_sample_attention",
    )(page_table, q3, *([ckt] * pg), *([cvt] * pg), scp, scn, thr, cut, k_new, v_new)
    return out.reshape(db * t, BR_W)


def _mlstm_chunk(q, k, v, li, lf, cst, n, m, c):
    r = _iota((c, c), 0)
    cc = _iota((c, c), 1)
    hh = range(len(q))
    qk = [_dot_nt(q[h], k[h], HIGHEST) for h in hh]
    qc = [_dot_nt(q[h], cst[h], HIGHEST) for h in hh]
    b_col, mt, ei, s = [], [], [], []
    for h in hh:
        b_row, bc_ = _cumsum_col(lf[h], c)
        log_d = jnp.where(cc <= r, bc_ - b_row + _col_to_row(li[h], c), NEG_INF)
        inter = bc_ + m[h]
        mt_ = jnp.maximum(inter, jnp.max(log_d, axis=1, keepdims=True))
        b_col.append(bc_)
        mt.append(mt_)
        ei.append(jnp.exp(inter - mt_))
        s.append(qk[h] * jnp.exp(log_d - mt_))
    sv = [_hdot(s[h], v[h]) for h in hh]
    outs, c_new, n_new, m_new, wsv = [], [], [], [], []
    for h in hh:
        den = jnp.sum(s[h], axis=1, keepdims=True) + ei[h] * jnp.sum(q[h] * n[h], axis=1, keepdims=True)
        outs.append((sv[h] + ei[h] * qc[h]) / jnp.maximum(jnp.abs(den), jnp.exp(-mt[h])))
        bc = b_col[h][c - 1:c, :]
        gs = bc - b_col[h] + li[h]
        mn = jnp.maximum(bc + m[h], jnp.max(gs, axis=0, keepdims=True))
        ws = jnp.exp(gs - mn)
        dec = jnp.exp(bc + m[h] - mn)
        wsv.append(ws * v[h])
        c_new.append(dec * cst[h])
        n_new.append(dec * n[h] + jnp.sum(ws * k[h], axis=0, keepdims=True))
        m_new.append(mn)
    c_new = [c_new[h] + _dot_tn(wsv[h], k[h], HIGHEST) for h in hh]
    return outs, c_new, n_new, m_new


def _mlstm_kernel(x_ref, sm_ref, c0_ref, n0_ref, m0_ref, lng_ref, o_ref, c1_ref, n1_ref, m1_ref,
                  c_s, n_s, m_s, *, c, cb):
    i = pl.program_id(1)

    @pl.when(i == 0)
    def _():
        c_s[...] = c0_ref[...]
        n_s[...] = n0_ref[...]
        m_s[...] = m0_ref[...]

    def chunk(ci, carry):
        r0 = pl.multiple_of(ci * c, c)
        hh = range(HEADS)
        col = lambda base, h: x_ref[pl.ds(r0, c), base + HD * h:base + HD * (h + 1)]
        hs, c_new, n_new, m_new = _mlstm_chunk(
            [col(0, h) for h in hh], [col(BR_W, h) * (HD ** -0.5) for h in hh], [col(2 * BR_W, h) for h in hh],
            [sm_ref[pl.ds(r0, c), SM_MI + h:SM_MI + h + 1] for h in hh],
            [sm_ref[pl.ds(r0, c), SM_MF + h:SM_MF + h + 1] for h in hh],
            [c_s[h] for h in hh], [n_s[h] for h in hh], [m_s[h] for h in hh], c)
        outs = []
        for h in hh:
            c_s[h] = c_new[h]
            n_s[h] = n_new[h]
            m_s[h] = m_new[h]
            mu = jnp.mean(hs[h], axis=1, keepdims=True)
            var = jnp.mean(jnp.square(hs[h] - mu), axis=1, keepdims=True)
            y = (hs[h] - mu) * lax.rsqrt(var + LN_EPS) * lng_ref[:, HD * h:HD * (h + 1)]
            outs.append(y * jax.nn.sigmoid(col(3 * BR_W, h)))
        o_ref[pl.ds(r0, c), :] = jnp.concatenate(outs, axis=1)
        return carry

    lax.fori_loop(0, cb, chunk, 0)

    @pl.when(i == pl.num_programs(1) - 1)
    def _():
        c1_ref[...] = c_s[...]
        n1_ref[...] = n_s[...]
        m1_ref[...] = m_s[...]


def _seq_blocks(l, chunk, max_rows):
    nc = l // chunk
    cb = math.gcd(nc, max(1, max_rows // chunk))
    return chunk * cb, cb, nc // cb


def _state_spec(shape):
    return pl.BlockSpec((None,) + shape, lambda b, i: (b,) + (0,) * len(shape))


def _mlstm(mslab, sm, row_off, b, l, c0, n0, m0, ln_g):
    c = math.gcd(l, MLSTM_CHUNK)
    rows, cb, nblk = _seq_blocks(l, c, 512)
    off = row_off // rows
    seq = lambda w: pl.BlockSpec((rows, w), lambda bb, i: (off + bb * nblk + i, 0))
    o, c1, n1, m1 = pl.pallas_call(
        functools.partial(_mlstm_kernel, c=c, cb=cb),
        out_shape=(SDS((b * l, BR_W), F32), SDS((b, HEADS, HD, HD), F32), SDS((b, HEADS, 1, HD), F32),
                   SDS((b, HEADS, 1, 1), F32)),
        grid=(b, nblk),
        in_specs=[seq(4 * BR_W), seq(LANES), _state_spec((HEADS, HD, HD)), _state_spec((HEADS, 1, HD)),
                  _state_spec((HEADS, 1, 1)), pl.BlockSpec((1, BR_W), lambda bb, i: (0, 0))],
        out_specs=(pl.BlockSpec((rows, BR_W), lambda bb, i: (bb * nblk + i, 0)), _state_spec((HEADS, HD, HD)),
                   _state_spec((HEADS, 1, HD)), _state_spec((HEADS, 1, 1))),
        scratch_shapes=[pltpu.VMEM((HEADS, HD, HD), F32), pltpu.VMEM((HEADS, 1, HD), F32), pltpu.VMEM((HEADS, 1, 1), F32)],
        compiler_params=_cp(("parallel", "arbitrary")),
        name="mlstm",
    )(mslab, sm, c0, n0.reshape(b, HEADS, 1, HD), m0.reshape(b, HEADS, 1, 1), ln_g.reshape(1, BR_W))
    return o, c1, n1.reshape(b, HEADS, HD), m1.reshape(b, HEADS)


def _shift_rows(x, halo, s):
    rows = x.shape[0]
    rolled = pltpu.roll(x, s, 0)
    first = jnp.where(_iota((SUBLANES, x.shape[1]), 0) < s, pltpu.roll(halo, s, 0), rolled[0:SUBLANES])
    return first if rows == SUBLANES else jnp.concatenate([first, rolled[SUBLANES:]], axis=0)


def _halo_specs(rows, w, off_rows, nt):
    hb = rows // SUBLANES
    tile = pl.BlockSpec((rows, w), lambda b, i: (off_rows // rows + b * nt + i, 0))
    halo = pl.BlockSpec((SUBLANES, w), lambda b, i: (jnp.maximum(off_rows // SUBLANES + (b * nt + i) * hb - 1, 0), 0))
    state = pl.BlockSpec((None, SUBLANES, w), lambda b, i: (b, 0, 0))
    return tile, halo, state


def _pad_state_rows(st, w):
    b, r, w0 = st.shape
    return jnp.pad(st, ((0, 0), (SUBLANES - r, 0), (0, w - w0)))


def _cumsum_rows(x):
    row = _iota(x.shape, 0)
    sh = 1
    while sh < x.shape[0]:
        x = x + jnp.where(row >= sh, pltpu.roll(x, sh, 0), 0.0)
        sh *= 2
    return x


def _unit_lower_inverse(ms, c):
    r = _iota((c, c), 0)
    cc = _iota((c, c), 1)
    fold = jnp.where((_iota((c, SUBLANES), 0) & 7) == _iota((c, SUBLANES), 1), 1.0, 0.0)
    same8 = (r >> 3) == (cc >> 3)
    m8s = [_dot3(jnp.where(same8, m, 0.0), fold) for m in ms]
    zs = [jnp.where(r == cc, 1.0, 0.0)] * len(ms)

    def pivot(z, s):
        return jnp.broadcast_to(z.reshape(c // SUBLANES, SUBLANES, c)[:, s:s + 1, :],
                                (c // SUBLANES, SUBLANES, c)).reshape(c, c)

    for s in range(SUBLANES - 1):
        zs = [z + m8[:, s:s + 1] * pivot(z, s) for z, m8 in zip(zs, m8s)]
    sh = 3
    while (1 << sh) < c:
        off = ((r >> (sh + 1)) == (cc >> (sh + 1))) & ((r >> sh) != (cc >> sh))
        ts = [_dot3(z, jnp.where(off, m, 0.0)) for z, m in zip(zs, ms)]
        zs = [z + _dot3(t, z) for z, t in zip(zs, ts)]
        sh += 1
    return zs


def _lin_scan(a_ref, y0_ref, phi_ref, psi_ref, s_s, y_s, c, cb):
    def chunk(ci, carry):
        r0 = pl.multiple_of(ci * c, c)
        sts = [s_s[h] for h in range(HEADS)]
        new = [_dot3(sts[h], phi_ref[ci, h]) for h in range(HEADS)]
        ys = [_dot3(a_ref[pl.ds(r0, c), HD * h:HD * (h + 1)], sts[h], _NT) for h in range(HEADS)]
        for h in range(HEADS):
            s_s[h] = new[h] + psi_ref[ci, h]
        y_s[pl.ds(r0, c), :] = jnp.concatenate(ys, axis=1) + y0_ref[pl.ds(r0, c), :]
        return carry

    lax.fori_loop(0, cb, chunk, 0)


def _chunk_specs(rows, cb, nblk, widths):
    row_specs = [pl.BlockSpec((rows, w), lambda bb, i: (bb * nblk + i, 0)) for w in widths]
    mat = pl.BlockSpec((cb, HEADS, HD, HD), lambda bb, i: (bb * nblk + i, 0, 0, 0))
    return row_specs, mat


def _rwkv_prep_kernel(p_ref, halo_ref, st_ref, vec_ref, w2_ref, a2_ref, g2_ref, rs_ref, post_ref):
    i = pl.program_id(1)
    p = p_ref[...]
    rows = p.shape[0]
    halo = jnp.where(i == 0, st_ref[...], halo_ref[...])
    prev = _shift_rows(p, halo, 1)
    mu = jnp.concatenate([vec_ref[0:1, :], vec_ref[1:2, :], vec_ref[2:3, :], vec_ref[3:4, :]], axis=1)
    pm = p + (prev - p) * mu
    r = pm[:, 0:BR_W]
    k = pm[:, BR_W:2 * BR_W]
    v = pm[:, 2 * BR_W:3 * BR_W]
    o = 3 * BR_W
    wd = pm[:, o:o + RWKV_W_LORA]
    ad = pm[:, o + RWKV_W_LORA:o + RWKV_W_LORA + RWKV_A_LORA]
    gd = pm[:, o + RWKV_W_LORA + RWKV_A_LORA:RWKV_PROJ]
    w0, a0, kkp, ka, rk = (vec_ref[4:5, :], vec_ref[5:6, :], vec_ref[6:7, :], vec_ref[7:8, :], vec_ref[8:9, :])
    wlog = -jax.nn.softplus(-(w0 + _hdot(jnp.tanh(wd), w2_ref[...]))) - 0.5
    log_decay = -jnp.exp(wlog)
    a = jax.nn.sigmoid(a0 + _hdot(ad, a2_ref[...]))
    g = _hdot(jax.nn.sigmoid(gd), g2_ref[...])
    masks = _head_masks(rows)
    kkr = k * kkp
    kk = kkr * lax.rsqrt(jnp.maximum(_seg_sum(kkr * kkr, masks), 1e-12))
    k2 = k * (1.0 + (a - 1.0) * ka)
    bonus = _seg_sum(r * k2 * rk, masks) * v
    for n, t in enumerate((r, log_decay, k2, v, kk, kk * a)):
        rs_ref[:, n * BR_W:(n + 1) * BR_W] = t
    post_ref[:, 0:BR_W] = bonus
    post_ref[:, BR_W:2 * BR_W] = g


def _rwkv_chunk_kernel(rs_ref, rh_ref, y0_ref, phi_ref, psi_ref, *, c, cb, gc):
    r_i = _iota((c, c), 0)
    c_i = _iota((c, c), 1)
    eye = _iota((HD, HD), 0) == _iota((HD, HD), 1)
    jobs = [(g, h) for g in range(gc) for h in range(HEADS)]
    hs = lambda x, h: x[:, HD * h:HD * (h + 1)]

    def group(gi, carry):
        r0s = [pl.multiple_of((gi * gc + g) * c, c) for g in range(gc)]
        w = []
        for r0 in r0s:
            r, lw, k, v, kk, kka = (rs_ref[pl.ds(r0, c), n * BR_W:(n + 1) * BR_W] for n in range(6))
            cum = _cumsum_rows(lw)
            pc = cum[c - 1:c, :]
            inv = jnp.exp(-cum)
            rest = jnp.exp(pc - cum)
            w.append(dict(at=-kk * jnp.exp(cum - lw),
                          bt=kka * inv, kt=k * inv,
                          rt=r * jnp.exp(cum), bd=kka * rest, kd=k * rest, v=v, epc=jnp.exp(pc)))
        big = [_dot3(jnp.concatenate([hs(w[g]["at"], h), hs(w[g]["rt"], h)], axis=0),
                     jnp.concatenate([hs(w[g]["bt"], h), hs(w[g]["kt"], h)], axis=0), _NT) for g, h in jobs]
        mab = [jnp.where(c_i < r_i, x[0:c, 0:c], 0.0) for x in big]
        mak = [jnp.where(c_i < r_i, x[0:c, c:2 * c], 0.0) for x in big]
        nn = [jnp.concatenate([jnp.where(c_i <= r_i, x[c:2 * c, 0:c], 0.0),
                               jnp.where(c_i <= r_i, x[c:2 * c, c:2 * c], 0.0)], axis=1) for x in big]
        tinv = _unit_lower_inverse(mab, c)
        mv = [_dot3(m, hs(w[g]["v"], h)) for m, (g, h) in zip(mak, jobs)]
        hg = [_dot3(t, jnp.concatenate([hs(w[g]["at"], h), x], axis=1)) for t, x, (g, h) in zip(tinv, mv, jobs)]
        out = [_dot3(n2, jnp.concatenate([x, jnp.concatenate([jnp.zeros((c, HD), F32), hs(w[g]["v"], h)], axis=1)], axis=0))
               for n2, x, (g, h) in zip(nn, hg, jobs)]
        hb = [_dot3(x, hs(w[g]["bd"], h), _TN) for x, (g, h) in zip(hg, jobs)]
        vk = [_dot3(hs(w[g]["v"], h), hs(w[g]["kd"], h), _TN) for g, h in jobs]
        for g in range(gc):
            sel = [n for n, (gg, _) in enumerate(jobs) if gg == g]
            rh_ref[pl.ds(r0s[g], c), :] = w[g]["rt"] + jnp.concatenate([out[n][:, 0:HD] for n in sel], axis=1)
            y0_ref[pl.ds(r0s[g], c), :] = jnp.concatenate([out[n][:, HD:2 * HD] for n in sel], axis=1)
            phi_ref[gi * gc + g] = jnp.stack([jnp.where(eye, jnp.broadcast_to(hs(w[g]["epc"], jobs[n][1]), (HD, HD)), 0.0)
                                              + hb[n][0:HD] for n in sel])
            psi_ref[gi * gc + g] = jnp.stack([hb[n][HD:2 * HD] + vk[n] for n in sel])
        return carry

    lax.fori_loop(0, cb // gc, group, 0)


def _rwkv_scan_kernel(rh_ref, y0_ref, phi_ref, psi_ref, post_ref, s0_ref, ln_ref, o_ref, s1_ref, s_s, y_s, *, c, cb):
    i = pl.program_id(1)

    @pl.when(i == 0)
    def _():
        s_s[...] = s0_ref[...]

    _lin_scan(rh_ref, y0_ref, phi_ref, psi_ref, s_s, y_s, c, cb)
    y = y_s[...]
    masks = _head_masks(c * cb)
    mu = _seg_sum(y, masks) * (1.0 / HD)
    var = _seg_sum(jnp.square(y - mu), masks) * (1.0 / HD)
    y = (y - mu) * lax.rsqrt(var + LN_EPS) * ln_ref[0:1, :] + ln_ref[1:2, :]
    o_ref[...] = (y + post_ref[:, 0:BR_W]) * post_ref[:, BR_W:2 * BR_W]

    @pl.when(i == pl.num_programs(1) - 1)
    def _():
        s1_ref[...] = s_s[...]


def _rwkv(rslab, row_off, b, l, sh0, s0, lp):
    rows = math.gcd(l, 256)
    nt = l // rows
    tile, halo, state = _halo_specs(rows, RWKV_PROJ, row_off, nt)
    quarter = lambda a: a.reshape(4, BR_W)
    vec = jnp.concatenate([quarter(lp["rw_mu"]), lp["rw_w0"][None], lp["rw_a0"][None], lp["rw_kk"][None],
                           lp["rw_ka"][None], lp["rw_rk"].reshape(1, BR_W), jnp.zeros((7, BR_W), F32)], axis=0)
    rs, post = pl.pallas_call(
        _rwkv_prep_kernel,
        out_shape=(SDS((b * l, 6 * BR_W), F32), SDS((b * l, 2 * BR_W), F32)),
        grid=(b, nt),
        in_specs=[tile, halo, state, pl.BlockSpec((16, BR_W), lambda bb, i: (0, 0)),
                  pl.BlockSpec((RWKV_W_LORA, BR_W), lambda bb, i: (0, 0)),
                  pl.BlockSpec((RWKV_A_LORA, BR_W), lambda bb, i: (0, 0)),
                  pl.BlockSpec((RWKV_G_LORA, BR_W), lambda bb, i: (0, 0))],
        out_specs=(pl.BlockSpec((rows, 6 * BR_W), lambda bb, i: (bb * nt + i, 0)),
                   pl.BlockSpec((rows, 2 * BR_W), lambda bb, i: (bb * nt + i, 0))),
        compiler_params=_cp(("parallel", "parallel")),
        name="rwkv_prep",
    )(rslab, rslab, _pad_state_rows(sh0[:, None, :], RWKV_PROJ), vec, lp["rw_w2"], lp["rw_a2"], lp["rw_g2"])

    c = math.gcd(l, 64)
    nc = l // c
    prow, pcb, pblk = _seq_blocks(l, c, 256)
    (rs_spec, rh_spec, y0_spec), mat = _chunk_specs(prow, pcb, pblk, (6 * BR_W, BR_W, BR_W))
    rh, y0, phi, psi = pl.pallas_call(
        functools.partial(_rwkv_chunk_kernel, c=c, cb=pcb, gc=math.gcd(pcb, 4)),
        out_shape=(SDS((b * l, BR_W), F32), SDS((b * l, BR_W), F32),
                   SDS((b * nc, HEADS, HD, HD), F32), SDS((b * nc, HEADS, HD, HD), F32)),
        grid=(b, pblk),
        in_specs=[rs_spec],
        out_specs=(rh_spec, y0_spec, mat, mat),
        compiler_params=_cp(("parallel", "parallel")),
        name="rwkv_chunk",
    )(rs)

    srow, scb, sblk = _seq_blocks(l, c, 512)
    (rh_spec, y0_spec, post_spec, o_spec), mat = _chunk_specs(srow, scb, sblk, (BR_W, BR_W, 2 * BR_W, BR_W))
    ln = jnp.stack([lp["rw_ln_g"], lp["rw_ln_b"]] + [jnp.zeros((BR_W,), F32)] * 6)
    o, s1 = pl.pallas_call(
        functools.partial(_rwkv_scan_kernel, c=c, cb=scb),
        out_shape=(SDS((b * l, BR_W), F32), SDS((b, HEADS, HD, HD), F32)),
        grid=(b, sblk),
        in_specs=[rh_spec, y0_spec, mat, mat, post_spec, _state_spec((HEADS, HD, HD)),
                  pl.BlockSpec((SUBLANES, BR_W), lambda bb, i: (0, 0))],
        out_specs=(o_spec, _state_spec((HEADS, HD, HD))),
        scratch_shapes=[pltpu.VMEM((HEADS, HD, HD), F32), pltpu.VMEM((srow, BR_W), F32)],
        compiler_params=_cp(("parallel", "arbitrary")),
        name="rwkv_scan",
    )(rh, y0, phi, psi, post, s0, ln)
    return o, s1


def _gdn_prep_kernel(x_ref, halo_ref, st_ref, taps_ref, o_ref):
    i = pl.program_id(1)
    x = x_ref[:, 0:GDN_CONV_CH]
    rows = x.shape[0]
    halo = jnp.where(i == 0, st_ref[:, 0:GDN_CONV_CH], halo_ref[:, 0:GDN_CONV_CH])
    conv = x * taps_ref[CONV_W - 1:CONV_W, :]
    for s in range(1, CONV_W):
        conv = conv + _shift_rows(x, halo, s) * taps_ref[CONV_W - 1 - s:CONV_W - s, :]
    conv = conv * jax.nn.sigmoid(conv)
    masks = _head_masks(rows)
    for n in range(3):
        t = conv[:, n * BR_W:(n + 1) * BR_W]
        if n < 2:
            t = t * lax.rsqrt(jnp.maximum(_seg_sum(t * t, masks), 1e-12))
        if n == 0:
            t = t * (HD ** -0.5)
        o_ref[:, n * BR_W:(n + 1) * BR_W] = t


def _gdn_chunk_kernel(x_ref, sm_ref, qh_ref, o0_ref, phi_ref, psi_ref, *, c, cb, gc):
    r_i = _iota((c, c), 0)
    c_i = _iota((c, c), 1)
    eye = _iota((HD, HD), 0) == _iota((HD, HD), 1)
    jobs = [(g, h) for g in range(gc) for h in range(HEADS)]

    def group(gi, carry):
        r0s = [pl.multiple_of((gi * gc + g) * c, c) for g in range(gc)]
        q, k, v, beta, dec, eg, gl, wv = [], [], [], [], [], [], [], []
        for g, h in jobs:
            r0, lo = r0s[g], HD * h
            q.append(x_ref[pl.ds(r0, c), lo:lo + HD])
            k.append(x_ref[pl.ds(r0, c), BR_W + lo:BR_W + lo + HD])
            v.append(x_ref[pl.ds(r0, c), 2 * BR_W + lo:2 * BR_W + lo + HD])
            beta.append(sm_ref[pl.ds(r0, c), SM_GB + h:SM_GB + h + 1])
            gam_row, gam_col = _cumsum_col(sm_ref[pl.ds(r0, c), SM_GA + h:SM_GA + h + 1], c)
            dec.append(jnp.exp(jnp.where(c_i <= r_i, gam_col - gam_row, NEG_INF)))
            eg.append(jnp.exp(gam_col))
            gl.append(gam_col[c - 1:c, :])
            wv.append(jnp.exp(gam_col[c - 1:c, :] - gam_col))
        n = range(len(jobs))
        big = [_dot3(jnp.concatenate([k[i], q[i]], axis=0), k[i], _NT) for i in n]
        tinv = _unit_lower_inverse([-(beta[i] * big[i][0:c] * jnp.where(c_i < r_i, dec[i], 0.0)) for i in n], c)
        x = [_dot3(tinv[i], jnp.concatenate([beta[i] * v[i], (beta[i] * eg[i]) * k[i]], axis=1)) for i in n]
        out = [_dot3(big[i][c:2 * c] * dec[i], x[i]) for i in n]
        xb = [_dot3(x[i], wv[i] * k[i], _TN) for i in n]
        for g in range(gc):
            sel = [i for i in n if jobs[i][0] == g]
            qh_ref[pl.ds(r0s[g], c), :] = jnp.concatenate([eg[i] * q[i] - out[i][:, HD:2 * HD] for i in sel], axis=1)
            o0_ref[pl.ds(r0s[g], c), :] = jnp.concatenate([out[i][:, 0:HD] for i in sel], axis=1)
            phi_ref[gi * gc + g] = jnp.stack([jnp.where(eye, jnp.broadcast_to(jnp.exp(gl[i]), (HD, HD)), 0.0)
                                              - xb[i][HD:2 * HD] for i in sel])
            psi_ref[gi * gc + g] = jnp.stack([xb[i][0:HD] for i in sel])
        return carry

    lax.fori_loop(0, cb // gc, group, 0)


def _gdn_scan_kernel(qh_ref, o0_ref, phi_ref, psi_ref, z_ref, s0_ref, gn_ref, o_ref, s1_ref, s_s, y_s, *, c, cb):
    i = pl.program_id(1)

    @pl.when(i == 0)
    def _():
        s_s[...] = s0_ref[...]

    _lin_scan(qh_ref, o0_ref, phi_ref, psi_ref, s_s, y_s, c, cb)
    o = y_s[...]
    ms = _seg_sum(o * o, _head_masks(c * cb)) * (1.0 / HD)
    z = z_ref[:, 3 * BR_W:4 * BR_W]
    o_ref[...] = o * lax.rsqrt(ms + EPS) * gn_ref[...] * (z * jax.nn.sigmoid(z))

    @pl.when(i == pl.num_programs(1) - 1)
    def _():
        s1_ref[...] = s_s[...]


def _gdn(gslab, sm, row_off, b, l, cv0, s0, lp):
    rows = math.gcd(l, 256)
    nt = l // rows
    w = 4 * BR_W
    tile, halo, state = _halo_specs(rows, w, row_off, nt)
    taps = jnp.pad(lp["gd_conv"], ((0, SUBLANES - CONV_W), (0, 0)))
    qkv = pl.pallas_call(
        _gdn_prep_kernel,
        out_shape=SDS((b * l, GDN_CONV_CH), F32),
        grid=(b, nt),
        in_specs=[tile, halo, state, pl.BlockSpec((SUBLANES, GDN_CONV_CH), lambda bb, i: (0, 0))],
        out_specs=pl.BlockSpec((rows, GDN_CONV_CH), lambda bb, i: (bb * nt + i, 0)),
        compiler_params=_cp(("parallel", "parallel")),
        name="gdn_prep",
    )(gslab, gslab, _pad_state_rows(cv0, w), taps)

    c = math.gcd(l, GDN_CHUNK)
    nc = l // c
    prow, pcb, pblk = _seq_blocks(l, c, 256)
    (x_spec, qh_spec, o0_spec), mat = _chunk_specs(prow, pcb, pblk, (GDN_CONV_CH, BR_W, BR_W))
    sm_spec = pl.BlockSpec((prow, LANES), lambda bb, i: (row_off // prow + bb * pblk + i, 0))
    qh, o0, phi, psi = pl.pallas_call(
        functools.partial(_gdn_chunk_kernel, c=c, cb=pcb, gc=math.gcd(pcb, 2)),
        out_shape=(SDS((b * l, BR_W), F32), SDS((b * l, BR_W), F32),
                   SDS((b * nc, HEADS, HD, HD), F32), SDS((b * nc, HEADS, HD, HD), F32)),
        grid=(b, pblk),
        in_specs=[x_spec, sm_spec],
        out_specs=(qh_spec, o0_spec, mat, mat),
        compiler_params=_cp(("parallel", "parallel")),
        name="gdn_chunk",
    )(qkv, sm)

    srow, scb, sblk = _seq_blocks(l, c, 512)
    (qh_spec, o0_spec, o_spec), mat = _chunk_specs(srow, scb, sblk, (BR_W, BR_W, BR_W))
    z_spec = pl.BlockSpec((srow, w), lambda bb, i: (row_off // srow + bb * sblk + i, 0))
    o, s1 = pl.pallas_call(
        functools.partial(_gdn_scan_kernel, c=c, cb=scb),
        out_shape=(SDS((b * l, BR_W), F32), SDS((b, HEADS, HD, HD), F32)),
        grid=(b, sblk),
        in_specs=[qh_spec, o0_spec, mat, mat, z_spec, _state_spec((HEADS, HD, HD)),
                  pl.BlockSpec((1, BR_W), lambda bb, i: (0, 0))],
        out_specs=(o_spec, _state_spec((HEADS, HD, HD))),
        scratch_shapes=[pltpu.VMEM((HEADS, HD, HD), F32), pltpu.VMEM((srow, BR_W), F32)],
        compiler_params=_cp(("parallel", "arbitrary")),
        name="gdn_scan",
    )(qh, o0, phi, psi, gslab, s0, jnp.tile(lp["gd_norm"], HEADS).reshape(1, BR_W))
    return o, s1


def _rope_tables(pos):
    half = HD // 2
    inv = ROPE_THETA ** (-jnp.arange(half, dtype=F32) / half)
    ang = pos.astype(F32)[:, None] * inv[None, :]
    cos, sin = jnp.cos(ang), jnp.sin(ang)
    zero = jnp.zeros_like(sin)
    tile = lambda a, b: jnp.concatenate([a, b, a, b], axis=1)
    return tile(cos, cos), tile(-sin, zero), tile(zero, sin)


def _pad_w_in(w):
    sizes = (BR_W, BR_W, BR_W, IDX_HEADS * IDX_DIM, IDX_DIM, IDX_HEADS, BR_W, BR_W, BR_W, HEADS, HEADS, BR_W,
             RWKV_PROJ, GDN_CONV_CH, HEADS, HEADS, BR_W, N_BRANCH * w.shape[0])
    cuts = np.concatenate([[0], np.cumsum(sizes)])
    col = lambda i: w[:, cuts[i]:cuts[i + 1]]
    (a_q, a_k, a_v, a_iq, a_ik, a_iw, m_q, m_k, m_v, m_i, m_f, m_o, r_p, g_qkv, g_b, g_a, g_z, gate) = (
        col(i) for i in range(len(sizes)))
    small = jnp.concatenate([a_ik, a_iw, m_i, m_f, g_b, g_a, jnp.zeros((w.shape[0], LANES - SM_END), w.dtype)], axis=1)
    return jnp.concatenate([a_q, a_k, a_v, a_iq, small, m_q, m_k, m_v, m_o, r_p, g_qkv, g_z, gate], axis=1).astype(BF16)


def _small_params(lp):
    lane_put = lambda v, at: jnp.zeros((LANES,), F32).at[at:at + v.shape[0]].set(v)
    rows = [lane_put(lp["idx_k_norm"], SM_IK),
            lane_put(lp["ml_i_bias"], SM_MI) + lane_put(lp["ml_f_bias"], SM_MF) + lane_put(lp["gd_dt_bias"], SM_GA),
            lane_put(lp["gd_A_log"], SM_GA)]
    return jnp.stack(rows + [jnp.zeros((LANES,), F32)] * (SUBLANES - len(rows)))


def kernel(x_prompt, x_sample, cache_k, cache_v, cache_idx_k, state_mlstm_C, state_mlstm_n, state_mlstm_m, state_rwkv_S, state_rwkv_shift, state_gdn_S, state_gdn_conv, page_table, ffn1_norm, ffn1_w1, ffn1_w3, ffn1_w2, mix_norm, w_in, idx_k_norm, ml_i_bias, ml_f_bias, ml_norm, rw_mu, rw_w0, rw_w2, rw_a0, rw_a2, rw_g2, rw_kk, rw_ka, rw_rk, rw_ln_g, rw_ln_b, gd_conv, gd_A_log, gd_dt_bias, gd_norm, w_branch, w_out, ffn2_norm, ffn2_w1, ffn2_w3, ffn2_w2, final_norm):
    bp, lp_, d = x_prompt.shape
    db, t, _ = x_sample.shape
    assert bp == 1 and t % SUBLANES == 0 and t <= LANES and lp_ % SUBLANES == 0
    depth = w_in.shape[0]
    past = page_table.shape[1] * PAGE_SIZE
    ns = db * t

    x = jnp.concatenate([x_prompt.reshape(lp_, d), x_sample.reshape(ns, d)], axis=0)
    pos = jnp.concatenate([jnp.arange(lp_, dtype=I32), jnp.tile(past + jnp.arange(t, dtype=I32), db)])
    cos, sa, sb = _rope_tables(pos)
    zeros = lambda *s: jnp.zeros(s, F32)

    outs_p, outs_s = [], []
    for l in range(depth):
        lp = dict(idx_k_norm=idx_k_norm[l], ml_i_bias=ml_i_bias[l], ml_f_bias=ml_f_bias[l],
                  rw_mu=rw_mu[l], rw_w0=rw_w0[l], rw_w2=rw_w2[l], rw_a0=rw_a0[l], rw_a2=rw_a2[l], rw_g2=rw_g2[l],
                  rw_kk=rw_kk[l], rw_ka=rw_ka[l], rw_rk=rw_rk[l], rw_ln_g=rw_ln_g[l], rw_ln_b=rw_ln_b[l],
                  gd_conv=gd_conv[l], gd_A_log=gd_A_log[l], gd_dt_bias=gd_dt_bias[l], gd_norm=gd_norm[l])
        x = _ffn(x, ffn1_norm[l], ffn1_w1[l], ffn1_w3[l], ffn1_w2[l])
        aqkv, iq, sm, mslab, rslab, gslab, gate = _proj(x, mix_norm[l], _pad_w_in(w_in[l]), cos, sa, sb, _small_params(lp))

        oa_p = _dsa_prompt(iq, sm, aqkv, lp_)
        oa_s = _dsa_sample(iq[lp_:], sm[lp_:], aqkv[lp_:], cache_k, cache_v, cache_idx_k, page_table, l, db, t)
        om_p, c_p, n_p, m_p = _mlstm(mslab, sm, 0, 1, lp_, zeros(1, HEADS, HD, HD), zeros(1, HEADS, HD), zeros(1, HEADS), ml_norm[l])
        om_s, c_s, n_s, m_s = _mlstm(mslab, sm, lp_, db, t, state_mlstm_C[l], state_mlstm_n[l], state_mlstm_m[l], ml_norm[l])
        or_p, rs_p = _rwkv(rslab, 0, 1, lp_, zeros(1, RWKV_PROJ), zeros(1, HEADS, HD, HD), lp)
        or_s, rs_s = _rwkv(rslab, lp_, db, t, state_rwkv_shift[l], state_rwkv_S[l], lp)
        og_p, gs_p = _gdn(gslab, sm, 0, 1, lp_, zeros(1, CONV_W - 1, GDN_CONV_CH), zeros(1, HEADS, HD, HD), lp)
        og_s, gs_s = _gdn(gslab, sm, lp_, db, t, state_gdn_conv[l], state_gdn_S[l], lp)

        cat = lambda a, b: jnp.concatenate([a, b], axis=0)
        x = _merge(x, cat(oa_p, oa_s), cat(om_p, om_s), cat(or_p, or_s), cat(og_p, og_s), gate, w_branch[l], w_out[l])
        x = _ffn(x, ffn2_norm[l], ffn2_w1[l], ffn2_w3[l], ffn2_w2[l])

        k_all = aqkv[:, BR_W:2 * BR_W]
        v_all = aqkv[:, 2 * BR_W:3 * BR_W]
        ik_all = sm[:, SM_IK:SM_IK + IDX_DIM]
        gq = gslab[:, 0:GDN_CONV_CH]
        outs_p.append((k_all[:lp_].reshape(1, lp_, HEADS, HD), v_all[:lp_].reshape(1, lp_, HEADS, HD),
                       ik_all[:lp_].reshape(1, lp_, IDX_DIM), c_p, n_p, m_p, rs_p, rslab[lp_ - 1:lp_],
                       gs_p, gq[lp_ - (CONV_W - 1):lp_].reshape(1, CONV_W - 1, GDN_CONV_CH)))
        outs_s.append((k_all[lp_:].reshape(db, t, HEADS, HD), v_all[lp_:].reshape(db, t, HEADS, HD),
                       ik_all[lp_:].reshape(db, t, IDX_DIM), c_s, n_s, m_s, rs_s,
                       rslab[lp_:].reshape(db, t, RWKV_PROJ)[:, -1],
                       gs_s, gq[lp_:].reshape(db, t, GDN_CONV_CH)[:, t - (CONV_W - 1):]))

    y = _final_rms(x, final_norm)
    st_p = [jnp.stack(z) for z in zip(*outs_p)]
    st_s = [jnp.stack(z) for z in zip(*outs_s)]
    return (y[:lp_].reshape(1, lp_, d), y[lp_:].reshape(db, t, d), *st_p, *st_s)
```

```python
import functools
import math

import jax
import jax.numpy as jnp
import numpy as np
from jax import lax
from jax.experimental import pallas as pl
from jax.experimental.pallas import tpu as pltpu

F32 = jnp.float32
BF16 = jnp.bfloat16
I32 = jnp.int32
SDS = jax.ShapeDtypeStruct

HD = 64
HEADS = 4
BR_W = HEADS * HD
N_BRANCH = 4
IDX_HEADS = 8
IDX_DIM = 64
TOPK_MAX = 256
TOPM = 12
ROPE_THETA = 10000.0
MLSTM_CHUNK = 256
GDN_CHUNK = 64
CONV_W = 4
RWKV_W_LORA = 64
RWKV_A_LORA = 64
RWKV_G_LORA = 128
RWKV_PROJ = 3 * BR_W + RWKV_W_LORA + RWKV_A_LORA + RWKV_G_LORA
GDN_CONV_CH = 3 * BR_W
EPS = 1e-6
LN_EPS = 1e-5
PAGE_SIZE = 128

LANES = 128
SUBLANES = 8
VMEM_LIMIT_BYTES = 58 * 1024 * 1024

SM_IK = 0
SM_IW = 64
SM_MI = 72
SM_MF = 76
SM_GB = 80
SM_GA = 84
SM_END = 88

PC_AQKV = 0
PC_IQ = 768
PC_SM = 1280
PC_M = 1408
PC_R = 2432
PC_G = 3456
PC_GATE = 4480
PC_END = 8576

HIGHEST = lax.Precision.HIGHEST
INT_MIN = -2147483648
NEG_INF = float("-inf")


def _cp(sem):
    return pltpu.CompilerParams(dimension_semantics=sem, vmem_limit_bytes=VMEM_LIMIT_BYTES)


def _resident(shape):
    return pl.BlockSpec(shape, lambda *_: (0,) * len(shape), pipeline_mode=pl.Buffered(1))


def _rows(tm, w, off_blocks=0):
    return pl.BlockSpec((tm, w), lambda i, *_: (i + off_blocks, 0))


def _dot(a, b):
    return jnp.dot(a, b, preferred_element_type=F32)


def _dot_nt(a, b, precision=None):
    return lax.dot_general(a, b, (((1,), (1,)), ((), ())), preferred_element_type=F32, precision=precision)


def _dot_tn(a, b, precision=None):
    return lax.dot_general(a, b, (((0,), (0,)), ((), ())), preferred_element_type=F32, precision=precision)


def _hdot(a, b):
    return jnp.dot(a, b, preferred_element_type=F32, precision=HIGHEST)


def _dot3(a, b, dims=(((1,), (0,)), ((), ()))):
    ah, bh = a.astype(BF16), b.astype(BF16)
    al, bl = (a - ah.astype(F32)).astype(BF16), (b - bh.astype(F32)).astype(BF16)
    d = lambda x, y: lax.dot_general(x, y, dims, preferred_element_type=F32)
    return d(ah, bh) + (d(ah, bl) + d(al, bh))


_NT = (((1,), (1,)), ((), ()))
_TN = (((0,), (0,)), ((), ()))


def _rms(x, g):
    return x * lax.rsqrt(jnp.mean(x * x, axis=-1, keepdims=True) + EPS) * g


def _iota(shape, dim):
    return lax.broadcasted_iota(I32, shape, dim)


def _head_masks(rows):
    lane = _iota((rows, BR_W), 1)
    return [(lane >= HD * h) & (lane < HD * (h + 1)) for h in range(HEADS)]


def _seg_sum(x, masks):
    out = jnp.zeros_like(x)
    for m in masks:
        s = jnp.sum(jnp.where(m, x, 0.0), axis=1, keepdims=True)
        out = jnp.where(m, s, out)
    return out


def _col_to_row(col, n):
    eye = _iota((n, n), 0) == _iota((n, n), 1)
    return jnp.sum(jnp.where(eye, jnp.broadcast_to(col, (n, n)), 0.0), axis=0, keepdims=True)


def _row_to_col(row, n):
    eye = _iota((n, n), 0) == _iota((n, n), 1)
    return jnp.sum(jnp.where(eye, jnp.broadcast_to(row, (n, n)), 0.0), axis=1, keepdims=True)


def _cumsum_col(col, n):
    r = _iota((n, n), 0)
    c = _iota((n, n), 1)
    row = jnp.sum(jnp.where(r <= c, jnp.broadcast_to(col, (n, n)), 0.0), axis=0, keepdims=True)
    return row, _row_to_col(row, n)


def _ffn_kernel(x_ref, g_ref, w1_ref, w3_ref, w2_ref, o_ref):
    x = x_ref[...]
    h = _rms(x, g_ref[...]).astype(BF16)
    a = _dot(h, w1_ref[...])
    b = _dot(h, w3_ref[...])
    y = (a * jax.nn.sigmoid(a) * b).astype(BF16)
    o_ref[...] = x + 0.5 * _dot(y, w2_ref[...])


def _ffn(x, g, w1, w3, w2):
    na, d = x.shape
    f = w1.shape[1]
    tm = math.gcd(na, 256)
    return pl.pallas_call(
        _ffn_kernel,
        out_shape=SDS((na, d), F32),
        grid=(na // tm,),
        in_specs=[_rows(tm, d), _resident((1, d)), _resident((d, f)), _resident((d, f)), _resident((f, d))],
        out_specs=_rows(tm, d),
        compiler_params=_cp(("parallel",)),
        name="ffn",
    )(x, g.reshape(1, d), w1.astype(BF16), w3.astype(BF16), w2.astype(BF16))


def _rms_kernel(x_ref, g_ref, o_ref):
    o_ref[...] = _rms(x_ref[...], g_ref[...])


def _final_rms(x, g):
    na, d = x.shape
    tm = math.gcd(na, 512)
    return pl.pallas_call(
        _rms_kernel,
        out_shape=SDS((na, d), F32),
        grid=(na // tm,),
        in_specs=[_rows(tm, d), _resident((1, d))],
        out_specs=_rows(tm, d),
        compiler_params=_cp(("parallel",)),
        name="final_rms",
    )(x, g.reshape(1, d))


def _rope_slab(z, cos, sa, sb):
    return z * cos + pltpu.roll(z, LANES - HD // 2, 1) * sa + pltpu.roll(z, HD // 2, 1) * sb


def _proj_kernel(x_ref, g_ref, w_ref, cos_ref, sa_ref, sb_ref, sp_ref,
                 aqkv_ref, iq_ref, sm_ref, m_ref, r_ref, gq_ref, gate_ref):
    h = _rms(x_ref[...], g_ref[...]).astype(BF16)
    cos, sa, sb = cos_ref[...], sa_ref[...], sb_ref[...]

    def mm(a, b):
        return _dot(h, w_ref[:, a:b])

    za = mm(PC_AQKV, PC_IQ)
    for s in range(4):
        aqkv_ref[:, s * LANES:(s + 1) * LANES] = _rope_slab(za[:, s * LANES:(s + 1) * LANES], cos, sa, sb)
    aqkv_ref[:, 2 * BR_W:3 * BR_W] = za[:, 2 * BR_W:3 * BR_W]
    zi = mm(PC_IQ, PC_SM)
    for s in range(4):
        iq_ref[:, s * LANES:(s + 1) * LANES] = _rope_slab(zi[:, s * LANES:(s + 1) * LANES], cos, sa, sb)

    zs = mm(PC_SM, PC_M)
    lane = _iota(zs.shape, 1)
    gn, bias, alog = sp_ref[0:1, :], sp_ref[1:2, :], sp_ref[2:3, :]
    ms = jnp.sum(jnp.where(lane < IDX_DIM, zs * zs, 0.0), axis=1, keepdims=True) * (1.0 / IDX_DIM)
    ik = _rope_slab(zs * lax.rsqrt(ms + EPS) * gn, cos, sa, sb)
    t = zs + bias
    out = jnp.where(lane < SM_IW, ik,
          jnp.where(lane < SM_MI, zs * (IDX_HEADS ** -0.5),
          jnp.where(lane < SM_MF, t,
          jnp.where(lane < SM_GB, jax.nn.log_sigmoid(t),
          jnp.where(lane < SM_GA, jax.nn.sigmoid(zs),
          jnp.where(lane < SM_END, -jnp.exp(alog) * jax.nn.softplus(t), 0.0))))))
    sm_ref[...] = out

    m_ref[...] = mm(PC_M, PC_R)
    r_ref[...] = mm(PC_R, PC_G)
    gq_ref[...] = mm(PC_G, PC_GATE)
    gate_ref[...] = mm(PC_GATE, PC_END)


def _proj(x, g, w_pad, cos, sa, sb, sp):
    na, d = x.shape
    tm = math.gcd(na, 256)
    widths = (3 * BR_W, IDX_HEADS * IDX_DIM, LANES, 4 * BR_W, RWKV_PROJ, 4 * BR_W, N_BRANCH * d)
    return pl.pallas_call(
        _proj_kernel,
        out_shape=tuple(SDS((na, w), F32) for w in widths),
        grid=(na // tm,),
        in_specs=[_rows(tm, d), _resident((1, d)), _resident((d, PC_END)),
                  _rows(tm, LANES), _rows(tm, LANES), _rows(tm, LANES), _resident((SUBLANES, LANES))],
        out_specs=tuple(_rows(tm, w) for w in widths),
        compiler_params=_cp(("parallel",)),
        name="proj_in",
    )(x, g.reshape(1, d), w_pad, cos, sa, sb, sp)


def _merge_kernel(x_ref, oa_ref, om_ref, or_ref, og_ref, gate_ref, wb_ref, wo_ref, o_ref):
    d = x_ref.shape[1]
    mix = None
    for n, br in enumerate((oa_ref, om_ref, or_ref, og_ref)):
        up = _dot(br[...].astype(BF16), wb_ref[n])
        t = jax.nn.sigmoid(gate_ref[:, n * d:(n + 1) * d]) * up
        mix = t if mix is None else mix + t
    o_ref[...] = x_ref[...] + _dot(mix.astype(BF16), wo_ref[...])


def _merge(x, oa, om, orr, og, gate, w_branch, w_out):
    na, d = x.shape
    tm = math.gcd(na, 256)
    return pl.pallas_call(
        _merge_kernel,
        out_shape=SDS((na, d), F32),
        grid=(na // tm,),
        in_specs=[_rows(tm, d)] + [_rows(tm, BR_W)] * 4 + [_rows(tm, N_BRANCH * d),
                  _resident((N_BRANCH, BR_W, d)), _resident((d, d))],
        out_specs=_rows(tm, d),
        compiler_params=_cp(("parallel",)),
        name="merge",
    )(x, oa, om, orr, og, gate, w_branch.astype(BF16), w_out.astype(BF16))


def _idx_queries(qi, sm, tq):
    qrows = jnp.concatenate([qi[:, h * IDX_DIM:(h + 1) * IDX_DIM] for h in range(IDX_HEADS)], axis=0).astype(BF16)
    wb = [jnp.broadcast_to(sm[:, SM_IW + h:SM_IW + h + 1] * (IDX_DIM ** -0.5), (tq, LANES)) for h in range(IDX_HEADS)]
    return qrows, wb


def _head_sum(dots, wb, tq, tk):
    cols = []
    for j in range(tk // LANES):
        acc = None
        for h in range(IDX_HEADS):
            t = jnp.maximum(dots[h * tq:(h + 1) * tq, j * LANES:(j + 1) * LANES], 0.0) * wb[h]
            acc = t if acc is None else acc + t
        cols.append(acc)
    return cols[0] if len(cols) == 1 else jnp.concatenate(cols, axis=1)


def _to_key(sc):
    b = pltpu.bitcast(jnp.where(sc == 0.0, 0.0, sc), I32)
    return b ^ ((b >> 31) & I32(0x7FFFFFFF))


def _key_to_f32(key):
    return pltpu.bitcast(key ^ ((key >> 31) & I32(0x7FFFFFFF)), F32)


def _kth_largest(count, rows, n_sel, pos_bits, tkey_ref, cut_ref):
    int_min = I32(INT_MIN)

    def bit_step(it, ub):
        cand = ub | lax.shift_left(I32(1), 31 - it)
        cs = cand ^ int_min
        return jnp.where(count(lambda k, p: k >= cs) >= n_sel, cand, ub)

    ub = lax.fori_loop(0, 32, bit_step, jnp.zeros((rows, LANES), I32))
    tkey = ub ^ int_min
    n_gt = count(lambda k, p: k > tkey)
    n_ge = count(lambda k, p: k >= tkey)
    need = n_sel - n_gt
    ambiguous = (n_ge - n_gt) > need
    tkey_ref[...] = tkey
    cut_ref[...] = jnp.full((rows, LANES), 1 << pos_bits, I32)

    @pl.when(jnp.max(jnp.where(ambiguous, 1.0, 0.0)) > 0.0)
    def _():
        def pos_step(it, q):
            cand = q | lax.shift_left(I32(1), pos_bits - 1 - it)
            below = count(lambda k, p: (k == tkey) & (p < cand))
            return jnp.where(below < need, cand, q)

        q = lax.fori_loop(0, pos_bits, pos_step, jnp.zeros((rows, LANES), I32))
        cut_ref[...] = jnp.where(ambiguous, q + 1, 1 << pos_bits)


def _flash_update(s, v, m_s, l_s, acc_s, v_transposed=False):
    m_prev = m_s[...]
    m_new = jnp.maximum(m_prev, jnp.max(s, axis=1, keepdims=True))
    m_safe = jnp.where(m_new == NEG_INF, 0.0, m_new)
    alpha = jnp.exp(m_prev - m_safe)
    p = jnp.exp(s - m_safe)
    l_s[...] = alpha * l_s[...] + jnp.sum(p, axis=1, keepdims=True)
    m_s[...] = m_new
    pv = _dot_nt(p.astype(BF16), v) if v_transposed else _dot(p.astype(BF16), v)
    acc_s[...] = alpha * acc_s[...] + pv


def _stack_heads(q, t):
    masks = _head_masks(t)
    return jnp.concatenate([jnp.where(m, q * (HD ** -0.5), 0.0) for m in masks], axis=0).astype(BF16)


def _unstack_heads(acc, l, t):
    masks = _head_masks(t)
    out = jnp.zeros((t, BR_W), F32)
    for h, m in enumerate(masks):
        out = jnp.where(m, acc[h * t:(h + 1) * t] / l[h * t:(h + 1) * t], out)
    return out


def _dsa_prompt_kernel(qi_ref, smq_ref, q_ref, ki_ref, k_ref, v_ref, o_ref,
                       sc_s, cand_s, tkey_s, cut_s, need_s, neq_s, m_s, l_s, acc_s, sa_s, sb_s, da_s, db_s,
                       *, tq, tk, pf, n_sel, pos_bits):
    i = pl.program_id(0)
    q0 = i * tq
    nch = (q0 + tq - 1) // tk + 1
    qrows, wb = _idx_queries(qi_ref[...], smq_ref[...], tq)
    qpos = q0 + _iota((tq, tk), 0)

    ngrp = (nch + pf - 1) // pf
    nfill = ngrp * pf

    def dots_into(buf, c):
        k0 = pl.multiple_of(jnp.minimum(c, nfill - 1) * tk, tk)
        buf[...] = _dot_nt(qrows, ki_ref[pl.ds(k0, tk), :])

    def scores_from(buf, c):
        cc = jnp.minimum(c, nfill - 1)
        sc = _head_sum(buf, wb, tq, tk)
        sc = jnp.where(cc * tk + _iota((tq, tk), 1) <= qpos, sc, NEG_INF)
        sc_s[cc] = jnp.where(sc == 0.0, 0.0, sc)

    dots_into(da_s, 0)

    def fill2(j, carry):
        dots_into(db_s, 2 * j + 1)
        scores_from(da_s, 2 * j)
        dots_into(da_s, 2 * j + 2)
        scores_from(db_s, 2 * j + 1)
        return carry

    lax.fori_loop(0, (nfill + 1) // 2, fill2, 0)
    int_min = I32(INT_MIN)
    neg_key = _to_key(jnp.full((tq, LANES), NEG_INF, F32))
    big = 1 << pos_bits

    def count(pred):
        def body(c, acc):
            for j in range(tk // LANES):
                acc = acc + jnp.where(pred(sc_s[c, :, j * LANES:(j + 1) * LANES]), 1.0, 0.0)
            return acc

        acc = lax.fori_loop(0, nch, body, jnp.zeros((tq, LANES), F32))
        return jnp.broadcast_to(jnp.sum(acc, axis=1, keepdims=True), (tq, LANES))

    rg = 2 * SUBLANES
    for g in range(tq // rg):
        def insert(c, best, g=g):
            for j in range(tk // LANES):
                x = sc_s[c, g * rg:(g + 1) * rg, j * LANES:(j + 1) * LANES]
                nxt = []
                for b in best:
                    nxt.append(jnp.maximum(b, x))
                    x = jnp.minimum(b, x)
                best = tuple(nxt)
            return best

        best = lax.fori_loop(0, nch, insert, tuple(jnp.full((rg, LANES), NEG_INF, F32) for _ in range(TOPM)))
        for u in range(TOPM):
            cand_s[u, g * rg:(g + 1) * rg, :] = _to_key(best[u])

    def count_cand(pred):
        terms = [jnp.where(pred(cand_s[u]), 1.0, 0.0) for u in range(TOPM)]
        while len(terms) > 1:
            terms = [terms[n] + terms[n + 1] for n in range(0, len(terms) - 1, 2)] + terms[len(terms) & ~1:]
        return jnp.broadcast_to(jnp.sum(terms[0], axis=1, keepdims=True), (tq, LANES))

    def bit_step(it, ub):
        cand = ub | lax.shift_left(I32(1), 31 - it)
        cs = cand ^ int_min
        return jnp.where(count_cand(lambda k: k >= cs) >= n_sel, cand, ub)

    first_bit, ub0 = 0, jnp.zeros((tq, LANES), I32)
    if 2 * LANES >= n_sel > LANES and TOPM >= 2:
        second = _key_to_f32(cand_s[1])
        lo = _to_key(jnp.broadcast_to(jnp.min(second, axis=1, keepdims=True), (tq, LANES))) ^ int_min
        hi = _to_key(jnp.broadcast_to(jnp.max(second, axis=1, keepdims=True), (tq, LANES))) ^ int_min
        shared = lax.clz(lo ^ hi).astype(F32)
        first_bit = jnp.min(shared).astype(I32)
        keep = jnp.where(first_bit == 0, 0, lax.shift_left(I32(-1), jnp.minimum(32 - first_bit, 31)))
        keep = jnp.where(first_bit == 32, I32(-1), keep)
        ub0 = lo & keep
    tk_c = lax.fori_loop(first_bit, 32, bit_step, ub0) ^ int_min
    n_gt = count_cand(lambda k: k > tk_c)
    tkey_s[...] = tk_c
    need_s[...] = n_sel - n_gt
    neq_s[...] = count_cand(lambda k: k >= tk_c) - n_gt
    dropped = (cand_s[TOPM - 1] >= tk_c) & (nch * (tk // LANES) > TOPM)

    @pl.when(jnp.max(jnp.where(dropped, 1.0, 0.0)) > 0.0)
    def _():
        def step(it, ub):
            cand = ub | lax.shift_left(I32(1), 31 - it)
            t = _key_to_f32(jnp.maximum(cand ^ int_min, neg_key))
            return jnp.where(count(lambda s: s >= t) >= n_sel, cand, ub)

        tk_f = lax.fori_loop(0, 32, step, jnp.zeros((tq, LANES), I32)) ^ int_min
        t = _key_to_f32(tk_f)
        n_gt_f = count(lambda s: s > t)
        tkey_s[...] = tk_f
        need_s[...] = n_sel - n_gt_f
        neq_s[...] = count(lambda s: s >= t) - n_gt_f

    thr128 = _key_to_f32(tkey_s[...])
    ambiguous = neq_s[...] > need_s[...]
    cut_s[...] = jnp.full((tq, LANES), big, I32)

    @pl.when(jnp.max(jnp.where(ambiguous, 1.0, 0.0)) > 0.0)
    def _():
        thr1 = thr128[:, 0:1]
        need1 = need_s[:, 0:1]
        tri = jnp.where(_iota((tk, tk), 0) <= _iota((tk, tk), 1), 1.0, 0.0).astype(BF16)

        def body(c, carry):
            base, cutf = carry
            eq = sc_s[c] == thr1
            run = base + _dot(jnp.where(eq, 1.0, 0.0).astype(BF16), tri)
            posf = (c * tk + 1 + _iota((tq, tk), 1)).astype(F32)
            hit = jnp.where(eq & (run == need1), posf, float(big))
            return run[:, tk - 1:tk], jnp.minimum(cutf, jnp.min(hit, axis=1, keepdims=True))

        _, cutf = lax.fori_loop(0, nch, body, (jnp.zeros((tq, 1), F32), jnp.full((tq, 1), float(big), F32)))
        cut_s[...] = jnp.where(ambiguous, jnp.broadcast_to(cutf, (tq, LANES)).astype(I32), big)

    qend = q0 + _iota((tq, LANES), 0) + 1
    cut_s[...] = jnp.where(tkey_s[...] == neg_key, jnp.minimum(cut_s[...], qend), cut_s[...])

    thr = thr128[:, 0:1]
    cut = cut_s[:, 0:1]
    qbd = _stack_heads(q_ref[:, 0:BR_W], tq)
    m_s[...] = jnp.full(m_s.shape, NEG_INF, F32)
    l_s[...] = jnp.zeros(l_s.shape, F32)
    acc_s[...] = jnp.zeros(acc_s.shape, F32)

    def logits_into(buf, g):
        gc = jnp.minimum(g, ngrp - 1)
        k0 = pl.multiple_of(gc * (pf * tk), pf * tk)
        sc = jnp.concatenate([sc_s[gc * pf + u] for u in range(pf)], axis=1) if pf > 1 else sc_s[gc]
        sel = (sc > thr) | ((sc == thr) & (k0 + _iota((tq, pf * tk), 1) < cut))
        bias = jnp.where(sel, 0.0, NEG_INF)
        buf[...] = _dot_nt(qbd, k_ref[pl.ds(k0, pf * tk), :]) + jnp.concatenate([bias] * HEADS, axis=0)

    def consume(buf, g):
        k0 = pl.multiple_of(g * (pf * tk), pf * tk)
        _flash_update(buf[...], v_ref[pl.ds(k0, pf * tk), :], m_s, l_s, acc_s)

    logits_into(sa_s, 0)

    def pair(j, carry):
        logits_into(sb_s, 2 * j + 1)
        consume(sa_s, 2 * j)
        logits_into(sa_s, 2 * j + 2)
        consume(sb_s, 2 * j + 1)
        return carry

    lax.fori_loop(0, ngrp // 2, pair, 0)

    @pl.when((ngrp & 1) == 1)
    def _():
        consume(sa_s, ngrp - 1)

    o_ref[...] = _unstack_heads(acc_s[...], l_s[...], tq)


def _dsa_prompt(iq, sm, aqkv, lp):
    tq = min(128, lp)
    tk = min(512, lp)
    n_sel = min(TOPK_MAX, lp // 4)
    pos_bits = int(lp).bit_length()
    pf = 2 if (lp // tk) % 2 == 0 else 1
    ki = sm[:lp, SM_IK:SM_IK + IDX_DIM].astype(BF16)
    kb = aqkv[:lp, BR_W:2 * BR_W].astype(BF16)
    vb = aqkv[:lp, 2 * BR_W:3 * BR_W].astype(BF16)
    return pl.pallas_call(
        functools.partial(_dsa_prompt_kernel, tq=tq, tk=tk, pf=pf, n_sel=n_sel, pos_bits=pos_bits),
        out_shape=SDS((lp, BR_W), F32),
        grid=(lp // tq,),
        in_specs=[_rows(tq, IDX_HEADS * IDX_DIM), _rows(tq, LANES), _rows(tq, 3 * BR_W),
                  _resident((lp, IDX_DIM)), _resident((lp, BR_W)), _resident((lp, BR_W))],
        out_specs=_rows(tq, BR_W),
        scratch_shapes=[pltpu.VMEM((lp // tk, tq, tk), F32), pltpu.VMEM((TOPM, tq, LANES), I32),
                        pltpu.VMEM((tq, LANES), I32), pltpu.VMEM((tq, LANES), I32),
                        pltpu.VMEM((tq, LANES), F32), pltpu.VMEM((tq, LANES), F32),
                        pltpu.VMEM((HEADS * tq, 1), F32), pltpu.VMEM((HEADS * tq, 1), F32),
                        pltpu.VMEM((HEADS * tq, BR_W), F32),
                        pltpu.VMEM((HEADS * tq, pf * tk), F32), pltpu.VMEM((HEADS * tq, pf * tk), F32),
                        pltpu.VMEM((IDX_HEADS * tq, tk), F32), pltpu.VMEM((IDX_HEADS * tq, tk), F32)],
        compiler_params=_cp(("arbitrary",)),
        name="dsa_prompt",
    )(iq, sm, aqkv, ki, kb, vb)


def _dsa_sample_score_kernel(pt_ref, qr_ref, wb_ref, *refs, t, pg, ng, past, n_sel, pos_bits):
    pages = refs[:pg]
    knew_ref, scp_ref, scn_ref, thr_ref, cut_ref, keys_s, keyn_s, tkey_s = refs[pg:]
    j = pl.program_id(1)
    qrows = qr_ref[...]
    wb = wb_ref[...] * (IDX_DIM ** -0.5)

    def reduce_heads(dots):
        w = jnp.maximum(dots, 0.0) * wb
        s = w[0:t]
        for h in range(1, IDX_HEADS):
            s = s + w[h * t:(h + 1) * t]
        return s

    sc = jnp.concatenate([reduce_heads(_dot(qrows, p[...].astype(BF16))) for p in pages], axis=1)
    scp_ref[...] = sc
    keys_s[j] = _to_key(sc)

    @pl.when(j == ng - 1)
    def _():
        lane = _iota((t, LANES), 1)
        sn = reduce_heads(_dot_nt(qrows, knew_ref[...].astype(BF16)))
        sn = jnp.where(lane <= _iota((t, LANES), 0), sn, NEG_INF)
        scn_ref[...] = sn
        keyn_s[...] = _to_key(sn)

        def count(pred):
            def body(c, acc):
                for jj in range(pg):
                    p = c * (pg * LANES) + jj * LANES + lane
                    acc = acc + jnp.where(pred(keys_s[c, :, jj * LANES:(jj + 1) * LANES], p), 1.0, 0.0)
                return acc

            acc = lax.fori_loop(0, ng, body, jnp.zeros((t, LANES), F32))
            acc = acc + jnp.where(pred(keyn_s[...], past + lane), 1.0, 0.0)
            return jnp.broadcast_to(jnp.sum(acc, axis=1, keepdims=True), (t, LANES))

        _kth_largest(count, t, n_sel, pos_bits, tkey_s, cut_ref)
        thr_ref[...] = _key_to_f32(tkey_s[...])


def _dsa_sample_attn_kernel(pt_ref, q_ref, *refs, t, pg, ng, past):
    kpages = refs[:pg]
    vpages = refs[pg:2 * pg]
    (scp_ref, scn_ref, thr_ref, cut_ref, knew_ref, vnew_ref, o_ref, m_s, l_s, acc_s) = refs[2 * pg:]
    j = pl.program_id(1)

    @pl.when(j == 0)
    def _():
        m_s[...] = jnp.full(m_s.shape, NEG_INF, F32)
        l_s[...] = jnp.zeros(l_s.shape, F32)
        acc_s[...] = jnp.zeros(acc_s.shape, F32)

    qbd = _stack_heads(q_ref[...], t)
    thr = thr_ref[:, 0:1]
    cut = cut_ref[:, 0:1]
    w = pg * LANES
    sc = scp_ref[...]
    kidx = j * w + _iota((t, w), 1)
    bias = jnp.where((sc > thr) | ((sc == thr) & (kidx < cut)), 0.0, NEG_INF)
    kcat = jnp.concatenate([p[...].reshape(BR_W, PAGE_SIZE).astype(BF16) for p in kpages], axis=1)
    vcat = jnp.concatenate([p[...].reshape(BR_W, PAGE_SIZE).astype(BF16) for p in vpages], axis=1)
    s = _dot(qbd, kcat) + jnp.concatenate([bias] * HEADS, axis=0)
    _flash_update(s, vcat, m_s, l_s, acc_s, v_transposed=True)

    @pl.when(j == ng - 1)
    def _():
        lane = _iota((t, LANES), 1)
        scn = scn_ref[...]
        seln = ((scn > thr) | ((scn == thr) & (past + lane < cut))) & (lane <= _iota((t, LANES), 0))
        biasn = jnp.where(seln, 0.0, NEG_INF)
        sn = _dot_nt(qbd, knew_ref[...].astype(BF16)) + jnp.concatenate([biasn] * HEADS, axis=0)
        _flash_update(sn, vnew_ref[...].astype(BF16), m_s, l_s, acc_s)
        o_ref[...] = _unstack_heads(acc_s[...], l_s[...], t)


def _dsa_sample(iq_s, sm_s, aqkv_s, cache_k, cache_v, cache_idx_k, page_table, layer, db, t):
    n_pages = page_table.shape[1]
    past = n_pages * PAGE_SIZE
    n_sel = min(TOPK_MAX, (past + t) // 4)
    pos_bits = int(past + LANES).bit_length()
    pg = math.gcd(n_pages, 32)
    ng = n_pages // pg
    ckt = cache_k.transpose(0, 1, 3, 4, 2)
    cvt = cache_v.transpose(0, 1, 3, 4, 2)
    cit = cache_idx_k.transpose(0, 1, 3, 2)

    qrows = iq_s.reshape(db, t, IDX_HEADS, IDX_DIM).transpose(0, 2, 1, 3).reshape(db, IDX_HEADS * t, IDX_DIM).astype(BF16)
    wcol = sm_s[:, SM_IW:SM_IW + IDX_HEADS].reshape(db, t, IDX_HEADS).transpose(0, 2, 1).reshape(db, IDX_HEADS * t, 1)
    wbs = jnp.broadcast_to(wcol, (db, IDX_HEADS * t, LANES))
    pad = lambda a: jnp.pad(a.reshape(db, t, a.shape[-1]), ((0, 0), (0, LANES - t), (0, 0)))
    ki_new = pad(sm_s[:, SM_IK:SM_IK + IDX_DIM])
    k_new = pad(aqkv_s[:, BR_W:2 * BR_W])
    v_new = pad(aqkv_s[:, 2 * BR_W:3 * BR_W])
    q3 = aqkv_s[:, 0:BR_W].reshape(db, t, BR_W)

    def idx_page(i):
        return pl.BlockSpec((None, None, IDX_DIM, PAGE_SIZE), lambda b, j, pt: (layer, pt[b, j * pg + i], 0, 0))

    def kv_page(i):
        return pl.BlockSpec((None, None, HEADS, HD, PAGE_SIZE), lambda b, j, pt: (layer, pt[b, j * pg + i], 0, 0, 0))

    per_b = lambda r, w: pl.BlockSpec((None, r, w), lambda b, j, pt: (b, 0, 0))
    sc_spec = pl.BlockSpec((None, t, pg * LANES), lambda b, j, pt: (b, 0, j))

    scp, scn, thr, cut = pl.pallas_call(
        functools.partial(_dsa_sample_score_kernel, t=t, pg=pg, ng=ng, past=past, n_sel=n_sel, pos_bits=pos_bits),
        out_shape=(SDS((db, t, past), F32), SDS((db, t, LANES), F32), SDS((db, t, LANES), F32), SDS((db, t, LANES), I32)),
        grid_spec=pltpu.PrefetchScalarGridSpec(
            num_scalar_prefetch=1, grid=(db, ng),
            in_specs=[per_b(IDX_HEADS * t, IDX_DIM), per_b(IDX_HEADS * t, LANES)]
                     + [idx_page(i) for i in range(pg)] + [per_b(LANES, IDX_DIM)],
            out_specs=(sc_spec, per_b(t, LANES), per_b(t, LANES), per_b(t, LANES)),
            scratch_shapes=[pltpu.VMEM((ng, t, pg * LANES), I32), pltpu.VMEM((t, LANES), I32), pltpu.VMEM((t, LANES), I32)]),
        compiler_params=_cp(("arbitrary", "arbitrary")),
        name="dsa_sample_scores",
    )(page_table, qrows, wbs, *([cit] * pg), ki_new)

    out = pl.pallas_call(
        functools.partial(_dsa_sample_attn_kernel, t=t, pg=pg, ng=ng, past=past),
        out_shape=SDS((db, t, BR_W), F32),
        grid_spec=pltpu.PrefetchScalarGridSpec(
            num_scalar_prefetch=1, grid=(db, ng),
            in_specs=[per_b(t, BR_W)] + [kv_page(i) for i in range(pg)] * 2
                     + [sc_spec, per_b(t, LANES), per_b(t, LANES), per_b(t, LANES), per_b(LANES, BR_W), per_b(LANES, BR_W)],
            out_specs=per_b(t, BR_W),
            scratch_shapes=[pltpu.VMEM((HEADS * t, 1), F32), pltpu.VMEM((HEADS * t, 1), F32),
                            pltpu.VMEM((HEADS * t, BR_W), F32)]),
        compiler_params=_cp(("arbitrary", "arbitrary")),
        name="dsa_sample_attention",
    )(page_table, q3, *([ckt] * pg), *([cvt] * pg), scp, scn, thr, cut, k_new, v_new)
    return out.reshape(db * t, BR_W)


def _mlstm_chunk(q, k, v, li, lf, cst, n, m, c):
    r = _iota((c, c), 0)
    cc = _iota((c, c), 1)
    hh = range(len(q))
    qk = [_dot_nt(q[h], k[h], HIGHEST) for h in hh]
    qc = [_dot_nt(q[h], cst[h], HIGHEST) for h in hh]
    b_col, mt, ei, s = [], [], [], []
    for h in hh:
        b_row, bc_ = _cumsum_col(lf[h], c)
        log_d = jnp.where(cc <= r, bc_ - b_row + _col_to_row(li[h], c), NEG_INF)
        inter = bc_ + m[h]
        mt_ = jnp.maximum(inter, jnp.max(log_d, axis=1, keepdims=True))
        b_col.append(bc_)
        mt.append(mt_)
        ei.append(jnp.exp(inter - mt_))
        s.append(qk[h] * jnp.exp(log_d - mt_))
    sv = [_hdot(s[h], v[h]) for h in hh]
    outs, c_new, n_new, m_new, wsv = [], [], [], [], []
    for h in hh:
        den = jnp.sum(s[h], axis=1, keepdims=True) + ei[h] * jnp.sum(q[h] * n[h], axis=1, keepdims=True)
        outs.append((sv[h] + ei[h] * qc[h]) / jnp.maximum(jnp.abs(den), jnp.exp(-mt[h])))
        bc = b_col[h][c - 1:c, :]
        gs = bc - b_col[h] + li[h]
        mn = jnp.maximum(bc + m[h], jnp.max(gs, axis=0, keepdims=True))
        ws = jnp.exp(gs - mn)
        dec = jnp.exp(bc + m[h] - mn)
        wsv.append(ws * v[h])
        c_new.append(dec * cst[h])
        n_new.append(dec * n[h] + jnp.sum(ws * k[h], axis=0, keepdims=True))
        m_new.append(mn)
    c_new = [c_new[h] + _dot_tn(wsv[h], k[h], HIGHEST) for h in hh]
    return outs, c_new, n_new, m_new


def _mlstm_kernel(x_ref, sm_ref, c0_ref, n0_ref, m0_ref, lng_ref, o_ref, c1_ref, n1_ref, m1_ref,
                  c_s, n_s, m_s, *, c, cb):
    i = pl.program_id(1)

    @pl.when(i == 0)
    def _():
        c_s[...] = c0_ref[...]
        n_s[...] = n0_ref[...]
        m_s[...] = m0_ref[...]

    def chunk(ci, carry):
        r0 = pl.multiple_of(ci * c, c)
        hh = range(HEADS)
        col = lambda base, h: x_ref[pl.ds(r0, c), base + HD * h:base + HD * (h + 1)]
        hs, c_new, n_new, m_new = _mlstm_chunk(
            [col(0, h) for h in hh], [col(BR_W, h) * (HD ** -0.5) for h in hh], [col(2 * BR_W, h) for h in hh],
            [sm_ref[pl.ds(r0, c), SM_MI + h:SM_MI + h + 1] for h in hh],
            [sm_ref[pl.ds(r0, c), SM_MF + h:SM_MF + h + 1] for h in hh],
            [c_s[h] for h in hh], [n_s[h] for h in hh], [m_s[h] for h in hh], c)
        outs = []
        for h in hh:
            c_s[h] = c_new[h]
            n_s[h] = n_new[h]
            m_s[h] = m_new[h]
            mu = jnp.mean(hs[h], axis=1, keepdims=True)
            var = jnp.mean(jnp.square(hs[h] - mu), axis=1, keepdims=True)
            y = (hs[h] - mu) * lax.rsqrt(var + LN_EPS) * lng_ref[:, HD * h:HD * (h + 1)]
            outs.append(y * jax.nn.sigmoid(col(3 * BR_W, h)))
        o_ref[pl.ds(r0, c), :] = jnp.concatenate(outs, axis=1)
        return carry

    lax.fori_loop(0, cb, chunk, 0)

    @pl.when(i == pl.num_programs(1) - 1)
    def _():
        c1_ref[...] = c_s[...]
        n1_ref[...] = n_s[...]
        m1_ref[...] = m_s[...]


def _seq_blocks(l, chunk, max_rows):
    nc = l // chunk
    cb = math.gcd(nc, max(1, max_rows // chunk))
    return chunk * cb, cb, nc // cb


def _state_spec(shape):
    return pl.BlockSpec((None,) + shape, lambda b, i: (b,) + (0,) * len(shape))


def _mlstm(mslab, sm, row_off, b, l, c0, n0, m0, ln_g):
    c = math.gcd(l, MLSTM_CHUNK)
    rows, cb, nblk = _seq_blocks(l, c, 512)
    off = row_off // rows
    seq = lambda w: pl.BlockSpec((rows, w), lambda bb, i: (off + bb * nblk + i, 0))
    o, c1, n1, m1 = pl.pallas_call(
        functools.partial(_mlstm_kernel, c=c, cb=cb),
        out_shape=(SDS((b * l, BR_W), F32), SDS((b, HEADS, HD, HD), F32), SDS((b, HEADS, 1, HD), F32),
                   SDS((b, HEADS, 1, 1), F32)),
        grid=(b, nblk),
        in_specs=[seq(4 * BR_W), seq(LANES), _state_spec((HEADS, HD, HD)), _state_spec((HEADS, 1, HD)),
                  _state_spec((HEADS, 1, 1)), pl.BlockSpec((1, BR_W), lambda bb, i: (0, 0))],
        out_specs=(pl.BlockSpec((rows, BR_W), lambda bb, i: (bb * nblk + i, 0)), _state_spec((HEADS, HD, HD)),
                   _state_spec((HEADS, 1, HD)), _state_spec((HEADS, 1, 1))),
        scratch_shapes=[pltpu.VMEM((HEADS, HD, HD), F32), pltpu.VMEM((HEADS, 1, HD), F32), pltpu.VMEM((HEADS, 1, 1), F32)],
        compiler_params=_cp(("parallel", "arbitrary")),
        name="mlstm",
    )(mslab, sm, c0, n0.reshape(b, HEADS, 1, HD), m0.reshape(b, HEADS, 1, 1), ln_g.reshape(1, BR_W))
    return o, c1, n1.reshape(b, HEADS, HD), m1.reshape(b, HEADS)


def _shift_rows(x, halo, s):
    rows = x.shape[0]
    rolled = pltpu.roll(x, s, 0)
    first = jnp.where(_iota((SUBLANES, x.shape[1]), 0) < s, pltpu.roll(halo, s, 0), rolled[0:SUBLANES])
    return first if rows == SUBLANES else jnp.concatenate([first, rolled[SUBLANES:]], axis=0)


def _halo_specs(rows, w, off_rows, nt):
    hb = rows // SUBLANES
    tile = pl.BlockSpec((rows, w), lambda b, i: (off_rows // rows + b * nt + i, 0))
    halo = pl.BlockSpec((SUBLANES, w), lambda b, i: (jnp.maximum(off_rows // SUBLANES + (b * nt + i) * hb - 1, 0), 0))
    state = pl.BlockSpec((None, SUBLANES, w), lambda b, i: (b, 0, 0))
    return tile, halo, state


def _pad_state_rows(st, w):
    b, r, w0 = st.shape
    return jnp.pad(st, ((0, 0), (SUBLANES - r, 0), (0, w - w0)))


def _cumsum_rows(x):
    row = _iota(x.shape, 0)
    sh = 1
    while sh < x.shape[0]:
        x = x + jnp.where(row >= sh, pltpu.roll(x, sh, 0), 0.0)
        sh *= 2
    return x


def _unit_lower_inverse(ms, c):
    r = _iota((c, c), 0)
    cc = _iota((c, c), 1)
    fold = jnp.where((_iota((c, SUBLANES), 0) & 7) == _iota((c, SUBLANES), 1), 1.0, 0.0)
    same8 = (r >> 3) == (cc >> 3)
    m8s = [_dot3(jnp.where(same8, m, 0.0), fold) for m in ms]
    zs = [jnp.where(r == cc, 1.0, 0.0)] * len(ms)

    def pivot(z, s):
        return jnp.broadcast_to(z.reshape(c // SUBLANES, SUBLANES, c)[:, s:s + 1, :],
                                (c // SUBLANES, SUBLANES, c)).reshape(c, c)

    for s in range(SUBLANES - 1):
        zs = [z + m8[:, s:s + 1] * pivot(z, s) for z, m8 in zip(zs, m8s)]
    sh = 3
    while (1 << sh) < c:
        off = ((r >> (sh + 1)) == (cc >> (sh + 1))) & ((r >> sh) != (cc >> sh))
        ts = [_dot3(z, jnp.where(off, m, 0.0)) for z, m in zip(zs, ms)]
        zs = [z + _dot3(t, z) for z, t in zip(zs, ts)]
        sh += 1
    return zs


def _lin_scan(a_ref, y0_ref, phi_ref, psi_ref, s_s, y_s, c, cb):
    def chunk(ci, carry):
        r0 = pl.multiple_of(ci * c, c)
        sts = [s_s[h] for h in range(HEADS)]
        new = [_dot3(sts[h], phi_ref[ci, h]) for h in range(HEADS)]
        ys = [_dot3(a_ref[pl.ds(r0, c), HD * h:HD * (h + 1)], sts[h], _NT) for h in range(HEADS)]
        for h in range(HEADS):
            s_s[h] = new[h] + psi_ref[ci, h]
        y_s[pl.ds(r0, c), :] = jnp.concatenate(ys, axis=1) + y0_ref[pl.ds(r0, c), :]
        return carry

    lax.fori_loop(0, cb, chunk, 0)


def _chunk_specs(rows, cb, nblk, widths):
    row_specs = [pl.BlockSpec((rows, w), lambda bb, i: (bb * nblk + i, 0)) for w in widths]
    mat = pl.BlockSpec((cb, HEADS, HD, HD), lambda bb, i: (bb * nblk + i, 0, 0, 0))
    return row_specs, mat


def _rwkv_prep_kernel(p_ref, halo_ref, st_ref, vec_ref, w2_ref, a2_ref, g2_ref, rs_ref, post_ref):
    i = pl.program_id(1)
    p = p_ref[...]
    rows = p.shape[0]
    halo = jnp.where(i == 0, st_ref[...], halo_ref[...])
    prev = _shift_rows(p, halo, 1)
    mu = jnp.concatenate([vec_ref[0:1, :], vec_ref[1:2, :], vec_ref[2:3, :], vec_ref[3:4, :]], axis=1)
    pm = p + (prev - p) * mu
    r = pm[:, 0:BR_W]
    k = pm[:, BR_W:2 * BR_W]
    v = pm[:, 2 * BR_W:3 * BR_W]
    o = 3 * BR_W
    wd = pm[:, o:o + RWKV_W_LORA]
    ad = pm[:, o + RWKV_W_LORA:o + RWKV_W_LORA + RWKV_A_LORA]
    gd = pm[:, o + RWKV_W_LORA + RWKV_A_LORA:RWKV_PROJ]
    w0, a0, kkp, ka, rk = (vec_ref[4:5, :], vec_ref[5:6, :], vec_ref[6:7, :], vec_ref[7:8, :], vec_ref[8:9, :])
    wlog = -jax.nn.softplus(-(w0 + _hdot(jnp.tanh(wd), w2_ref[...]))) - 0.5
    log_decay = -jnp.exp(wlog)
    a = jax.nn.sigmoid(a0 + _hdot(ad, a2_ref[...]))
    g = _hdot(jax.nn.sigmoid(gd), g2_ref[...])
    masks = _head_masks(rows)
    kkr = k * kkp
    kk = kkr * lax.rsqrt(jnp.maximum(_seg_sum(kkr * kkr, masks), 1e-12))
    k2 = k * (1.0 + (a - 1.0) * ka)
    bonus = _seg_sum(r * k2 * rk, masks) * v
    for n, t in enumerate((r, log_decay, k2, v, kk, kk * a)):
        rs_ref[:, n * BR_W:(n + 1) * BR_W] = t
    post_ref[:, 0:BR_W] = bonus
    post_ref[:, BR_W:2 * BR_W] = g


def _rwkv_chunk_kernel(rs_ref, rh_ref, y0_ref, phi_ref, psi_ref, *, c, cb, gc):
    r_i = _iota((c, c), 0)
    c_i = _iota((c, c), 1)
    eye = _iota((HD, HD), 0) == _iota((HD, HD), 1)
    jobs = [(g, h) for g in range(gc) for h in range(HEADS)]
    hs = lambda x, h: x[:, HD * h:HD * (h + 1)]

    def group(gi, carry):
        r0s = [pl.multiple_of((gi * gc + g) * c, c) for g in range(gc)]
        w = []
        for r0 in r0s:
            r, lw, k, v, kk, kka = (rs_ref[pl.ds(r0, c), n * BR_W:(n + 1) * BR_W] for n in range(6))
            cum = _cumsum_rows(lw)
            pc = cum[c - 1:c, :]
            inv = jnp.exp(-cum)
            rest = jnp.exp(pc - cum)
            w.append(dict(at=-kk * jnp.exp(cum - lw),
                          bt=kka * inv, kt=k * inv,
                          rt=r * jnp.exp(cum), bd=kka * rest, kd=k * rest, v=v, epc=jnp.exp(pc)))
        big = [_dot3(jnp.concatenate([hs(w[g]["at"], h), hs(w[g]["rt"], h)], axis=0),
                     jnp.concatenate([hs(w[g]["bt"], h), hs(w[g]["kt"], h)], axis=0), _NT) for g, h in jobs]
        mab = [jnp.where(c_i < r_i, x[0:c, 0:c], 0.0) for x in big]
        mak = [jnp.where(c_i < r_i, x[0:c, c:2 * c], 0.0) for x in big]
        nn = [jnp.concatenate([jnp.where(c_i <= r_i, x[c:2 * c, 0:c], 0.0),
                               jnp.where(c_i <= r_i, x[c:2 * c, c:2 * c], 0.0)], axis=1) for x in big]
        tinv = _unit_lower_inverse(mab, c)
        mv = [_dot3(m, hs(w[g]["v"], h)) for m, (g, h) in zip(mak, jobs)]
        hg = [_dot3(t, jnp.concatenate([hs(w[g]["at"], h), x], axis=1)) for t, x, (g, h) in zip(tinv, mv, jobs)]
        out = [_dot3(n2, jnp.concatenate([x, jnp.concatenate([jnp.zeros((c, HD), F32), hs(w[g]["v"], h)], axis=1)], axis=0))
               for n2, x, (g, h) in zip(nn, hg, jobs)]
        hb = [_dot3(x, hs(w[g]["bd"], h), _TN) for x, (g, h) in zip(hg, jobs)]
        vk = [_dot3(hs(w[g]["v"], h), hs(w[g]["kd"], h), _TN) for g, h in jobs]
        for g in range(gc):
            sel = [n for n, (gg, _) in enumerate(jobs) if gg == g]
            rh_ref[pl.ds(r0s[g], c), :] = w[g]["rt"] + jnp.concatenate([out[n][:, 0:HD] for n in sel], axis=1)
            y0_ref[pl.ds(r0s[g], c), :] = jnp.concatenate([out[n][:, HD:2 * HD] for n in sel], axis=1)
            phi_ref[gi * gc + g] = jnp.stack([jnp.where(eye, jnp.broadcast_to(hs(w[g]["epc"], jobs[n][1]), (HD, HD)), 0.0)
                                              + hb[n][0:HD] for n in sel])
            psi_ref[gi * gc + g] = jnp.stack([hb[n][HD:2 * HD] + vk[n] for n in sel])
        return carry

    lax.fori_loop(0, cb // gc, group, 0)


def _rwkv_scan_kernel(rh_ref, y0_ref, phi_ref, psi_ref, post_ref, s0_ref, ln_ref, o_ref, s1_ref, s_s, y_s, *, c, cb):
    i = pl.program_id(1)

    @pl.when(i == 0)
    def _():
        s_s[...] = s0_ref[...]

    _lin_scan(rh_ref, y0_ref, phi_ref, psi_ref, s_s, y_s, c, cb)
    y = y_s[...]
    masks = _head_masks(c * cb)
    mu = _seg_sum(y, masks) * (1.0 / HD)
    var = _seg_sum(jnp.square(y - mu), masks) * (1.0 / HD)
    y = (y - mu) * lax.rsqrt(var + LN_EPS) * ln_ref[0:1, :] + ln_ref[1:2, :]
    o_ref[...] = (y + post_ref[:, 0:BR_W]) * post_ref[:, BR_W:2 * BR_W]

    @pl.when(i == pl.num_programs(1) - 1)
    def _():
        s1_ref[...] = s_s[...]


def _rwkv(rslab, row_off, b, l, sh0, s0, lp):
    rows = math.gcd(l, 256)
    nt = l // rows
    tile, halo, state = _halo_specs(rows, RWKV_PROJ, row_off, nt)
    quarter = lambda a: a.reshape(4, BR_W)
    vec = jnp.concatenate([quarter(lp["rw_mu"]), lp["rw_w0"][None], lp["rw_a0"][None], lp["rw_kk"][None],
                           lp["rw_ka"][None], lp["rw_rk"].reshape(1, BR_W), jnp.zeros((7, BR_W), F32)], axis=0)
    rs, post = pl.pallas_call(
        _rwkv_prep_kernel,
        out_shape=(SDS((b * l, 6 * BR_W), F32), SDS((b * l, 2 * BR_W), F32)),
        grid=(b, nt),
        in_specs=[tile, halo, state, pl.BlockSpec((16, BR_W), lambda bb, i: (0, 0)),
                  pl.BlockSpec((RWKV_W_LORA, BR_W), lambda bb, i: (0, 0)),
                  pl.BlockSpec((RWKV_A_LORA, BR_W), lambda bb, i: (0, 0)),
                  pl.BlockSpec((RWKV_G_LORA, BR_W), lambda bb, i: (0, 0))],
        out_specs=(pl.BlockSpec((rows, 6 * BR_W), lambda bb, i: (bb * nt + i, 0)),
                   pl.BlockSpec((rows, 2 * BR_W), lambda bb, i: (bb * nt + i, 0))),
        compiler_params=_cp(("parallel", "parallel")),
        name="rwkv_prep",
    )(rslab, rslab, _pad_state_rows(sh0[:, None, :], RWKV_PROJ), vec, lp["rw_w2"], lp["rw_a2"], lp["rw_g2"])

    c = math.gcd(l, 64)
    nc = l // c
    prow, pcb, pblk = _seq_blocks(l, c, 256)
    (rs_spec, rh_spec, y0_spec), mat = _chunk_specs(prow, pcb, pblk, (6 * BR_W, BR_W, BR_W))
    rh, y0, phi, psi = pl.pallas_call(
        functools.partial(_rwkv_chunk_kernel, c=c, cb=pcb, gc=math.gcd(pcb, 4)),
        out_shape=(SDS((b * l, BR_W), F32), SDS((b * l, BR_W), F32),
                   SDS((b * nc, HEADS, HD, HD), F32), SDS((b * nc, HEADS, HD, HD), F32)),
        grid=(b, pblk),
        in_specs=[rs_spec],
        out_specs=(rh_spec, y0_spec, mat, mat),
        compiler_params=_cp(("parallel", "parallel")),
        name="rwkv_chunk",
    )(rs)

    srow, scb, sblk = _seq_blocks(l, c, 512)
    (rh_spec, y0_spec, post_spec, o_spec), mat = _chunk_specs(srow, scb, sblk, (BR_W, BR_W, 2 * BR_W, BR_W))
    ln = jnp.stack([lp["rw_ln_g"], lp["rw_ln_b"]] + [jnp.zeros((BR_W,), F32)] * 6)
    o, s1 = pl.pallas_call(
        functools.partial(_rwkv_scan_kernel, c=c, cb=scb),
        out_shape=(SDS((b * l, BR_W), F32), SDS((b, HEADS, HD, HD), F32)),
        grid=(b, sblk),
        in_specs=[rh_spec, y0_spec, mat, mat, post_spec, _state_spec((HEADS, HD, HD)),
                  pl.BlockSpec((SUBLANES, BR_W), lambda bb, i: (0, 0))],
        out_specs=(o_spec, _state_spec((HEADS, HD, HD))),
        scratch_shapes=[pltpu.VMEM((HEADS, HD, HD), F32), pltpu.VMEM((srow, BR_W), F32)],
        compiler_params=_cp(("parallel", "arbitrary")),
        name="rwkv_scan",
    )(rh, y0, phi, psi, post, s0, ln)
    return o, s1


def _gdn_prep_kernel(x_ref, halo_ref, st_ref, taps_ref, o_ref):
    i = pl.program_id(1)
    x = x_ref[:, 0:GDN_CONV_CH]
    rows = x.shape[0]
    halo = jnp.where(i == 0, st_ref[:, 0:GDN_CONV_CH], halo_ref[:, 0:GDN_CONV_CH])
    conv = x * taps_ref[CONV_W - 1:CONV_W, :]
    for s in range(1, CONV_W):
        conv = conv + _shift_rows(x, halo, s) * taps_ref[CONV_W - 1 - s:CONV_W - s, :]
    conv = conv * jax.nn.sigmoid(conv)
    masks = _head_masks(rows)
    for n in range(3):
        t = conv[:, n * BR_W:(n + 1) * BR_W]
        if n < 2:
            t = t * lax.rsqrt(jnp.maximum(_seg_sum(t * t, masks), 1e-12))
        if n == 0:
            t = t * (HD ** -0.5)
        o_ref[:, n * BR_W:(n + 1) * BR_W] = t


def _gdn_chunk_kernel(x_ref, sm_ref, qh_ref, o0_ref, phi_ref, psi_ref, *, c, cb, gc):
    r_i = _iota((c, c), 0)
    c_i = _iota((c, c), 1)
    eye = _iota((HD, HD), 0) == _iota((HD, HD), 1)
    jobs = [(g, h) for g in range(gc) for h in range(HEADS)]

    def group(gi, carry):
        r0s = [pl.multiple_of((gi * gc + g) * c, c) for g in range(gc)]
        q, k, v, beta, dec, eg, gl, wv = [], [], [], [], [], [], [], []
        for g, h in jobs:
            r0, lo = r0s[g], HD * h
            q.append(x_ref[pl.ds(r0, c), lo:lo + HD])
            k.append(x_ref[pl.ds(r0, c), BR_W + lo:BR_W + lo + HD])
            v.append(x_ref[pl.ds(r0, c), 2 * BR_W + lo:2 * BR_W + lo + HD])
            beta.append(sm_ref[pl.ds(r0, c), SM_GB + h:SM_GB + h + 1])
            gam_row, gam_col = _cumsum_col(sm_ref[pl.ds(r0, c), SM_GA + h:SM_GA + h + 1], c)
            dec.append(jnp.exp(jnp.where(c_i <= r_i, gam_col - gam_row, NEG_INF)))
            eg.append(jnp.exp(gam_col))
            gl.append(gam_col[c - 1:c, :])
            wv.append(jnp.exp(gam_col[c - 1:c, :] - gam_col))
        n = range(len(jobs))
        big = [_dot3(jnp.concatenate([k[i], q[i]], axis=0), k[i], _NT) for i in n]
        tinv = _unit_lower_inverse([-(beta[i] * big[i][0:c] * jnp.where(c_i < r_i, dec[i], 0.0)) for i in n], c)
        x = [_dot3(tinv[i], jnp.concatenate([beta[i] * v[i], (beta[i] * eg[i]) * k[i]], axis=1)) for i in n]
        out = [_dot3(big[i][c:2 * c] * dec[i], x[i]) for i in n]
        xb = [_dot3(x[i], wv[i] * k[i], _TN) for i in n]
        for g in range(gc):
            sel = [i for i in n if jobs[i][0] == g]
            qh_ref[pl.ds(r0s[g], c), :] = jnp.concatenate([eg[i] * q[i] - out[i][:, HD:2 * HD] for i in sel], axis=1)
            o0_ref[pl.ds(r0s[g], c), :] = jnp.concatenate([out[i][:, 0:HD] for i in sel], axis=1)
            phi_ref[gi * gc + g] = jnp.stack([jnp.where(eye, jnp.broadcast_to(jnp.exp(gl[i]), (HD, HD)), 0.0)
                                              - xb[i][HD:2 * HD] for i in sel])
            psi_ref[gi * gc + g] = jnp.stack([xb[i][0:HD] for i in sel])
        return carry

    lax.fori_loop(0, cb // gc, group, 0)


def _gdn_scan_kernel(qh_ref, o0_ref, phi_ref, psi_ref, z_ref, s0_ref, gn_ref, o_ref, s1_ref, s_s, y_s, *, c, cb):
    i = pl.program_id(1)

    @pl.when(i == 0)
    def _():
        s_s[...] = s0_ref[...]

    _lin_scan(qh_ref, o0_ref, phi_ref, psi_ref, s_s, y_s, c, cb)
    o = y_s[...]
    ms = _seg_sum(o * o, _head_masks(c * cb)) * (1.0 / HD)
    z = z_ref[:, 3 * BR_W:4 * BR_W]
    o_ref[...] = o * lax.rsqrt(ms + EPS) * gn_ref[...] * (z * jax.nn.sigmoid(z))

    @pl.when(i == pl.num_programs(1) - 1)
    def _():
        s1_ref[...] = s_s[...]


def _gdn(gslab, sm, row_off, b, l, cv0, s0, lp):
    rows = math.gcd(l, 256)
    nt = l // rows
    w = 4 * BR_W
    tile, halo, state = _halo_specs(rows, w, row_off, nt)
    taps = jnp.pad(lp["gd_conv"], ((0, SUBLANES - CONV_W), (0, 0)))
    qkv = pl.pallas_call(
        _gdn_prep_kernel,
        out_shape=SDS((b * l, GDN_CONV_CH), F32),
        grid=(b, nt),
        in_specs=[tile, halo, state, pl.BlockSpec((SUBLANES, GDN_CONV_CH), lambda bb, i: (0, 0))],
        out_specs=pl.BlockSpec((rows, GDN_CONV_CH), lambda bb, i: (bb * nt + i, 0)),
        compiler_params=_cp(("parallel", "parallel")),
        name="gdn_prep",
    )(gslab, gslab, _pad_state_rows(cv0, w), taps)

    c = math.gcd(l, GDN_CHUNK)
    nc = l // c
    prow, pcb, pblk = _seq_blocks(l, c, 256)
    (x_spec, qh_spec, o0_spec), mat = _chunk_specs(prow, pcb, pblk, (GDN_CONV_CH, BR_W, BR_W))
    sm_spec = pl.BlockSpec((prow, LANES), lambda bb, i: (row_off // prow + bb * pblk + i, 0))
    qh, o0, phi, psi = pl.pallas_call(
        functools.partial(_gdn_chunk_kernel, c=c, cb=pcb, gc=math.gcd(pcb, 2)),
        out_shape=(SDS((b * l, BR_W), F32), SDS((b * l, BR_W), F32),
                   SDS((b * nc, HEADS, HD, HD), F32), SDS((b * nc, HEADS, HD, HD), F32)),
        grid=(b, pblk),
        in_specs=[x_spec, sm_spec],
        out_specs=(qh_spec, o0_spec, mat, mat),
        compiler_params=_cp(("parallel", "parallel")),
        name="gdn_chunk",
    )(qkv, sm)

    srow, scb, sblk = _seq_blocks(l, c, 512)
    (qh_spec, o0_spec, o_spec), mat = _chunk_specs(srow, scb, sblk, (BR_W, BR_W, BR_W))
    z_spec = pl.BlockSpec((srow, w), lambda bb, i: (row_off // srow + bb * sblk + i, 0))
    o, s1 = pl.pallas_call(
        functools.partial(_gdn_scan_kernel, c=c, cb=scb),
        out_shape=(SDS((b * l, BR_W), F32), SDS((b, HEADS, HD, HD), F32)),
        grid=(b, sblk),
        in_specs=[qh_spec, o0_spec, mat, mat, z_spec, _state_spec((HEADS, HD, HD)),
                  pl.BlockSpec((1, BR_W), lambda bb, i: (0, 0))],
        out_specs=(o_spec, _state_spec((HEADS, HD, HD))),
        scratch_shapes=[pltpu.VMEM((HEADS, HD, HD), F32), pltpu.VMEM((srow, BR_W), F32)],
        compiler_params=_cp(("parallel", "arbitrary")),
        name="gdn_scan",
    )(qh, o0, phi, psi, gslab, s0, jnp.tile(lp["gd_norm"], HEADS).reshape(1, BR_W))
    return o, s1


def _rope_tables(pos):
    half = HD // 2
    inv = ROPE_THETA ** (-jnp.arange(half, dtype=F32) / half)
    ang = pos.astype(F32)[:, None] * inv[None, :]
    cos, sin = jnp.cos(ang), jnp.sin(ang)
    zero = jnp.zeros_like(sin)
    tile = lambda a, b: jnp.concatenate([a, b, a, b], axis=1)
    return tile(cos, cos), tile(-sin, zero), tile(zero, sin)


def _pad_w_in(w):
    sizes = (BR_W, BR_W, BR_W, IDX_HEADS * IDX_DIM, IDX_DIM, IDX_HEADS, BR_W, BR_W, BR_W, HEADS, HEADS, BR_W,
             RWKV_PROJ, GDN_CONV_CH, HEADS, HEADS, BR_W, N_BRANCH * w.shape[0])
    cuts = np.concatenate([[0], np.cumsum(sizes)])
    col = lambda i: w[:, cuts[i]:cuts[i + 1]]
    (a_q, a_k, a_v, a_iq, a_ik, a_iw, m_q, m_k, m_v, m_i, m_f, m_o, r_p, g_qkv, g_b, g_a, g_z, gate) = (
        col(i) for i in range(len(sizes)))
    small = jnp.concatenate([a_ik, a_iw, m_i, m_f, g_b, g_a, jnp.zeros((w.shape[0], LANES - SM_END), w.dtype)], axis=1)
    return jnp.concatenate([a_q, a_k, a_v, a_iq, small, m_q, m_k, m_v, m_o, r_p, g_qkv, g_z, gate], axis=1).astype(BF16)


def _small_params(lp):
    lane_put = lambda v, at: jnp.zeros((LANES,), F32).at[at:at + v.shape[0]].set(v)
    rows = [lane_put(lp["idx_k_norm"], SM_IK),
            lane_put(lp["ml_i_bias"], SM_MI) + lane_put(lp["ml_f_bias"], SM_MF) + lane_put(lp["gd_dt_bias"], SM_GA),
            lane_put(lp["gd_A_log"], SM_GA)]
    return jnp.stack(rows + [jnp.zeros((LANES,), F32)] * (SUBLANES - len(rows)))


def kernel(x_prompt, x_sample, cache_k, cache_v, cache_idx_k, state_mlstm_C, state_mlstm_n, state_mlstm_m, state_rwkv_S, state_rwkv_shift, state_gdn_S, state_gdn_conv, page_table, ffn1_norm, ffn1_w1, ffn1_w3, ffn1_w2, mix_norm, w_in, idx_k_norm, ml_i_bias, ml_f_bias, ml_norm, rw_mu, rw_w0, rw_w2, rw_a0, rw_a2, rw_g2, rw_kk, rw_ka, rw_rk, rw_ln_g, rw_ln_b, gd_conv, gd_A_log, gd_dt_bias, gd_norm, w_branch, w_out, ffn2_norm, ffn2_w1, ffn2_w3, ffn2_w2, final_norm):
    bp, lp_, d = x_prompt.shape
    db, t, _ = x_sample.shape
    assert bp == 1 and t % SUBLANES == 0 and t <= LANES and lp_ % SUBLANES == 0
    depth = w_in.shape[0]
    past = page_table.shape[1] * PAGE_SIZE
    ns = db * t

    x = jnp.concatenate([x_prompt.reshape(lp_, d), x_sample.reshape(ns, d)], axis=0)
    pos = jnp.concatenate([jnp.arange(lp_, dtype=I32), jnp.tile(past + jnp.arange(t, dtype=I32), db)])
    cos, sa, sb = _rope_tables(pos)
    zeros = lambda *s: jnp.zeros(s, F32)

    outs_p, outs_s = [], []
    for l in range(depth):
        lp = dict(idx_k_norm=idx_k_norm[l], ml_i_bias=ml_i_bias[l], ml_f_bias=ml_f_bias[l],
                  rw_mu=rw_mu[l], rw_w0=rw_w0[l], rw_w2=rw_w2[l], rw_a0=rw_a0[l], rw_a2=rw_a2[l], rw_g2=rw_g2[l],
                  rw_kk=rw_kk[l], rw_ka=rw_ka[l], rw_rk=rw_rk[l], rw_ln_g=rw_ln_g[l], rw_ln_b=rw_ln_b[l],
                  gd_conv=gd_conv[l], gd_A_log=gd_A_log[l], gd_dt_bias=gd_dt_bias[l], gd_norm=gd_norm[l])
        x = _ffn(x, ffn1_norm[l], ffn1_w1[l], ffn1_w3[l], ffn1_w2[l])
        aqkv, iq, sm, mslab, rslab, gslab, gate = _proj(x, mix_norm[l], _pad_w_in(w_in[l]), cos, sa, sb, _small_params(lp))

        oa_p = _dsa_prompt(iq, sm, aqkv, lp_)
        oa_s = _dsa_sample(iq[lp_:], sm[lp_:], aqkv[lp_:], cache_k, cache_v, cache_idx_k, page_table, l, db, t)
        om_p, c_p, n_p, m_p = _mlstm(mslab, sm, 0, 1, lp_, zeros(1, HEADS, HD, HD), zeros(1, HEADS, HD), zeros(1, HEADS), ml_norm[l])
        om_s, c_s, n_s, m_s = _mlstm(mslab, sm, lp_, db, t, state_mlstm_C[l], state_mlstm_n[l], state_mlstm_m[l], ml_norm[l])
        or_p, rs_p = _rwkv(rslab, 0, 1, lp_, zeros(1, RWKV_PROJ), zeros(1, HEADS, HD, HD), lp)
        or_s, rs_s = _rwkv(rslab, lp_, db, t, state_rwkv_shift[l], state_rwkv_S[l], lp)
        og_p, gs_p = _gdn(gslab, sm, 0, 1, lp_, zeros(1, CONV_W - 1, GDN_CONV_CH), zeros(1, HEADS, HD, HD), lp)
        og_s, gs_s = _gdn(gslab, sm, lp_, db, t, state_gdn_conv[l], state_gdn_S[l], lp)

        cat = lambda a, b: jnp.concatenate([a, b], axis=0)
        x = _merge(x, cat(oa_p, oa_s), cat(om_p, om_s), cat(or_p, or_s), cat(og_p, og_s), gate, w_branch[l], w_out[l])
        x = _ffn(x, ffn2_norm[l], ffn2_w1[l], ffn2_w3[l], ffn2_w2[l])

        k_all = aqkv[:, BR_W:2 * BR_W]
        v_all = aqkv[:, 2 * BR_W:3 * BR_W]
        ik_all = sm[:, SM_IK:SM_IK + IDX_DIM]
        gq = gslab[:, 0:GDN_CONV_CH]
        outs_p.append((k_all[:lp_].reshape(1, lp_, HEADS, HD), v_all[:lp_].reshape(1, lp_, HEADS, HD),
                       ik_all[:lp_].reshape(1, lp_, IDX_DIM), c_p, n_p, m_p, rs_p, rslab[lp_ - 1:lp_],
                       gs_p, gq[lp_ - (CONV_W - 1):lp_].reshape(1, CONV_W - 1, GDN_CONV_CH)))
        outs_s.append((k_all[lp_:].reshape(db, t, HEADS, HD), v_all[lp_:].reshape(db, t, HEADS, HD),
                       ik_all[lp_:].reshape(db, t, IDX_DIM), c_s, n_s, m_s, rs_s,
                       rslab[lp_:].reshape(db, t, RWKV_PROJ)[:, -1],
                       gs_s, gq[lp_:].reshape(db, t, GDN_CONV_CH)[:, t - (CONV_W - 1):]))

    y = _final_rms(x, final_norm)
    st_p = [jnp.stack(z) for z in zip(*outs_p)]
    st_s = [jnp.stack(z) for z in zip(*outs_s)]
    return (y[:lp_].reshape(1, lp_, d), y[lp_:].reshape(db, t, d), *st_p, *st_s)
```
